```python
import jax, jax.numpy as jnp
from jax import lax
import numpy as np

D_MODEL = 2048
BATCH = 2
SEQ = 4096
DEPTH = 1
DEC_BATCH = 32
DEC_SEQ = 4
PAST_LEN = 16384
PAGE_SIZE = 128

D_CONV = D_MODEL // 2
CONV_W = 3
DILATED_GROUPS = ((128, 1), (512, 4), (2048, 16))
N_DIL = len(DILATED_GROUPS)
HEADS_PER_GROUP = 4
HEAD_DIM = 128
D_ATT = N_DIL * HEADS_PER_GROUP * HEAD_DIM
D_ATT_OUT = HEADS_PER_GROUP * HEAD_DIM
OFF_B = 0
OFF_C = D_CONV
OFF_H = 2 * D_CONV
OFF_Q = 3 * D_CONV
OFF_K = OFF_Q + D_ATT
OFF_V = OFF_K + D_ATT
OFF_GC = OFF_V + D_ATT
OFF_GA = OFF_GC + D_MODEL
N_IN = OFF_GA + D_MODEL
N_EXPERT_GROUPS = 8
EXPERTS_PER_GROUP = 8
N_EXPERTS = N_EXPERT_GROUPS * EXPERTS_PER_GROUP
TOP_K = 2
D_EXPERT = D_MODEL // 4
MOE_BLOCK = 64
LN_EPS = 1e-5
ALPHA = (2 * DEPTH) ** 0.25
BETA = (8 * DEPTH) ** -0.25
NEG_INF = -1e30

kernel_name = 'hybrid_conv_dilated_attn_hmoe_step'


def layer_norm(x, g, b):
    xf = x.astype(jnp.float32)
    mu = jnp.mean(xf, -1, keepdims=True)
    var = jnp.mean(jnp.square(xf - mu), -1, keepdims=True)
    y = (xf - mu) * lax.rsqrt(var + LN_EPS) * g.astype(jnp.float32) + b.astype(jnp.float32)
    return y.astype(x.dtype)


def dilated_attn_prompt(q, k, v, window, dil):
    bsz, seq, nh, hd = q.shape
    span = window // dil
    unit = dil * span
    seq_p = -(-seq // unit) * unit
    n_sub = seq_p // dil
    nb = n_sub // span

    def to_blocks(a):
        a = jnp.pad(a, ((0, 0), (0, seq_p - seq), (0, 0), (0, 0)))
        a = a.reshape(bsz, n_sub, dil, nh, hd).transpose(0, 2, 1, 3, 4)
        return a.reshape(bsz, dil, nb, span, nh, hd)

    def with_prev(a):
        prev = jnp.pad(a, ((0, 0), (0, 0), (1, 0), (0, 0), (0, 0), (0, 0)))[:, :, :-1]
        return jnp.concatenate([prev, a], axis=3)

    qb = to_blocks(q)
    kk = with_prev(to_blocks(k))
    vv = with_prev(to_blocks(v))
    s = jnp.einsum('brcqhe,brckhe->brchqk', qb, kk,
                   preferred_element_type=jnp.float32) * (hd ** -0.5)
    qi = jnp.arange(span)[:, None]
    kj = jnp.arange(2 * span)[None, :]
    dist = span + qi - kj
    band = (dist >= 0) & (dist <= span)
    has_prev = jnp.arange(nb)[:, None, None] > 0
    mask = band[None] & (has_prev | (kj >= span)[None])
    s = jnp.where(mask[None, None, :, None], s, NEG_INF)
    m = jnp.max(s, -1, keepdims=True)
    p = jnp.exp(s - m)
    den = jnp.sum(p, -1, keepdims=True)
    o = jnp.einsum('brchqk,brckhe->brcqhe', p / den, vv.astype(jnp.float32))
    lse = (m + jnp.log(den))[..., 0].transpose(0, 1, 2, 4, 3)
    o = o.reshape(bsz, dil, n_sub, nh, hd).transpose(0, 2, 1, 3, 4).reshape(bsz, seq_p, nh, hd)[:, :seq]
    lse = lse.reshape(bsz, dil, n_sub, nh).transpose(0, 2, 1, 3).reshape(bsz, seq_p, nh)[:, :seq]
    return o, lse


def dilated_attn_sample(q, k, v, kv_buf, window, dil):
    n_buf = kv_buf.shape[1]
    t_new = q.shape[1]
    hd = q.shape[-1]
    span = window // dil
    kv_all = jnp.concatenate([kv_buf, jnp.stack([k, v], axis=2)], axis=1)
    idx = n_buf + jnp.arange(t_new)[:, None] - dil * jnp.arange(span + 1)[None, :]
    valid = idx >= 0
    kvg = kv_all[:, jnp.maximum(idx, 0)]
    s = jnp.einsum('bthe,btkhe->bthk', q, kvg[:, :, :, 0],
                   preferred_element_type=jnp.float32) * (hd ** -0.5)
    s = jnp.where(valid[None, :, None, :], s, NEG_INF)
    m = jnp.max(s, -1, keepdims=True)
    p = jnp.exp(s - m)
    den = jnp.sum(p, -1, keepdims=True)
    o = jnp.einsum('bthk,btkhe->bthe', p / den, kvg[:, :, :, 1].astype(jnp.float32))
    lse = (m + jnp.log(den))[..., 0]
    new_buf = kv_all[:, -min(window, n_buf + t_new):]
    return o, lse, new_buf


def token_mixer(x, w_in, conv_w, w_conv_out, w_att_out, w_o, conv_buf, kv_bufs):
    bsz, seq, _ = x.shape
    z = x @ w_in
    zb = z[..., OFF_B:OFF_C]
    zc = z[..., OFF_C:OFF_H]
    zh = z[..., OFF_H:OFF_Q]
    q = z[..., OFF_Q:OFF_K].reshape(bsz, seq, N_DIL, HEADS_PER_GROUP, HEAD_DIM)
    k = z[..., OFF_K:OFF_V].reshape(bsz, seq, N_DIL, HEADS_PER_GROUP, HEAD_DIM)
    v = z[..., OFF_V:OFF_GC].reshape(bsz, seq, N_DIL, HEADS_PER_GROUP, HEAD_DIM)
    gate_c = z[..., OFF_GC:OFF_GA]
    gate_a = z[..., OFF_GA:]
    u = zc * zh
    if conv_buf is None:
        u_ext = jnp.pad(u, ((0, 0), (CONV_W - 1, 0), (0, 0)))
    else:
        u_ext = jnp.concatenate([conv_buf, u], axis=1)
    conv = sum(conv_w[j] * u_ext[:, j:j + seq] for j in range(CONV_W))
    y_conv = (zb * conv) @ w_conv_out
    outs, lses, new_kv = [], [], []
    for g, (window, dil) in enumerate(DILATED_GROUPS):
        qg, kg, vg = q[:, :, g], k[:, :, g], v[:, :, g]
        if kv_bufs is None:
            o, lse = dilated_attn_prompt(qg, kg, vg, window, dil)
            nkv = jnp.stack([kg, vg], axis=2)[:, -min(window, seq):]
        else:
            o, lse, nkv = dilated_attn_sample(qg, kg, vg, kv_bufs[g], window, dil)
        outs.append(o)
        lses.append(lse)
        new_kv.append(nkv)
    wts = jax.nn.softmax(jnp.stack(lses), axis=0)
    o = jnp.einsum('gbsh,gbshe->bshe', wts, jnp.stack(outs))
    y_att = o.reshape(bsz, seq, D_ATT_OUT).astype(x.dtype) @ w_att_out
    merged = jax.nn.sigmoid(gate_c) * y_conv + jax.nn.sigmoid(gate_a) * y_att
    return merged @ w_o, u_ext[:, -(CONV_W - 1):], new_kv


def hier_route(hf, w_rg, b_rg, w_re, b_re):
    lg = jnp.einsum('nd,dg->ng', hf, w_rg, preferred_element_type=jnp.float32) + b_rg.astype(jnp.float32)
    pg = jax.nn.softmax(lg, axis=-1)
    _, gsel = lax.top_k(lg, 1)
    p_group = jnp.take_along_axis(pg, gsel, axis=1)
    le_all = jnp.einsum('nd,gde->nge', hf, w_re, preferred_element_type=jnp.float32) + b_re.astype(jnp.float32)
    le = jnp.take_along_axis(le_all, gsel[:, :, None], axis=1)[:, 0]
    top_v, top_i = lax.top_k(le, TOP_K)
    gate_w = jax.nn.softmax(top_v, axis=-1) * p_group
    expert_idx = gsel * EXPERTS_PER_GROUP + top_i
    return expert_idx, gate_w


def moe_ffn(hf, expert_idx, gate_w, w_gate, w_up, w_down):
    n_tok, d = hf.shape
    n_asg = n_tok * TOP_K
    flat_e = expert_idx.reshape(n_asg)
    flat_tok = jnp.arange(n_asg, dtype=jnp.int32) // TOP_K
    order = jnp.argsort(flat_e)
    sorted_e = flat_e[order]
    counts = jnp.bincount(flat_e, length=N_EXPERTS)
    padded = (counts + MOE_BLOCK - 1) // MOE_BLOCK * MOE_BLOCK
    pad_end = jnp.cumsum(padded)
    pad_start = pad_end - padded
    start = jnp.cumsum(counts) - counts
    dest = (pad_start[sorted_e] + jnp.arange(n_asg) - start[sorted_e]).astype(jnp.int32)
    n_blocks = -(-(n_asg + N_EXPERTS * (MOE_BLOCK - 1)) // MOE_BLOCK)
    n_rows = n_blocks * MOE_BLOCK
    row_tok = jnp.full((n_rows,), n_tok, jnp.int32).at[dest].set(flat_tok[order])
    blk_e = jnp.minimum(jnp.searchsorted(pad_end, jnp.arange(n_blocks) * MOE_BLOCK, side='right'),
                        N_EXPERTS - 1)
    x_rows = jnp.concatenate([hf, jnp.zeros((1, d), hf.dtype)], axis=0)[row_tok]
    x_rows = x_rows.reshape(n_blocks, MOE_BLOCK, d)

    def expert_block(args):
        xb, e = args
        hb = jax.nn.silu(xb @ w_gate[e]) * (xb @ w_up[e])
        return hb @ w_down[e]

    y_rows = lax.map(expert_block, (x_rows, blk_e)).reshape(n_rows, d)
    pos = jnp.zeros((n_asg,), jnp.int32).at[order].set(dest)
    y = y_rows[pos].reshape(n_tok, TOP_K, d)
    return jnp.einsum('nk,nkd->nd', gate_w.astype(y.dtype), y)


def decoder_layer(x, lw, conv_buf, kv_bufs):
    (w_in, conv_w, w_conv_out, w_att_out, w_o, ln1_g, ln1_b, w_rg, b_rg, w_re, b_re,
     w_eg, w_eu, w_ed, ln2_g, ln2_b) = lw
    mix, new_conv, new_kv = token_mixer(x, w_in, conv_w, w_conv_out, w_att_out, w_o, conv_buf, kv_bufs)
    h = layer_norm(ALPHA * x + mix, ln1_g, ln1_b)
    bsz, seq, d = h.shape
    hf = h.reshape(bsz * seq, d)
    expert_idx, gate_w = hier_route(hf, w_rg, b_rg, w_re, b_re)
    f = moe_ffn(hf, expert_idx, gate_w, w_eg, w_eu, w_ed).reshape(bsz, seq, d)
    y = layer_norm(ALPHA * h + f, ln2_g, ln2_b)
    return y, new_conv, new_kv


def setup_inputs(seed: int = 0) -> dict:
    key = jax.random.key(seed)
    ks = jax.random.split(key, 24)

    def nrm(k, shape, scale):
        return jax.random.normal(k, shape, jnp.float32) * scale

    inp = {}
    inp['x_prompt'] = nrm(ks[0], (BATCH, SEQ, D_MODEL), 1.0)
    inp['x_sample'] = nrm(ks[1], (DEC_BATCH, DEC_SEQ, D_MODEL), 1.0)
    inp['cache_kv_w128'] = nrm(ks[2], (DEPTH, DEC_BATCH, min(DILATED_GROUPS[0][0], PAST_LEN), 2,
                                       HEADS_PER_GROUP, HEAD_DIM), 1.0)
    inp['cache_kv_w512'] = nrm(ks[3], (DEPTH, DEC_BATCH, min(DILATED_GROUPS[1][0], PAST_LEN), 2,
                                       HEADS_PER_GROUP, HEAD_DIM), 1.0)
    inp['cache_kv_w2048'] = nrm(ks[4], (DEPTH, DEC_BATCH, min(DILATED_GROUPS[2][0], PAST_LEN), 2,
                                        HEADS_PER_GROUP, HEAD_DIM), 1.0)
    inp['state_conv'] = nrm(ks[5], (DEPTH, DEC_BATCH, CONV_W - 1, D_CONV), 1.0)
    col_scale = jnp.ones((N_IN,), jnp.float32).at[OFF_V:OFF_GC].set(BETA)
    inp['w_in'] = nrm(ks[6], (DEPTH, D_MODEL, N_IN), D_MODEL ** -0.5) * col_scale
    inp['conv_w'] = nrm(ks[7], (DEPTH, CONV_W, D_CONV), CONV_W ** -0.5)
    inp['w_conv_out'] = nrm(ks[8], (DEPTH, D_CONV, D_MODEL), D_CONV ** -0.5)
    inp['w_att_out'] = nrm(ks[9], (DEPTH, D_ATT_OUT, D_MODEL), D_ATT_OUT ** -0.5)
    inp['w_o'] = nrm(ks[10], (DEPTH, D_MODEL, D_MODEL), BETA * D_MODEL ** -0.5)
    inp['ln1_g'] = 1.0 + nrm(ks[11], (DEPTH, D_MODEL), 0.02)
    inp['ln1_b'] = nrm(ks[12], (DEPTH, D_MODEL), 0.02)
    inp['w_router_group'] = nrm(ks[13], (DEPTH, D_MODEL, N_EXPERT_GROUPS), D_MODEL ** -0.5)
    inp['b_router_group'] = nrm(ks[14], (DEPTH, N_EXPERT_GROUPS), 0.01)
    inp['w_router_expert'] = nrm(ks[15], (DEPTH, N_EXPERT_GROUPS, D_MODEL, EXPERTS_PER_GROUP),
                                 D_MODEL ** -0.5)
    inp['b_router_expert'] = nrm(ks[16], (DEPTH, N_EXPERT_GROUPS, EXPERTS_PER_GROUP), 0.01)
    inp['w_expert_gate'] = nrm(ks[17], (DEPTH, N_EXPERTS, D_MODEL, D_EXPERT), D_MODEL ** -0.5)
    inp['w_expert_up'] = nrm(ks[18], (DEPTH, N_EXPERTS, D_MODEL, D_EXPERT), D_MODEL ** -0.5)
    inp['w_expert_down'] = nrm(ks[19], (DEPTH, N_EXPERTS, D_EXPERT, D_MODEL), BETA * D_EXPERT ** -0.5)
    inp['ln2_g'] = 1.0 + nrm(ks[20], (DEPTH, D_MODEL), 0.02)
    inp['ln2_b'] = nrm(ks[21], (DEPTH, D_MODEL), 0.02)
    return inp


def reference(x_prompt, x_sample, cache_kv_w128, cache_kv_w512, cache_kv_w2048, state_conv,
              w_in, conv_w, w_conv_out, w_att_out, w_o, ln1_g, ln1_b,
              w_router_group, b_router_group, w_router_expert, b_router_expert,
              w_expert_gate, w_expert_up, w_expert_down, ln2_g, ln2_b):
    kv_sample_in = (cache_kv_w128, cache_kv_w512, cache_kv_w2048)
    y_prompt, y_sample = x_prompt, x_sample
    st_p = [[] for _ in range(N_DIL + 1)]
    st_s = [[] for _ in range(N_DIL + 1)]
    for l in range(DEPTH):
        lw = (w_in[l], conv_w[l], w_conv_out[l], w_att_out[l], w_o[l], ln1_g[l], ln1_b[l],
              w_router_group[l], b_router_group[l], w_router_expert[l], b_router_expert[l],
              w_expert_gate[l], w_expert_up[l], w_expert_down[l], ln2_g[l], ln2_b[l])
        y_prompt, conv_p, kv_p = decoder_layer(y_prompt, lw, None, None)
        y_sample, conv_s, kv_s = decoder_layer(y_sample, lw, state_conv[l],
                                               tuple(c[l] for c in kv_sample_in))
        for g in range(N_DIL):
            st_p[g].append(kv_p[g])
            st_s[g].append(kv_s[g])
        st_p[N_DIL].append(conv_p)
        st_s[N_DIL].append(conv_s)
    kv128_p, kv512_p, kv2048_p, conv_p_new = [jnp.stack(a) for a in st_p]
    kv128_s, kv512_s, kv2048_s, conv_s_new = [jnp.stack(a) for a in st_s]
    return (y_prompt, y_sample, kv128_p, kv512_p, kv2048_p, conv_p_new,
            kv128_s, kv512_s, kv2048_s, conv_s_new)
```

```python
import functools

import jax
import jax.numpy as jnp
from jax import lax
from jax.experimental import pallas as pl
from jax.experimental.pallas import tpu as pltpu

F32 = jnp.float32
BF16 = jnp.bfloat16

D_MODEL = 2048
D_CONV = D_MODEL // 2
CONV_W = 3
DILATED_GROUPS = ((128, 1), (512, 4), (2048, 16))
N_DIL = len(DILATED_GROUPS)
HEADS_PER_GROUP = 4
HEAD_DIM = 128
SPAN = 128
D_GRP = HEADS_PER_GROUP * HEAD_DIM
D_ATT = N_DIL * D_GRP
N_EXPERT_GROUPS = 8
EXPERTS_PER_GROUP = 8
N_EXPERTS = N_EXPERT_GROUPS * EXPERTS_PER_GROUP
TOP_K = 2
D_EXPERT = D_MODEL // 4
DEPTH = 1
LN_EPS = 1e-5
ALPHA = (2 * DEPTH) ** 0.25
NEG_INF = -1e30
ATT_SCALE = HEAD_DIM ** -0.5

OFF_B = 0
OFF_C = D_CONV
OFF_H = 2 * D_CONV
OFF_Q = 3 * D_CONV
OFF_K = OFF_Q + D_ATT
OFF_V = OFF_K + D_ATT
OFF_GC = OFF_V + D_ATT
OFF_GA = OFF_GC + D_MODEL
N_IN = OFF_GA + D_MODEL

LANES = 128
SUBLANES = 8
VMEM_LIMIT = 56 * 1024 * 1024

PROJ_TN = 512
PROJ_TM = 1024
Z_COLS = 2 * D_MODEL + 3 * D_CONV
QKV_COLS = 3 * D_ATT
NZ_TILES = Z_COLS // PROJ_TN
NQ_TILES = QKV_COLS // PROJ_TN
CONV_TM = 512
MIX_TM = 256
ROW_TILE = 128
ROUTER_TILE = 640
MOE_BLK = 256


def _proj_tile_order():
    t = lambda off, width: list(range(off // PROJ_TN, (off + width) // PROJ_TN))
    z_tiles = t(OFF_GC, D_MODEL) + t(OFF_GA, D_MODEL) + t(OFF_B, D_CONV) + t(OFF_C, D_CONV) + t(OFF_H, D_CONV)
    q_tiles = t(OFF_Q, D_ATT) + t(OFF_K, D_ATT) + t(OFF_V, D_ATT)
    return z_tiles + q_tiles


def _params(*sem):
    return pltpu.CompilerParams(dimension_semantics=sem, vmem_limit_bytes=VMEM_LIMIT)


def _proj_kernel(order_ref, x_ref, w_ref, z_ref, qkv_ref):
    del order_ref
    j = pl.program_id(1)

    @pl.when(j < NZ_TILES)
    def _():
        z_ref[...] = jnp.dot(x_ref[...], w_ref[...], preferred_element_type=F32).astype(BF16)

    @pl.when(j >= NZ_TILES)
    def _():
        qkv_ref[...] = jnp.dot(x_ref[...], w_ref[...], preferred_element_type=F32)


def _proj_in(x16, w16, tm):
    m = x16.shape[0]
    order = jnp.asarray(_proj_tile_order(), jnp.int32)
    grid_spec = pltpu.PrefetchScalarGridSpec(
        num_scalar_prefetch=1,
        grid=(m // tm, NZ_TILES + NQ_TILES),
        in_specs=[
            pl.BlockSpec((tm, D_MODEL), lambda i, j, o: (i, 0)),
            pl.BlockSpec((D_MODEL, PROJ_TN), lambda i, j, o: (0, o[j])),
        ],
        out_specs=[
            pl.BlockSpec((tm, PROJ_TN), lambda i, j, o: (i, jnp.minimum(j, NZ_TILES - 1))),
            pl.BlockSpec((tm, PROJ_TN), lambda i, j, o: (i, jnp.maximum(j - NZ_TILES, 0))),
        ],
    )
    return pl.pallas_call(
        _proj_kernel,
        grid_spec=grid_spec,
        out_shape=[jax.ShapeDtypeStruct((m, Z_COLS), BF16), jax.ShapeDtypeStruct((m, QKV_COLS), F32)],
        compiler_params=_params("arbitrary", "arbitrary"),
        name="proj_in",
    )(order, x16, w16)


ZB_BLK = (2 * D_MODEL) // D_CONV


def _conv_prompt_kernel(zb_ref, zc_ref, zh_ref, cw_ref, p_ref, st_ref, uext):
    tm = zb_ref.shape[0]

    @pl.when(pl.program_id(1) == 0)
    def _():
        uext[0:SUBLANES, :] = jnp.zeros((SUBLANES, D_CONV), F32)

    u = zc_ref[...].astype(F32) * zh_ref[...].astype(F32)
    uext[SUBLANES:SUBLANES + tm, :] = u
    conv = (cw_ref[0:1, :] * uext[SUBLANES - 2:SUBLANES - 2 + tm, :]
            + cw_ref[1:2, :] * uext[SUBLANES - 1:SUBLANES - 1 + tm, :]
            + cw_ref[2:3, :] * u)
    p_ref[...] = (zb_ref[...].astype(F32) * conv).astype(BF16)
    last = uext[tm:tm + SUBLANES, :]
    uext[0:SUBLANES, :] = last
    st_ref[0] = last[SUBLANES - 2:SUBLANES, :]


def _conv_prompt(z, conv_w, bsz, seq):
    tm = min(CONV_TM, seq)
    nsb = seq // tm
    zspec = lambda blk: pl.BlockSpec((tm, D_CONV), lambda b, s: (b * nsb + s, blk))
    return pl.pallas_call(
        _conv_prompt_kernel,
        grid=(bsz, nsb),
        in_specs=[zspec(ZB_BLK), zspec(ZB_BLK + 1), zspec(ZB_BLK + 2),
                  pl.BlockSpec((CONV_W, D_CONV), lambda b, s: (0, 0))],
        out_specs=[pl.BlockSpec((tm, D_CONV), lambda b, s: (b * nsb + s, 0)),
                   pl.BlockSpec((1, CONV_W - 1, D_CONV), lambda b, s: (b, 0, 0))],
        out_shape=[jax.ShapeDtypeStruct((bsz * seq, D_CONV), BF16),
                   jax.ShapeDtypeStruct((bsz, CONV_W - 1, D_CONV), F32)],
        scratch_shapes=[pltpu.VMEM((tm + SUBLANES, D_CONV), F32)],
        compiler_params=_params("arbitrary", "arbitrary"),
        name="conv_prompt",
    )(z, z, z, conv_w)


def _conv_sample_kernel(t_new, zb_ref, zc_ref, zh_ref, cw_ref, s1_ref, s2_ref, p_ref, u_ref, uext):
    m = zb_ref.shape[0]
    u = zc_ref[...].astype(F32) * zh_ref[...].astype(F32)
    u_ref[...] = u
    uext[0:SUBLANES, :] = jnp.zeros((SUBLANES, D_CONV), F32)
    uext[SUBLANES:SUBLANES + m, :] = u
    t = lax.broadcasted_iota(jnp.int32, (m, D_CONV), 0) % t_new
    um1 = jnp.where(t < 1, s1_ref[...], uext[SUBLANES - 1:SUBLANES - 1 + m, :])
    um2 = jnp.where(t < 2, s2_ref[...], uext[SUBLANES - 2:SUBLANES - 2 + m, :])
    conv = cw_ref[0:1, :] * um2 + cw_ref[1:2, :] * um1 + cw_ref[2:3, :] * u
    p_ref[...] = (zb_ref[...].astype(F32) * conv).astype(BF16)


def _conv_sample(z, conv_w, state, t_new):
    m = z.shape[0]
    bd = m // t_new
    s1 = jnp.pad(state[:, 1:2], ((0, 0), (0, t_new - 1), (0, 0))).reshape(m, D_CONV)
    s2 = jnp.pad(state, ((0, 0), (0, t_new - 2), (0, 0))).reshape(m, D_CONV)
    zspec = lambda blk: pl.BlockSpec((m, D_CONV), lambda i: (0, blk))
    full = lambda r: pl.BlockSpec((r, D_CONV), lambda i: (0, 0))
    p, u = pl.pallas_call(
        functools.partial(_conv_sample_kernel, t_new),
        grid=(1,),
        in_specs=[zspec(ZB_BLK), zspec(ZB_BLK + 1), zspec(ZB_BLK + 2), full(CONV_W), full(m), full(m)],
        out_specs=[full(m), full(m)],
        out_shape=[jax.ShapeDtypeStruct((m, D_CONV), BF16), jax.ShapeDtypeStruct((m, D_CONV), F32)],
        scratch_shapes=[pltpu.VMEM((m + SUBLANES, D_CONV), F32)],
        compiler_params=_params("arbitrary"),
        name="conv_sample",
    )(z, z, z, conv_w, s1, s2)
    new_state = u.reshape(bd, t_new, D_CONV)[:, t_new - (CONV_W - 1):]
    return p, new_state


def _merge_groups(o_list, lse_list):
    top = functools.reduce(jnp.maximum, lse_list)
    es = [jnp.exp(l - top) for l in lse_list]
    num = functools.reduce(lambda a, b: a + b, [e * o for e, o in zip(es, o_list)])
    den = functools.reduce(lambda a, b: a + b, es)
    return num / den


def _nt_dot(a, b):
    return lax.dot_general(a, b, (((1,), (1,)), ((), ())), preferred_element_type=F32)


def _attn_prompt_group(q_ref, k_ref, v_ref, o_sc, lse_sc, g, dil, seq):
    shift = dil.bit_length() - 1
    qi = lax.broadcasted_iota(jnp.int32, (SPAN, SPAN), 0)
    kj = lax.broadcasted_iota(jnp.int32, (SPAN, SPAN), 1)
    own_ok = kj <= qi
    prev_ok = kj >= qi

    def rows(ref, start):
        if dil == 1:
            return ref[pl.ds(start, SPAN), :]
        return ref[pl.ds(start, SPAN, stride=dil), :]

    def body(idx, carry):
        r = idx & (dil - 1)
        c = idx >> shift
        base = c * (SPAN * dil) + r
        pbase = jnp.maximum(c - 1, 0) * (SPAN * dil) + r
        q = rows(q_ref, base).astype(BF16)
        s_own = _nt_dot(q, rows(k_ref, base).astype(BF16)) * ATT_SCALE
        s_prev = _nt_dot(q, rows(k_ref, pbase).astype(BF16)) * ATT_SCALE
        s_own = jnp.where(own_ok, s_own, NEG_INF)
        s_prev = jnp.where(jnp.logical_and(prev_ok, c > 0), s_prev, NEG_INF)
        m = jnp.maximum(jnp.max(s_own, axis=-1, keepdims=True), jnp.max(s_prev, axis=-1, keepdims=True))
        p_own = jnp.exp(s_own - m)
        p_prev = jnp.exp(s_prev - m)
        den = jnp.sum(p_own, axis=-1, keepdims=True) + jnp.sum(p_prev, axis=-1, keepdims=True)
        acc = (jnp.dot(p_own.astype(BF16), rows(v_ref, base).astype(BF16), preferred_element_type=F32)
               + jnp.dot(p_prev.astype(BF16), rows(v_ref, pbase).astype(BF16), preferred_element_type=F32))
        o = acc / den
        lse = jnp.broadcast_to(m + jnp.log(den), (SPAN, HEAD_DIM))
        if dil == 1:
            o_sc[g, pl.ds(base, SPAN), :] = o
            lse_sc[g, pl.ds(base, SPAN), :] = lse
        else:
            o_sc[g, pl.ds(base, SPAN, stride=dil), :] = o
            lse_sc[g, pl.ds(base, SPAN, stride=dil), :] = lse
        return carry

    lax.fori_loop(0, seq // SPAN, body, 0)


def _attn_prompt_kernel(q_ref, k_ref, v_ref, o_ref, o_sc, lse_sc):
    seq = q_ref.shape[0]
    g_id = pl.program_id(2)
    for g, (_, dil) in enumerate(DILATED_GROUPS):
        @pl.when(g_id == g)
        def _(g=g, dil=dil):
            _attn_prompt_group(q_ref, k_ref, v_ref, o_sc, lse_sc, g, dil, seq)

    @pl.when(g_id == N_DIL - 1)
    def _():
        chunk = 256

        def body(i, carry):
            sl = pl.ds(pl.multiple_of(i * chunk, chunk), chunk)
            o = _merge_groups([o_sc[g, sl, :] for g in range(N_DIL)], [lse_sc[g, sl, :] for g in range(N_DIL)])
            o_ref[sl, :] = o.astype(BF16)
            return carry

        lax.fori_loop(0, seq // chunk, body, 0)


def _attn_prompt(qkv, bsz, seq):
    qkv3 = qkv.reshape(bsz, seq, QKV_COLS)
    per_sel = D_ATT // HEAD_DIM
    spec = lambda sel: pl.BlockSpec((None, seq, HEAD_DIM),
                                    lambda b, j, g: (b, 0, sel * per_sel + g * HEADS_PER_GROUP + j))
    out = pl.pallas_call(
        _attn_prompt_kernel,
        grid=(bsz, HEADS_PER_GROUP, N_DIL),
        in_specs=[spec(0), spec(1), spec(2)],
        out_specs=pl.BlockSpec((None, seq, HEAD_DIM), lambda b, j, g: (b, 0, j)),
        out_shape=jax.ShapeDtypeStruct((bsz, seq, D_GRP), BF16),
        scratch_shapes=[pltpu.VMEM((N_DIL, seq, HEAD_DIM), F32), pltpu.VMEM((N_DIL, seq, HEAD_DIM), F32)],
        compiler_params=_params("arbitrary", "arbitrary", "arbitrary"),
        name="attn_prompt",
    )(qkv3, qkv3, qkv3)
    return out.reshape(bsz * seq, D_GRP)


KV_ROWS = 2 * HEADS_PER_GROUP


def _attn_sample_kernel(dil, q_ref, cache_ref, new_ref, buf_ref, o_ref, lse_ref):
    t_new = q_ref.shape[0]
    n_buf = cache_ref.shape[0] // KV_ROWS
    tq = lax.broadcasted_iota(jnp.int32, (t_new, n_buf), 0)
    pos = lax.broadcasted_iota(jnp.int32, (t_new, n_buf), 1)
    back = n_buf + tq - pos
    cache_ok = jnp.logical_and(back <= dil * SPAN, (back & (dil - 1)) == 0)
    tq2 = lax.broadcasted_iota(jnp.int32, (t_new, t_new), 0)
    tk2 = lax.broadcasted_iota(jnp.int32, (t_new, t_new), 1)
    new_ok = jnp.logical_and(tk2 <= tq2, ((tq2 - tk2) & (dil - 1)) == 0)
    for h in range(HEADS_PER_GROUP):
        cols = slice(h * HEAD_DIM, (h + 1) * HEAD_DIM)
        q = q_ref[:, cols].astype(BF16)
        k_c = cache_ref[pl.ds(h, n_buf, stride=KV_ROWS), :].astype(BF16)
        v_c = cache_ref[pl.ds(HEADS_PER_GROUP + h, n_buf, stride=KV_ROWS), :].astype(BF16)
        k_n = new_ref[pl.ds(h, t_new, stride=KV_ROWS), :].astype(BF16)
        v_n = new_ref[pl.ds(HEADS_PER_GROUP + h, t_new, stride=KV_ROWS), :].astype(BF16)
        s_c = jnp.where(cache_ok, _nt_dot(q, k_c) * ATT_SCALE, NEG_INF)
        s_n = jnp.where(new_ok, _nt_dot(q, k_n) * ATT_SCALE, NEG_INF)
        m = jnp.maximum(jnp.max(s_c, axis=-1, keepdims=True), jnp.max(s_n, axis=-1, keepdims=True))
        p_c = jnp.exp(s_c - m)
        p_n = jnp.exp(s_n - m)
        den = jnp.sum(p_c, axis=-1, keepdims=True) + jnp.sum(p_n, axis=-1, keepdims=True)
        acc = (jnp.dot(p_c.astype(BF16), v_c, preferred_element_type=F32)
               + jnp.dot(p_n.astype(BF16), v_n, preferred_element_type=F32))
        o_ref[:, cols] = acc / den
        lse_ref[:, cols] = jnp.broadcast_to(m + jnp.log(den), (t_new, HEAD_DIM))
    keep = (n_buf - t_new) * KV_ROWS
    buf_ref[0:keep, :] = cache_ref[t_new * KV_ROWS:, :]
    buf_ref[keep:, :] = new_ref[...]


def _attn_sample(q3, cache, new_kv, g):
    window, dil = DILATED_GROUPS[g]
    bd, t_new, _ = q3.shape
    rows = cache.shape[1]
    assert rows == window * KV_ROWS, "the cache must hold a full window"
    row_spec = lambda r: pl.BlockSpec((None, r, HEAD_DIM), lambda b: (b, 0, 0))
    tok_spec = pl.BlockSpec((None, t_new, D_GRP), lambda b: (b, 0, 0))
    return pl.pallas_call(
        functools.partial(_attn_sample_kernel, dil),
        grid=(bd,),
        in_specs=[pl.BlockSpec((None, t_new, D_GRP), lambda b: (b, 0, g)), row_spec(rows),
                  row_spec(t_new * KV_ROWS)],
        out_specs=[row_spec(rows), tok_spec, tok_spec],
        out_shape=[jax.ShapeDtypeStruct((bd, rows, HEAD_DIM), F32),
                   jax.ShapeDtypeStruct((bd, t_new, D_GRP), F32),
                   jax.ShapeDtypeStruct((bd, t_new, D_GRP), F32)],
        compiler_params=_params("arbitrary"),
        name=f"attn_sample_w{window}",
    )(q3, cache, new_kv)


def _merge_sample_kernel(o0, o1, o2, l0, l1, l2, out_ref):
    out_ref[...] = _merge_groups([o0[...], o1[...], o2[...]], [l0[...], l1[...], l2[...]]).astype(BF16)


def _merge_sample(os_, ls_):
    m = os_[0].shape[0]
    spec = pl.BlockSpec((m, D_GRP), lambda i: (0, 0))
    return pl.pallas_call(
        _merge_sample_kernel,
        grid=(1,),
        in_specs=[spec] * 6,
        out_specs=spec,
        out_shape=jax.ShapeDtypeStruct((m, D_GRP), BF16),
        compiler_params=_params("arbitrary"),
        name="merge_sample",
    )(*os_, *ls_)


def _layer_norm(v, g, b):
    mu = jnp.mean(v, axis=-1, keepdims=True)
    d = v - mu
    var = jnp.mean(d * d, axis=-1, keepdims=True)
    return d * lax.rsqrt(var + LN_EPS) * g + b


def _mix_kernel(p_ref, o_ref, gc_ref, ga_ref, x_ref, wc_ref, wa_ref, wo_ref, g_ref, b_ref, *rest):
    h_ref = rest[-1]
    y_conv = jnp.dot(p_ref[...], wc_ref[...], preferred_element_type=F32)
    y_att = jnp.dot(o_ref[...], wa_ref[...], preferred_element_type=F32)
    merged = (jax.nn.sigmoid(gc_ref[...].astype(F32)) * y_conv
              + jax.nn.sigmoid(ga_ref[...].astype(F32)) * y_att)
    mix = jnp.dot(merged.astype(BF16), wo_ref[...], preferred_element_type=F32)
    h_ref[...] = _layer_norm(ALPHA * x_ref[...] + mix, g_ref[...], b_ref[...])


def _mix(p, o, z, x, wc16, wa16, wo16, ln_g, ln_b, tm, total, h_all=None, row_off=0):
    m = p.shape[0]
    assert row_off % tm == 0
    blk_off = row_off // tm
    row = lambda w: pl.BlockSpec((tm, w), lambda i: (i, 0))
    const = lambda a: pl.BlockSpec(a.shape, lambda i: (0, 0))
    in_specs = [row(D_CONV), row(D_GRP),
                pl.BlockSpec((tm, D_MODEL), lambda i: (i, 0)), pl.BlockSpec((tm, D_MODEL), lambda i: (i, 1)),
                row(D_MODEL), const(wc16), const(wa16), const(wo16), const(ln_g), const(ln_b)]
    args = [p, o, z, z, x, wc16, wa16, wo16, ln_g, ln_b]
    aliases = {}
    if h_all is not None:
        in_specs.append(pl.BlockSpec(memory_space=pl.ANY))
        args.append(h_all)
        aliases = {len(args) - 1: 0}
    return pl.pallas_call(
        _mix_kernel,
        grid=(m // tm,),
        in_specs=in_specs,
        out_specs=pl.BlockSpec((tm, D_MODEL), lambda i: (i + blk_off, 0)),
        out_shape=jax.ShapeDtypeStruct((total, D_MODEL), F32),
        input_output_aliases=aliases,
        compiler_params=_params("arbitrary"),
        name="mix_ln1",
    )(*args)


def _router_kernel(h_ref, w_ref, b_ref, idx_ref, gate_ref, cnt_ref, carry):
    tile = h_ref.shape[0]

    @pl.when(pl.program_id(0) == 0)
    def _():
        carry[...] = jnp.zeros(carry.shape, F32)

    logits = jnp.dot(h_ref[...], w_ref[...], preferred_element_type=F32,
                     precision=lax.Precision.HIGHEST) + b_ref[...]
    lane = lax.broadcasted_iota(jnp.int32, (tile, LANES), 1)
    lane_f = lane.astype(F32)
    first_max = lambda v, mx: jnp.min(jnp.where(v == mx, lane_f, float(LANES)), axis=-1, keepdims=True)

    is_grp = lane < N_EXPERT_GROUPS
    lg = jnp.where(is_grp, logits, -jnp.inf)
    mg = jnp.max(lg, axis=-1, keepdims=True)
    gsel = first_max(lg, mg).astype(jnp.int32)
    p_group = 1.0 / jnp.sum(jnp.where(is_grp, jnp.exp(logits - mg), 0.0), axis=-1, keepdims=True)

    e_lane = lane - N_EXPERT_GROUPS
    in_grp = jnp.logical_and(e_lane >= 0, (e_lane >> 3) == gsel)
    le = jnp.where(in_grp, logits, -jnp.inf)
    v1 = jnp.max(le, axis=-1, keepdims=True)
    i1 = first_max(le, v1).astype(jnp.int32)
    le2 = jnp.where(lane == i1, -jnp.inf, le)
    v2 = jnp.max(le2, axis=-1, keepdims=True)
    i2 = first_max(le2, v2).astype(jnp.int32)
    ex = jnp.exp(v2 - v1)
    g1 = p_group / (1.0 + ex)
    g2 = p_group * ex / (1.0 + ex)
    e1 = i1 - N_EXPERT_GROUPS
    e2 = i2 - N_EXPERT_GROUPS

    hot1 = lane == e1
    hot2 = lane == e2
    hot = jnp.where(jnp.logical_or(hot1, hot2), 1.0, 0.0)
    rr = lax.broadcasted_iota(jnp.int32, (tile, tile), 0)
    cc = lax.broadcasted_iota(jnp.int32, (tile, tile), 1)
    tri = jnp.where(cc < rr, 1.0, 0.0).astype(BF16)
    before = jnp.dot(tri, hot.astype(BF16), preferred_element_type=F32) + carry[0:1, :]
    r1 = jnp.sum(jnp.where(hot1, before, 0.0), axis=-1, keepdims=True).astype(jnp.int32)
    r2 = jnp.sum(jnp.where(hot2, before, 0.0), axis=-1, keepdims=True).astype(jnp.int32)
    carry[...] = carry[...] + jnp.sum(hot, axis=0, keepdims=True)

    idx_ref[...] = jnp.where(lane == 0, e1, jnp.where(lane == 1, e2, jnp.where(lane == 2, r1, r2)))
    gate_ref[...] = jnp.where(lane == 0, g1, g2)
    cnt_ref[...] = carry[...]


def _router(h_all, w_r, b_r):
    m = h_all.shape[0]
    tile = ROUTER_TILE if m % ROUTER_TILE == 0 else ROW_TILE
    row = pl.BlockSpec((tile, LANES), lambda i: (i, 0))
    return pl.pallas_call(
        _router_kernel,
        grid=(m // tile,),
        in_specs=[pl.BlockSpec((tile, D_MODEL), lambda i: (i, 0)),
                  pl.BlockSpec((D_MODEL, LANES), lambda i: (0, 0)),
                  pl.BlockSpec((1, LANES), lambda i: (0, 0))],
        out_specs=[row, row, pl.BlockSpec((SUBLANES, LANES), lambda i: (0, 0))],
        out_shape=[jax.ShapeDtypeStruct((m, LANES), jnp.int32), jax.ShapeDtypeStruct((m, LANES), F32),
                   jax.ShapeDtypeStruct((SUBLANES, LANES), F32)],
        scratch_shapes=[pltpu.VMEM((SUBLANES, LANES), F32)],
        compiler_params=_params("arbitrary"),
        name="router",
    )(h_all, w_r, b_r)


def _dispatch_kernel(dest_ref, h_hbm, x_hbm, sem):
    base = pl.program_id(0) * ROW_TILE

    def copy(r, k):
        tok = base + r
        return pltpu.make_async_copy(h_hbm.at[pl.ds(tok, 1), :],
                                     x_hbm.at[pl.ds(dest_ref[tok * TOP_K + k], 1), :], sem)

    def start(r, c):
        for k in range(TOP_K):
            copy(r, k).start()
        return c

    def wait(r, c):
        for k in range(TOP_K):
            copy(r, k).wait()
        return c

    lax.fori_loop(0, ROW_TILE, start, 0)
    lax.fori_loop(0, ROW_TILE, wait, 0)


def _dispatch(h_all, dest_flat, n_rows):
    m = h_all.shape[0]
    grid_spec = pltpu.PrefetchScalarGridSpec(
        num_scalar_prefetch=1,
        grid=(m // ROW_TILE,),
        in_specs=[pl.BlockSpec(memory_space=pl.ANY)],
        out_specs=pl.BlockSpec(memory_space=pl.ANY),
        scratch_shapes=[pltpu.SemaphoreType.DMA(())],
    )
    return pl.pallas_call(
        _dispatch_kernel,
        grid_spec=grid_spec,
        out_shape=jax.ShapeDtypeStruct((n_rows, D_MODEL), F32),
        compiler_params=pltpu.CompilerParams(dimension_semantics=("arbitrary",), has_side_effects=True),
        name="moe_dispatch",
    )(dest_flat, h_all)


def _expert_kernel(blk_e, blk_first, blk_valid, n_used, x_ref, wg_ref, wu_ref, wd_ref, y_ref, wg16, wu16, wd16):
    del blk_e
    i = pl.program_id(0)

    @pl.when(i < n_used[0])
    def _():
        @pl.when(blk_first[i] == 1)
        def _():
            wg16[...] = wg_ref[...].astype(BF16)
            wu16[...] = wu_ref[...].astype(BF16)
            wd16[...] = wd_ref[...].astype(BF16)

        row = lax.broadcasted_iota(jnp.int32, x_ref.shape, 0)
        x = jnp.where(row < blk_valid[i], x_ref[...], 0.0).astype(BF16)
        gate = jnp.dot(x, wg16[...], preferred_element_type=F32)
        up = jnp.dot(x, wu16[...], preferred_element_type=F32)
        hid = (jax.nn.silu(gate) * up).astype(BF16)
        y_ref[...] = jnp.dot(hid, wd16[...], preferred_element_type=F32)


def _experts(x_rows, blk_e, blk_first, blk_valid, n_used, w_gate, w_up, w_down):
    n_rows = x_rows.shape[0]
    n_blocks = n_rows // MOE_BLK
    row_map = lambda i, be, bf, bv, nu: (jnp.minimum(i, nu[0] - 1), 0)
    w_map = lambda i, be, bf, bv, nu: (be[i], 0, 0)
    grid_spec = pltpu.PrefetchScalarGridSpec(
        num_scalar_prefetch=4,
        grid=(n_blocks,),
        in_specs=[pl.BlockSpec((MOE_BLK, D_MODEL), row_map),
                  pl.BlockSpec((None, D_MODEL, D_EXPERT), w_map),
                  pl.BlockSpec((None, D_MODEL, D_EXPERT), w_map),
                  pl.BlockSpec((None, D_EXPERT, D_MODEL), w_map)],
        out_specs=pl.BlockSpec((MOE_BLK, D_MODEL), row_map),
        scratch_shapes=[pltpu.VMEM((D_MODEL, D_EXPERT), BF16), pltpu.VMEM((D_MODEL, D_EXPERT), BF16),
                        pltpu.VMEM((D_EXPERT, D_MODEL), BF16)],
    )
    return pl.pallas_call(
        _expert_kernel,
        grid_spec=grid_spec,
        out_shape=jax.ShapeDtypeStruct((n_rows, D_MODEL), F32),
        compiler_params=_params("arbitrary"),
        name="moe_experts",
    )(blk_e, blk_first, blk_valid, n_used, x_rows, w_gate, w_up, w_down)


def _combine_kernel(tok_off, dest_ref, y_hbm, h_ref, gate_ref, g_ref, b_ref, out_ref, ybuf, sem):
    base = tok_off + pl.program_id(0) * ROW_TILE

    def copy(r, k):
        return pltpu.make_async_copy(y_hbm.at[pl.ds(dest_ref[(base + r) * TOP_K + k], 1), :],
                                     ybuf.at[k, pl.ds(r, 1), :], sem)

    def start(r, c):
        for k in range(TOP_K):
            copy(r, k).start()
        return c

    def wait(r, c):
        for k in range(TOP_K):
            copy(r, k).wait()
        return c

    lax.fori_loop(0, ROW_TILE, start, 0)
    lax.fori_loop(0, ROW_TILE, wait, 0)
    f = gate_ref[:, 0:1] * ybuf[0] + gate_ref[:, 1:2] * ybuf[1]
    out_ref[...] = _layer_norm(ALPHA * h_ref[...] + f, g_ref[...], b_ref[...])


def _combine(y_rows, dest_flat, h_all, gates, ln_g, ln_b, tok_off, n_tok):
    blk_off = tok_off // ROW_TILE
    grid_spec = pltpu.PrefetchScalarGridSpec(
        num_scalar_prefetch=1,
        grid=(n_tok // ROW_TILE,),
        in_specs=[pl.BlockSpec(memory_space=pl.ANY),
                  pl.BlockSpec((ROW_TILE, D_MODEL), lambda i, d: (i + blk_off, 0)),
                  pl.BlockSpec((ROW_TILE, LANES), lambda i, d: (i + blk_off, 0)),
                  pl.BlockSpec((1, D_MODEL), lambda i, d: (0, 0)),
                  pl.BlockSpec((1, D_MODEL), lambda i, d: (0, 0))],
        out_specs=pl.BlockSpec((ROW_TILE, D_MODEL), lambda i, d: (i, 0)),
        scratch_shapes=[pltpu.VMEM((TOP_K, ROW_TILE, D_MODEL), F32), pltpu.SemaphoreType.DMA(())],
    )
    return pl.pallas_call(
        functools.partial(_combine_kernel, tok_off),
        grid_spec=grid_spec,
        out_shape=jax.ShapeDtypeStruct((n_tok, D_MODEL), F32),
        compiler_params=_params("arbitrary"),
        name="combine_ln2",
    )(dest_flat, y_rows, h_all, gates, ln_g, ln_b)


def _moe_plan(idx, counts_f):
    m = idx.shape[0]
    n_asg = m * TOP_K
    experts = idx[:, 0:TOP_K]
    ranks = idx[:, TOP_K:2 * TOP_K]
    counts = counts_f[0, :N_EXPERTS].astype(jnp.int32)
    padded = (counts + MOE_BLK - 1) // MOE_BLK * MOE_BLK
    pad_end = jnp.cumsum(padded)
    pad_start = pad_end - padded
    dest = (pad_start[experts] + ranks).reshape(n_asg).astype(jnp.int32)
    n_blocks = -(-(n_asg + N_EXPERTS * (MOE_BLK - 1)) // MOE_BLK)
    n_used = jnp.maximum(pad_end[-1] // MOE_BLK, 1).astype(jnp.int32)
    blk = jnp.minimum(jnp.arange(n_blocks, dtype=jnp.int32), n_used - 1)
    blk_e = jnp.minimum(jnp.searchsorted(pad_end, blk * MOE_BLK, side='right'), N_EXPERTS - 1).astype(jnp.int32)
    blk_first = (blk * MOE_BLK == pad_start[blk_e]).astype(jnp.int32)
    blk_valid = jnp.clip(counts[blk_e] - (blk * MOE_BLK - pad_start[blk_e]), 0, MOE_BLK).astype(jnp.int32)
    return dest, n_blocks * MOE_BLK, blk_e, blk_first, blk_valid, n_used.reshape(1)


def kernel(x_prompt, x_sample, cache_kv_w128, cache_kv_w512, cache_kv_w2048, state_conv, w_in, conv_w, w_conv_out, w_att_out, w_o, ln1_g, ln1_b, w_router_group, b_router_group, w_router_expert, b_router_expert, w_expert_gate, w_expert_up, w_expert_down, ln2_g, ln2_b):
    bsz, seq, _ = x_prompt.shape
    bd, t_new, _ = x_sample.shape
    m_p = bsz * seq
    m_s = bd * t_new
    m_all = m_p + m_s
    caches = (cache_kv_w128, cache_kv_w512, cache_kv_w2048)
    lyr = 0

    w_in16 = w_in[lyr].astype(BF16)
    wc16 = w_conv_out[lyr].astype(BF16)
    wa16 = w_att_out[lyr].astype(BF16)
    wo16 = w_o[lyr].astype(BF16)
    cw = conv_w[lyr]
    g1, b1 = ln1_g[lyr][None, :], ln1_b[lyr][None, :]
    g2, b2 = ln2_g[lyr][None, :], ln2_b[lyr][None, :]
    w_re = jnp.transpose(w_router_expert[lyr], (1, 0, 2)).reshape(D_MODEL, N_EXPERTS)
    w_r = jnp.pad(jnp.concatenate([w_router_group[lyr], w_re], axis=1),
                  ((0, 0), (0, LANES - N_EXPERT_GROUPS - N_EXPERTS)))
    b_r = jnp.pad(jnp.concatenate([b_router_group[lyr], b_router_expert[lyr].reshape(N_EXPERTS)]),
                  (0, LANES - N_EXPERT_GROUPS - N_EXPERTS))[None, :]

    xp = x_prompt.reshape(m_p, D_MODEL)
    xs = x_sample.reshape(m_s, D_MODEL)

    z_p, qkv_p = _proj_in(xp.astype(BF16), w_in16, min(PROJ_TM, m_p))
    p_p, conv_p = _conv_prompt(z_p, cw, bsz, seq)
    o_p = _attn_prompt(qkv_p, bsz, seq)
    h_buf = _mix(p_p, o_p, z_p, xp, wc16, wa16, wo16, g1, b1, min(MIX_TM, m_p), m_all)

    z_s, qkv_s = _proj_in(xs.astype(BF16), w_in16, m_s)
    p_s, conv_s = _conv_sample(z_s, cw, state_conv[lyr], t_new)
    q3 = qkv_s.reshape(bd, t_new, QKV_COLS)
    kv5 = qkv_s.reshape(bd, t_new, 3, N_DIL, HEADS_PER_GROUP, HEAD_DIM)
    os_, ls_, bufs_s = [], [], []
    for g, (window, _) in enumerate(DILATED_GROUPS):
        cache = caches[g][lyr].reshape(bd, -1, HEAD_DIM)
        new_kv = kv5[:, :, 1:3, g].reshape(bd, t_new * KV_ROWS, HEAD_DIM)
        buf, o_g, lse_g = _attn_sample(q3, cache, new_kv, g)
        bufs_s.append(buf.reshape(1, bd, window, 2, HEADS_PER_GROUP, HEAD_DIM))
        os_.append(o_g.reshape(m_s, D_GRP))
        ls_.append(lse_g.reshape(m_s, D_GRP))
    o_s = _merge_sample(os_, ls_)
    h_buf = _mix(p_s, o_s, z_s, xs, wc16, wa16, wo16, g1, b1, m_s, m_all, h_all=h_buf, row_off=m_p)

    idx, gates, counts = _router(h_buf, w_r, b_r)
    dest, n_rows, blk_e, blk_first, blk_valid, n_used = _moe_plan(idx, counts)
    x_rows = _dispatch(h_buf, dest, n_rows)
    y_rows = _experts(x_rows, blk_e, blk_first, blk_valid, n_used,
                      w_expert_gate[lyr], w_expert_up[lyr], w_expert_down[lyr])
    y_p = _combine(y_rows, dest, h_buf, gates, g2, b2, 0, m_p)
    y_s = _combine(y_rows, dest, h_buf, gates, g2, b2, m_p, m_s)

    kv_p = qkv_p.reshape(bsz, seq, 3, N_DIL, HEADS_PER_GROUP, HEAD_DIM)
    bufs_p = [kv_p[:, seq - min(w, seq):, 1:3, g][None] for g, (w, _) in enumerate(DILATED_GROUPS)]
    return (y_p.reshape(bsz, seq, D_MODEL), y_s.reshape(bd, t_new, D_MODEL),
            bufs_p[0], bufs_p[1], bufs_p[2], conv_p[None],
            bufs_s[0], bufs_s[1], bufs_s[2], conv_s[None])
```

```python
import functools

import jax
import jax.numpy as jnp
from jax import lax
from jax.experimental import pallas as pl
from jax.experimental.pallas import tpu as pltpu

F32 = jnp.float32
BF16 = jnp.bfloat16

D_MODEL = 2048
D_CONV = D_MODEL // 2
CONV_W = 3
DILATED_GROUPS = ((128, 1), (512, 4), (2048, 16))
N_DIL = len(DILATED_GROUPS)
HEADS_PER_GROUP = 4
HEAD_DIM = 128
SPAN = 128
D_GRP = HEADS_PER_GROUP * HEAD_DIM
D_ATT = N_DIL * D_GRP
N_EXPERT_GROUPS = 8
EXPERTS_PER_GROUP = 8
N_EXPERTS = N_EXPERT_GROUPS * EXPERTS_PER_GROUP
TOP_K = 2
D_EXPERT = D_MODEL // 4
DEPTH = 1
LN_EPS = 1e-5
ALPHA = (2 * DEPTH) ** 0.25
NEG_INF = -1e30
ATT_SCALE = HEAD_DIM ** -0.5

OFF_B = 0
OFF_C = D_CONV
OFF_H = 2 * D_CONV
OFF_Q = 3 * D_CONV
OFF_K = OFF_Q + D_ATT
OFF_V = OFF_K + D_ATT
OFF_GC = OFF_V + D_ATT
OFF_GA = OFF_GC + D_MODEL
N_IN = OFF_GA + D_MODEL

LANES = 128
SUBLANES = 8
VMEM_LIMIT = 56 * 1024 * 1024

PROJ_TN = 512
PROJ_TM = 1024
Z_COLS = 2 * D_MODEL + 3 * D_CONV
QKV_COLS = 3 * D_ATT
NZ_TILES = Z_COLS // PROJ_TN
NQ_TILES = QKV_COLS // PROJ_TN
CONV_TM = 512
MIX_TM = 256
ROW_TILE = 128
ROUTER_TILE = 640
MOE_BLK = 256


def _proj_tile_order():
    t = lambda off, width: list(range(off // PROJ_TN, (off + width) // PROJ_TN))
    z_tiles = t(OFF_GC, D_MODEL) + t(OFF_GA, D_MODEL) + t(OFF_B, D_CONV) + t(OFF_C, D_CONV) + t(OFF_H, D_CONV)
    q_tiles = t(OFF_Q, D_ATT) + t(OFF_K, D_ATT) + t(OFF_V, D_ATT)
    return z_tiles + q_tiles


def _params(*sem):
    return pltpu.CompilerParams(dimension_semantics=sem, vmem_limit_bytes=VMEM_LIMIT)


def _proj_kernel(order_ref, x_ref, w_ref, z_ref, qkv_ref):
    del order_ref
    j = pl.program_id(1)

    @pl.when(j < NZ_TILES)
    def _():
        z_ref[...] = jnp.dot(x_ref[...], w_ref[...], preferred_element_type=F32).astype(BF16)

    @pl.when(j >= NZ_TILES)
    def _():
        qkv_ref[...] = jnp.dot(x_ref[...], w_ref[...], preferred_element_type=F32)


def _proj_in(x16, w16, tm):
    m = x16.shape[0]
    order = jnp.asarray(_proj_tile_order(), jnp.int32)
    grid_spec = pltpu.PrefetchScalarGridSpec(
        num_scalar_prefetch=1,
        grid=(m // tm, NZ_TILES + NQ_TILES),
        in_specs=[
            pl.BlockSpec((tm, D_MODEL), lambda i, j, o: (i, 0)),
            pl.BlockSpec((D_MODEL, PROJ_TN), lambda i, j, o: (0, o[j])),
        ],
        out_specs=[
            pl.BlockSpec((tm, PROJ_TN), lambda i, j, o: (i, jnp.minimum(j, NZ_TILES - 1))),
            pl.BlockSpec((tm, PROJ_TN), lambda i, j, o: (i, jnp.maximum(j - NZ_TILES, 0))),
        ],
    )
    return pl.pallas_call(
        _proj_kernel,
        grid_spec=grid_spec,
        out_shape=[jax.ShapeDtypeStruct((m, Z_COLS), BF16), jax.ShapeDtypeStruct((m, QKV_COLS), F32)],
        compiler_params=_params("arbitrary", "arbitrary"),
        name="proj_in",
    )(order, x16, w16)


ZB_BLK = (2 * D_MODEL) // D_CONV


def _conv_prompt_kernel(zb_ref, zc_ref, zh_ref, cw_ref, p_ref, st_ref, uext):
    tm = zb_ref.shape[0]

    @pl.when(pl.program_id(1) == 0)
    def _():
        uext[0:SUBLANES, :] = jnp.zeros((SUBLANES, D_CONV), F32)

    u = zc_ref[...].astype(F32) * zh_ref[...].astype(F32)
    uext[SUBLANES:SUBLANES + tm, :] = u
    conv = (cw_ref[0:1, :] * uext[SUBLANES - 2:SUBLANES - 2 + tm, :]
            + cw_ref[1:2, :] * uext[SUBLANES - 1:SUBLANES - 1 + tm, :]
            + cw_ref[2:3, :] * u)
    p_ref[...] = (zb_ref[...].astype(F32) * conv).astype(BF16)
    last = uext[tm:tm + SUBLANES, :]
    uext[0:SUBLANES, :] = last
    st_ref[0] = last[SUBLANES - 2:SUBLANES, :]


def _conv_prompt(z, conv_w, bsz, seq):
    tm = min(CONV_TM, seq)
    nsb = seq // tm
    zspec = lambda blk: pl.BlockSpec((tm, D_CONV), lambda b, s: (b * nsb + s, blk))
    return pl.pallas_call(
        _conv_prompt_kernel,
        grid=(bsz, nsb),
        in_specs=[zspec(ZB_BLK), zspec(ZB_BLK + 1), zspec(ZB_BLK + 2),
                  pl.BlockSpec((CONV_W, D_CONV), lambda b, s: (0, 0))],
        out_specs=[pl.BlockSpec((tm, D_CONV), lambda b, s: (b * nsb + s, 0)),
                   pl.BlockSpec((1, CONV_W - 1, D_CONV), lambda b, s: (b, 0, 0))],
        out_shape=[jax.ShapeDtypeStruct((bsz * seq, D_CONV), BF16),
                   jax.ShapeDtypeStruct((bsz, CONV_W - 1, D_CONV), F32)],
        scratch_shapes=[pltpu.VMEM((tm + SUBLANES, D_CONV), F32)],
        compiler_params=_params("arbitrary", "arbitrary"),
        name="conv_prompt",
    )(z, z, z, conv_w)


def _conv_sample_kernel(t_new, zb_ref, zc_ref, zh_ref, cw_ref, s1_ref, s2_ref, p_ref, u_ref, uext):
    m = zb_ref.shape[0]
    u = zc_ref[...].astype(F32) * zh_ref[...].astype(F32)
    u_ref[...] = u
    uext[0:SUBLANES, :] = jnp.zeros((SUBLANES, D_CONV), F32)
    uext[SUBLANES:SUBLANES + m, :] = u
    t = lax.broadcasted_iota(jnp.int32, (m, D_CONV), 0) % t_new
    um1 = jnp.where(t < 1, s1_ref[...], uext[SUBLANES - 1:SUBLANES - 1 + m, :])
    um2 = jnp.where(t < 2, s2_ref[...], uext[SUBLANES - 2:SUBLANES - 2 + m, :])
    conv = cw_ref[0:1, :] * um2 + cw_ref[1:2, :] * um1 + cw_ref[2:3, :] * u
    p_ref[...] = (zb_ref[...].astype(F32) * conv).astype(BF16)


def _conv_sample(z, conv_w, state, t_new):
    m = z.shape[0]
    bd = m // t_new
    s1 = jnp.pad(state[:, 1:2], ((0, 0), (0, t_new - 1), (0, 0))).reshape(m, D_CONV)
    s2 = jnp.pad(state, ((0, 0), (0, t_new - 2), (0, 0))).reshape(m, D_CONV)
    zspec = lambda blk: pl.BlockSpec((m, D_CONV), lambda i: (0, blk))
    full = lambda r: pl.BlockSpec((r, D_CONV), lambda i: (0, 0))
    p, u = pl.pallas_call(
        functools.partial(_conv_sample_kernel, t_new),
        grid=(1,),
        in_specs=[zspec(ZB_BLK), zspec(ZB_BLK + 1), zspec(ZB_BLK + 2), full(CONV_W), full(m), full(m)],
        out_specs=[full(m), full(m)],
        out_shape=[jax.ShapeDtypeStruct((m, D_CONV), BF16), jax.ShapeDtypeStruct((m, D_CONV), F32)],
        scratch_shapes=[pltpu.VMEM((m + SUBLANES, D_CONV), F32)],
        compiler_params=_params("arbitrary"),
        name="conv_sample",
    )(z, z, z, conv_w, s1, s2)
    new_state = u.reshape(bd, t_new, D_CONV)[:, t_new - (CONV_W - 1):]
    return p, new_state


def _merge_groups(o_list, lse_list):
    top = functools.reduce(jnp.maximum, lse_list)
    es = [jnp.exp(l - top) for l in lse_list]
    num = functools.reduce(lambda a, b: a + b, [e * o for e, o in zip(es, o_list)])
    den = functools.reduce(lambda a, b: a + b, es)
    return num / den


def _nt_dot(a, b):
    return lax.dot_general(a, b, (((1,), (1,)), ((), ())), preferred_element_type=F32)


def _attn_prompt_group(q_ref, k_ref, v_ref, o_sc, lse_sc, g, dil, seq):
    shift = dil.bit_length() - 1
    qi = lax.broadcasted_iota(jnp.int32, (SPAN, SPAN), 0)
    kj = lax.broadcasted_iota(jnp.int32, (SPAN, SPAN), 1)
    own_ok = kj <= qi
    prev_ok = kj >= qi

    def rows(ref, start):
        if dil == 1:
            return ref[pl.ds(start, SPAN), :]
        return ref[pl.ds(start, SPAN, stride=dil), :]

    def body(idx, carry):
        r = idx & (dil - 1)
        c = idx >> shift
        base = c * (SPAN * dil) + r
        pbase = jnp.maximum(c - 1, 0) * (SPAN * dil) + r
        q = rows(q_ref, base).astype(BF16)
        s_own = _nt_dot(q, rows(k_ref, base).astype(BF16)) * ATT_SCALE
        s_prev = _nt_dot(q, rows(k_ref, pbase).astype(BF16)) * ATT_SCALE
        s_own = jnp.where(own_ok, s_own, NEG_INF)
        s_prev = jnp.where(jnp.logical_and(prev_ok, c > 0), s_prev, NEG_INF)
        m = jnp.maximum(jnp.max(s_own, axis=-1, keepdims=True), jnp.max(s_prev, axis=-1, keepdims=True))
        p_own = jnp.exp(s_own - m)
        p_prev = jnp.exp(s_prev - m)
        den = jnp.sum(p_own, axis=-1, keepdims=True) + jnp.sum(p_prev, axis=-1, keepdims=True)
        acc = (jnp.dot(p_own.astype(BF16), rows(v_ref, base).astype(BF16), preferred_element_type=F32)
               + jnp.dot(p_prev.astype(BF16), rows(v_ref, pbase).astype(BF16), preferred_element_type=F32))
        o = acc / den
        lse = jnp.broadcast_to(m + jnp.log(den), (SPAN, HEAD_DIM))
        if dil == 1:
            o_sc[g, pl.ds(base, SPAN), :] = o
            lse_sc[g, pl.ds(base, SPAN), :] = lse
        else:
            o_sc[g, pl.ds(base, SPAN, stride=dil), :] = o
            lse_sc[g, pl.ds(base, SPAN, stride=dil), :] = lse
        return carry

    lax.fori_loop(0, seq // SPAN, body, 0)


def _attn_prompt_kernel(q_ref, k_ref, v_ref, o_ref, o_sc, lse_sc):
    seq = q_ref.shape[0]
    g_id = pl.program_id(2)
    for g, (_, dil) in enumerate(DILATED_GROUPS):
        @pl.when(g_id == g)
        def _(g=g, dil=dil):
            _attn_prompt_group(q_ref, k_ref, v_ref, o_sc, lse_sc, g, dil, seq)

    @pl.when(g_id == N_DIL - 1)
    def _():
        chunk = 256

        def body(i, carry):
            sl = pl.ds(pl.multiple_of(i * chunk, chunk), chunk)
            o = _merge_groups([o_sc[g, sl, :] for g in range(N_DIL)], [lse_sc[g, sl, :] for g in range(N_DIL)])
            o_ref[sl, :] = o.astype(BF16)
            return carry

        lax.fori_loop(0, seq // chunk, body, 0)


def _attn_prompt(qkv, bsz, seq):
    qkv3 = qkv.reshape(bsz, seq, QKV_COLS)
    per_sel = D_ATT // HEAD_DIM
    spec = lambda sel: pl.BlockSpec((None, seq, HEAD_DIM),
                                    lambda b, j, g: (b, 0, sel * per_sel + g * HEADS_PER_GROUP + j))
    out = pl.pallas_call(
        _attn_prompt_kernel,
        grid=(bsz, HEADS_PER_GROUP, N_DIL),
        in_specs=[spec(0), spec(1), spec(2)],
        out_specs=pl.BlockSpec((None, seq, HEAD_DIM), lambda b, j, g: (b, 0, j)),
        out_shape=jax.ShapeDtypeStruct((bsz, seq, D_GRP), BF16),
        scratch_shapes=[pltpu.VMEM((N_DIL, seq, HEAD_DIM), F32), pltpu.VMEM((N_DIL, seq, HEAD_DIM), F32)],
        compiler_params=_params("arbitrary", "arbitrary", "arbitrary"),
        name="attn_prompt",
    )(qkv3, qkv3, qkv3)
    return out.reshape(bsz * seq, D_GRP)


KV_ROWS = 2 * HEADS_PER_GROUP


def _attn_sample_kernel(dil, q_ref, cache_ref, new_ref, buf_ref, o_ref, lse_ref):
    t_new = q_ref.shape[0]
    n_buf = cache_ref.shape[0] // KV_ROWS
    tq = lax.broadcasted_iota(jnp.int32, (t_new, n_buf), 0)
    pos = lax.broadcasted_iota(jnp.int32, (t_new, n_buf), 1)
    back = n_buf + tq - pos
    cache_ok = jnp.logical_and(back <= dil * SPAN, (back & (dil - 1)) == 0)
    tq2 = lax.broadcasted_iota(jnp.int32, (t_new, t_new), 0)
    tk2 = lax.broadcasted_iota(jnp.int32, (t_new, t_new), 1)
    new_ok = jnp.logical_and(tk2 <= tq2, ((tq2 - tk2) & (dil - 1)) == 0)
    for h in range(HEADS_PER_GROUP):
        cols = slice(h * HEAD_DIM, (h + 1) * HEAD_DIM)
        q = q_ref[:, cols].astype(BF16)
        k_c = cache_ref[pl.ds(h, n_buf, stride=KV_ROWS), :].astype(BF16)
        v_c = cache_ref[pl.ds(HEADS_PER_GROUP + h, n_buf, stride=KV_ROWS), :].astype(BF16)
        k_n = new_ref[pl.ds(h, t_new, stride=KV_ROWS), :].astype(BF16)
        v_n = new_ref[pl.ds(HEADS_PER_GROUP + h, t_new, stride=KV_ROWS), :].astype(BF16)
        s_c = jnp.where(cache_ok, _nt_dot(q, k_c) * ATT_SCALE, NEG_INF)
        s_n = jnp.where(new_ok, _nt_dot(q, k_n) * ATT_SCALE, NEG_INF)
        m = jnp.maximum(jnp.max(s_c, axis=-1, keepdims=True), jnp.max(s_n, axis=-1, keepdims=True))
        p_c = jnp.exp(s_c - m)
        p_n = jnp.exp(s_n - m)
        den = jnp.sum(p_c, axis=-1, keepdims=True) + jnp.sum(p_n, axis=-1, keepdims=True)
        acc = (jnp.dot(p_c.astype(BF16), v_c, preferred_element_type=F32)
               + jnp.dot(p_n.astype(BF16), v_n, preferred_element_type=F32))
        o_ref[:, cols] = acc / den
        lse_ref[:, cols] = jnp.broadcast_to(m + jnp.log(den), (t_new, HEAD_DIM))
    keep = (n_buf - t_new) * KV_ROWS
    buf_ref[0:keep, :] = cache_ref[t_new * KV_ROWS:, :]
    buf_ref[keep:, :] = new_ref[...]


def _attn_sample(q3, cache, new_kv, g):
    window, dil = DILATED_GROUPS[g]
    bd, t_new, _ = q3.shape
    rows = cache.shape[1]
    assert rows == window * KV_ROWS, "the cache must hold a full window"
    row_spec = lambda r: pl.BlockSpec((None, r, HEAD_DIM), lambda b: (b, 0, 0))
    tok_spec = pl.BlockSpec((None, t_new, D_GRP), lambda b: (b, 0, 0))
    return pl.pallas_call(
        functools.partial(_attn_sample_kernel, dil),
        grid=(bd,),
        in_specs=[pl.BlockSpec((None, t_new, D_GRP), lambda b: (b, 0, g)), row_spec(rows),
                  row_spec(t_new * KV_ROWS)],
        out_specs=[row_spec(rows), tok_spec, tok_spec],
        out_shape=[jax.ShapeDtypeStruct((bd, rows, HEAD_DIM), F32),
                   jax.ShapeDtypeStruct((bd, t_new, D_GRP), F32),
                   jax.ShapeDtypeStruct((bd, t_new, D_GRP), F32)],
        compiler_params=_params("arbitrary"),
        name=f"attn_sample_w{window}",
    )(q3, cache, new_kv)


def _merge_sample_kernel(o0, o1, o2, l0, l1, l2, out_ref):
    out_ref[...] = _merge_groups([o0[...], o1[...], o2[...]], [l0[...], l1[...], l2[...]]).astype(BF16)


def _merge_sample(os_, ls_):
    m = os_[0].shape[0]
    spec = pl.BlockSpec((m, D_GRP), lambda i: (0, 0))
    return pl.pallas_call(
        _merge_sample_kernel,
        grid=(1,),
        in_specs=[spec] * 6,
        out_specs=spec,
        out_shape=jax.ShapeDtypeStruct((m, D_GRP), BF16),
        compiler_params=_params("arbitrary"),
        name="merge_sample",
    )(*os_, *ls_)


def _layer_norm(v, g, b):
    mu = jnp.mean(v, axis=-1, keepdims=True)
    d = v - mu
    var = jnp.mean(d * d, axis=-1, keepdims=True)
    return d * lax.rsqrt(var + LN_EPS) * g + b


def _mix_kernel(p_ref, o_ref, gc_ref, ga_ref, x_ref, wc_ref, wa_ref, wo_ref, g_ref, b_ref, *rest):
    h_ref = rest[-1]
    y_conv = jnp.dot(p_ref[...], wc_ref[...], preferred_element_type=F32)
    y_att = jnp.dot(o_ref[...], wa_ref[...], preferred_element_type=F32)
    merged = (jax.nn.sigmoid(gc_ref[...].astype(F32)) * y_conv
              + jax.nn.sigmoid(ga_ref[...].astype(F32)) * y_att)
    mix = jnp.dot(merged.astype(BF16), wo_ref[...], preferred_element_type=F32)
    h_ref[...] = _layer_norm(ALPHA * x_ref[...] + mix, g_ref[...], b_ref[...])


def _mix(p, o, z, x, wc16, wa16, wo16, ln_g, ln_b, tm, total, h_all=None, row_off=0):
    m = p.shape[0]
    assert row_off % tm == 0
    blk_off = row_off // tm
    row = lambda w: pl.BlockSpec((tm, w), lambda i: (i, 0))
    const = lambda a: pl.BlockSpec(a.shape, lambda i: (0, 0))
    in_specs = [row(D_CONV), row(D_GRP),
                pl.BlockSpec((tm, D_MODEL), lambda i: (i, 0)), pl.BlockSpec((tm, D_MODEL), lambda i: (i, 1)),
                row(D_MODEL), const(wc16), const(wa16), const(wo16), const(ln_g), const(ln_b)]
    args = [p, o, z, z, x, wc16, wa16, wo16, ln_g, ln_b]
    aliases = {}
    if h_all is not None:
        in_specs.append(pl.BlockSpec(memory_space=pl.ANY))
        args.append(h_all)
        aliases = {len(args) - 1: 0}
    return pl.pallas_call(
        _mix_kernel,
        grid=(m // tm,),
        in_specs=in_specs,
        out_specs=pl.BlockSpec((tm, D_MODEL), lambda i: (i + blk_off, 0)),
        out_shape=jax.ShapeDtypeStruct((total, D_MODEL), F32),
        input_output_aliases=aliases,
        compiler_params=_params("arbitrary"),
        name="mix_ln1",
    )(*args)


def _router_kernel(h_ref, w_ref, b_ref, idx_ref, gate_ref, cnt_ref, carry):
    tile = h_ref.shape[0]

    @pl.when(pl.program_id(0) == 0)
    def _():
        carry[...] = jnp.zeros(carry.shape, F32)

    logits = jnp.dot(h_ref[...], w_ref[...], preferred_element_type=F32,
                     precision=lax.Precision.HIGHEST) + b_ref[...]
    lane = lax.broadcasted_iota(jnp.int32, (tile, LANES), 1)
    lane_f = lane.astype(F32)
    first_max = lambda v, mx: jnp.min(jnp.where(v == mx, lane_f, float(LANES)), axis=-1, keepdims=True)

    is_grp = lane < N_EXPERT_GROUPS
    lg = jnp.where(is_grp, logits, -jnp.inf)
    mg = jnp.max(lg, axis=-1, keepdims=True)
    gsel = first_max(lg, mg).astype(jnp.int32)
    p_group = 1.0 / jnp.sum(jnp.where(is_grp, jnp.exp(logits - mg), 0.0), axis=-1, keepdims=True)

    e_lane = lane - N_EXPERT_GROUPS
    in_grp = jnp.logical_and(e_lane >= 0, (e_lane >> 3) == gsel)
    le = jnp.where(in_grp, logits, -jnp.inf)
    v1 = jnp.max(le, axis=-1, keepdims=True)
    i1 = first_max(le, v1).astype(jnp.int32)
    le2 = jnp.where(lane == i1, -jnp.inf, le)
    v2 = jnp.max(le2, axis=-1, keepdims=True)
    i2 = first_max(le2, v2).astype(jnp.int32)
    ex = jnp.exp(v2 - v1)
    g1 = p_group / (1.0 + ex)
    g2 = p_group * ex / (1.0 + ex)
    e1 = i1 - N_EXPERT_GROUPS
    e2 = i2 - N_EXPERT_GROUPS

    hot1 = lane == e1
    hot2 = lane == e2
    hot = jnp.where(jnp.logical_or(hot1, hot2), 1.0, 0.0)
    rr = lax.broadcasted_iota(jnp.int32, (tile, tile), 0)
    cc = lax.broadcasted_iota(jnp.int32, (tile, tile), 1)
    tri = jnp.where(cc < rr, 1.0, 0.0).astype(BF16)
    before = jnp.dot(tri, hot.astype(BF16), preferred_element_type=F32) + carry[0:1, :]
    r1 = jnp.sum(jnp.where(hot1, before, 0.0), axis=-1, keepdims=True).astype(jnp.int32)
    r2 = jnp.sum(jnp.where(hot2, before, 0.0), axis=-1, keepdims=True).astype(jnp.int32)
    carry[...] = carry[...] + jnp.sum(hot, axis=0, keepdims=True)

    idx_ref[...] = jnp.where(lane == 0, e1, jnp.where(lane == 1, e2, jnp.where(lane == 2, r1, r2)))
    gate_ref[...] = jnp.where(lane == 0, g1, g2)
    cnt_ref[...] = carry[...]


def _router(h_all, w_r, b_r):
    m = h_all.shape[0]
    tile = ROUTER_TILE if m % ROUTER_TILE == 0 else ROW_TILE
    row = pl.BlockSpec((tile, LANES), lambda i: (i, 0))
    return pl.pallas_call(
        _router_kernel,
        grid=(m // tile,),
        in_specs=[pl.BlockSpec((tile, D_MODEL), lambda i: (i, 0)),
                  pl.BlockSpec((D_MODEL, LANES), lambda i: (0, 0)),
                  pl.BlockSpec((1, LANES), lambda i: (0, 0))],
        out_specs=[row, row, pl.BlockSpec((SUBLANES, LANES), lambda i: (0, 0))],
        out_shape=[jax.ShapeDtypeStruct((m, LANES), jnp.int32), jax.ShapeDtypeStruct((m, LANES), F32),
                   jax.ShapeDtypeStruct((SUBLANES, LANES), F32)],
        scratch_shapes=[pltpu.VMEM((SUBLANES, LANES), F32)],
        compiler_params=_params("arbitrary"),
        name="router",
    )(h_all, w_r, b_r)


def _dispatch_kernel(dest_ref, h_ref, x_hbm, sem):
    base = pl.program_id(0) * ROW_TILE

    def copy(r, k):
        return pltpu.make_async_copy(h_ref.at[pl.ds(r, 1), :],
                                     x_hbm.at[pl.ds(dest_ref[(base + r) * TOP_K + k], 1), :], sem)

    def start(r, c):
        for k in range(TOP_K):
            copy(r, k).start()
        return c

    def wait(r, c):
        for k in range(TOP_K):
            copy(r, k).wait()
        return c

    lax.fori_loop(0, ROW_TILE, start, 0)
    lax.fori_loop(0, ROW_TILE, wait, 0)


def _dispatch(h_all, dest_flat, n_rows):
    m = h_all.shape[0]
    grid_spec = pltpu.PrefetchScalarGridSpec(
        num_scalar_prefetch=1,
        grid=(m // ROW_TILE,),
        in_specs=[pl.BlockSpec((ROW_TILE, D_MODEL), lambda i, d: (i, 0))],
        out_specs=pl.BlockSpec(memory_space=pl.ANY),
        scratch_shapes=[pltpu.SemaphoreType.DMA(())],
    )
    return pl.pallas_call(
        _dispatch_kernel,
        grid_spec=grid_spec,
        out_shape=jax.ShapeDtypeStruct((n_rows, D_MODEL), F32),
        compiler_params=pltpu.CompilerParams(dimension_semantics=("arbitrary",), has_side_effects=True),
        name="moe_dispatch",
    )(dest_flat, h_all)


def _expert_kernel(blk_e, blk_first, blk_valid, blk_ord, exp_list, n_used, n_exp,
                   x_ref, wg_hbm, wu_hbm, wd_hbm, y_ref, wg32, wu32, wd32, wg16, wu16, wd16, sems):
    i = pl.program_id(0)

    def weight_copies(e, slot):
        return [pltpu.make_async_copy(hbm.at[e], buf.at[slot], sems.at[slot, n])
                for n, (hbm, buf) in enumerate(((wg_hbm, wg32), (wu_hbm, wu32), (wd_hbm, wd32)))]

    def start_weights(ordinal, slot):
        for c in weight_copies(exp_list[ordinal], slot):
            c.start()

    @pl.when(i == 0)
    def _():
        start_weights(0, 0)

        @pl.when(n_exp[0] > 1)
        def _():
            start_weights(1, 1)

    @pl.when(i < n_used[0])
    def _():
        @pl.when(blk_first[i] == 1)
        def _():
            ordinal = blk_ord[i]
            slot = ordinal & 1
            for c in weight_copies(blk_e[i], slot):
                c.wait()
            wg16[...] = wg32[slot].astype(BF16)
            wu16[...] = wu32[slot].astype(BF16)
            wd16[...] = wd32[slot].astype(BF16)

            @pl.when(ordinal + 2 < n_exp[0])
            def _():
                start_weights(ordinal + 2, slot)

        row = lax.broadcasted_iota(jnp.int32, x_ref.shape, 0)
        x = jnp.where(row < blk_valid[i], x_ref[...], 0.0).astype(BF16)
        gate = jnp.dot(x, wg16[...], preferred_element_type=F32)
        up = jnp.dot(x, wu16[...], preferred_element_type=F32)
        hid = (jax.nn.silu(gate) * up).astype(BF16)
        y_ref[...] = jnp.dot(hid, wd16[...], preferred_element_type=F32)


def _experts(x_rows, plan, w_gate, w_up, w_down):
    n_rows = x_rows.shape[0]
    n_blocks = n_rows // MOE_BLK
    row_map = lambda i, be, bf, bv, bo, el, nu, ne: (jnp.minimum(i, nu[0] - 1), 0)
    hbm = pl.BlockSpec(memory_space=pl.ANY)
    grid_spec = pltpu.PrefetchScalarGridSpec(
        num_scalar_prefetch=7,
        grid=(n_blocks,),
        in_specs=[pl.BlockSpec((MOE_BLK, D_MODEL), row_map), hbm, hbm, hbm],
        out_specs=pl.BlockSpec((MOE_BLK, D_MODEL), row_map),
        scratch_shapes=[pltpu.VMEM((2, D_MODEL, D_EXPERT), F32), pltpu.VMEM((2, D_MODEL, D_EXPERT), F32),
                        pltpu.VMEM((2, D_EXPERT, D_MODEL), F32),
                        pltpu.VMEM((D_MODEL, D_EXPERT), BF16), pltpu.VMEM((D_MODEL, D_EXPERT), BF16),
                        pltpu.VMEM((D_EXPERT, D_MODEL), BF16),
                        pltpu.SemaphoreType.DMA((2, 3))],
    )
    return pl.pallas_call(
        _expert_kernel,
        grid_spec=grid_spec,
        out_shape=jax.ShapeDtypeStruct((n_rows, D_MODEL), F32),
        compiler_params=_params("arbitrary"),
        name="moe_experts",
    )(*plan, x_rows, w_gate, w_up, w_down)


def _combine_kernel(tok_off, dest_ref, y_hbm, h_ref, gate_ref, g_ref, b_ref, out_ref, ybuf, sems):
    i = pl.program_id(0)
    n_steps = pl.num_programs(0)

    def copy(step, r, k):
        slot = step & 1
        row = dest_ref[(tok_off + step * ROW_TILE + r) * TOP_K + k]
        return pltpu.make_async_copy(y_hbm.at[pl.ds(row, 1), :], ybuf.at[slot, k, pl.ds(r, 1), :], sems.at[slot])

    def gather(step):
        def body(r, c):
            for k in range(TOP_K):
                copy(step, r, k).start()
            return c
        lax.fori_loop(0, ROW_TILE, body, 0)

    @pl.when(i == 0)
    def _():
        gather(i)

    @pl.when(i + 1 < n_steps)
    def _():
        gather(i + 1)

    def wait(r, c):
        for k in range(TOP_K):
            copy(i, r, k).wait()
        return c

    lax.fori_loop(0, ROW_TILE, wait, 0)
    slot = i & 1
    f = gate_ref[:, 0:1] * ybuf[slot, 0] + gate_ref[:, 1:2] * ybuf[slot, 1]
    out_ref[...] = _layer_norm(ALPHA * h_ref[...] + f, g_ref[...], b_ref[...])


def _combine(y_rows, dest_flat, h_all, gates, ln_g, ln_b, tok_off, n_tok):
    blk_off = tok_off // ROW_TILE
    grid_spec = pltpu.PrefetchScalarGridSpec(
        num_scalar_prefetch=1,
        grid=(n_tok // ROW_TILE,),
        in_specs=[pl.BlockSpec(memory_space=pl.ANY),
                  pl.BlockSpec((ROW_TILE, D_MODEL), lambda i, d: (i + blk_off, 0)),
                  pl.BlockSpec((ROW_TILE, LANES), lambda i, d: (i + blk_off, 0)),
                  pl.BlockSpec((1, D_MODEL), lambda i, d: (0, 0)),
                  pl.BlockSpec((1, D_MODEL), lambda i, d: (0, 0))],
        out_specs=pl.BlockSpec((ROW_TILE, D_MODEL), lambda i, d: (i, 0)),
        scratch_shapes=[pltpu.VMEM((2, TOP_K, ROW_TILE, D_MODEL), F32), pltpu.SemaphoreType.DMA((2,))],
    )
    return pl.pallas_call(
        functools.partial(_combine_kernel, tok_off),
        grid_spec=grid_spec,
        out_shape=jax.ShapeDtypeStruct((n_tok, D_MODEL), F32),
        compiler_params=_params("arbitrary"),
        name="combine_ln2",
    )(dest_flat, y_rows, h_all, gates, ln_g, ln_b)


def _moe_plan(idx, counts_f):
    m = idx.shape[0]
    n_asg = m * TOP_K
    experts = idx[:, 0:TOP_K]
    ranks = idx[:, TOP_K:2 * TOP_K]
    counts = counts_f[0, :N_EXPERTS].astype(jnp.int32)
    padded = (counts + MOE_BLK - 1) // MOE_BLK * MOE_BLK
    pad_end = jnp.cumsum(padded)
    pad_start = pad_end - padded
    hot = experts[:, :, None] == jnp.arange(N_EXPERTS, dtype=jnp.int32)
    dest = (jnp.sum(jnp.where(hot, pad_start, 0), axis=-1) + ranks).reshape(n_asg).astype(jnp.int32)
    n_blocks = -(-(n_asg + N_EXPERTS * (MOE_BLK - 1)) // MOE_BLK)
    n_used = jnp.maximum(pad_end[-1] // MOE_BLK, 1).astype(jnp.int32)
    blk = jnp.minimum(jnp.arange(n_blocks, dtype=jnp.int32), n_used - 1)
    blk_e = jnp.minimum(jnp.searchsorted(pad_end, blk * MOE_BLK, side='right'), N_EXPERTS - 1).astype(jnp.int32)
    blk_first = (blk * MOE_BLK == pad_start[blk_e]).astype(jnp.int32)
    blk_valid = jnp.clip(counts[blk_e] - (blk * MOE_BLK - pad_start[blk_e]), 0, MOE_BLK).astype(jnp.int32)
    used = counts > 0
    exp_list = jnp.argsort(jnp.logical_not(used), stable=True).astype(jnp.int32)
    blk_ord = (jnp.cumsum(used.astype(jnp.int32)) - 1)[blk_e].astype(jnp.int32)
    n_exp = jnp.sum(used.astype(jnp.int32)).reshape(1)
    plan = (blk_e, blk_first, blk_valid, blk_ord, exp_list, n_used.reshape(1), n_exp)
    return dest, n_blocks * MOE_BLK, plan


def kernel(x_prompt, x_sample, cache_kv_w128, cache_kv_w512, cache_kv_w2048, state_conv, w_in, conv_w, w_conv_out, w_att_out, w_o, ln1_g, ln1_b, w_router_group, b_router_group, w_router_expert, b_router_expert, w_expert_gate, w_expert_up, w_expert_down, ln2_g, ln2_b):
    bsz, seq, _ = x_prompt.shape
    bd, t_new, _ = x_sample.shape
    m_p = bsz * seq
    m_s = bd * t_new
    m_all = m_p + m_s
    caches = (cache_kv_w128, cache_kv_w512, cache_kv_w2048)
    lyr = 0

    w_in16 = w_in[lyr].astype(BF16)
    wc16 = w_conv_out[lyr].astype(BF16)
    wa16 = w_att_out[lyr].astype(BF16)
    wo16 = w_o[lyr].astype(BF16)
    cw = conv_w[lyr]
    g1, b1 = ln1_g[lyr][None, :], ln1_b[lyr][None, :]
    g2, b2 = ln2_g[lyr][None, :], ln2_b[lyr][None, :]
    w_re = jnp.transpose(w_router_expert[lyr], (1, 0, 2)).reshape(D_MODEL, N_EXPERTS)
    w_r = jnp.pad(jnp.concatenate([w_router_group[lyr], w_re], axis=1),
                  ((0, 0), (0, LANES - N_EXPERT_GROUPS - N_EXPERTS)))
    b_r = jnp.pad(jnp.concatenate([b_router_group[lyr], b_router_expert[lyr].reshape(N_EXPERTS)]),
                  (0, LANES - N_EXPERT_GROUPS - N_EXPERTS))[None, :]

    xp = x_prompt.reshape(m_p, D_MODEL)
    xs = x_sample.reshape(m_s, D_MODEL)

    z_p, qkv_p = _proj_in(xp.astype(BF16), w_in16, min(PROJ_TM, m_p))
    p_p, conv_p = _conv_prompt(z_p, cw, bsz, seq)
    o_p = _attn_prompt(qkv_p, bsz, seq)
    h_buf = _mix(p_p, o_p, z_p, xp, wc16, wa16, wo16, g1, b1, min(MIX_TM, m_p), m_all)

    z_s, qkv_s = _proj_in(xs.astype(BF16), w_in16, m_s)
    p_s, conv_s = _conv_sample(z_s, cw, state_conv[lyr], t_new)
    q3 = qkv_s.reshape(bd, t_new, QKV_COLS)
    kv5 = qkv_s.reshape(bd, t_new, 3, N_DIL, HEADS_PER_GROUP, HEAD_DIM)
    os_, ls_, bufs_s = [], [], []
    for g, (window, _) in enumerate(DILATED_GROUPS):
        cache = caches[g][lyr].reshape(bd, -1, HEAD_DIM)
        new_kv = kv5[:, :, 1:3, g].reshape(bd, t_new * KV_ROWS, HEAD_DIM)
        buf, o_g, lse_g = _attn_sample(q3, cache, new_kv, g)
        bufs_s.append(buf.reshape(1, bd, window, 2, HEADS_PER_GROUP, HEAD_DIM))
        os_.append(o_g.reshape(m_s, D_GRP))
        ls_.append(lse_g.reshape(m_s, D_GRP))
    o_s = _merge_sample(os_, ls_)
    h_buf = _mix(p_s, o_s, z_s, xs, wc16, wa16, wo16, g1, b1, m_s, m_all, h_all=h_buf, row_off=m_p)

    idx, gates, counts = _router(h_buf, w_r, b_r)
    dest, n_rows, plan = _moe_plan(idx, counts)
    x_rows = _dispatch(h_buf, dest, n_rows)
    y_rows = _experts(x_rows, plan, w_expert_gate[lyr], w_expert_up[lyr], w_expert_down[lyr])
    y_p = _combine(y_rows, dest, h_buf, gates, g2, b2, 0, m_p)
    y_s = _combine(y_rows, dest, h_buf, gates, g2, b2, m_p, m_s)

    qkv_p3 = qkv_p.reshape(bsz, seq, QKV_COLS)
    bufs_p = []
    for g, (w, _) in enumerate(DILATED_GROUPS):
        keep = min(w, seq)
        k_g = qkv_p3[:, seq - keep:, D_ATT + g * D_GRP:D_ATT + (g + 1) * D_GRP]
        v_g = qkv_p3[:, seq - keep:, 2 * D_ATT + g * D_GRP:2 * D_ATT + (g + 1) * D_GRP]
        bufs_p.append(jnp.stack([k_g, v_g], axis=2).reshape(1, bsz, keep, 2, HEADS_PER_GROUP, HEAD_DIM))
    return (y_p.reshape(bsz, seq, D_MODEL), y_s.reshape(bd, t_new, D_MODEL),
            bufs_p[0], bufs_p[1], bufs_p[2], conv_p[None],
            bufs_s[0], bufs_s[1], bufs_s[2], conv_s[None])
```

```python
import functools

import jax
import jax.numpy as jnp
from jax import lax
from jax.experimental import pallas as pl
from jax.experimental.pallas import tpu as pltpu

F32 = jnp.float32
BF16 = jnp.bfloat16

D_MODEL = 2048
D_CONV = D_MODEL // 2
CONV_W = 3
DILATED_GROUPS = ((128, 1), (512, 4), (2048, 16))
N_DIL = len(DILATED_GROUPS)
HEADS_PER_GROUP = 4
HEAD_DIM = 128
SPAN = 128
D_GRP = HEADS_PER_GROUP * HEAD_DIM
D_ATT = N_DIL * D_GRP
N_EXPERT_GROUPS = 8
EXPERTS_PER_GROUP = 8
N_EXPERTS = N_EXPERT_GROUPS * EXPERTS_PER_GROUP
TOP_K = 2
D_EXPERT = D_MODEL // 4
DEPTH = 1
LN_EPS = 1e-5
ALPHA = (2 * DEPTH) ** 0.25
NEG_INF = -1e30
ATT_SCALE = HEAD_DIM ** -0.5

OFF_B = 0
OFF_C = D_CONV
OFF_H = 2 * D_CONV
OFF_Q = 3 * D_CONV
OFF_K = OFF_Q + D_ATT
OFF_V = OFF_K + D_ATT
OFF_GC = OFF_V + D_ATT
OFF_GA = OFF_GC + D_MODEL
N_IN = OFF_GA + D_MODEL

LANES = 128
SUBLANES = 8
VMEM_LIMIT = 56 * 1024 * 1024

PROJ_TN = 512
PROJ_TM = 1024
Z_COLS = 2 * D_MODEL + 3 * D_CONV
QKV_COLS = 3 * D_ATT
NZ_TILES = Z_COLS // PROJ_TN
NQ_TILES = QKV_COLS // PROJ_TN
CONV_TM = 512
MIX_TM = 512
ROW_TILE = 128
ROUTER_TILE = 640
MOE_BLK = 256
ATTN_UNROLL = 4


def _proj_tile_order():
    t = lambda off, width: list(range(off // PROJ_TN, (off + width) // PROJ_TN))
    z_tiles = t(OFF_GC, D_MODEL) + t(OFF_GA, D_MODEL) + t(OFF_B, D_CONV) + t(OFF_C, D_CONV) + t(OFF_H, D_CONV)
    q_tiles = t(OFF_Q, D_ATT) + t(OFF_K, D_ATT) + t(OFF_V, D_ATT)
    return z_tiles + q_tiles


def _params(*sem):
    return pltpu.CompilerParams(dimension_semantics=sem, vmem_limit_bytes=VMEM_LIMIT)


def _proj_kernel(order_ref, x_ref, w_ref, z_ref, qkv_ref, x16):
    del order_ref
    j = pl.program_id(1)

    @pl.when(j == 0)
    def _():
        x16[...] = x_ref[...].astype(BF16)

    @pl.when(j < NZ_TILES)
    def _():
        z_ref[...] = jnp.dot(x16[...], w_ref[...], preferred_element_type=F32).astype(BF16)

    @pl.when(j >= NZ_TILES)
    def _():
        qkv_ref[...] = jnp.dot(x16[...], w_ref[...], preferred_element_type=F32)


def _proj_in(x, w16, tm):
    m = x.shape[0]
    order = jnp.asarray(_proj_tile_order(), jnp.int32)
    grid_spec = pltpu.PrefetchScalarGridSpec(
        num_scalar_prefetch=1,
        grid=(m // tm, NZ_TILES + NQ_TILES),
        in_specs=[
            pl.BlockSpec((tm, D_MODEL), lambda i, j, o: (i, 0)),
            pl.BlockSpec((D_MODEL, PROJ_TN), lambda i, j, o: (0, o[j])),
        ],
        out_specs=[
            pl.BlockSpec((tm, PROJ_TN), lambda i, j, o: (i, jnp.minimum(j, NZ_TILES - 1))),
            pl.BlockSpec((tm, PROJ_TN), lambda i, j, o: (i, jnp.maximum(j - NZ_TILES, 0))),
        ],
        scratch_shapes=[pltpu.VMEM((tm, D_MODEL), BF16)],
    )
    return pl.pallas_call(
        _proj_kernel,
        grid_spec=grid_spec,
        out_shape=[jax.ShapeDtypeStruct((m, Z_COLS), BF16), jax.ShapeDtypeStruct((m, QKV_COLS), F32)],
        compiler_params=_params("arbitrary", "arbitrary"),
        name="proj_in",
    )(order, x, w16)


ZB_BLK = (2 * D_MODEL) // D_CONV


def _conv_prompt_kernel(zb_ref, zc_ref, zh_ref, cw_ref, p_ref, st_ref, uext):
    tm = zb_ref.shape[0]

    @pl.when(pl.program_id(1) == 0)
    def _():
        uext[0:SUBLANES, :] = jnp.zeros((SUBLANES, D_CONV), F32)

    u = zc_ref[...].astype(F32) * zh_ref[...].astype(F32)
    uext[SUBLANES:SUBLANES + tm, :] = u
    conv = (cw_ref[0:1, :] * uext[SUBLANES - 2:SUBLANES - 2 + tm, :]
            + cw_ref[1:2, :] * uext[SUBLANES - 1:SUBLANES - 1 + tm, :]
            + cw_ref[2:3, :] * u)
    p_ref[...] = (zb_ref[...].astype(F32) * conv).astype(BF16)
    last = uext[tm:tm + SUBLANES, :]
    uext[0:SUBLANES, :] = last
    st_ref[0] = last[SUBLANES - 2:SUBLANES, :]


def _conv_prompt(z, conv_w, bsz, seq):
    tm = min(CONV_TM, seq)
    nsb = seq // tm
    zspec = lambda blk: pl.BlockSpec((tm, D_CONV), lambda b, s: (b * nsb + s, blk))
    return pl.pallas_call(
        _conv_prompt_kernel,
        grid=(bsz, nsb),
        in_specs=[zspec(ZB_BLK), zspec(ZB_BLK + 1), zspec(ZB_BLK + 2),
                  pl.BlockSpec((CONV_W, D_CONV), lambda b, s: (0, 0))],
        out_specs=[pl.BlockSpec((tm, D_CONV), lambda b, s: (b * nsb + s, 0)),
                   pl.BlockSpec((1, CONV_W - 1, D_CONV), lambda b, s: (b, 0, 0))],
        out_shape=[jax.ShapeDtypeStruct((bsz * seq, D_CONV), BF16),
                   jax.ShapeDtypeStruct((bsz, CONV_W - 1, D_CONV), F32)],
        scratch_shapes=[pltpu.VMEM((tm + SUBLANES, D_CONV), F32)],
        compiler_params=_params("arbitrary", "arbitrary"),
        name="conv_prompt",
    )(z, z, z, conv_w)


def _conv_sample_kernel(t_new, zb_ref, zc_ref, zh_ref, cw_ref, s1_ref, s2_ref, p_ref, u_ref, uext):
    m = zb_ref.shape[0]
    u = zc_ref[...].astype(F32) * zh_ref[...].astype(F32)
    u_ref[...] = u
    uext[0:SUBLANES, :] = jnp.zeros((SUBLANES, D_CONV), F32)
    uext[SUBLANES:SUBLANES + m, :] = u
    t = lax.broadcasted_iota(jnp.int32, (m, D_CONV), 0) % t_new
    um1 = jnp.where(t < 1, s1_ref[...], uext[SUBLANES - 1:SUBLANES - 1 + m, :])
    um2 = jnp.where(t < 2, s2_ref[...], uext[SUBLANES - 2:SUBLANES - 2 + m, :])
    conv = cw_ref[0:1, :] * um2 + cw_ref[1:2, :] * um1 + cw_ref[2:3, :] * u
    p_ref[...] = (zb_ref[...].astype(F32) * conv).astype(BF16)


def _conv_sample(z, conv_w, state, t_new):
    m = z.shape[0]
    bd = m // t_new
    s1 = jnp.pad(state[:, 1:2], ((0, 0), (0, t_new - 1), (0, 0))).reshape(m, D_CONV)
    s2 = jnp.pad(state, ((0, 0), (0, t_new - 2), (0, 0))).reshape(m, D_CONV)
    zspec = lambda blk: pl.BlockSpec((m, D_CONV), lambda i: (0, blk))
    full = lambda r: pl.BlockSpec((r, D_CONV), lambda i: (0, 0))
    p, u = pl.pallas_call(
        functools.partial(_conv_sample_kernel, t_new),
        grid=(1,),
        in_specs=[zspec(ZB_BLK), zspec(ZB_BLK + 1), zspec(ZB_BLK + 2), full(CONV_W), full(m), full(m)],
        out_specs=[full(m), full(m)],
        out_shape=[jax.ShapeDtypeStruct((m, D_CONV), BF16), jax.ShapeDtypeStruct((m, D_CONV), F32)],
        scratch_shapes=[pltpu.VMEM((m + SUBLANES, D_CONV), F32)],
        compiler_params=_params("arbitrary"),
        name="conv_sample",
    )(z, z, z, conv_w, s1, s2)
    new_state = u.reshape(bd, t_new, D_CONV)[:, t_new - (CONV_W - 1):]
    return p, new_state


def _merge_groups(o_list, lse_list):
    top = functools.reduce(jnp.maximum, lse_list)
    es = [jnp.exp(l - top) for l in lse_list]
    num = functools.reduce(lambda a, b: a + b, [e * o for e, o in zip(es, o_list)])
    den = functools.reduce(lambda a, b: a + b, es)
    return num / den


def _nt_dot(a, b):
    return lax.dot_general(a, b, (((1,), (1,)), ((), ())), preferred_element_type=F32)


def _attn_prompt_group(q_ref, k_ref, v_ref, o_sc, lse_sc, g, dil, seq):
    shift = dil.bit_length() - 1
    qi = lax.broadcasted_iota(jnp.int32, (SPAN, 2 * SPAN), 0)
    kj = lax.broadcasted_iota(jnp.int32, (SPAN, 2 * SPAN), 1)
    band = jnp.logical_and(kj >= qi, kj <= qi + SPAN)
    is_prev = lax.broadcasted_iota(jnp.int32, (1, 2 * SPAN), 1) < SPAN
    ones = jnp.ones((2 * SPAN, HEAD_DIM), BF16)

    def rows(ref, start):
        if dil == 1:
            return ref[pl.ds(start, SPAN), :]
        return ref[pl.ds(start, SPAN, stride=dil), :]

    def body(trip, carry):
        blocks = range(ATTN_UNROLL)
        idx = [trip * ATTN_UNROLL + u for u in blocks]
        c = [i >> shift for i in idx]
        base = [c[u] * (SPAN * dil) + (idx[u] & (dil - 1)) for u in blocks]
        pbase = [jnp.maximum(c[u] - 1, 0) * (SPAN * dil) + (idx[u] & (dil - 1)) for u in blocks]
        q = [rows(q_ref, base[u]).astype(BF16) for u in blocks]
        k = [jnp.concatenate([rows(k_ref, pbase[u]), rows(k_ref, base[u])], axis=0).astype(BF16) for u in blocks]
        s = [_nt_dot(q[u], k[u]) for u in blocks]
        pen = [jnp.where(jnp.logical_and(is_prev, c[u] == 0), NEG_INF, 0.0) for u in blocks]
        s = [jnp.where(band, s[u] * ATT_SCALE + pen[u], NEG_INF) for u in blocks]
        m = [jnp.max(jnp.maximum(s[u][:, :SPAN], s[u][:, SPAN:]), axis=-1, keepdims=True) for u in blocks]
        p = [jnp.exp(s[u] - m[u]).astype(BF16) for u in blocks]
        v = [jnp.concatenate(
            [jnp.concatenate([rows(v_ref, pbase[u]), rows(v_ref, base[u])], axis=0).astype(BF16), ones], axis=1)
            for u in blocks]
        acc = [jnp.dot(p[u], v[u], preferred_element_type=F32) for u in blocks]
        for u in blocks:
            den = acc[u][:, HEAD_DIM:]
            o = acc[u][:, :HEAD_DIM] / den
            lse = m[u] + jnp.log(den)
            if dil == 1:
                o_sc[g, pl.ds(base[u], SPAN), :] = o
                lse_sc[g, pl.ds(base[u], SPAN), :] = lse
            else:
                o_sc[g, pl.ds(base[u], SPAN, stride=dil), :] = o
                lse_sc[g, pl.ds(base[u], SPAN, stride=dil), :] = lse
        return carry

    lax.fori_loop(0, seq // (SPAN * ATTN_UNROLL), body, 0)


def _attn_prompt_kernel(q_ref, k_ref, v_ref, o_ref, o_sc, lse_sc):
    seq = q_ref.shape[0]
    g_id = pl.program_id(2)
    for g, (_, dil) in enumerate(DILATED_GROUPS):
        @pl.when(g_id == g)
        def _(g=g, dil=dil):
            _attn_prompt_group(q_ref, k_ref, v_ref, o_sc, lse_sc, g, dil, seq)

    @pl.when(g_id == N_DIL - 1)
    def _():
        chunk = 256

        def body(i, carry):
            sl = pl.ds(pl.multiple_of(i * chunk, chunk), chunk)
            o = _merge_groups([o_sc[g, sl, :] for g in range(N_DIL)], [lse_sc[g, sl, :] for g in range(N_DIL)])
            o_ref[sl, :] = o.astype(BF16)
            return carry

        lax.fori_loop(0, seq // chunk, body, 0)


def _attn_prompt(qkv, bsz, seq):
    qkv3 = qkv.reshape(bsz, seq, QKV_COLS)
    per_sel = D_ATT // HEAD_DIM
    spec = lambda sel: pl.BlockSpec((None, seq, HEAD_DIM),
                                    lambda b, j, g: (b, 0, sel * per_sel + g * HEADS_PER_GROUP + j))
    out = pl.pallas_call(
        _attn_prompt_kernel,
        grid=(bsz, HEADS_PER_GROUP, N_DIL),
        in_specs=[spec(0), spec(1), spec(2)],
        out_specs=pl.BlockSpec((None, seq, HEAD_DIM), lambda b, j, g: (b, 0, j)),
        out_shape=jax.ShapeDtypeStruct((bsz, seq, D_GRP), BF16),
        scratch_shapes=[pltpu.VMEM((N_DIL, seq, HEAD_DIM), F32), pltpu.VMEM((N_DIL, seq, HEAD_DIM), F32)],
        compiler_params=_params("arbitrary", "arbitrary", "arbitrary"),
        name="attn_prompt",
    )(qkv3, qkv3, qkv3)
    return out.reshape(bsz * seq, D_GRP)


KV_ROWS = 2 * HEADS_PER_GROUP


def _attn_sample_kernel(dil, q_ref, cache_ref, new_ref, buf_ref, o_ref, lse_ref):
    t_new = q_ref.shape[0]
    n_buf = cache_ref.shape[0] // KV_ROWS
    tq = lax.broadcasted_iota(jnp.int32, (t_new, n_buf), 0)
    pos = lax.broadcasted_iota(jnp.int32, (t_new, n_buf), 1)
    back = n_buf + tq - pos
    cache_ok = jnp.logical_and(back <= dil * SPAN, (back & (dil - 1)) == 0)
    tq2 = lax.broadcasted_iota(jnp.int32, (t_new, t_new), 0)
    tk2 = lax.broadcasted_iota(jnp.int32, (t_new, t_new), 1)
    new_ok = jnp.logical_and(tk2 <= tq2, ((tq2 - tk2) & (dil - 1)) == 0)
    for h in range(HEADS_PER_GROUP):
        cols = slice(h * HEAD_DIM, (h + 1) * HEAD_DIM)
        q = q_ref[:, cols].astype(BF16)
        k_c = cache_ref[pl.ds(h, n_buf, stride=KV_ROWS), :].astype(BF16)
        v_c = cache_ref[pl.ds(HEADS_PER_GROUP + h, n_buf, stride=KV_ROWS), :].astype(BF16)
        k_n = new_ref[pl.ds(h, t_new, stride=KV_ROWS), :].astype(BF16)
        v_n = new_ref[pl.ds(HEADS_PER_GROUP + h, t_new, stride=KV_ROWS), :].astype(BF16)
        s_c = jnp.where(cache_ok, _nt_dot(q, k_c) * ATT_SCALE, NEG_INF)
        s_n = jnp.where(new_ok, _nt_dot(q, k_n) * ATT_SCALE, NEG_INF)
        m = jnp.maximum(jnp.max(s_c, axis=-1, keepdims=True), jnp.max(s_n, axis=-1, keepdims=True))
        p_c = jnp.exp(s_c - m)
        p_n = jnp.exp(s_n - m)
        den = jnp.sum(p_c, axis=-1, keepdims=True) + jnp.sum(p_n, axis=-1, keepdims=True)
        acc = (jnp.dot(p_c.astype(BF16), v_c, preferred_element_type=F32)
               + jnp.dot(p_n.astype(BF16), v_n, preferred_element_type=F32))
        o_ref[:, cols] = acc / den
        lse_ref[:, cols] = jnp.broadcast_to(m + jnp.log(den), (t_new, HEAD_DIM))
    keep = (n_buf - t_new) * KV_ROWS
    buf_ref[0:keep, :] = cache_ref[t_new * KV_ROWS:, :]
    buf_ref[keep:, :] = new_ref[...]


def _attn_sample(q3, cache, new_kv, g):
    window, dil = DILATED_GROUPS[g]
    bd, t_new, _ = q3.shape
    rows = cache.shape[1]
    assert rows == window * KV_ROWS, "the cache must hold a full window"
    row_spec = lambda r: pl.BlockSpec((None, r, HEAD_DIM), lambda b: (b, 0, 0))
    tok_spec = pl.BlockSpec((None, t_new, D_GRP), lambda b: (b, 0, 0))
    return pl.pallas_call(
        functools.partial(_attn_sample_kernel, dil),
        grid=(bd,),
        in_specs=[pl.BlockSpec((None, t_new, D_GRP), lambda b: (b, 0, g)), row_spec(rows),
                  row_spec(t_new * KV_ROWS)],
        out_specs=[row_spec(rows), tok_spec, tok_spec],
        out_shape=[jax.ShapeDtypeStruct((bd, rows, HEAD_DIM), F32),
                   jax.ShapeDtypeStruct((bd, t_new, D_GRP), F32),
                   jax.ShapeDtypeStruct((bd, t_new, D_GRP), F32)],
        compiler_params=_params("arbitrary"),
        name=f"attn_sample_w{window}",
    )(q3, cache, new_kv)


def _merge_sample_kernel(o0, o1, o2, l0, l1, l2, out_ref):
    out_ref[...] = _merge_groups([o0[...], o1[...], o2[...]], [l0[...], l1[...], l2[...]]).astype(BF16)


def _merge_sample(os_, ls_):
    m = os_[0].shape[0]
    spec = pl.BlockSpec((m, D_GRP), lambda i: (0, 0))
    return pl.pallas_call(
        _merge_sample_kernel,
        grid=(1,),
        in_specs=[spec] * 6,
        out_specs=spec,
        out_shape=jax.ShapeDtypeStruct((m, D_GRP), BF16),
        compiler_params=_params("arbitrary"),
        name="merge_sample",
    )(*os_, *ls_)


def _layer_norm(v, g, b):
    mu = jnp.mean(v, axis=-1, keepdims=True)
    d = v - mu
    var = jnp.mean(d * d, axis=-1, keepdims=True)
    return d * lax.rsqrt(var + LN_EPS) * g + b


def _mix_kernel(p_ref, o_ref, gc_ref, ga_ref, x_ref, wc_ref, wa_ref, wo_ref, g_ref, b_ref, *rest):
    h_ref = rest[-1]
    y_conv = jnp.dot(p_ref[...], wc_ref[...], preferred_element_type=F32)
    y_att = jnp.dot(o_ref[...], wa_ref[...], preferred_element_type=F32)
    merged = (jax.nn.sigmoid(gc_ref[...].astype(F32)) * y_conv
              + jax.nn.sigmoid(ga_ref[...].astype(F32)) * y_att)
    mix = jnp.dot(merged.astype(BF16), wo_ref[...], preferred_element_type=F32)
    h_ref[...] = _layer_norm(ALPHA * x_ref[...] + mix, g_ref[...], b_ref[...])


def _mix(p, o, z, x, wc16, wa16, wo16, ln_g, ln_b, tm, total, h_all=None, row_off=0):
    m = p.shape[0]
    assert row_off % tm == 0
    blk_off = row_off // tm
    row = lambda w: pl.BlockSpec((tm, w), lambda i: (i, 0))
    const = lambda a: pl.BlockSpec(a.shape, lambda i: (0, 0), pipeline_mode=pl.Buffered(1))
    in_specs = [row(D_CONV), row(D_GRP),
                pl.BlockSpec((tm, D_MODEL), lambda i: (i, 0)), pl.BlockSpec((tm, D_MODEL), lambda i: (i, 1)),
                row(D_MODEL), const(wc16), const(wa16), const(wo16), const(ln_g), const(ln_b)]
    args = [p, o, z, z, x, wc16, wa16, wo16, ln_g, ln_b]
    aliases = {}
    if h_all is not None:
        in_specs.append(pl.BlockSpec(memory_space=pl.ANY))
        args.append(h_all)
        aliases = {len(args) - 1: 0}
    return pl.pallas_call(
        _mix_kernel,
        grid=(m // tm,),
        in_specs=in_specs,
        out_specs=pl.BlockSpec((tm, D_MODEL), lambda i: (i + blk_off, 0)),
        out_shape=jax.ShapeDtypeStruct((total, D_MODEL), F32),
        input_output_aliases=aliases,
        compiler_params=_params("arbitrary"),
        name="mix_ln1",
    )(*args)


def _router_kernel(h_ref, w_ref, b_ref, idx_ref, gate_ref, cnt_ref, carry):
    tile = h_ref.shape[0]

    @pl.when(pl.program_id(0) == 0)
    def _():
        carry[...] = jnp.zeros(carry.shape, F32)

    logits = jnp.dot(h_ref[...], w_ref[...], preferred_element_type=F32,
                     precision=lax.Precision.HIGHEST) + b_ref[...]
    lane = lax.broadcasted_iota(jnp.int32, (tile, LANES), 1)
    lane_f = lane.astype(F32)
    first_max = lambda v, mx: jnp.min(jnp.where(v == mx, lane_f, float(LANES)), axis=-1, keepdims=True)

    is_grp = lane < N_EXPERT_GROUPS
    lg = jnp.where(is_grp, logits, -jnp.inf)
    mg = jnp.max(lg, axis=-1, keepdims=True)
    gsel = first_max(lg, mg).astype(jnp.int32)
    p_group = 1.0 / jnp.sum(jnp.where(is_grp, jnp.exp(logits - mg), 0.0), axis=-1, keepdims=True)

    e_lane = lane - N_EXPERT_GROUPS
    in_grp = jnp.logical_and(e_lane >= 0, (e_lane >> 3) == gsel)
    le = jnp.where(in_grp, logits, -jnp.inf)
    v1 = jnp.max(le, axis=-1, keepdims=True)
    i1 = first_max(le, v1).astype(jnp.int32)
    le2 = jnp.where(lane == i1, -jnp.inf, le)
    v2 = jnp.max(le2, axis=-1, keepdims=True)
    i2 = first_max(le2, v2).astype(jnp.int32)
    ex = jnp.exp(v2 - v1)
    g1 = p_group / (1.0 + ex)
    g2 = p_group * ex / (1.0 + ex)
    e1 = i1 - N_EXPERT_GROUPS
    e2 = i2 - N_EXPERT_GROUPS

    hot1 = lane == e1
    hot2 = lane == e2
    hot = jnp.where(jnp.logical_or(hot1, hot2), 1.0, 0.0)
    rr = lax.broadcasted_iota(jnp.int32, (tile, tile), 0)
    cc = lax.broadcasted_iota(jnp.int32, (tile, tile), 1)
    tri = jnp.where(cc < rr, 1.0, 0.0).astype(BF16)
    before = jnp.dot(tri, hot.astype(BF16), preferred_element_type=F32) + carry[0:1, :]
    r1 = jnp.sum(jnp.where(hot1, before, 0.0), axis=-1, keepdims=True).astype(jnp.int32)
    r2 = jnp.sum(jnp.where(hot2, before, 0.0), axis=-1, keepdims=True).astype(jnp.int32)
    carry[...] = carry[...] + jnp.sum(hot, axis=0, keepdims=True)

    idx_ref[...] = jnp.where(lane == 0, e1, jnp.where(lane == 1, e2, jnp.where(lane == 2, r1, r2)))
    gate_ref[...] = jnp.where(lane == 0, g1, g2)
    cnt_ref[...] = carry[...]


def _router(h_all, w_r, b_r):
    m = h_all.shape[0]
    tile = ROUTER_TILE if m % ROUTER_TILE == 0 else ROW_TILE
    row = pl.BlockSpec((tile, LANES), lambda i: (i, 0))
    return pl.pallas_call(
        _router_kernel,
        grid=(m // tile,),
        in_specs=[pl.BlockSpec((tile, D_MODEL), lambda i: (i, 0)),
                  pl.BlockSpec((D_MODEL, LANES), lambda i: (0, 0)),
                  pl.BlockSpec((1, LANES), lambda i: (0, 0))],
        out_specs=[row, row, pl.BlockSpec((SUBLANES, LANES), lambda i: (0, 0))],
        out_shape=[jax.ShapeDtypeStruct((m, LANES), jnp.int32), jax.ShapeDtypeStruct((m, LANES), F32),
                   jax.ShapeDtypeStruct((SUBLANES, LANES), F32)],
        scratch_shapes=[pltpu.VMEM((SUBLANES, LANES), F32)],
        compiler_params=_params("arbitrary"),
        name="router",
    )(h_all, w_r, b_r)


def _dispatch_kernel(dest_ref, h_ref, x_hbm, sem):
    base = pl.program_id(0) * ROW_TILE

    def copy(r, k):
        return pltpu.make_async_copy(h_ref.at[pl.ds(r, 1), :],
                                     x_hbm.at[pl.ds(dest_ref[(base + r) * TOP_K + k], 1), :], sem)

    def start(r, c):
        for k in range(TOP_K):
            copy(r, k).start()
        return c

    def wait(r, c):
        for k in range(TOP_K):
            copy(r, k).wait()
        return c

    lax.fori_loop(0, ROW_TILE, start, 0)
    lax.fori_loop(0, ROW_TILE, wait, 0)


def _dispatch(h_all, dest_flat, n_rows):
    m = h_all.shape[0]
    grid_spec = pltpu.PrefetchScalarGridSpec(
        num_scalar_prefetch=1,
        grid=(m // ROW_TILE,),
        in_specs=[pl.BlockSpec((ROW_TILE, D_MODEL), lambda i, d: (i, 0))],
        out_specs=pl.BlockSpec(memory_space=pl.ANY),
        scratch_shapes=[pltpu.SemaphoreType.DMA(())],
    )
    return pl.pallas_call(
        _dispatch_kernel,
        grid_spec=grid_spec,
        out_shape=jax.ShapeDtypeStruct((n_rows, D_MODEL), F32),
        compiler_params=pltpu.CompilerParams(dimension_semantics=("arbitrary",), has_side_effects=True),
        name="moe_dispatch",
    )(dest_flat, h_all)


def _expert_kernel(blk_e, blk_first, blk_valid, blk_ord, exp_list, n_used, n_exp,
                   x_ref, wg_hbm, wu_hbm, wd_hbm, y_ref, wg32, wu32, wd32, wg16, wu16, wd16, sems):
    i = pl.program_id(0)

    def weight_copies(e, slot):
        return [pltpu.make_async_copy(hbm.at[e], buf.at[slot], sems.at[slot, n])
                for n, (hbm, buf) in enumerate(((wg_hbm, wg32), (wu_hbm, wu32), (wd_hbm, wd32)))]

    def start_weights(ordinal, slot):
        for c in weight_copies(exp_list[ordinal], slot):
            c.start()

    @pl.when(i == 0)
    def _():
        start_weights(0, 0)

        @pl.when(n_exp[0] > 1)
        def _():
            start_weights(1, 1)

    @pl.when(i < n_used[0])
    def _():
        @pl.when(blk_first[i] == 1)
        def _():
            ordinal = blk_ord[i]
            slot = ordinal & 1
            for c in weight_copies(blk_e[i], slot):
                c.wait()
            wg16[...] = wg32[slot].astype(BF16)
            wu16[...] = wu32[slot].astype(BF16)
            wd16[...] = wd32[slot].astype(BF16)

            @pl.when(ordinal + 2 < n_exp[0])
            def _():
                start_weights(ordinal + 2, slot)

        row = lax.broadcasted_iota(jnp.int32, x_ref.shape, 0)
        x = jnp.where(row < blk_valid[i], x_ref[...], 0.0).astype(BF16)
        gate = jnp.dot(x, wg16[...], preferred_element_type=F32)
        up = jnp.dot(x, wu16[...], preferred_element_type=F32)
        hid = (jax.nn.silu(gate) * up).astype(BF16)
        y_ref[...] = jnp.dot(hid, wd16[...], preferred_element_type=F32)


def _experts(x_rows, plan, w_gate, w_up, w_down):
    n_rows = x_rows.shape[0]
    n_blocks = n_rows // MOE_BLK
    row_map = lambda i, be, bf, bv, bo, el, nu, ne: (jnp.minimum(i, nu[0] - 1), 0)
    hbm = pl.BlockSpec(memory_space=pl.ANY)
    grid_spec = pltpu.PrefetchScalarGridSpec(
        num_scalar_prefetch=7,
        grid=(n_blocks,),
        in_specs=[pl.BlockSpec((MOE_BLK, D_MODEL), row_map), hbm, hbm, hbm],
        out_specs=pl.BlockSpec((MOE_BLK, D_MODEL), row_map),
        scratch_shapes=[pltpu.VMEM((2, D_MODEL, D_EXPERT), F32), pltpu.VMEM((2, D_MODEL, D_EXPERT), F32),
                        pltpu.VMEM((2, D_EXPERT, D_MODEL), F32),
                        pltpu.VMEM((D_MODEL, D_EXPERT), BF16), pltpu.VMEM((D_MODEL, D_EXPERT), BF16),
                        pltpu.VMEM((D_EXPERT, D_MODEL), BF16),
                        pltpu.SemaphoreType.DMA((2, 3))],
    )
    return pl.pallas_call(
        _expert_kernel,
        grid_spec=grid_spec,
        out_shape=jax.ShapeDtypeStruct((n_rows, D_MODEL), F32),
        compiler_params=_params("arbitrary"),
        name="moe_experts",
    )(*plan, x_rows, w_gate, w_up, w_down)


def _combine_kernel(tok_off, dest_ref, y_hbm, h_ref, gate_ref, g_ref, b_ref, out_ref, ybuf, sems):
    i = pl.program_id(0)
    n_steps = pl.num_programs(0)

    def copy(step, r, k):
        slot = step & 1
        row = dest_ref[(tok_off + step * ROW_TILE + r) * TOP_K + k]
        return pltpu.make_async_copy(y_hbm.at[pl.ds(row, 1), :], ybuf.at[slot, k, pl.ds(r, 1), :], sems.at[slot])

    def gather(step):
        def body(r, c):
            for k in range(TOP_K):
                copy(step, r, k).start()
            return c
        lax.fori_loop(0, ROW_TILE, body, 0)

    @pl.when(i == 0)
    def _():
        gather(i)

    @pl.when(i + 1 < n_steps)
    def _():
        gather(i + 1)

    def wait(r, c):
        for k in range(TOP_K):
            copy(i, r, k).wait()
        return c

    lax.fori_loop(0, ROW_TILE, wait, 0)
    slot = i & 1
    f = gate_ref[:, 0:1] * ybuf[slot, 0] + gate_ref[:, 1:2] * ybuf[slot, 1]
    out_ref[...] = _layer_norm(ALPHA * h_ref[...] + f, g_ref[...], b_ref[...])


def _combine(y_rows, dest_flat, h_all, gates, ln_g, ln_b, tok_off, n_tok):
    blk_off = tok_off // ROW_TILE
    grid_spec = pltpu.PrefetchScalarGridSpec(
        num_scalar_prefetch=1,
        grid=(n_tok // ROW_TILE,),
        in_specs=[pl.BlockSpec(memory_space=pl.ANY),
                  pl.BlockSpec((ROW_TILE, D_MODEL), lambda i, d: (i + blk_off, 0)),
                  pl.BlockSpec((ROW_TILE, LANES), lambda i, d: (i + blk_off, 0)),
                  pl.BlockSpec((1, D_MODEL), lambda i, d: (0, 0)),
                  pl.BlockSpec((1, D_MODEL), lambda i, d: (0, 0))],
        out_specs=pl.BlockSpec((ROW_TILE, D_MODEL), lambda i, d: (i, 0)),
        scratch_shapes=[pltpu.VMEM((2, TOP_K, ROW_TILE, D_MODEL), F32), pltpu.SemaphoreType.DMA((2,))],
    )
    return pl.pallas_call(
        functools.partial(_combine_kernel, tok_off),
        grid_spec=grid_spec,
        out_shape=jax.ShapeDtypeStruct((n_tok, D_MODEL), F32),
        compiler_params=_params("arbitrary"),
        name="combine_ln2",
    )(dest_flat, y_rows, h_all, gates, ln_g, ln_b)


def _moe_plan(idx, counts_f):
    m = idx.shape[0]
    n_asg = m * TOP_K
    experts = idx[:, 0:TOP_K]
    ranks = idx[:, TOP_K:2 * TOP_K]
    counts = counts_f[0, :N_EXPERTS].astype(jnp.int32)
    padded = (counts + MOE_BLK - 1) // MOE_BLK * MOE_BLK
    pad_end = jnp.cumsum(padded)
    pad_start = pad_end - padded
    hot = experts[:, :, None] == jnp.arange(N_EXPERTS, dtype=jnp.int32)
    dest = (jnp.sum(jnp.where(hot, pad_start, 0), axis=-1) + ranks).reshape(n_asg).astype(jnp.int32)
    n_blocks = -(-(n_asg + N_EXPERTS * (MOE_BLK - 1)) // MOE_BLK)
    n_used = jnp.maximum(pad_end[-1] // MOE_BLK, 1).astype(jnp.int32)
    blk = jnp.minimum(jnp.arange(n_blocks, dtype=jnp.int32), n_used - 1)
    blk_e = jnp.minimum(jnp.searchsorted(pad_end, blk * MOE_BLK, side='right'), N_EXPERTS - 1).astype(jnp.int32)
    blk_first = (blk * MOE_BLK == pad_start[blk_e]).astype(jnp.int32)
    blk_valid = jnp.clip(counts[blk_e] - (blk * MOE_BLK - pad_start[blk_e]), 0, MOE_BLK).astype(jnp.int32)
    used = counts > 0
    exp_list = jnp.argsort(jnp.logical_not(used), stable=True).astype(jnp.int32)
    blk_ord = (jnp.cumsum(used.astype(jnp.int32)) - 1)[blk_e].astype(jnp.int32)
    n_exp = jnp.sum(used.astype(jnp.int32)).reshape(1)
    plan = (blk_e, blk_first, blk_valid, blk_ord, exp_list, n_used.reshape(1), n_exp)
    return dest, n_blocks * MOE_BLK, plan


def kernel(x_prompt, x_sample, cache_kv_w128, cache_kv_w512, cache_kv_w2048, state_conv, w_in, conv_w, w_conv_out, w_att_out, w_o, ln1_g, ln1_b, w_router_group, b_router_group, w_router_expert, b_router_expert, w_expert_gate, w_expert_up, w_expert_down, ln2_g, ln2_b):
    bsz, seq, _ = x_prompt.shape
    bd, t_new, _ = x_sample.shape
    m_p = bsz * seq
    m_s = bd * t_new
    m_all = m_p + m_s
    caches = (cache_kv_w128, cache_kv_w512, cache_kv_w2048)
    lyr = 0

    w_in16 = w_in[lyr].astype(BF16)
    wc16 = w_conv_out[lyr].astype(BF16)
    wa16 = w_att_out[lyr].astype(BF16)
    wo16 = w_o[lyr].astype(BF16)
    cw = conv_w[lyr]
    g1, b1 = ln1_g[lyr][None, :], ln1_b[lyr][None, :]
    g2, b2 = ln2_g[lyr][None, :], ln2_b[lyr][None, :]
    w_re = jnp.transpose(w_router_expert[lyr], (1, 0, 2)).reshape(D_MODEL, N_EXPERTS)
    w_r = jnp.pad(jnp.concatenate([w_router_group[lyr], w_re], axis=1),
                  ((0, 0), (0, LANES - N_EXPERT_GROUPS - N_EXPERTS)))
    b_r = jnp.pad(jnp.concatenate([b_router_group[lyr], b_router_expert[lyr].reshape(N_EXPERTS)]),
                  (0, LANES - N_EXPERT_GROUPS - N_EXPERTS))[None, :]

    xp = x_prompt.reshape(m_p, D_MODEL)
    xs = x_sample.reshape(m_s, D_MODEL)

    z_p, qkv_p = _proj_in(xp, w_in16, min(PROJ_TM, m_p))
    p_p, conv_p = _conv_prompt(z_p, cw, bsz, seq)
    o_p = _attn_prompt(qkv_p, bsz, seq)
    h_buf = _mix(p_p, o_p, z_p, xp, wc16, wa16, wo16, g1, b1, min(MIX_TM, m_p), m_all)

    z_s, qkv_s = _proj_in(xs, w_in16, m_s)
    p_s, conv_s = _conv_sample(z_s, cw, state_conv[lyr], t_new)
    q3 = qkv_s.reshape(bd, t_new, QKV_COLS)
    kv5 = qkv_s.reshape(bd, t_new, 3, N_DIL, HEADS_PER_GROUP, HEAD_DIM)
    os_, ls_, bufs_s = [], [], []
    for g, (window, _) in enumerate(DILATED_GROUPS):
        cache = caches[g][lyr].reshape(bd, -1, HEAD_DIM)
        new_kv = kv5[:, :, 1:3, g].reshape(bd, t_new * KV_ROWS, HEAD_DIM)
        buf, o_g, lse_g = _attn_sample(q3, cache, new_kv, g)
        bufs_s.append(buf.reshape(1, bd, window, 2, HEADS_PER_GROUP, HEAD_DIM))
        os_.append(o_g.reshape(m_s, D_GRP))
        ls_.append(lse_g.reshape(m_s, D_GRP))
    o_s = _merge_sample(os_, ls_)
    h_buf = _mix(p_s, o_s, z_s, xs, wc16, wa16, wo16, g1, b1, m_s, m_all, h_all=h_buf, row_off=m_p)

    idx, gates, counts = _router(h_buf, w_r, b_r)
    dest, n_rows, plan = _moe_plan(idx, counts)
    x_rows = _dispatch(h_buf, dest, n_rows)
    y_rows = _experts(x_rows, plan, w_expert_gate[lyr], w_expert_up[lyr], w_expert_down[lyr])
    y_p = _combine(y_rows, dest, h_buf, gates, g2, b2, 0, m_p)
    y_s = _combine(y_rows, dest, h_buf, gates, g2, b2, m_p, m_s)

    qkv_p3 = qkv_p.reshape(bsz, seq, QKV_COLS)
    bufs_p = []
    for g, (w, _) in enumerate(DILATED_GROUPS):
        keep = min(w, seq)
        k_g = qkv_p3[:, seq - keep:, D_ATT + g * D_GRP:D_ATT + (g + 1) * D_GRP]
        v_g = qkv_p3[:, seq - keep:, 2 * D_ATT + g * D_GRP:2 * D_ATT + (g + 1) * D_GRP]
        bufs_p.append(jnp.stack([k_g, v_g], axis=2).reshape(1, bsz, keep, 2, HEADS_PER_GROUP, HEAD_DIM))
    return (y_p.reshape(bsz, seq, D_MODEL), y_s.reshape(bd, t_new, D_MODEL),
            bufs_p[0], bufs_p[1], bufs_p[2], conv_p[None],
            bufs_s[0], bufs_s[1], bufs_s[2], conv_s[None])
```

```python
import functools

import jax
import jax.numpy as jnp
from jax import lax
from jax.experimental import pallas as pl
from jax.experimental.pallas import tpu as pltpu

F32 = jnp.float32
BF16 = jnp.bfloat16

D_MODEL = 2048
D_CONV = D_MODEL // 2
CONV_W = 3
DILATED_GROUPS = ((128, 1), (512, 4), (2048, 16))
N_DIL = len(DILATED_GROUPS)
HEADS_PER_GROUP = 4
HEAD_DIM = 128
SPAN = 128
D_GRP = HEADS_PER_GROUP * HEAD_DIM
D_ATT = N_DIL * D_GRP
N_EXPERT_GROUPS = 8
EXPERTS_PER_GROUP = 8
N_EXPERTS = N_EXPERT_GROUPS * EXPERTS_PER_GROUP
TOP_K = 2
D_EXPERT = D_MODEL // 4
DEPTH = 1
LN_EPS = 1e-5
ALPHA = (2 * DEPTH) ** 0.25
NEG_INF = -1e30
ATT_SCALE = HEAD_DIM ** -0.5

OFF_B = 0
OFF_C = D_CONV
OFF_H = 2 * D_CONV
OFF_Q = 3 * D_CONV
OFF_K = OFF_Q + D_ATT
OFF_V = OFF_K + D_ATT
OFF_GC = OFF_V + D_ATT
OFF_GA = OFF_GC + D_MODEL
N_IN = OFF_GA + D_MODEL

LANES = 128
SUBLANES = 8
VMEM_LIMIT = 56 * 1024 * 1024

PROJ_TN = 512
PROJ_TM = 1024
Z_COLS = 2 * D_MODEL + 3 * D_CONV
QKV_COLS = 3 * D_ATT
NZ_TILES = Z_COLS // PROJ_TN
NQ_TILES = QKV_COLS // PROJ_TN
CONV_TM = 512
MIX_TM = 512
ROW_TILE = 128
ROUTER_TILE = 640
MOE_BLK = 256
ATTN_UNROLL = 4
DISPATCH_TILE = 640
COMBINE_TILE = 256
DMA_UNROLL = 8


def _proj_tile_order():
    t = lambda off, width: list(range(off // PROJ_TN, (off + width) // PROJ_TN))
    z_tiles = t(OFF_GC, D_MODEL) + t(OFF_GA, D_MODEL) + t(OFF_B, D_CONV) + t(OFF_C, D_CONV) + t(OFF_H, D_CONV)
    q_tiles = t(OFF_Q, D_ATT) + t(OFF_K, D_ATT) + t(OFF_V, D_ATT)
    return z_tiles + q_tiles


def _params(*sem):
    return pltpu.CompilerParams(dimension_semantics=sem, vmem_limit_bytes=VMEM_LIMIT)


def _proj_kernel(order_ref, x_ref, w_ref, z_ref, qkv_ref, x16):
    del order_ref
    j = pl.program_id(1)

    @pl.when(j == 0)
    def _():
        x16[...] = x_ref[...].astype(BF16)

    @pl.when(j < NZ_TILES)
    def _():
        z_ref[...] = jnp.dot(x16[...], w_ref[...], preferred_element_type=F32).astype(BF16)

    @pl.when(j >= NZ_TILES)
    def _():
        qkv_ref[...] = jnp.dot(x16[...], w_ref[...], preferred_element_type=F32)


def _proj_in(x, w16, tm):
    m = x.shape[0]
    order = jnp.asarray(_proj_tile_order(), jnp.int32)
    grid_spec = pltpu.PrefetchScalarGridSpec(
        num_scalar_prefetch=1,
        grid=(m // tm, NZ_TILES + NQ_TILES),
        in_specs=[
            pl.BlockSpec((tm, D_MODEL), lambda i, j, o: (i, 0)),
            pl.BlockSpec((D_MODEL, PROJ_TN), lambda i, j, o: (0, o[j])),
        ],
        out_specs=[
            pl.BlockSpec((tm, PROJ_TN), lambda i, j, o: (i, jnp.minimum(j, NZ_TILES - 1))),
            pl.BlockSpec((tm, PROJ_TN), lambda i, j, o: (i, jnp.maximum(j - NZ_TILES, 0))),
        ],
        scratch_shapes=[pltpu.VMEM((tm, D_MODEL), BF16)],
    )
    return pl.pallas_call(
        _proj_kernel,
        grid_spec=grid_spec,
        out_shape=[jax.ShapeDtypeStruct((m, Z_COLS), BF16), jax.ShapeDtypeStruct((m, QKV_COLS), F32)],
        compiler_params=_params("arbitrary", "arbitrary"),
        name="proj_in",
    )(order, x, w16)


ZB_BLK = (2 * D_MODEL) // D_CONV


def _conv_prompt_kernel(zb_ref, zc_ref, zh_ref, cw_ref, p_ref, st_ref, uext):
    tm = zb_ref.shape[0]

    @pl.when(pl.program_id(1) == 0)
    def _():
        uext[0:SUBLANES, :] = jnp.zeros((SUBLANES, D_CONV), F32)

    u = zc_ref[...].astype(F32) * zh_ref[...].astype(F32)
    uext[SUBLANES:SUBLANES + tm, :] = u
    conv = (cw_ref[0:1, :] * uext[SUBLANES - 2:SUBLANES - 2 + tm, :]
            + cw_ref[1:2, :] * uext[SUBLANES - 1:SUBLANES - 1 + tm, :]
            + cw_ref[2:3, :] * u)
    p_ref[...] = (zb_ref[...].astype(F32) * conv).astype(BF16)
    last = uext[tm:tm + SUBLANES, :]
    uext[0:SUBLANES, :] = last
    st_ref[0] = last[SUBLANES - 2:SUBLANES, :]


def _conv_prompt(z, conv_w, bsz, seq):
    tm = min(CONV_TM, seq)
    nsb = seq // tm
    zspec = lambda blk: pl.BlockSpec((tm, D_CONV), lambda b, s: (b * nsb + s, blk))
    return pl.pallas_call(
        _conv_prompt_kernel,
        grid=(bsz, nsb),
        in_specs=[zspec(ZB_BLK), zspec(ZB_BLK + 1), zspec(ZB_BLK + 2),
                  pl.BlockSpec((CONV_W, D_CONV), lambda b, s: (0, 0))],
        out_specs=[pl.BlockSpec((tm, D_CONV), lambda b, s: (b * nsb + s, 0)),
                   pl.BlockSpec((1, CONV_W - 1, D_CONV), lambda b, s: (b, 0, 0))],
        out_shape=[jax.ShapeDtypeStruct((bsz * seq, D_CONV), BF16),
                   jax.ShapeDtypeStruct((bsz, CONV_W - 1, D_CONV), F32)],
        scratch_shapes=[pltpu.VMEM((tm + SUBLANES, D_CONV), F32)],
        compiler_params=_params("arbitrary", "arbitrary"),
        name="conv_prompt",
    )(z, z, z, conv_w)


def _conv_sample_kernel(t_new, zb_ref, zc_ref, zh_ref, cw_ref, s1_ref, s2_ref, p_ref, u_ref, uext):
    m = zb_ref.shape[0]
    u = zc_ref[...].astype(F32) * zh_ref[...].astype(F32)
    u_ref[...] = u
    uext[0:SUBLANES, :] = jnp.zeros((SUBLANES, D_CONV), F32)
    uext[SUBLANES:SUBLANES + m, :] = u
    t = lax.broadcasted_iota(jnp.int32, (m, D_CONV), 0) % t_new
    um1 = jnp.where(t < 1, s1_ref[...], uext[SUBLANES - 1:SUBLANES - 1 + m, :])
    um2 = jnp.where(t < 2, s2_ref[...], uext[SUBLANES - 2:SUBLANES - 2 + m, :])
    conv = cw_ref[0:1, :] * um2 + cw_ref[1:2, :] * um1 + cw_ref[2:3, :] * u
    p_ref[...] = (zb_ref[...].astype(F32) * conv).astype(BF16)


def _conv_sample(z, conv_w, state, t_new):
    m = z.shape[0]
    bd = m // t_new
    s1 = jnp.pad(state[:, 1:2], ((0, 0), (0, t_new - 1), (0, 0))).reshape(m, D_CONV)
    s2 = jnp.pad(state, ((0, 0), (0, t_new - 2), (0, 0))).reshape(m, D_CONV)
    zspec = lambda blk: pl.BlockSpec((m, D_CONV), lambda i: (0, blk))
    full = lambda r: pl.BlockSpec((r, D_CONV), lambda i: (0, 0))
    p, u = pl.pallas_call(
        functools.partial(_conv_sample_kernel, t_new),
        grid=(1,),
        in_specs=[zspec(ZB_BLK), zspec(ZB_BLK + 1), zspec(ZB_BLK + 2), full(CONV_W), full(m), full(m)],
        out_specs=[full(m), full(m)],
        out_shape=[jax.ShapeDtypeStruct((m, D_CONV), BF16), jax.ShapeDtypeStruct((m, D_CONV), F32)],
        scratch_shapes=[pltpu.VMEM((m + SUBLANES, D_CONV), F32)],
        compiler_params=_params("arbitrary"),
        name="conv_sample",
    )(z, z, z, conv_w, s1, s2)
    new_state = u.reshape(bd, t_new, D_CONV)[:, t_new - (CONV_W - 1):]
    return p, new_state


def _merge_groups(o_list, lse_list):
    top = functools.reduce(jnp.maximum, lse_list)
    es = [jnp.exp(l - top) for l in lse_list]
    num = functools.reduce(lambda a, b: a + b, [e * o for e, o in zip(es, o_list)])
    den = functools.reduce(lambda a, b: a + b, es)
    return num / den


def _nt_dot(a, b):
    return lax.dot_general(a, b, (((1,), (1,)), ((), ())), preferred_element_type=F32)


def _attn_prompt_group(q_ref, k_ref, v_ref, o_sc, lse_sc, g, dil, seq):
    shift = dil.bit_length() - 1
    qi = lax.broadcasted_iota(jnp.int32, (SPAN, 2 * SPAN), 0)
    kj = lax.broadcasted_iota(jnp.int32, (SPAN, 2 * SPAN), 1)
    band = jnp.logical_and(kj >= qi, kj <= qi + SPAN)
    is_prev = lax.broadcasted_iota(jnp.int32, (1, 2 * SPAN), 1) < SPAN
    ones = jnp.ones((2 * SPAN, HEAD_DIM), BF16)

    def rows(ref, start):
        if dil == 1:
            return ref[pl.ds(start, SPAN), :]
        return ref[pl.ds(start, SPAN, stride=dil), :]

    def body(trip, carry):
        blocks = range(ATTN_UNROLL)
        idx = [trip * ATTN_UNROLL + u for u in blocks]
        c = [i >> shift for i in idx]
        base = [c[u] * (SPAN * dil) + (idx[u] & (dil - 1)) for u in blocks]
        pbase = [jnp.maximum(c[u] - 1, 0) * (SPAN * dil) + (idx[u] & (dil - 1)) for u in blocks]
        q = [rows(q_ref, base[u]).astype(BF16) for u in blocks]
        k = [jnp.concatenate([rows(k_ref, pbase[u]), rows(k_ref, base[u])], axis=0).astype(BF16) for u in blocks]
        s = [_nt_dot(q[u], k[u]) for u in blocks]
        pen = [jnp.where(jnp.logical_and(is_prev, c[u] == 0), NEG_INF, 0.0) for u in blocks]
        s = [jnp.where(band, s[u] * ATT_SCALE + pen[u], NEG_INF) for u in blocks]
        m = [jnp.max(jnp.maximum(s[u][:, :SPAN], s[u][:, SPAN:]), axis=-1, keepdims=True) for u in blocks]
        p = [jnp.exp(s[u] - m[u]).astype(BF16) for u in blocks]
        v = [jnp.concatenate(
            [jnp.concatenate([rows(v_ref, pbase[u]), rows(v_ref, base[u])], axis=0).astype(BF16), ones], axis=1)
            for u in blocks]
        acc = [jnp.dot(p[u], v[u], preferred_element_type=F32) for u in blocks]
        for u in blocks:
            den = acc[u][:, HEAD_DIM:]
            o = acc[u][:, :HEAD_DIM] / den
            lse = m[u] + jnp.log(den)
            if dil == 1:
                o_sc[g, pl.ds(base[u], SPAN), :] = o
                lse_sc[g, pl.ds(base[u], SPAN), :] = lse
            else:
                o_sc[g, pl.ds(base[u], SPAN, stride=dil), :] = o
                lse_sc[g, pl.ds(base[u], SPAN, stride=dil), :] = lse
        return carry

    lax.fori_loop(0, seq // (SPAN * ATTN_UNROLL), body, 0)


def _attn_prompt_kernel(q_ref, k_ref, v_ref, o_ref, o_sc, lse_sc):
    seq = q_ref.shape[0]
    g_id = pl.program_id(2)
    for g, (_, dil) in enumerate(DILATED_GROUPS):
        @pl.when(g_id == g)
        def _(g=g, dil=dil):
            _attn_prompt_group(q_ref, k_ref, v_ref, o_sc, lse_sc, g, dil, seq)

    @pl.when(g_id == N_DIL - 1)
    def _():
        chunk = 256

        def body(i, carry):
            sl = pl.ds(pl.multiple_of(i * chunk, chunk), chunk)
            o = _merge_groups([o_sc[g, sl, :] for g in range(N_DIL)], [lse_sc[g, sl, :] for g in range(N_DIL)])
            o_ref[sl, :] = o.astype(BF16)
            return carry

        lax.fori_loop(0, seq // chunk, body, 0)


def _attn_prompt(qkv, bsz, seq):
    qkv3 = qkv.reshape(bsz, seq, QKV_COLS)
    per_sel = D_ATT // HEAD_DIM
    spec = lambda sel: pl.BlockSpec((None, seq, HEAD_DIM),
                                    lambda b, j, g: (b, 0, sel * per_sel + g * HEADS_PER_GROUP + j))
    out = pl.pallas_call(
        _attn_prompt_kernel,
        grid=(bsz, HEADS_PER_GROUP, N_DIL),
        in_specs=[spec(0), spec(1), spec(2)],
        out_specs=pl.BlockSpec((None, seq, HEAD_DIM), lambda b, j, g: (b, 0, j)),
        out_shape=jax.ShapeDtypeStruct((bsz, seq, D_GRP), BF16),
        scratch_shapes=[pltpu.VMEM((N_DIL, seq, HEAD_DIM), F32), pltpu.VMEM((N_DIL, seq, HEAD_DIM), F32)],
        compiler_params=_params("arbitrary", "arbitrary", "arbitrary"),
        name="attn_prompt",
    )(qkv3, qkv3, qkv3)
    return out.reshape(bsz * seq, D_GRP)


KV_ROWS = 2 * HEADS_PER_GROUP


def _attn_sample_kernel(dil, q_ref, cache_ref, new_ref, buf_ref, o_ref, lse_ref):
    t_new = q_ref.shape[0]
    n_buf = cache_ref.shape[0] // KV_ROWS
    tq = lax.broadcasted_iota(jnp.int32, (t_new, n_buf), 0)
    pos = lax.broadcasted_iota(jnp.int32, (t_new, n_buf), 1)
    back = n_buf + tq - pos
    cache_ok = jnp.logical_and(back <= dil * SPAN, (back & (dil - 1)) == 0)
    tq2 = lax.broadcasted_iota(jnp.int32, (t_new, t_new), 0)
    tk2 = lax.broadcasted_iota(jnp.int32, (t_new, t_new), 1)
    new_ok = jnp.logical_and(tk2 <= tq2, ((tq2 - tk2) & (dil - 1)) == 0)
    for h in range(HEADS_PER_GROUP):
        cols = slice(h * HEAD_DIM, (h + 1) * HEAD_DIM)
        q = q_ref[:, cols].astype(BF16)
        k_c = cache_ref[pl.ds(h, n_buf, stride=KV_ROWS), :].astype(BF16)
        v_c = cache_ref[pl.ds(HEADS_PER_GROUP + h, n_buf, stride=KV_ROWS), :].astype(BF16)
        k_n = new_ref[pl.ds(h, t_new, stride=KV_ROWS), :].astype(BF16)
        v_n = new_ref[pl.ds(HEADS_PER_GROUP + h, t_new, stride=KV_ROWS), :].astype(BF16)
        s_c = jnp.where(cache_ok, _nt_dot(q, k_c) * ATT_SCALE, NEG_INF)
        s_n = jnp.where(new_ok, _nt_dot(q, k_n) * ATT_SCALE, NEG_INF)
        m = jnp.maximum(jnp.max(s_c, axis=-1, keepdims=True), jnp.max(s_n, axis=-1, keepdims=True))
        p_c = jnp.exp(s_c - m)
        p_n = jnp.exp(s_n - m)
        den = jnp.sum(p_c, axis=-1, keepdims=True) + jnp.sum(p_n, axis=-1, keepdims=True)
        acc = (jnp.dot(p_c.astype(BF16), v_c, preferred_element_type=F32)
               + jnp.dot(p_n.astype(BF16), v_n, preferred_element_type=F32))
        o_ref[:, cols] = acc / den
        lse_ref[:, cols] = jnp.broadcast_to(m + jnp.log(den), (t_new, HEAD_DIM))
    keep = (n_buf - t_new) * KV_ROWS
    buf_ref[0:keep, :] = cache_ref[t_new * KV_ROWS:, :]
    buf_ref[keep:, :] = new_ref[...]


def _attn_sample(q3, cache, new_kv, g):
    window, dil = DILATED_GROUPS[g]
    bd, t_new, _ = q3.shape
    rows = cache.shape[1]
    assert rows == window * KV_ROWS, "the cache must hold a full window"
    row_spec = lambda r: pl.BlockSpec((None, r, HEAD_DIM), lambda b: (b, 0, 0))
    tok_spec = pl.BlockSpec((None, t_new, D_GRP), lambda b: (b, 0, 0))
    return pl.pallas_call(
        functools.partial(_attn_sample_kernel, dil),
        grid=(bd,),
        in_specs=[pl.BlockSpec((None, t_new, D_GRP), lambda b: (b, 0, g)), row_spec(rows),
                  row_spec(t_new * KV_ROWS)],
        out_specs=[row_spec(rows), tok_spec, tok_spec],
        out_shape=[jax.ShapeDtypeStruct((bd, rows, HEAD_DIM), F32),
                   jax.ShapeDtypeStruct((bd, t_new, D_GRP), F32),
                   jax.ShapeDtypeStruct((bd, t_new, D_GRP), F32)],
        compiler_params=_params("arbitrary"),
        name=f"attn_sample_w{window}",
    )(q3, cache, new_kv)


def _merge_sample_kernel(o0, o1, o2, l0, l1, l2, out_ref):
    out_ref[...] = _merge_groups([o0[...], o1[...], o2[...]], [l0[...], l1[...], l2[...]]).astype(BF16)


def _merge_sample(os_, ls_):
    m = os_[0].shape[0]
    spec = pl.BlockSpec((m, D_GRP), lambda i: (0, 0))
    return pl.pallas_call(
        _merge_sample_kernel,
        grid=(1,),
        in_specs=[spec] * 6,
        out_specs=spec,
        out_shape=jax.ShapeDtypeStruct((m, D_GRP), BF16),
        compiler_params=_params("arbitrary"),
        name="merge_sample",
    )(*os_, *ls_)


def _layer_norm(v, g, b):
    mu = jnp.mean(v, axis=-1, keepdims=True)
    d = v - mu
    var = jnp.mean(d * d, axis=-1, keepdims=True)
    return d * lax.rsqrt(var + LN_EPS) * g + b


SLAB_ROWS = SUBLANES
HALF = D_MODEL // 2
HI_MASK = 0xFFFF0000


def _bf16_bits(v):
    return pltpu.bitcast(v.astype(BF16).astype(F32), jnp.uint32)


def _pack_slabs(src_ref, slab_ref, n_tok):
    def body(tg, carry):
        t0 = pl.multiple_of(tg * SUBLANES, SUBLANES)
        for s in range(SLAB_ROWS):
            lo = src_ref[pl.ds(t0, SUBLANES), s * LANES:(s + 1) * LANES]
            hi = src_ref[pl.ds(t0, SUBLANES), HALF + s * LANES:HALF + (s + 1) * LANES]
            word = (_bf16_bits(lo) >> 16) | _bf16_bits(hi)
            slab_ref[pl.ds(t0 * SLAB_ROWS + s, SUBLANES, stride=SLAB_ROWS), :] = word
        return carry
    lax.fori_loop(0, n_tok // SUBLANES, body, 0)


def _unpack_slabs(slab_ref, dst_ref, n_tok):
    def body(tg, carry):
        t0 = pl.multiple_of(tg * SUBLANES, SUBLANES)
        for s in range(SLAB_ROWS):
            word = slab_ref[pl.ds(t0 * SLAB_ROWS + s, SUBLANES, stride=SLAB_ROWS), :]
            dst_ref[pl.ds(t0, SUBLANES), s * LANES:(s + 1) * LANES] = pltpu.bitcast(word << 16, F32)
            dst_ref[pl.ds(t0, SUBLANES), HALF + s * LANES:HALF + (s + 1) * LANES] = pltpu.bitcast(
                word & jnp.uint32(HI_MASK), F32)
        return carry
    lax.fori_loop(0, n_tok // SUBLANES, body, 0)


def _mix_kernel(p_ref, o_ref, gc_ref, ga_ref, x_ref, wc_ref, wa_ref, wo_ref, g_ref, b_ref, *rest):
    h_ref, hs_ref = rest[-2:]
    y_conv = jnp.dot(p_ref[...], wc_ref[...], preferred_element_type=F32)
    y_att = jnp.dot(o_ref[...], wa_ref[...], preferred_element_type=F32)
    merged = (jax.nn.sigmoid(gc_ref[...].astype(F32)) * y_conv
              + jax.nn.sigmoid(ga_ref[...].astype(F32)) * y_att)
    mix = jnp.dot(merged.astype(BF16), wo_ref[...], preferred_element_type=F32)
    h_ref[...] = _layer_norm(ALPHA * x_ref[...] + mix, g_ref[...], b_ref[...])
    _pack_slabs(h_ref, hs_ref, h_ref.shape[0])


def _mix(p, o, z, x, wc16, wa16, wo16, ln_g, ln_b, tm, total, h_all=None, row_off=0):
    m = p.shape[0]
    assert row_off % tm == 0
    blk_off = row_off // tm
    row = lambda w: pl.BlockSpec((tm, w), lambda i: (i, 0))
    const = lambda a: pl.BlockSpec(a.shape, lambda i: (0, 0), pipeline_mode=pl.Buffered(1))
    in_specs = [row(D_CONV), row(D_GRP),
                pl.BlockSpec((tm, D_MODEL), lambda i: (i, 0)), pl.BlockSpec((tm, D_MODEL), lambda i: (i, 1)),
                row(D_MODEL), const(wc16), const(wa16), const(wo16), const(ln_g), const(ln_b)]
    args = [p, o, z, z, x, wc16, wa16, wo16, ln_g, ln_b]
    aliases = {}
    if h_all is not None:
        in_specs += [pl.BlockSpec(memory_space=pl.ANY)] * 2
        args += list(h_all)
        aliases = {len(args) - 2: 0, len(args) - 1: 1}
    return pl.pallas_call(
        _mix_kernel,
        grid=(m // tm,),
        in_specs=in_specs,
        out_specs=[pl.BlockSpec((tm, D_MODEL), lambda i: (i + blk_off, 0)),
                   pl.BlockSpec((tm * SLAB_ROWS, LANES), lambda i: (i + blk_off, 0))],
        out_shape=[jax.ShapeDtypeStruct((total, D_MODEL), F32),
                   jax.ShapeDtypeStruct((total * SLAB_ROWS, LANES), jnp.uint32)],
        input_output_aliases=aliases,
        compiler_params=_params("arbitrary"),
        name="mix_ln1",
    )(*args)


def _router_kernel(h_ref, w_ref, b_ref, idx_ref, gate_ref, cnt_ref, carry):
    tile = h_ref.shape[0]

    @pl.when(pl.program_id(0) == 0)
    def _():
        carry[...] = jnp.zeros(carry.shape, F32)

    logits = jnp.dot(h_ref[...], w_ref[...], preferred_element_type=F32,
                     precision=lax.Precision.HIGHEST) + b_ref[...]
    lane = lax.broadcasted_iota(jnp.int32, (tile, LANES), 1)
    lane_f = lane.astype(F32)
    first_max = lambda v, mx: jnp.min(jnp.where(v == mx, lane_f, float(LANES)), axis=-1, keepdims=True)

    is_grp = lane < N_EXPERT_GROUPS
    lg = jnp.where(is_grp, logits, -jnp.inf)
    mg = jnp.max(lg, axis=-1, keepdims=True)
    gsel = first_max(lg, mg).astype(jnp.int32)
    p_group = 1.0 / jnp.sum(jnp.where(is_grp, jnp.exp(logits - mg), 0.0), axis=-1, keepdims=True)

    e_lane = lane - N_EXPERT_GROUPS
    in_grp = jnp.logical_and(e_lane >= 0, (e_lane >> 3) == gsel)
    le = jnp.where(in_grp, logits, -jnp.inf)
    v1 = jnp.max(le, axis=-1, keepdims=True)
    i1 = first_max(le, v1).astype(jnp.int32)
    le2 = jnp.where(lane == i1, -jnp.inf, le)
    v2 = jnp.max(le2, axis=-1, keepdims=True)
    i2 = first_max(le2, v2).astype(jnp.int32)
    ex = jnp.exp(v2 - v1)
    g1 = p_group / (1.0 + ex)
    g2 = p_group * ex / (1.0 + ex)
    e1 = i1 - N_EXPERT_GROUPS
    e2 = i2 - N_EXPERT_GROUPS

    hot1 = lane == e1
    hot2 = lane == e2
    hot = jnp.where(jnp.logical_or(hot1, hot2), 1.0, 0.0)
    rr = lax.broadcasted_iota(jnp.int32, (tile, tile), 0)
    cc = lax.broadcasted_iota(jnp.int32, (tile, tile), 1)
    tri = jnp.where(cc < rr, 1.0, 0.0).astype(BF16)
    before = jnp.dot(tri, hot.astype(BF16), preferred_element_type=F32) + carry[0:1, :]
    r1 = jnp.sum(jnp.where(hot1, before, 0.0), axis=-1, keepdims=True).astype(jnp.int32)
    r2 = jnp.sum(jnp.where(hot2, before, 0.0), axis=-1, keepdims=True).astype(jnp.int32)
    carry[...] = carry[...] + jnp.sum(hot, axis=0, keepdims=True)

    idx_ref[...] = jnp.where(lane == 0, e1, jnp.where(lane == 1, e2, jnp.where(lane == 2, r1, r2)))
    gate_ref[...] = jnp.where(lane == 0, g1, g2)
    cnt_ref[...] = carry[...]


def _router(h_all, w_r, b_r):
    m = h_all.shape[0]
    tile = ROUTER_TILE if m % ROUTER_TILE == 0 else ROW_TILE
    row = pl.BlockSpec((tile, LANES), lambda i: (i, 0))
    return pl.pallas_call(
        _router_kernel,
        grid=(m // tile,),
        in_specs=[pl.BlockSpec((tile, D_MODEL), lambda i: (i, 0)),
                  pl.BlockSpec((D_MODEL, LANES), lambda i: (0, 0)),
                  pl.BlockSpec((1, LANES), lambda i: (0, 0))],
        out_specs=[row, row, pl.BlockSpec((SUBLANES, LANES), lambda i: (0, 0))],
        out_shape=[jax.ShapeDtypeStruct((m, LANES), jnp.int32), jax.ShapeDtypeStruct((m, LANES), F32),
                   jax.ShapeDtypeStruct((SUBLANES, LANES), F32)],
        scratch_shapes=[pltpu.VMEM((SUBLANES, LANES), F32)],
        compiler_params=_params("arbitrary"),
        name="router",
    )(h_all, w_r, b_r)


def _slab(ref, row):
    return ref.at[pl.ds(pl.multiple_of(row * SLAB_ROWS, SLAB_ROWS), SLAB_ROWS), :]


def _dispatch_kernel(tile, dest_ref, h_ref, x_hbm, sem):
    base = pl.program_id(0) * tile

    def copy(r, k):
        return pltpu.make_async_copy(_slab(h_ref, r), _slab(x_hbm, dest_ref[(base + r) * TOP_K + k]), sem)

    def start(r, c):
        for k in range(TOP_K):
            copy(r, k).start()
        return c

    def wait(r, c):
        for k in range(TOP_K):
            copy(r, k).wait()
        return c

    lax.fori_loop(0, tile, start, 0, unroll=DMA_UNROLL)
    lax.fori_loop(0, tile, wait, 0, unroll=DMA_UNROLL)


def _dispatch(h_slabs, dest_flat, n_rows):
    m = h_slabs.shape[0] // SLAB_ROWS
    tile = DISPATCH_TILE if m % DISPATCH_TILE == 0 else ROW_TILE
    grid_spec = pltpu.PrefetchScalarGridSpec(
        num_scalar_prefetch=1,
        grid=(m // tile,),
        in_specs=[pl.BlockSpec((tile * SLAB_ROWS, LANES), lambda i, d: (i, 0))],
        out_specs=pl.BlockSpec(memory_space=pl.ANY),
        scratch_shapes=[pltpu.SemaphoreType.DMA(())],
    )
    return pl.pallas_call(
        functools.partial(_dispatch_kernel, tile),
        grid_spec=grid_spec,
        out_shape=jax.ShapeDtypeStruct((n_rows * SLAB_ROWS, LANES), jnp.uint32),
        compiler_params=pltpu.CompilerParams(dimension_semantics=("arbitrary",), has_side_effects=True),
        name="moe_dispatch",
    )(dest_flat, h_slabs)


def _expert_kernel(blk_e, blk_first, blk_valid, blk_ord, exp_list, n_used, n_exp,
                   x_ref, wg_hbm, wu_hbm, wd_hbm, y_ref, wg32, wu32, wd32, wg16, wu16, wd16, sems, rows32):
    i = pl.program_id(0)

    def weight_copies(e, slot):
        return [pltpu.make_async_copy(hbm.at[e], buf.at[slot], sems.at[slot, n])
                for n, (hbm, buf) in enumerate(((wg_hbm, wg32), (wu_hbm, wu32), (wd_hbm, wd32)))]

    def start_weights(ordinal, slot):
        for c in weight_copies(exp_list[ordinal], slot):
            c.start()

    @pl.when(i == 0)
    def _():
        start_weights(0, 0)

        @pl.when(n_exp[0] > 1)
        def _():
            start_weights(1, 1)

    @pl.when(i < n_used[0])
    def _():
        @pl.when(blk_first[i] == 1)
        def _():
            ordinal = blk_ord[i]
            slot = ordinal & 1
            for c in weight_copies(blk_e[i], slot):
                c.wait()
            wg16[...] = wg32[slot].astype(BF16)
            wu16[...] = wu32[slot].astype(BF16)
            wd16[...] = wd32[slot].astype(BF16)

            @pl.when(ordinal + 2 < n_exp[0])
            def _():
                start_weights(ordinal + 2, slot)

        _unpack_slabs(x_ref, rows32, MOE_BLK)
        row = lax.broadcasted_iota(jnp.int32, rows32.shape, 0)
        x = jnp.where(row < blk_valid[i], rows32[...], 0.0).astype(BF16)
        gate = jnp.dot(x, wg16[...], preferred_element_type=F32)
        up = jnp.dot(x, wu16[...], preferred_element_type=F32)
        hid = (jax.nn.silu(gate) * up).astype(BF16)
        rows32[...] = jnp.dot(hid, wd16[...], preferred_element_type=F32)
        _pack_slabs(rows32, y_ref, MOE_BLK)


def _experts(x_slabs, plan, w_gate, w_up, w_down):
    n_rows = x_slabs.shape[0] // SLAB_ROWS
    n_blocks = n_rows // MOE_BLK
    row_map = lambda i, be, bf, bv, bo, el, nu, ne: (jnp.minimum(i, nu[0] - 1), 0)
    hbm = pl.BlockSpec(memory_space=pl.ANY)
    grid_spec = pltpu.PrefetchScalarGridSpec(
        num_scalar_prefetch=7,
        grid=(n_blocks,),
        in_specs=[pl.BlockSpec((MOE_BLK * SLAB_ROWS, LANES), row_map), hbm, hbm, hbm],
        out_specs=pl.BlockSpec((MOE_BLK * SLAB_ROWS, LANES), row_map),
        scratch_shapes=[pltpu.VMEM((2, D_MODEL, D_EXPERT), F32), pltpu.VMEM((2, D_MODEL, D_EXPERT), F32),
                        pltpu.VMEM((2, D_EXPERT, D_MODEL), F32),
                        pltpu.VMEM((D_MODEL, D_EXPERT), BF16), pltpu.VMEM((D_MODEL, D_EXPERT), BF16),
                        pltpu.VMEM((D_EXPERT, D_MODEL), BF16),
                        pltpu.SemaphoreType.DMA((2, 3)),
                        pltpu.VMEM((MOE_BLK, D_MODEL), F32)],
    )
    return pl.pallas_call(
        _expert_kernel,
        grid_spec=grid_spec,
        out_shape=jax.ShapeDtypeStruct((n_rows * SLAB_ROWS, LANES), jnp.uint32),
        compiler_params=_params("arbitrary"),
        name="moe_experts",
    )(*plan, x_slabs, w_gate, w_up, w_down)


def _combine_kernel(tok_off, tile, dest_ref, y_hbm, h_ref, gate_ref, g_ref, b_ref, out_ref, ybuf, y32, sems):
    i = pl.program_id(0)
    n_steps = pl.num_programs(0)

    def copy(step, r, k):
        slot = step & 1
        row = dest_ref[(tok_off + step * tile + r) * TOP_K + k]
        return pltpu.make_async_copy(_slab(y_hbm, row), _slab(ybuf.at[slot, k], r), sems.at[slot])

    def gather(step):
        def body(r, c):
            for k in range(TOP_K):
                copy(step, r, k).start()
            return c
        lax.fori_loop(0, tile, body, 0, unroll=DMA_UNROLL)

    @pl.when(i == 0)
    def _():
        gather(i)

    @pl.when(i + 1 < n_steps)
    def _():
        gather(i + 1)

    def wait(r, c):
        for k in range(TOP_K):
            copy(i, r, k).wait()
        return c

    lax.fori_loop(0, tile, wait, 0, unroll=DMA_UNROLL)
    slot = i & 1
    for k in range(TOP_K):
        _unpack_slabs(ybuf.at[slot, k], y32.at[k], tile)
    f = gate_ref[:, 0:1] * y32[0] + gate_ref[:, 1:2] * y32[1]
    out_ref[...] = _layer_norm(ALPHA * h_ref[...] + f, g_ref[...], b_ref[...])


def _combine(y_slabs, dest_flat, h_all, gates, ln_g, ln_b, tok_off, n_tok):
    tile = COMBINE_TILE if (n_tok % COMBINE_TILE == 0 and tok_off % COMBINE_TILE == 0) else ROW_TILE
    blk_off = tok_off // tile
    grid_spec = pltpu.PrefetchScalarGridSpec(
        num_scalar_prefetch=1,
        grid=(n_tok // tile,),
        in_specs=[pl.BlockSpec(memory_space=pl.ANY),
                  pl.BlockSpec((tile, D_MODEL), lambda i, d: (i + blk_off, 0)),
                  pl.BlockSpec((tile, LANES), lambda i, d: (i + blk_off, 0)),
                  pl.BlockSpec((1, D_MODEL), lambda i, d: (0, 0)),
                  pl.BlockSpec((1, D_MODEL), lambda i, d: (0, 0))],
        out_specs=pl.BlockSpec((tile, D_MODEL), lambda i, d: (i, 0)),
        scratch_shapes=[pltpu.VMEM((2, TOP_K, tile * SLAB_ROWS, LANES), jnp.uint32),
                        pltpu.VMEM((TOP_K, tile, D_MODEL), F32), pltpu.SemaphoreType.DMA((2,))],
    )
    return pl.pallas_call(
        functools.partial(_combine_kernel, tok_off, tile),
        grid_spec=grid_spec,
        out_shape=jax.ShapeDtypeStruct((n_tok, D_MODEL), F32),
        compiler_params=_params("arbitrary"),
        name="combine_ln2",
    )(dest_flat, y_slabs, h_all, gates, ln_g, ln_b)


def _moe_plan(idx, counts_f):
    m = idx.shape[0]
    n_asg = m * TOP_K
    experts = idx[:, 0:TOP_K]
    ranks = idx[:, TOP_K:2 * TOP_K]
    counts = counts_f[0, :N_EXPERTS].astype(jnp.int32)
    padded = (counts + MOE_BLK - 1) // MOE_BLK * MOE_BLK
    pad_end = jnp.cumsum(padded)
    pad_start = pad_end - padded
    hot = experts[:, :, None] == jnp.arange(N_EXPERTS, dtype=jnp.int32)
    dest = (jnp.sum(jnp.where(hot, pad_start, 0), axis=-1) + ranks).reshape(n_asg).astype(jnp.int32)
    n_blocks = -(-(n_asg + N_EXPERTS * (MOE_BLK - 1)) // MOE_BLK)
    n_used = jnp.maximum(pad_end[-1] // MOE_BLK, 1).astype(jnp.int32)
    blk = jnp.minimum(jnp.arange(n_blocks, dtype=jnp.int32), n_used - 1)
    blk_e = jnp.minimum(jnp.searchsorted(pad_end, blk * MOE_BLK, side='right'), N_EXPERTS - 1).astype(jnp.int32)
    blk_first = (blk * MOE_BLK == pad_start[blk_e]).astype(jnp.int32)
    blk_valid = jnp.clip(counts[blk_e] - (blk * MOE_BLK - pad_start[blk_e]), 0, MOE_BLK).astype(jnp.int32)
    used = counts > 0
    exp_list = jnp.argsort(jnp.logical_not(used), stable=True).astype(jnp.int32)
    blk_ord = (jnp.cumsum(used.astype(jnp.int32)) - 1)[blk_e].astype(jnp.int32)
    n_exp = jnp.sum(used.astype(jnp.int32)).reshape(1)
    plan = (blk_e, blk_first, blk_valid, blk_ord, exp_list, n_used.reshape(1), n_exp)
    return dest, n_blocks * MOE_BLK, plan


def kernel(x_prompt, x_sample, cache_kv_w128, cache_kv_w512, cache_kv_w2048, state_conv, w_in, conv_w, w_conv_out, w_att_out, w_o, ln1_g, ln1_b, w_router_group, b_router_group, w_router_expert, b_router_expert, w_expert_gate, w_expert_up, w_expert_down, ln2_g, ln2_b):
    bsz, seq, _ = x_prompt.shape
    bd, t_new, _ = x_sample.shape
    m_p = bsz * seq
    m_s = bd * t_new
    m_all = m_p + m_s
    caches = (cache_kv_w128, cache_kv_w512, cache_kv_w2048)
    lyr = 0

    w_in16 = w_in[lyr].astype(BF16)
    wc16 = w_conv_out[lyr].astype(BF16)
    wa16 = w_att_out[lyr].astype(BF16)
    wo16 = w_o[lyr].astype(BF16)
    cw = conv_w[lyr]
    g1, b1 = ln1_g[lyr][None, :], ln1_b[lyr][None, :]
    g2, b2 = ln2_g[lyr][None, :], ln2_b[lyr][None, :]
    w_re = jnp.transpose(w_router_expert[lyr], (1, 0, 2)).reshape(D_MODEL, N_EXPERTS)
    w_r = jnp.pad(jnp.concatenate([w_router_group[lyr], w_re], axis=1),
                  ((0, 0), (0, LANES - N_EXPERT_GROUPS - N_EXPERTS)))
    b_r = jnp.pad(jnp.concatenate([b_router_group[lyr], b_router_expert[lyr].reshape(N_EXPERTS)]),
                  (0, LANES - N_EXPERT_GROUPS - N_EXPERTS))[None, :]

    xp = x_prompt.reshape(m_p, D_MODEL)
    xs = x_sample.reshape(m_s, D_MODEL)

    z_p, qkv_p = _proj_in(xp, w_in16, min(PROJ_TM, m_p))
    p_p, conv_p = _conv_prompt(z_p, cw, bsz, seq)
    o_p = _attn_prompt(qkv_p, bsz, seq)
    h_buf = _mix(p_p, o_p, z_p, xp, wc16, wa16, wo16, g1, b1, min(MIX_TM, m_p), m_all)

    z_s, qkv_s = _proj_in(xs, w_in16, m_s)
    p_s, conv_s = _conv_sample(z_s, cw, state_conv[lyr], t_new)
    q3 = qkv_s.reshape(bd, t_new, QKV_COLS)
    kv5 = qkv_s.reshape(bd, t_new, 3, N_DIL, HEADS_PER_GROUP, HEAD_DIM)
    os_, ls_, bufs_s = [], [], []
    for g, (window, _) in enumerate(DILATED_GROUPS):
        cache = caches[g][lyr].reshape(bd, -1, HEAD_DIM)
        new_kv = kv5[:, :, 1:3, g].reshape(bd, t_new * KV_ROWS, HEAD_DIM)
        buf, o_g, lse_g = _attn_sample(q3, cache, new_kv, g)
        bufs_s.append(buf.reshape(1, bd, window, 2, HEADS_PER_GROUP, HEAD_DIM))
        os_.append(o_g.reshape(m_s, D_GRP))
        ls_.append(lse_g.reshape(m_s, D_GRP))
    o_s = _merge_sample(os_, ls_)
    h_buf = _mix(p_s, o_s, z_s, xs, wc16, wa16, wo16, g1, b1, m_s, m_all, h_all=h_buf, row_off=m_p)

    h_all, h_slabs = h_buf
    idx, gates, counts = _router(h_all, w_r, b_r)
    dest, n_rows, plan = _moe_plan(idx, counts)
    x_slabs = _dispatch(h_slabs, dest, n_rows)
    y_slabs = _experts(x_slabs, plan, w_expert_gate[lyr], w_expert_up[lyr], w_expert_down[lyr])
    y_p = _combine(y_slabs, dest, h_all, gates, g2, b2, 0, m_p)
    y_s = _combine(y_slabs, dest, h_all, gates, g2, b2, m_p, m_s)

    qkv_p3 = qkv_p.reshape(bsz, seq, QKV_COLS)
    bufs_p = []
    for g, (w, _) in enumerate(DILATED_GROUPS):
        keep = min(w, seq)
        k_g = qkv_p3[:, seq - keep:, D_ATT + g * D_GRP:D_ATT + (g + 1) * D_GRP]
        v_g = qkv_p3[:, seq - keep:, 2 * D_ATT + g * D_GRP:2 * D_ATT + (g + 1) * D_GRP]
        bufs_p.append(jnp.stack([k_g, v_g], axis=2).reshape(1, bsz, keep, 2, HEADS_PER_GROUP, HEAD_DIM))
    return (y_p.reshape(bsz, seq, D_MODEL), y_s.reshape(bd, t_new, D_MODEL),
            bufs_p[0], bufs_p[1], bufs_p[2], conv_p[None],
            bufs_s[0], bufs_s[1], bufs_s[2], conv_s[None])
```

```python
import functools

import jax
import jax.numpy as jnp
from jax import lax
from jax.experimental import pallas as pl
from jax.experimental.pallas import tpu as pltpu

F32 = jnp.float32
BF16 = jnp.bfloat16

D_MODEL = 2048
D_CONV = D_MODEL // 2
CONV_W = 3
DILATED_GROUPS = ((128, 1), (512, 4), (2048, 16))
N_DIL = len(DILATED_GROUPS)
HEADS_PER_GROUP = 4
HEAD_DIM = 128
SPAN = 128
D_GRP = HEADS_PER_GROUP * HEAD_DIM
D_ATT = N_DIL * D_GRP
N_EXPERT_GROUPS = 8
EXPERTS_PER_GROUP = 8
N_EXPERTS = N_EXPERT_GROUPS * EXPERTS_PER_GROUP
TOP_K = 2
D_EXPERT = D_MODEL // 4
DEPTH = 1
LN_EPS = 1e-5
ALPHA = (2 * DEPTH) ** 0.25
NEG_INF = -1e30
ATT_SCALE = HEAD_DIM ** -0.5

OFF_B = 0
OFF_C = D_CONV
OFF_H = 2 * D_CONV
OFF_Q = 3 * D_CONV
OFF_K = OFF_Q + D_ATT
OFF_V = OFF_K + D_ATT
OFF_GC = OFF_V + D_ATT
OFF_GA = OFF_GC + D_MODEL
N_IN = OFF_GA + D_MODEL

LANES = 128
SUBLANES = 8
VMEM_LIMIT = 56 * 1024 * 1024

PROJ_TN = 512
PROJ_TM = 1024
Z_COLS = 2 * D_MODEL + 3 * D_CONV
QKV_COLS = 3 * D_ATT
NZ_TILES = Z_COLS // PROJ_TN
NQ_TILES = QKV_COLS // PROJ_TN
CONV_TM = 512
MIX_TM = 512
ROW_TILE = 128
ROUTER_TILE = 640
MOE_BLK = 256
ATTN_UNROLL = 4
DISPATCH_TILE = 640
COMBINE_TILE = 256
DMA_UNROLL = 8


def _proj_tile_order():
    t = lambda off, width: list(range(off // PROJ_TN, (off + width) // PROJ_TN))
    z_tiles = t(OFF_GC, D_MODEL) + t(OFF_GA, D_MODEL) + t(OFF_B, D_CONV) + t(OFF_C, D_CONV) + t(OFF_H, D_CONV)
    q_tiles = t(OFF_Q, D_ATT) + t(OFF_K, D_ATT) + t(OFF_V, D_ATT)
    return z_tiles + q_tiles


def _params(*sem):
    return pltpu.CompilerParams(dimension_semantics=sem, vmem_limit_bytes=VMEM_LIMIT)


def _proj_kernel(n_shift, drop, order_ref, x_ref, w_ref, *rest):
    del order_ref
    srcs = rest[:n_shift]
    z_ref, qkv_ref = rest[n_shift:n_shift + 2]
    dsts = rest[n_shift + 2:2 * n_shift + 2]
    x16 = rest[2 * n_shift + 2]
    i = pl.program_id(0)
    j = pl.program_id(1)

    def shift_copy(n):
        keep = srcs[n].shape[1] - drop
        return pltpu.make_async_copy(srcs[n].at[:, pl.ds(drop, keep), :], dsts[n].at[:, pl.ds(0, keep), :],
                                     rest[-1].at[n])

    if n_shift:
        @pl.when(jnp.logical_and(i == 0, j == 0))
        def _():
            for n in range(n_shift):
                shift_copy(n).start()

        @pl.when(jnp.logical_and(i == pl.num_programs(0) - 1, j == pl.num_programs(1) - 1))
        def _():
            for n in range(n_shift):
                shift_copy(n).wait()

    @pl.when(j == 0)
    def _():
        x16[...] = x_ref[...].astype(BF16)

    @pl.when(j < NZ_TILES)
    def _():
        z_ref[...] = jnp.dot(x16[...], w_ref[...], preferred_element_type=F32).astype(BF16)

    @pl.when(j >= NZ_TILES)
    def _():
        qkv_ref[...] = jnp.dot(x16[...], w_ref[...], preferred_element_type=F32)


def _proj_in(x, w16, tm, shift_srcs=(), drop=0):
    m = x.shape[0]
    n_shift = len(shift_srcs)
    order = jnp.asarray(_proj_tile_order(), jnp.int32)
    hbm = pl.BlockSpec(memory_space=pl.ANY)
    grid_spec = pltpu.PrefetchScalarGridSpec(
        num_scalar_prefetch=1,
        grid=(m // tm, NZ_TILES + NQ_TILES),
        in_specs=[
            pl.BlockSpec((tm, D_MODEL), lambda i, j, o: (i, 0)),
            pl.BlockSpec((D_MODEL, PROJ_TN), lambda i, j, o: (0, o[j])),
        ] + [hbm] * n_shift,
        out_specs=[
            pl.BlockSpec((tm, PROJ_TN), lambda i, j, o: (i, jnp.minimum(j, NZ_TILES - 1))),
            pl.BlockSpec((tm, PROJ_TN), lambda i, j, o: (i, jnp.maximum(j - NZ_TILES, 0))),
        ] + [hbm] * n_shift,
        scratch_shapes=[pltpu.VMEM((tm, D_MODEL), BF16)] + ([pltpu.SemaphoreType.DMA((n_shift,))] if n_shift else []),
    )
    outs = pl.pallas_call(
        functools.partial(_proj_kernel, n_shift, drop),
        grid_spec=grid_spec,
        out_shape=[jax.ShapeDtypeStruct((m, Z_COLS), BF16), jax.ShapeDtypeStruct((m, QKV_COLS), F32)]
        + [jax.ShapeDtypeStruct(s.shape, s.dtype) for s in shift_srcs],
        compiler_params=_params("arbitrary", "arbitrary"),
        name="proj_in",
    )(order, x, w16, *shift_srcs)
    return outs[0], outs[1], outs[2:]


ZB_BLK = (2 * D_MODEL) // D_CONV


def _conv_prompt_kernel(zb_ref, zc_ref, zh_ref, cw_ref, p_ref, st_ref, uext):
    tm = zb_ref.shape[0]

    @pl.when(pl.program_id(1) == 0)
    def _():
        uext[0:SUBLANES, :] = jnp.zeros((SUBLANES, D_CONV), F32)

    u = zc_ref[...].astype(F32) * zh_ref[...].astype(F32)
    uext[SUBLANES:SUBLANES + tm, :] = u
    conv = (cw_ref[0:1, :] * uext[SUBLANES - 2:SUBLANES - 2 + tm, :]
            + cw_ref[1:2, :] * uext[SUBLANES - 1:SUBLANES - 1 + tm, :]
            + cw_ref[2:3, :] * u)
    p_ref[...] = (zb_ref[...].astype(F32) * conv).astype(BF16)
    last = uext[tm:tm + SUBLANES, :]
    uext[0:SUBLANES, :] = last
    st_ref[0] = last[SUBLANES - 2:SUBLANES, :]


def _conv_prompt(z, conv_w, bsz, seq):
    tm = min(CONV_TM, seq)
    nsb = seq // tm
    zspec = lambda blk: pl.BlockSpec((tm, D_CONV), lambda b, s: (b * nsb + s, blk))
    return pl.pallas_call(
        _conv_prompt_kernel,
        grid=(bsz, nsb),
        in_specs=[zspec(ZB_BLK), zspec(ZB_BLK + 1), zspec(ZB_BLK + 2),
                  pl.BlockSpec((CONV_W, D_CONV), lambda b, s: (0, 0))],
        out_specs=[pl.BlockSpec((tm, D_CONV), lambda b, s: (b * nsb + s, 0)),
                   pl.BlockSpec((1, CONV_W - 1, D_CONV), lambda b, s: (b, 0, 0))],
        out_shape=[jax.ShapeDtypeStruct((bsz * seq, D_CONV), BF16),
                   jax.ShapeDtypeStruct((bsz, CONV_W - 1, D_CONV), F32)],
        scratch_shapes=[pltpu.VMEM((tm + SUBLANES, D_CONV), F32)],
        compiler_params=_params("arbitrary", "arbitrary"),
        name="conv_prompt",
    )(z, z, z, conv_w)


def _conv_sample_kernel(t_new, zb_ref, zc_ref, zh_ref, cw_ref, s1_ref, s2_ref, p_ref, u_ref, uext):
    m = zb_ref.shape[0]
    u = zc_ref[...].astype(F32) * zh_ref[...].astype(F32)
    u_ref[...] = u
    uext[0:SUBLANES, :] = jnp.zeros((SUBLANES, D_CONV), F32)
    uext[SUBLANES:SUBLANES + m, :] = u
    t = lax.broadcasted_iota(jnp.int32, (m, D_CONV), 0) % t_new
    um1 = jnp.where(t < 1, s1_ref[...], uext[SUBLANES - 1:SUBLANES - 1 + m, :])
    um2 = jnp.where(t < 2, s2_ref[...], uext[SUBLANES - 2:SUBLANES - 2 + m, :])
    conv = cw_ref[0:1, :] * um2 + cw_ref[1:2, :] * um1 + cw_ref[2:3, :] * u
    p_ref[...] = (zb_ref[...].astype(F32) * conv).astype(BF16)


def _conv_sample(z, conv_w, state, t_new):
    m = z.shape[0]
    bd = m // t_new
    s1 = jnp.pad(state[:, 1:2], ((0, 0), (0, t_new - 1), (0, 0))).reshape(m, D_CONV)
    s2 = jnp.pad(state, ((0, 0), (0, t_new - 2), (0, 0))).reshape(m, D_CONV)
    zspec = lambda blk: pl.BlockSpec((m, D_CONV), lambda i: (0, blk))
    full = lambda r: pl.BlockSpec((r, D_CONV), lambda i: (0, 0))
    p, u = pl.pallas_call(
        functools.partial(_conv_sample_kernel, t_new),
        grid=(1,),
        in_specs=[zspec(ZB_BLK), zspec(ZB_BLK + 1), zspec(ZB_BLK + 2), full(CONV_W), full(m), full(m)],
        out_specs=[full(m), full(m)],
        out_shape=[jax.ShapeDtypeStruct((m, D_CONV), BF16), jax.ShapeDtypeStruct((m, D_CONV), F32)],
        scratch_shapes=[pltpu.VMEM((m + SUBLANES, D_CONV), F32)],
        compiler_params=_params("arbitrary"),
        name="conv_sample",
    )(z, z, z, conv_w, s1, s2)
    new_state = u.reshape(bd, t_new, D_CONV)[:, t_new - (CONV_W - 1):]
    return p, new_state


def _merge_groups(o_list, lse_list):
    top = functools.reduce(jnp.maximum, lse_list)
    es = [jnp.exp(l - top) for l in lse_list]
    num = functools.reduce(lambda a, b: a + b, [e * o for e, o in zip(es, o_list)])
    den = functools.reduce(lambda a, b: a + b, es)
    return num / den


def _nt_dot(a, b):
    return lax.dot_general(a, b, (((1,), (1,)), ((), ())), preferred_element_type=F32)


def _attn_prompt_group(q_ref, k_ref, v_ref, o_sc, lse_sc, g, dil, seq):
    shift = dil.bit_length() - 1
    qi = lax.broadcasted_iota(jnp.int32, (SPAN, 2 * SPAN), 0)
    kj = lax.broadcasted_iota(jnp.int32, (SPAN, 2 * SPAN), 1)
    band = jnp.logical_and(kj >= qi, kj <= qi + SPAN)
    is_prev = lax.broadcasted_iota(jnp.int32, (1, 2 * SPAN), 1) < SPAN
    ones = jnp.ones((2 * SPAN, HEAD_DIM), BF16)

    def rows(ref, start):
        if dil == 1:
            return ref[pl.ds(start, SPAN), :]
        return ref[pl.ds(start, SPAN, stride=dil), :]

    def body(trip, carry):
        blocks = range(ATTN_UNROLL)
        idx = [trip * ATTN_UNROLL + u for u in blocks]
        c = [i >> shift for i in idx]
        base = [c[u] * (SPAN * dil) + (idx[u] & (dil - 1)) for u in blocks]
        pbase = [jnp.maximum(c[u] - 1, 0) * (SPAN * dil) + (idx[u] & (dil - 1)) for u in blocks]
        q = [rows(q_ref, base[u]).astype(BF16) for u in blocks]
        k = [jnp.concatenate([rows(k_ref, pbase[u]), rows(k_ref, base[u])], axis=0).astype(BF16) for u in blocks]
        s = [_nt_dot(q[u], k[u]) for u in blocks]
        pen = [jnp.where(jnp.logical_and(is_prev, c[u] == 0), NEG_INF, 0.0) for u in blocks]
        s = [jnp.where(band, s[u] * ATT_SCALE + pen[u], NEG_INF) for u in blocks]
        m = [jnp.max(jnp.maximum(s[u][:, :SPAN], s[u][:, SPAN:]), axis=-1, keepdims=True) for u in blocks]
        p = [jnp.exp(s[u] - m[u]).astype(BF16) for u in blocks]
        v = [jnp.concatenate(
            [jnp.concatenate([rows(v_ref, pbase[u]), rows(v_ref, base[u])], axis=0).astype(BF16), ones], axis=1)
            for u in blocks]
        acc = [jnp.dot(p[u], v[u], preferred_element_type=F32) for u in blocks]
        for u in blocks:
            den = acc[u][:, HEAD_DIM:]
            o = acc[u][:, :HEAD_DIM] / den
            lse = m[u] + jnp.log(den)
            if dil == 1:
                o_sc[g, pl.ds(base[u], SPAN), :] = o
                lse_sc[g, pl.ds(base[u], SPAN), :] = lse
            else:
                o_sc[g, pl.ds(base[u], SPAN, stride=dil), :] = o
                lse_sc[g, pl.ds(base[u], SPAN, stride=dil), :] = lse
        return carry

    lax.fori_loop(0, seq // (SPAN * ATTN_UNROLL), body, 0)


def _attn_prompt_kernel(q_ref, k_ref, v_ref, o_ref, o_sc, lse_sc):
    seq = q_ref.shape[0]
    g_id = pl.program_id(2)
    for g, (_, dil) in enumerate(DILATED_GROUPS):
        @pl.when(g_id == g)
        def _(g=g, dil=dil):
            _attn_prompt_group(q_ref, k_ref, v_ref, o_sc, lse_sc, g, dil, seq)

    @pl.when(g_id == N_DIL - 1)
    def _():
        chunk = 256

        def body(i, carry):
            sl = pl.ds(pl.multiple_of(i * chunk, chunk), chunk)
            o = _merge_groups([o_sc[g, sl, :] for g in range(N_DIL)], [lse_sc[g, sl, :] for g in range(N_DIL)])
            o_ref[sl, :] = o.astype(BF16)
            return carry

        lax.fori_loop(0, seq // chunk, body, 0)


def _attn_prompt(qkv, bsz, seq):
    qkv3 = qkv.reshape(bsz, seq, QKV_COLS)
    per_sel = D_ATT // HEAD_DIM
    spec = lambda sel: pl.BlockSpec((None, seq, HEAD_DIM),
                                    lambda b, j, g: (b, 0, sel * per_sel + g * HEADS_PER_GROUP + j))
    out = pl.pallas_call(
        _attn_prompt_kernel,
        grid=(bsz, HEADS_PER_GROUP, N_DIL),
        in_specs=[spec(0), spec(1), spec(2)],
        out_specs=pl.BlockSpec((None, seq, HEAD_DIM), lambda b, j, g: (b, 0, j)),
        out_shape=jax.ShapeDtypeStruct((bsz, seq, D_GRP), BF16),
        scratch_shapes=[pltpu.VMEM((N_DIL, seq, HEAD_DIM), F32), pltpu.VMEM((N_DIL, seq, HEAD_DIM), F32)],
        compiler_params=_params("arbitrary", "arbitrary", "arbitrary"),
        name="attn_prompt",
    )(qkv3, qkv3, qkv3)
    return out.reshape(bsz * seq, D_GRP)


KV_ROWS = 2 * HEADS_PER_GROUP


def _attn_sample_kernel(dil, q_ref, cache_ref, new_ref, shifted_hbm, buf_ref, o_ref, lse_ref):
    del shifted_hbm
    t_new = q_ref.shape[0]
    n_buf = cache_ref.shape[0] // KV_ROWS
    tq = lax.broadcasted_iota(jnp.int32, (t_new, n_buf), 0)
    pos = lax.broadcasted_iota(jnp.int32, (t_new, n_buf), 1)
    back = n_buf + tq - pos
    cache_ok = jnp.logical_and(back <= dil * SPAN, (back & (dil - 1)) == 0)
    tq2 = lax.broadcasted_iota(jnp.int32, (t_new, t_new), 0)
    tk2 = lax.broadcasted_iota(jnp.int32, (t_new, t_new), 1)
    new_ok = jnp.logical_and(tk2 <= tq2, ((tq2 - tk2) & (dil - 1)) == 0)
    for h in range(HEADS_PER_GROUP):
        cols = slice(h * HEAD_DIM, (h + 1) * HEAD_DIM)
        q = q_ref[:, cols].astype(BF16)
        k_c = cache_ref[pl.ds(h, n_buf, stride=KV_ROWS), :].astype(BF16)
        v_c = cache_ref[pl.ds(HEADS_PER_GROUP + h, n_buf, stride=KV_ROWS), :].astype(BF16)
        k_n = new_ref[pl.ds(h, t_new, stride=KV_ROWS), :].astype(BF16)
        v_n = new_ref[pl.ds(HEADS_PER_GROUP + h, t_new, stride=KV_ROWS), :].astype(BF16)
        s_c = jnp.where(cache_ok, _nt_dot(q, k_c) * ATT_SCALE, NEG_INF)
        s_n = jnp.where(new_ok, _nt_dot(q, k_n) * ATT_SCALE, NEG_INF)
        m = jnp.maximum(jnp.max(s_c, axis=-1, keepdims=True), jnp.max(s_n, axis=-1, keepdims=True))
        p_c = jnp.exp(s_c - m)
        p_n = jnp.exp(s_n - m)
        den = jnp.sum(p_c, axis=-1, keepdims=True) + jnp.sum(p_n, axis=-1, keepdims=True)
        acc = (jnp.dot(p_c.astype(BF16), v_c, preferred_element_type=F32)
               + jnp.dot(p_n.astype(BF16), v_n, preferred_element_type=F32))
        o_ref[:, cols] = acc / den
        lse_ref[:, cols] = jnp.broadcast_to(m + jnp.log(den), (t_new, HEAD_DIM))
    buf_ref[...] = new_ref[...]


def _attn_sample(q3, cache, new_kv, shifted, g):
    window, dil = DILATED_GROUPS[g]
    bd, t_new, _ = q3.shape
    rows = cache.shape[1]
    new_rows = t_new * KV_ROWS
    assert rows == window * KV_ROWS, "the cache must hold a full window"
    assert rows % new_rows == 0
    row_spec = lambda r: pl.BlockSpec((None, r, HEAD_DIM), lambda b: (b, 0, 0))
    tok_spec = pl.BlockSpec((None, t_new, D_GRP), lambda b: (b, 0, 0))
    return pl.pallas_call(
        functools.partial(_attn_sample_kernel, dil),
        grid=(bd,),
        in_specs=[pl.BlockSpec((None, t_new, D_GRP), lambda b: (b, 0, g)), row_spec(rows),
                  row_spec(new_rows), pl.BlockSpec(memory_space=pl.ANY)],
        out_specs=[pl.BlockSpec((None, new_rows, HEAD_DIM), lambda b: (b, rows // new_rows - 1, 0)),
                   tok_spec, tok_spec],
        out_shape=[jax.ShapeDtypeStruct((bd, rows, HEAD_DIM), F32),
                   jax.ShapeDtypeStruct((bd, t_new, D_GRP), F32),
                   jax.ShapeDtypeStruct((bd, t_new, D_GRP), F32)],
        input_output_aliases={3: 0},
        compiler_params=_params("arbitrary"),
        name=f"attn_sample_w{window}",
    )(q3, cache, new_kv, shifted)


def _merge_sample_kernel(o0, o1, o2, l0, l1, l2, out_ref):
    out_ref[...] = _merge_groups([o0[...], o1[...], o2[...]], [l0[...], l1[...], l2[...]]).astype(BF16)


def _merge_sample(os_, ls_):
    m = os_[0].shape[0]
    spec = pl.BlockSpec((m, D_GRP), lambda i: (0, 0))
    return pl.pallas_call(
        _merge_sample_kernel,
        grid=(1,),
        in_specs=[spec] * 6,
        out_specs=spec,
        out_shape=jax.ShapeDtypeStruct((m, D_GRP), BF16),
        compiler_params=_params("arbitrary"),
        name="merge_sample",
    )(*os_, *ls_)


def _layer_norm(v, g, b):
    mu = jnp.mean(v, axis=-1, keepdims=True)
    d = v - mu
    var = jnp.mean(d * d, axis=-1, keepdims=True)
    return d * lax.rsqrt(var + LN_EPS) * g + b


SLAB_ROWS = SUBLANES
HALF = D_MODEL // 2
HI_MASK = 0xFFFF0000


def _bf16_bits(v):
    return pltpu.bitcast(v.astype(BF16).astype(F32), jnp.uint32)


def _pack_slabs(src_ref, slab_ref, n_tok):
    def body(tg, carry):
        t0 = pl.multiple_of(tg * SUBLANES, SUBLANES)
        for s in range(SLAB_ROWS):
            lo = src_ref[pl.ds(t0, SUBLANES), s * LANES:(s + 1) * LANES]
            hi = src_ref[pl.ds(t0, SUBLANES), HALF + s * LANES:HALF + (s + 1) * LANES]
            word = (_bf16_bits(lo) >> 16) | _bf16_bits(hi)
            slab_ref[pl.ds(t0 * SLAB_ROWS + s, SUBLANES, stride=SLAB_ROWS), :] = word
        return carry
    lax.fori_loop(0, n_tok // SUBLANES, body, 0)


def _unpack_slabs(slab_ref, dst_ref, n_tok):
    def body(tg, carry):
        t0 = pl.multiple_of(tg * SUBLANES, SUBLANES)
        for s in range(SLAB_ROWS):
            word = slab_ref[pl.ds(t0 * SLAB_ROWS + s, SUBLANES, stride=SLAB_ROWS), :]
            dst_ref[pl.ds(t0, SUBLANES), s * LANES:(s + 1) * LANES] = pltpu.bitcast(word << 16, F32)
            dst_ref[pl.ds(t0, SUBLANES), HALF + s * LANES:HALF + (s + 1) * LANES] = pltpu.bitcast(
                word & jnp.uint32(HI_MASK), F32)
        return carry
    lax.fori_loop(0, n_tok // SUBLANES, body, 0)


def _mix_kernel(p_ref, o_ref, gc_ref, ga_ref, x_ref, wc_ref, wa_ref, wo_ref, g_ref, b_ref, *rest):
    h_ref, hs_ref = rest[-2:]
    y_conv = jnp.dot(p_ref[...], wc_ref[...], preferred_element_type=F32)
    y_att = jnp.dot(o_ref[...], wa_ref[...], preferred_element_type=F32)
    merged = (jax.nn.sigmoid(gc_ref[...].astype(F32)) * y_conv
              + jax.nn.sigmoid(ga_ref[...].astype(F32)) * y_att)
    mix = jnp.dot(merged.astype(BF16), wo_ref[...], preferred_element_type=F32)
    h_ref[...] = _layer_norm(ALPHA * x_ref[...] + mix, g_ref[...], b_ref[...])
    _pack_slabs(h_ref, hs_ref, h_ref.shape[0])


def _mix(p, o, z, x, wc16, wa16, wo16, ln_g, ln_b, tm, total, h_all=None, row_off=0):
    m = p.shape[0]
    assert row_off % tm == 0
    blk_off = row_off // tm
    row = lambda w: pl.BlockSpec((tm, w), lambda i: (i, 0))
    const = lambda a: pl.BlockSpec(a.shape, lambda i: (0, 0), pipeline_mode=pl.Buffered(1))
    in_specs = [row(D_CONV), row(D_GRP),
                pl.BlockSpec((tm, D_MODEL), lambda i: (i, 0)), pl.BlockSpec((tm, D_MODEL), lambda i: (i, 1)),
                row(D_MODEL), const(wc16), const(wa16), const(wo16), const(ln_g), const(ln_b)]
    args = [p, o, z, z, x, wc16, wa16, wo16, ln_g, ln_b]
    aliases = {}
    if h_all is not None:
        in_specs += [pl.BlockSpec(memory_space=pl.ANY)] * 2
        args += list(h_all)
        aliases = {len(args) - 2: 0, len(args) - 1: 1}
    return pl.pallas_call(
        _mix_kernel,
        grid=(m // tm,),
        in_specs=in_specs,
        out_specs=[pl.BlockSpec((tm, D_MODEL), lambda i: (i + blk_off, 0)),
                   pl.BlockSpec((tm * SLAB_ROWS, LANES), lambda i: (i + blk_off, 0))],
        out_shape=[jax.ShapeDtypeStruct((total, D_MODEL), F32),
                   jax.ShapeDtypeStruct((total * SLAB_ROWS, LANES), jnp.uint32)],
        input_output_aliases=aliases,
        compiler_params=_params("arbitrary"),
        name="mix_ln1",
    )(*args)


def _router_kernel(h_ref, w_ref, b_ref, idx_ref, gate_ref, cnt_ref, carry):
    tile = h_ref.shape[0]

    @pl.when(pl.program_id(0) == 0)
    def _():
        carry[...] = jnp.zeros(carry.shape, F32)

    logits = jnp.dot(h_ref[...], w_ref[...], preferred_element_type=F32,
                     precision=lax.Precision.HIGHEST) + b_ref[...]
    lane = lax.broadcasted_iota(jnp.int32, (tile, LANES), 1)
    lane_f = lane.astype(F32)
    first_max = lambda v, mx: jnp.min(jnp.where(v == mx, lane_f, float(LANES)), axis=-1, keepdims=True)

    is_grp = lane < N_EXPERT_GROUPS
    lg = jnp.where(is_grp, logits, -jnp.inf)
    mg = jnp.max(lg, axis=-1, keepdims=True)
    gsel = first_max(lg, mg).astype(jnp.int32)
    p_group = 1.0 / jnp.sum(jnp.where(is_grp, jnp.exp(logits - mg), 0.0), axis=-1, keepdims=True)

    e_lane = lane - N_EXPERT_GROUPS
    in_grp = jnp.logical_and(e_lane >= 0, (e_lane >> 3) == gsel)
    le = jnp.where(in_grp, logits, -jnp.inf)
    v1 = jnp.max(le, axis=-1, keepdims=True)
    i1 = first_max(le, v1).astype(jnp.int32)
    le2 = jnp.where(lane == i1, -jnp.inf, le)
    v2 = jnp.max(le2, axis=-1, keepdims=True)
    i2 = first_max(le2, v2).astype(jnp.int32)
    ex = jnp.exp(v2 - v1)
    g1 = p_group / (1.0 + ex)
    g2 = p_group * ex / (1.0 + ex)
    e1 = i1 - N_EXPERT_GROUPS
    e2 = i2 - N_EXPERT_GROUPS

    hot1 = lane == e1
    hot2 = lane == e2
    hot = jnp.where(jnp.logical_or(hot1, hot2), 1.0, 0.0)
    rr = lax.broadcasted_iota(jnp.int32, (tile, tile), 0)
    cc = lax.broadcasted_iota(jnp.int32, (tile, tile), 1)
    tri = jnp.where(cc < rr, 1.0, 0.0).astype(BF16)
    before = jnp.dot(tri, hot.astype(BF16), preferred_element_type=F32) + carry[0:1, :]
    r1 = jnp.sum(jnp.where(hot1, before, 0.0), axis=-1, keepdims=True).astype(jnp.int32)
    r2 = jnp.sum(jnp.where(hot2, before, 0.0), axis=-1, keepdims=True).astype(jnp.int32)
    carry[...] = carry[...] + jnp.sum(hot, axis=0, keepdims=True)

    idx_ref[...] = jnp.where(lane == 0, e1, jnp.where(lane == 1, e2, jnp.where(lane == 2, r1, r2)))
    gate_ref[...] = jnp.where(lane == 0, g1, g2)
    cnt_ref[...] = carry[...]


def _router(h_all, w_r, b_r):
    m = h_all.shape[0]
    tile = ROUTER_TILE if m % ROUTER_TILE == 0 else ROW_TILE
    row = pl.BlockSpec((tile, LANES), lambda i: (i, 0))
    return pl.pallas_call(
        _router_kernel,
        grid=(m // tile,),
        in_specs=[pl.BlockSpec((tile, D_MODEL), lambda i: (i, 0)),
                  pl.BlockSpec((D_MODEL, LANES), lambda i: (0, 0)),
                  pl.BlockSpec((1, LANES), lambda i: (0, 0))],
        out_specs=[row, row, pl.BlockSpec((SUBLANES, LANES), lambda i: (0, 0))],
        out_shape=[jax.ShapeDtypeStruct((m, LANES), jnp.int32), jax.ShapeDtypeStruct((m, LANES), F32),
                   jax.ShapeDtypeStruct((SUBLANES, LANES), F32)],
        scratch_shapes=[pltpu.VMEM((SUBLANES, LANES), F32)],
        compiler_params=_params("arbitrary"),
        name="router",
    )(h_all, w_r, b_r)


def _slab(ref, row):
    return ref.at[pl.ds(pl.multiple_of(row * SLAB_ROWS, SLAB_ROWS), SLAB_ROWS), :]


def _dispatch_kernel(tile, dest_ref, h_ref, x_hbm, sem):
    base = pl.program_id(0) * tile

    def copy(r, k):
        return pltpu.make_async_copy(_slab(h_ref, r), _slab(x_hbm, dest_ref[(base + r) * TOP_K + k]), sem)

    def start(r, c):
        for k in range(TOP_K):
            copy(r, k).start()
        return c

    def wait(r, c):
        for k in range(TOP_K):
            copy(r, k).wait()
        return c

    lax.fori_loop(0, tile, start, 0, unroll=DMA_UNROLL)
    lax.fori_loop(0, tile, wait, 0, unroll=DMA_UNROLL)


def _dispatch(h_slabs, dest_flat, n_rows):
    m = h_slabs.shape[0] // SLAB_ROWS
    tile = DISPATCH_TILE if m % DISPATCH_TILE == 0 else ROW_TILE
    grid_spec = pltpu.PrefetchScalarGridSpec(
        num_scalar_prefetch=1,
        grid=(m // tile,),
        in_specs=[pl.BlockSpec((tile * SLAB_ROWS, LANES), lambda i, d: (i, 0))],
        out_specs=pl.BlockSpec(memory_space=pl.ANY),
        scratch_shapes=[pltpu.SemaphoreType.DMA(())],
    )
    return pl.pallas_call(
        functools.partial(_dispatch_kernel, tile),
        grid_spec=grid_spec,
        out_shape=jax.ShapeDtypeStruct((n_rows * SLAB_ROWS, LANES), jnp.uint32),
        compiler_params=pltpu.CompilerParams(dimension_semantics=("arbitrary",), has_side_effects=True),
        name="moe_dispatch",
    )(dest_flat, h_slabs)


def _expert_kernel(blk_e, blk_first, blk_valid, blk_ord, exp_list, n_used, n_exp,
                   x_ref, wg_hbm, wu_hbm, wd_hbm, y_ref, wg32, wu32, wd32, wg16, wu16, wd16, sems, rows32):
    i = pl.program_id(0)

    def weight_copies(e, slot):
        return [pltpu.make_async_copy(hbm.at[e], buf.at[slot], sems.at[slot, n])
                for n, (hbm, buf) in enumerate(((wg_hbm, wg32), (wu_hbm, wu32), (wd_hbm, wd32)))]

    def start_weights(ordinal, slot):
        for c in weight_copies(exp_list[ordinal], slot):
            c.start()

    @pl.when(i == 0)
    def _():
        start_weights(0, 0)

        @pl.when(n_exp[0] > 1)
        def _():
            start_weights(1, 1)

    @pl.when(i < n_used[0])
    def _():
        @pl.when(blk_first[i] == 1)
        def _():
            ordinal = blk_ord[i]
            slot = ordinal & 1
            for c in weight_copies(blk_e[i], slot):
                c.wait()
            wg16[...] = wg32[slot].astype(BF16)
            wu16[...] = wu32[slot].astype(BF16)
            wd16[...] = wd32[slot].astype(BF16)

            @pl.when(ordinal + 2 < n_exp[0])
            def _():
                start_weights(ordinal + 2, slot)

        _unpack_slabs(x_ref, rows32, MOE_BLK)
        row = lax.broadcasted_iota(jnp.int32, rows32.shape, 0)
        x = jnp.where(row < blk_valid[i], rows32[...], 0.0).astype(BF16)
        gate = jnp.dot(x, wg16[...], preferred_element_type=F32)
        up = jnp.dot(x, wu16[...], preferred_element_type=F32)
        hid = (jax.nn.silu(gate) * up).astype(BF16)
        rows32[...] = jnp.dot(hid, wd16[...], preferred_element_type=F32)
        _pack_slabs(rows32, y_ref, MOE_BLK)


def _experts(x_slabs, plan, w_gate, w_up, w_down):
    n_rows = x_slabs.shape[0] // SLAB_ROWS
    n_blocks = n_rows // MOE_BLK
    row_map = lambda i, be, bf, bv, bo, el, nu, ne: (jnp.minimum(i, nu[0] - 1), 0)
    hbm = pl.BlockSpec(memory_space=pl.ANY)
    grid_spec = pltpu.PrefetchScalarGridSpec(
        num_scalar_prefetch=7,
        grid=(n_blocks,),
        in_specs=[pl.BlockSpec((MOE_BLK * SLAB_ROWS, LANES), row_map), hbm, hbm, hbm],
        out_specs=pl.BlockSpec((MOE_BLK * SLAB_ROWS, LANES), row_map),
        scratch_shapes=[pltpu.VMEM((2, D_MODEL, D_EXPERT), F32), pltpu.VMEM((2, D_MODEL, D_EXPERT), F32),
                        pltpu.VMEM((2, D_EXPERT, D_MODEL), F32),
                        pltpu.VMEM((D_MODEL, D_EXPERT), BF16), pltpu.VMEM((D_MODEL, D_EXPERT), BF16),
                        pltpu.VMEM((D_EXPERT, D_MODEL), BF16),
                        pltpu.SemaphoreType.DMA((2, 3)),
                        pltpu.VMEM((MOE_BLK, D_MODEL), F32)],
    )
    return pl.pallas_call(
        _expert_kernel,
        grid_spec=grid_spec,
        out_shape=jax.ShapeDtypeStruct((n_rows * SLAB_ROWS, LANES), jnp.uint32),
        compiler_params=_params("arbitrary"),
        name="moe_experts",
    )(*plan, x_slabs, w_gate, w_up, w_down)


def _combine_kernel(tok_off, tile, dest_ref, y_hbm, h_ref, gate_ref, g_ref, b_ref, out_ref, ybuf, y32, sems):
    i = pl.program_id(0)
    n_steps = pl.num_programs(0)

    def copy(step, r, k):
        slot = step & 1
        row = dest_ref[(tok_off + step * tile + r) * TOP_K + k]
        return pltpu.make_async_copy(_slab(y_hbm, row), _slab(ybuf.at[slot, k], r), sems.at[slot])

    def gather(step):
        def body(r, c):
            for k in range(TOP_K):
                copy(step, r, k).start()
            return c
        lax.fori_loop(0, tile, body, 0, unroll=DMA_UNROLL)

    @pl.when(i == 0)
    def _():
        gather(i)

    @pl.when(i + 1 < n_steps)
    def _():
        gather(i + 1)

    def wait(r, c):
        for k in range(TOP_K):
            copy(i, r, k).wait()
        return c

    lax.fori_loop(0, tile, wait, 0, unroll=DMA_UNROLL)
    slot = i & 1
    for k in range(TOP_K):
        _unpack_slabs(ybuf.at[slot, k], y32.at[k], tile)
    f = gate_ref[:, 0:1] * y32[0] + gate_ref[:, 1:2] * y32[1]
    out_ref[...] = _layer_norm(ALPHA * h_ref[...] + f, g_ref[...], b_ref[...])


def _combine(y_slabs, dest_flat, h_all, gates, ln_g, ln_b, tok_off, n_tok):
    tile = COMBINE_TILE if (n_tok % COMBINE_TILE == 0 and tok_off % COMBINE_TILE == 0) else ROW_TILE
    blk_off = tok_off // tile
    grid_spec = pltpu.PrefetchScalarGridSpec(
        num_scalar_prefetch=1,
        grid=(n_tok // tile,),
        in_specs=[pl.BlockSpec(memory_space=pl.ANY),
                  pl.BlockSpec((tile, D_MODEL), lambda i, d: (i + blk_off, 0)),
                  pl.BlockSpec((tile, LANES), lambda i, d: (i + blk_off, 0)),
                  pl.BlockSpec((1, D_MODEL), lambda i, d: (0, 0)),
                  pl.BlockSpec((1, D_MODEL), lambda i, d: (0, 0))],
        out_specs=pl.BlockSpec((tile, D_MODEL), lambda i, d: (i, 0)),
        scratch_shapes=[pltpu.VMEM((2, TOP_K, tile * SLAB_ROWS, LANES), jnp.uint32),
                        pltpu.VMEM((TOP_K, tile, D_MODEL), F32), pltpu.SemaphoreType.DMA((2,))],
    )
    return pl.pallas_call(
        functools.partial(_combine_kernel, tok_off, tile),
        grid_spec=grid_spec,
        out_shape=jax.ShapeDtypeStruct((n_tok, D_MODEL), F32),
        compiler_params=_params("arbitrary"),
        name="combine_ln2",
    )(dest_flat, y_slabs, h_all, gates, ln_g, ln_b)


def _moe_plan(idx, counts_f):
    m = idx.shape[0]
    n_asg = m * TOP_K
    experts = idx[:, 0:TOP_K]
    ranks = idx[:, TOP_K:2 * TOP_K]
    counts = counts_f[0, :N_EXPERTS].astype(jnp.int32)
    padded = (counts + MOE_BLK - 1) // MOE_BLK * MOE_BLK
    pad_end = jnp.cumsum(padded)
    pad_start = pad_end - padded
    hot = experts[:, :, None] == jnp.arange(N_EXPERTS, dtype=jnp.int32)
    dest = (jnp.sum(jnp.where(hot, pad_start, 0), axis=-1) + ranks).reshape(n_asg).astype(jnp.int32)
    n_blocks = -(-(n_asg + N_EXPERTS * (MOE_BLK - 1)) // MOE_BLK)
    n_used = jnp.maximum(pad_end[-1] // MOE_BLK, 1).astype(jnp.int32)
    blk = jnp.minimum(jnp.arange(n_blocks, dtype=jnp.int32), n_used - 1)
    blk_e = jnp.minimum(jnp.searchsorted(pad_end, blk * MOE_BLK, side='right'), N_EXPERTS - 1).astype(jnp.int32)
    blk_first = (blk * MOE_BLK == pad_start[blk_e]).astype(jnp.int32)
    blk_valid = jnp.clip(counts[blk_e] - (blk * MOE_BLK - pad_start[blk_e]), 0, MOE_BLK).astype(jnp.int32)
    used = counts > 0
    exp_list = jnp.argsort(jnp.logical_not(used), stable=True).astype(jnp.int32)
    blk_ord = (jnp.cumsum(used.astype(jnp.int32)) - 1)[blk_e].astype(jnp.int32)
    n_exp = jnp.sum(used.astype(jnp.int32)).reshape(1)
    plan = (blk_e, blk_first, blk_valid, blk_ord, exp_list, n_used.reshape(1), n_exp)
    return dest, n_blocks * MOE_BLK, plan


def kernel(x_prompt, x_sample, cache_kv_w128, cache_kv_w512, cache_kv_w2048, state_conv, w_in, conv_w, w_conv_out, w_att_out, w_o, ln1_g, ln1_b, w_router_group, b_router_group, w_router_expert, b_router_expert, w_expert_gate, w_expert_up, w_expert_down, ln2_g, ln2_b):
    bsz, seq, _ = x_prompt.shape
    bd, t_new, _ = x_sample.shape
    m_p = bsz * seq
    m_s = bd * t_new
    m_all = m_p + m_s
    caches = (cache_kv_w128, cache_kv_w512, cache_kv_w2048)
    lyr = 0

    w_in16 = w_in[lyr].astype(BF16)
    wc16 = w_conv_out[lyr].astype(BF16)
    wa16 = w_att_out[lyr].astype(BF16)
    wo16 = w_o[lyr].astype(BF16)
    cw = conv_w[lyr]
    g1, b1 = ln1_g[lyr][None, :], ln1_b[lyr][None, :]
    g2, b2 = ln2_g[lyr][None, :], ln2_b[lyr][None, :]
    w_re = jnp.transpose(w_router_expert[lyr], (1, 0, 2)).reshape(D_MODEL, N_EXPERTS)
    w_r = jnp.pad(jnp.concatenate([w_router_group[lyr], w_re], axis=1),
                  ((0, 0), (0, LANES - N_EXPERT_GROUPS - N_EXPERTS)))
    b_r = jnp.pad(jnp.concatenate([b_router_group[lyr], b_router_expert[lyr].reshape(N_EXPERTS)]),
                  (0, LANES - N_EXPERT_GROUPS - N_EXPERTS))[None, :]

    xp = x_prompt.reshape(m_p, D_MODEL)
    xs = x_sample.reshape(m_s, D_MODEL)

    caches3 = [c[lyr].reshape(bd, -1, HEAD_DIM) for c in caches]
    z_p, qkv_p, shifted = _proj_in(xp, w_in16, min(PROJ_TM, m_p), caches3, t_new * KV_ROWS)
    p_p, conv_p = _conv_prompt(z_p, cw, bsz, seq)
    o_p = _attn_prompt(qkv_p, bsz, seq)
    h_buf = _mix(p_p, o_p, z_p, xp, wc16, wa16, wo16, g1, b1, min(MIX_TM, m_p), m_all)

    z_s, qkv_s, _ = _proj_in(xs, w_in16, m_s)
    p_s, conv_s = _conv_sample(z_s, cw, state_conv[lyr], t_new)
    q3 = qkv_s.reshape(bd, t_new, QKV_COLS)
    kv5 = qkv_s.reshape(bd, t_new, 3, N_DIL, HEADS_PER_GROUP, HEAD_DIM)
    os_, ls_, bufs_s = [], [], []
    for g, (window, _) in enumerate(DILATED_GROUPS):
        new_kv = kv5[:, :, 1:3, g].reshape(bd, t_new * KV_ROWS, HEAD_DIM)
        buf, o_g, lse_g = _attn_sample(q3, caches3[g], new_kv, shifted[g], g)
        bufs_s.append(buf.reshape(1, bd, window, 2, HEADS_PER_GROUP, HEAD_DIM))
        os_.append(o_g.reshape(m_s, D_GRP))
        ls_.append(lse_g.reshape(m_s, D_GRP))
    o_s = _merge_sample(os_, ls_)
    h_buf = _mix(p_s, o_s, z_s, xs, wc16, wa16, wo16, g1, b1, m_s, m_all, h_all=h_buf, row_off=m_p)

    h_all, h_slabs = h_buf
    idx, gates, counts = _router(h_all, w_r, b_r)
    dest, n_rows, plan = _moe_plan(idx, counts)
    x_slabs = _dispatch(h_slabs, dest, n_rows)
    y_slabs = _experts(x_slabs, plan, w_expert_gate[lyr], w_expert_up[lyr], w_expert_down[lyr])
    y_p = _combine(y_slabs, dest, h_all, gates, g2, b2, 0, m_p)
    y_s = _combine(y_slabs, dest, h_all, gates, g2, b2, m_p, m_s)

    qkv_p3 = qkv_p.reshape(bsz, seq, QKV_COLS)
    bufs_p = []
    for g, (w, _) in enumerate(DILATED_GROUPS):
        keep = min(w, seq)
        k_g = qkv_p3[:, seq - keep:, D_ATT + g * D_GRP:D_ATT + (g + 1) * D_GRP]
        v_g = qkv_p3[:, seq - keep:, 2 * D_ATT + g * D_GRP:2 * D_ATT + (g + 1) * D_GRP]
        bufs_p.append(jnp.stack([k_g, v_g], axis=2).reshape(1, bsz, keep, 2, HEADS_PER_GROUP, HEAD_DIM))
    return (y_p.reshape(bsz, seq, D_MODEL), y_s.reshape(bd, t_new, D_MODEL),
            bufs_p[0], bufs_p[1], bufs_p[2], conv_p[None],
            bufs_s[0], bufs_s[1], bufs_s[2], conv_s[None])
```

```python
import functools

import jax
import jax.numpy as jnp
from jax import lax
from jax.experimental import pallas as pl
from jax.experimental.pallas import tpu as pltpu

F32 = jnp.float32
BF16 = jnp.bfloat16

D_MODEL = 2048
D_CONV = D_MODEL // 2
CONV_W = 3
DILATED_GROUPS = ((128, 1), (512, 4), (2048, 16))
N_DIL = len(DILATED_GROUPS)
HEADS_PER_GROUP = 4
HEAD_DIM = 128
SPAN = 128
D_GRP = HEADS_PER_GROUP * HEAD_DIM
D_ATT = N_DIL * D_GRP
N_EXPERT_GROUPS = 8
EXPERTS_PER_GROUP = 8
N_EXPERTS = N_EXPERT_GROUPS * EXPERTS_PER_GROUP
TOP_K = 2
D_EXPERT = D_MODEL // 4
DEPTH = 1
LN_EPS = 1e-5
ALPHA = (2 * DEPTH) ** 0.25
NEG_INF = -1e30
ATT_SCALE = HEAD_DIM ** -0.5

OFF_B = 0
OFF_C = D_CONV
OFF_H = 2 * D_CONV
OFF_Q = 3 * D_CONV
OFF_K = OFF_Q + D_ATT
OFF_V = OFF_K + D_ATT
OFF_GC = OFF_V + D_ATT
OFF_GA = OFF_GC + D_MODEL
N_IN = OFF_GA + D_MODEL

LANES = 128
SUBLANES = 8
VMEM_LIMIT = 56 * 1024 * 1024

PROJ_TN = 512
PROJ_TM = 1024
Z_COLS = 2 * D_MODEL + 3 * D_CONV
QKV_COLS = 3 * D_ATT
NZ_TILES = Z_COLS // PROJ_TN
NQ_TILES = QKV_COLS // PROJ_TN
CONV_TM = 512
MIX_TM = 512
ROW_TILE = 128
ROUTER_TILE = 640
MOE_BLK = 256
ATTN_UNROLL = 4
DISPATCH_TILE = 640
COMBINE_TILE = 256
SHIFT_CHUNK_ROWS = 4128
DMA_UNROLL = 8


def _proj_tile_order():
    t = lambda off, width: list(range(off // PROJ_TN, (off + width) // PROJ_TN))
    z_tiles = t(OFF_GC, D_MODEL) + t(OFF_GA, D_MODEL) + t(OFF_B, D_CONV) + t(OFF_C, D_CONV) + t(OFF_H, D_CONV)
    q_tiles = t(OFF_Q, D_ATT) + t(OFF_K, D_ATT) + t(OFF_V, D_ATT)
    return z_tiles + q_tiles


def _params(*sem):
    return pltpu.CompilerParams(dimension_semantics=sem, vmem_limit_bytes=VMEM_LIMIT)


def _shift_chunks(rows, drop, max_rows):
    keep = rows - drop
    best = SUBLANES
    for c in range(SUBLANES, max_rows + 1, SUBLANES):
        if keep % c == 0:
            best = c
    return best, keep // best


def _proj_kernel(shift_plan, drop, order_ref, x_ref, w_ref, *rest):
    del order_ref
    n_shift = len(shift_plan)
    srcs = rest[:n_shift]
    z_ref, qkv_ref = rest[n_shift:n_shift + 2]
    dsts = rest[n_shift + 2:2 * n_shift + 2]
    x16 = rest[2 * n_shift + 2]
    j = pl.program_id(1)

    if n_shift:
        stage, sem_in, sem_out = rest[2 * n_shift + 3:]
        t = pl.program_id(0) * pl.num_programs(1) + j
        n_chunks = shift_plan[-1][2]

        def for_chunk(q, action):
            for n, (chunk, lo, hi) in enumerate(shift_plan):
                @pl.when(jnp.logical_and(q >= lo, q < hi))
                def _(n=n, chunk=chunk, lo=lo):
                    action(n, chunk, pl.multiple_of((q - lo) * chunk, SUBLANES), q & 1)

        def load(n, chunk, start, slot):
            return pltpu.make_async_copy(srcs[n].at[pl.ds(start + drop, chunk), :],
                                         stage.at[slot, pl.ds(0, chunk), :], sem_in.at[slot])

        def store(n, chunk, start, slot):
            return pltpu.make_async_copy(stage.at[slot, pl.ds(0, chunk), :],
                                         dsts[n].at[pl.ds(start, chunk), :], sem_out.at[slot])

        @pl.when(t == 0)
        def _():
            for_chunk(t, lambda *a: load(*a).start())

        @pl.when(t < n_chunks)
        def _():
            for_chunk(t, lambda *a: load(*a).wait())
            for_chunk(t, lambda *a: store(*a).start())

        @pl.when(jnp.logical_and(t >= 1, t <= n_chunks))
        def _():
            for_chunk(t - 1, lambda *a: store(*a).wait())

        @pl.when(t + 1 < n_chunks)
        def _():
            for_chunk(t + 1, lambda *a: load(*a).start())

    @pl.when(j == 0)
    def _():
        x16[...] = x_ref[...].astype(BF16)

    @pl.when(j < NZ_TILES)
    def _():
        z_ref[...] = jnp.dot(x16[...], w_ref[...], preferred_element_type=F32).astype(BF16)

    @pl.when(j >= NZ_TILES)
    def _():
        qkv_ref[...] = jnp.dot(x16[...], w_ref[...], preferred_element_type=F32)


def _proj_in(x, w16, tm, shift_srcs=(), drop=0):
    m = x.shape[0]
    n_shift = len(shift_srcs)
    n_steps = (m // tm) * (NZ_TILES + NQ_TILES)
    order = jnp.asarray(_proj_tile_order(), jnp.int32)
    hbm = pl.BlockSpec(memory_space=pl.ANY)
    shift_plan, first = [], 0
    for s in shift_srcs:
        chunk, count = _shift_chunks(s.shape[0], drop, SHIFT_CHUNK_ROWS)
        shift_plan.append((chunk, first, first + count))
        first += count
    assert first < n_steps, "one shift chunk per grid step, plus one step to drain"
    shift_scratch = [pltpu.VMEM((2, SHIFT_CHUNK_ROWS, LANES), F32), pltpu.SemaphoreType.DMA((2,)),
                     pltpu.SemaphoreType.DMA((2,))] if n_shift else []
    grid_spec = pltpu.PrefetchScalarGridSpec(
        num_scalar_prefetch=1,
        grid=(m // tm, NZ_TILES + NQ_TILES),
        in_specs=[
            pl.BlockSpec((tm, D_MODEL), lambda i, j, o: (i, 0)),
            pl.BlockSpec((D_MODEL, PROJ_TN), lambda i, j, o: (0, o[j])),
        ] + [hbm] * n_shift,
        out_specs=[
            pl.BlockSpec((tm, PROJ_TN), lambda i, j, o: (i, jnp.minimum(j, NZ_TILES - 1))),
            pl.BlockSpec((tm, PROJ_TN), lambda i, j, o: (i, jnp.maximum(j - NZ_TILES, 0))),
        ] + [hbm] * n_shift,
        scratch_shapes=[pltpu.VMEM((tm, D_MODEL), BF16)] + shift_scratch,
    )
    outs = pl.pallas_call(
        functools.partial(_proj_kernel, tuple(shift_plan), drop),
        grid_spec=grid_spec,
        out_shape=[jax.ShapeDtypeStruct((m, Z_COLS), BF16), jax.ShapeDtypeStruct((m, QKV_COLS), F32)]
        + [jax.ShapeDtypeStruct(s.shape, s.dtype) for s in shift_srcs],
        compiler_params=_params("arbitrary", "arbitrary"),
        name="proj_in",
    )(order, x, w16, *shift_srcs)
    return outs[0], outs[1], outs[2:]


ZB_BLK = (2 * D_MODEL) // D_CONV


def _conv_prompt_kernel(zb_ref, zc_ref, zh_ref, cw_ref, p_ref, st_ref, uext):
    tm = zb_ref.shape[0]

    @pl.when(pl.program_id(1) == 0)
    def _():
        uext[0:SUBLANES, :] = jnp.zeros((SUBLANES, D_CONV), F32)

    u = zc_ref[...].astype(F32) * zh_ref[...].astype(F32)
    uext[SUBLANES:SUBLANES + tm, :] = u
    conv = (cw_ref[0:1, :] * uext[SUBLANES - 2:SUBLANES - 2 + tm, :]
            + cw_ref[1:2, :] * uext[SUBLANES - 1:SUBLANES - 1 + tm, :]
            + cw_ref[2:3, :] * u)
    p_ref[...] = (zb_ref[...].astype(F32) * conv).astype(BF16)
    last = uext[tm:tm + SUBLANES, :]
    uext[0:SUBLANES, :] = last
    st_ref[0] = last[SUBLANES - 2:SUBLANES, :]


def _conv_prompt(z, conv_w, bsz, seq):
    tm = min(CONV_TM, seq)
    nsb = seq // tm
    zspec = lambda blk: pl.BlockSpec((tm, D_CONV), lambda b, s: (b * nsb + s, blk))
    return pl.pallas_call(
        _conv_prompt_kernel,
        grid=(bsz, nsb),
        in_specs=[zspec(ZB_BLK), zspec(ZB_BLK + 1), zspec(ZB_BLK + 2),
                  pl.BlockSpec((CONV_W, D_CONV), lambda b, s: (0, 0))],
        out_specs=[pl.BlockSpec((tm, D_CONV), lambda b, s: (b * nsb + s, 0)),
                   pl.BlockSpec((1, CONV_W - 1, D_CONV), lambda b, s: (b, 0, 0))],
        out_shape=[jax.ShapeDtypeStruct((bsz * seq, D_CONV), BF16),
                   jax.ShapeDtypeStruct((bsz, CONV_W - 1, D_CONV), F32)],
        scratch_shapes=[pltpu.VMEM((tm + SUBLANES, D_CONV), F32)],
        compiler_params=_params("arbitrary", "arbitrary"),
        name="conv_prompt",
    )(z, z, z, conv_w)


def _conv_sample_kernel(t_new, zb_ref, zc_ref, zh_ref, cw_ref, s1_ref, s2_ref, p_ref, u_ref, uext):
    m = zb_ref.shape[0]
    u = zc_ref[...].astype(F32) * zh_ref[...].astype(F32)
    u_ref[...] = u
    uext[0:SUBLANES, :] = jnp.zeros((SUBLANES, D_CONV), F32)
    uext[SUBLANES:SUBLANES + m, :] = u
    t = lax.broadcasted_iota(jnp.int32, (m, D_CONV), 0) % t_new
    um1 = jnp.where(t < 1, s1_ref[...], uext[SUBLANES - 1:SUBLANES - 1 + m, :])
    um2 = jnp.where(t < 2, s2_ref[...], uext[SUBLANES - 2:SUBLANES - 2 + m, :])
    conv = cw_ref[0:1, :] * um2 + cw_ref[1:2, :] * um1 + cw_ref[2:3, :] * u
    p_ref[...] = (zb_ref[...].astype(F32) * conv).astype(BF16)


def _conv_sample(z, conv_w, state, t_new):
    m = z.shape[0]
    bd = m // t_new
    s1 = jnp.pad(state[:, 1:2], ((0, 0), (0, t_new - 1), (0, 0))).reshape(m, D_CONV)
    s2 = jnp.pad(state, ((0, 0), (0, t_new - 2), (0, 0))).reshape(m, D_CONV)
    zspec = lambda blk: pl.BlockSpec((m, D_CONV), lambda i: (0, blk))
    full = lambda r: pl.BlockSpec((r, D_CONV), lambda i: (0, 0))
    p, u = pl.pallas_call(
        functools.partial(_conv_sample_kernel, t_new),
        grid=(1,),
        in_specs=[zspec(ZB_BLK), zspec(ZB_BLK + 1), zspec(ZB_BLK + 2), full(CONV_W), full(m), full(m)],
        out_specs=[full(m), full(m)],
        out_shape=[jax.ShapeDtypeStruct((m, D_CONV), BF16), jax.ShapeDtypeStruct((m, D_CONV), F32)],
        scratch_shapes=[pltpu.VMEM((m + SUBLANES, D_CONV), F32)],
        compiler_params=_params("arbitrary"),
        name="conv_sample",
    )(z, z, z, conv_w, s1, s2)
    new_state = u.reshape(bd, t_new, D_CONV)[:, t_new - (CONV_W - 1):]
    return p, new_state


def _merge_groups(o_list, lse_list):
    top = functools.reduce(jnp.maximum, lse_list)
    es = [jnp.exp(l - top) for l in lse_list]
    num = functools.reduce(lambda a, b: a + b, [e * o for e, o in zip(es, o_list)])
    den = functools.reduce(lambda a, b: a + b, es)
    return num / den


def _nt_dot(a, b):
    return lax.dot_general(a, b, (((1,), (1,)), ((), ())), preferred_element_type=F32)


def _attn_prompt_group(q_ref, k_ref, v_ref, o_sc, lse_sc, g, dil, seq):
    shift = dil.bit_length() - 1
    qi = lax.broadcasted_iota(jnp.int32, (SPAN, 2 * SPAN), 0)
    kj = lax.broadcasted_iota(jnp.int32, (SPAN, 2 * SPAN), 1)
    band = jnp.logical_and(kj >= qi, kj <= qi + SPAN)
    is_prev = lax.broadcasted_iota(jnp.int32, (1, 2 * SPAN), 1) < SPAN
    ones = jnp.ones((2 * SPAN, HEAD_DIM), BF16)

    def rows(ref, start):
        if dil == 1:
            return ref[pl.ds(start, SPAN), :]
        return ref[pl.ds(start, SPAN, stride=dil), :]

    def body(trip, carry):
        blocks = range(ATTN_UNROLL)
        idx = [trip * ATTN_UNROLL + u for u in blocks]
        c = [i >> shift for i in idx]
        base = [c[u] * (SPAN * dil) + (idx[u] & (dil - 1)) for u in blocks]
        pbase = [jnp.maximum(c[u] - 1, 0) * (SPAN * dil) + (idx[u] & (dil - 1)) for u in blocks]
        q = [rows(q_ref, base[u]).astype(BF16) for u in blocks]
        k = [jnp.concatenate([rows(k_ref, pbase[u]), rows(k_ref, base[u])], axis=0).astype(BF16) for u in blocks]
        s = [_nt_dot(q[u], k[u]) for u in blocks]
        pen = [jnp.where(jnp.logical_and(is_prev, c[u] == 0), NEG_INF, 0.0) for u in blocks]
        s = [jnp.where(band, s[u] * ATT_SCALE + pen[u], NEG_INF) for u in blocks]
        m = [jnp.max(jnp.maximum(s[u][:, :SPAN], s[u][:, SPAN:]), axis=-1, keepdims=True) for u in blocks]
        p = [jnp.exp(s[u] - m[u]).astype(BF16) for u in blocks]
        v = [jnp.concatenate(
            [jnp.concatenate([rows(v_ref, pbase[u]), rows(v_ref, base[u])], axis=0).astype(BF16), ones], axis=1)
            for u in blocks]
        acc = [jnp.dot(p[u], v[u], preferred_element_type=F32) for u in blocks]
        for u in blocks:
            den = acc[u][:, HEAD_DIM:]
            o = acc[u][:, :HEAD_DIM] / den
            lse = m[u] + jnp.log(den)
            if dil == 1:
                o_sc[g, pl.ds(base[u], SPAN), :] = o
                lse_sc[g, pl.ds(base[u], SPAN), :] = lse
            else:
                o_sc[g, pl.ds(base[u], SPAN, stride=dil), :] = o
                lse_sc[g, pl.ds(base[u], SPAN, stride=dil), :] = lse
        return carry

    lax.fori_loop(0, seq // (SPAN * ATTN_UNROLL), body, 0)


def _attn_prompt_kernel(q_ref, k_ref, v_ref, o_ref, o_sc, lse_sc):
    seq = q_ref.shape[0]
    g_id = pl.program_id(2)
    for g, (_, dil) in enumerate(DILATED_GROUPS):
        @pl.when(g_id == g)
        def _(g=g, dil=dil):
            _attn_prompt_group(q_ref, k_ref, v_ref, o_sc, lse_sc, g, dil, seq)

    @pl.when(g_id == N_DIL - 1)
    def _():
        chunk = 256

        def body(i, carry):
            sl = pl.ds(pl.multiple_of(i * chunk, chunk), chunk)
            o = _merge_groups([o_sc[g, sl, :] for g in range(N_DIL)], [lse_sc[g, sl, :] for g in range(N_DIL)])
            o_ref[sl, :] = o.astype(BF16)
            return carry

        lax.fori_loop(0, seq // chunk, body, 0)


def _attn_prompt(qkv, bsz, seq):
    qkv3 = qkv.reshape(bsz, seq, QKV_COLS)
    per_sel = D_ATT // HEAD_DIM
    spec = lambda sel: pl.BlockSpec((None, seq, HEAD_DIM),
                                    lambda b, j, g: (b, 0, sel * per_sel + g * HEADS_PER_GROUP + j))
    out = pl.pallas_call(
        _attn_prompt_kernel,
        grid=(bsz, HEADS_PER_GROUP, N_DIL),
        in_specs=[spec(0), spec(1), spec(2)],
        out_specs=pl.BlockSpec((None, seq, HEAD_DIM), lambda b, j, g: (b, 0, j)),
        out_shape=jax.ShapeDtypeStruct((bsz, seq, D_GRP), BF16),
        scratch_shapes=[pltpu.VMEM((N_DIL, seq, HEAD_DIM), F32), pltpu.VMEM((N_DIL, seq, HEAD_DIM), F32)],
        compiler_params=_params("arbitrary", "arbitrary", "arbitrary"),
        name="attn_prompt",
    )(qkv3, qkv3, qkv3)
    return out.reshape(bsz * seq, D_GRP)


KV_ROWS = 2 * HEADS_PER_GROUP


def _attn_sample_kernel(dil, q_ref, cache_ref, new_ref, shifted_hbm, buf_ref, o_ref, lse_ref):
    del shifted_hbm
    t_new = q_ref.shape[0]
    n_buf = cache_ref.shape[0] // KV_ROWS
    tq = lax.broadcasted_iota(jnp.int32, (t_new, n_buf), 0)
    pos = lax.broadcasted_iota(jnp.int32, (t_new, n_buf), 1)
    back = n_buf + tq - pos
    cache_ok = jnp.logical_and(back <= dil * SPAN, (back & (dil - 1)) == 0)
    tq2 = lax.broadcasted_iota(jnp.int32, (t_new, t_new), 0)
    tk2 = lax.broadcasted_iota(jnp.int32, (t_new, t_new), 1)
    new_ok = jnp.logical_and(tk2 <= tq2, ((tq2 - tk2) & (dil - 1)) == 0)
    for h in range(HEADS_PER_GROUP):
        cols = slice(h * HEAD_DIM, (h + 1) * HEAD_DIM)
        q = q_ref[:, cols].astype(BF16)
        k_c = cache_ref[pl.ds(h, n_buf, stride=KV_ROWS), :].astype(BF16)
        v_c = cache_ref[pl.ds(HEADS_PER_GROUP + h, n_buf, stride=KV_ROWS), :].astype(BF16)
        k_n = new_ref[pl.ds(h, t_new, stride=KV_ROWS), :].astype(BF16)
        v_n = new_ref[pl.ds(HEADS_PER_GROUP + h, t_new, stride=KV_ROWS), :].astype(BF16)
        s_c = jnp.where(cache_ok, _nt_dot(q, k_c) * ATT_SCALE, NEG_INF)
        s_n = jnp.where(new_ok, _nt_dot(q, k_n) * ATT_SCALE, NEG_INF)
        m = jnp.maximum(jnp.max(s_c, axis=-1, keepdims=True), jnp.max(s_n, axis=-1, keepdims=True))
        p_c = jnp.exp(s_c - m)
        p_n = jnp.exp(s_n - m)
        den = jnp.sum(p_c, axis=-1, keepdims=True) + jnp.sum(p_n, axis=-1, keepdims=True)
        acc = (jnp.dot(p_c.astype(BF16), v_c, preferred_element_type=F32)
               + jnp.dot(p_n.astype(BF16), v_n, preferred_element_type=F32))
        o_ref[:, cols] = acc / den
        lse_ref[:, cols] = jnp.broadcast_to(m + jnp.log(den), (t_new, HEAD_DIM))
    buf_ref[...] = new_ref[...]


def _attn_sample(q3, cache, new_kv, shifted, g):
    window, dil = DILATED_GROUPS[g]
    bd, t_new, _ = q3.shape
    rows = cache.shape[1]
    new_rows = t_new * KV_ROWS
    assert rows == window * KV_ROWS, "the cache must hold a full window"
    assert rows % new_rows == 0
    row_spec = lambda r: pl.BlockSpec((None, r, HEAD_DIM), lambda b: (b, 0, 0))
    tok_spec = pl.BlockSpec((None, t_new, D_GRP), lambda b: (b, 0, 0))
    return pl.pallas_call(
        functools.partial(_attn_sample_kernel, dil),
        grid=(bd,),
        in_specs=[pl.BlockSpec((None, t_new, D_GRP), lambda b: (b, 0, g)), row_spec(rows),
                  row_spec(new_rows), pl.BlockSpec(memory_space=pl.ANY)],
        out_specs=[pl.BlockSpec((None, new_rows, HEAD_DIM), lambda b: (b, rows // new_rows - 1, 0)),
                   tok_spec, tok_spec],
        out_shape=[jax.ShapeDtypeStruct((bd, rows, HEAD_DIM), F32),
                   jax.ShapeDtypeStruct((bd, t_new, D_GRP), F32),
                   jax.ShapeDtypeStruct((bd, t_new, D_GRP), F32)],
        input_output_aliases={3: 0},
        compiler_params=_params("arbitrary"),
        name=f"attn_sample_w{window}",
    )(q3, cache, new_kv, shifted)


def _merge_sample_kernel(o0, o1, o2, l0, l1, l2, out_ref):
    out_ref[...] = _merge_groups([o0[...], o1[...], o2[...]], [l0[...], l1[...], l2[...]]).astype(BF16)


def _merge_sample(os_, ls_):
    m = os_[0].shape[0]
    spec = pl.BlockSpec((m, D_GRP), lambda i: (0, 0))
    return pl.pallas_call(
        _merge_sample_kernel,
        grid=(1,),
        in_specs=[spec] * 6,
        out_specs=spec,
        out_shape=jax.ShapeDtypeStruct((m, D_GRP), BF16),
        compiler_params=_params("arbitrary"),
        name="merge_sample",
    )(*os_, *ls_)


def _layer_norm(v, g, b):
    mu = jnp.mean(v, axis=-1, keepdims=True)
    d = v - mu
    var = jnp.mean(d * d, axis=-1, keepdims=True)
    return d * lax.rsqrt(var + LN_EPS) * g + b


SLAB_ROWS = SUBLANES
HALF = D_MODEL // 2
HI_MASK = 0xFFFF0000


def _bf16_bits(v):
    return pltpu.bitcast(v.astype(BF16).astype(F32), jnp.uint32)


def _pack_slabs(src_ref, slab_ref, n_tok):
    def body(tg, carry):
        t0 = pl.multiple_of(tg * SUBLANES, SUBLANES)
        for s in range(SLAB_ROWS):
            lo = src_ref[pl.ds(t0, SUBLANES), s * LANES:(s + 1) * LANES]
            hi = src_ref[pl.ds(t0, SUBLANES), HALF + s * LANES:HALF + (s + 1) * LANES]
            word = (_bf16_bits(lo) >> 16) | _bf16_bits(hi)
            slab_ref[pl.ds(t0 * SLAB_ROWS + s, SUBLANES, stride=SLAB_ROWS), :] = word
        return carry
    lax.fori_loop(0, n_tok // SUBLANES, body, 0)


def _unpack_slabs(slab_ref, dst_ref, n_tok):
    def body(tg, carry):
        t0 = pl.multiple_of(tg * SUBLANES, SUBLANES)
        for s in range(SLAB_ROWS):
            word = slab_ref[pl.ds(t0 * SLAB_ROWS + s, SUBLANES, stride=SLAB_ROWS), :]
            dst_ref[pl.ds(t0, SUBLANES), s * LANES:(s + 1) * LANES] = pltpu.bitcast(word << 16, F32)
            dst_ref[pl.ds(t0, SUBLANES), HALF + s * LANES:HALF + (s + 1) * LANES] = pltpu.bitcast(
                word & jnp.uint32(HI_MASK), F32)
        return carry
    lax.fori_loop(0, n_tok // SUBLANES, body, 0)


def _unpack_slabs_bf16(slab_ref, dst_ref, n_tok, n_valid):
    pair = 2 * SUBLANES
    sub = lax.broadcasted_iota(jnp.int32, (SUBLANES, LANES), 0)

    def body(tg, carry):
        t0 = pl.multiple_of(tg * pair, pair)
        for s in range(SLAB_ROWS):
            words = []
            for half in range(2):
                t1 = t0 + half * SUBLANES
                w = slab_ref[pl.ds(t1 * SLAB_ROWS + s, SUBLANES, stride=SLAB_ROWS), :]
                words.append(jnp.where(t1 + sub < n_valid, w, jnp.uint32(0)))
            lo = jnp.concatenate([pltpu.bitcast(w << 16, F32) for w in words], axis=0)
            hi = jnp.concatenate([pltpu.bitcast(w & jnp.uint32(HI_MASK), F32) for w in words], axis=0)
            dst_ref[pl.ds(t0, pair), s * LANES:(s + 1) * LANES] = lo.astype(BF16)
            dst_ref[pl.ds(t0, pair), HALF + s * LANES:HALF + (s + 1) * LANES] = hi.astype(BF16)
        return carry
    lax.fori_loop(0, n_tok // pair, body, 0)


def _mix_kernel(p_ref, o_ref, gc_ref, ga_ref, x_ref, wc_ref, wa_ref, wo_ref, g_ref, b_ref, *rest):
    h_ref, hs_ref = rest[-2:]
    y_conv = jnp.dot(p_ref[...], wc_ref[...], preferred_element_type=F32)
    y_att = jnp.dot(o_ref[...], wa_ref[...], preferred_element_type=F32)
    merged = (jax.nn.sigmoid(gc_ref[...].astype(F32)) * y_conv
              + jax.nn.sigmoid(ga_ref[...].astype(F32)) * y_att)
    mix = jnp.dot(merged.astype(BF16), wo_ref[...], preferred_element_type=F32)
    h_ref[...] = _layer_norm(ALPHA * x_ref[...] + mix, g_ref[...], b_ref[...])
    _pack_slabs(h_ref, hs_ref, h_ref.shape[0])


def _mix(p, o, z, x, wc16, wa16, wo16, ln_g, ln_b, tm, total, h_all=None, row_off=0):
    m = p.shape[0]
    assert row_off % tm == 0
    blk_off = row_off // tm
    row = lambda w: pl.BlockSpec((tm, w), lambda i: (i, 0))
    const = lambda a: pl.BlockSpec(a.shape, lambda i: (0, 0), pipeline_mode=pl.Buffered(1))
    in_specs = [row(D_CONV), row(D_GRP),
                pl.BlockSpec((tm, D_MODEL), lambda i: (i, 0)), pl.BlockSpec((tm, D_MODEL), lambda i: (i, 1)),
                row(D_MODEL), const(wc16), const(wa16), const(wo16), const(ln_g), const(ln_b)]
    args = [p, o, z, z, x, wc16, wa16, wo16, ln_g, ln_b]
    aliases = {}
    if h_all is not None:
        in_specs += [pl.BlockSpec(memory_space=pl.ANY)] * 2
        args += list(h_all)
        aliases = {len(args) - 2: 0, len(args) - 1: 1}
    return pl.pallas_call(
        _mix_kernel,
        grid=(m // tm,),
        in_specs=in_specs,
        out_specs=[pl.BlockSpec((tm, D_MODEL), lambda i: (i + blk_off, 0)),
                   pl.BlockSpec((tm * SLAB_ROWS, LANES), lambda i: (i + blk_off, 0))],
        out_shape=[jax.ShapeDtypeStruct((total, D_MODEL), F32),
                   jax.ShapeDtypeStruct((total * SLAB_ROWS, LANES), jnp.uint32)],
        input_output_aliases=aliases,
        compiler_params=_params("arbitrary"),
        name="mix_ln1",
    )(*args)


def _router_kernel(h_ref, w_ref, b_ref, idx_ref, gate_ref, cnt_ref, carry):
    tile = h_ref.shape[0]

    @pl.when(pl.program_id(0) == 0)
    def _():
        carry[...] = jnp.zeros(carry.shape, F32)

    logits = jnp.dot(h_ref[...], w_ref[...], preferred_element_type=F32,
                     precision=lax.Precision.HIGHEST) + b_ref[...]
    lane = lax.broadcasted_iota(jnp.int32, (tile, LANES), 1)
    lane_f = lane.astype(F32)
    first_max = lambda v, mx: jnp.min(jnp.where(v == mx, lane_f, float(LANES)), axis=-1, keepdims=True)

    is_grp = lane < N_EXPERT_GROUPS
    lg = jnp.where(is_grp, logits, -jnp.inf)
    mg = jnp.max(lg, axis=-1, keepdims=True)
    gsel = first_max(lg, mg).astype(jnp.int32)
    p_group = 1.0 / jnp.sum(jnp.where(is_grp, jnp.exp(logits - mg), 0.0), axis=-1, keepdims=True)

    e_lane = lane - N_EXPERT_GROUPS
    in_grp = jnp.logical_and(e_lane >= 0, (e_lane >> 3) == gsel)
    le = jnp.where(in_grp, logits, -jnp.inf)
    v1 = jnp.max(le, axis=-1, keepdims=True)
    i1 = first_max(le, v1).astype(jnp.int32)
    le2 = jnp.where(lane == i1, -jnp.inf, le)
    v2 = jnp.max(le2, axis=-1, keepdims=True)
    i2 = first_max(le2, v2).astype(jnp.int32)
    ex = jnp.exp(v2 - v1)
    g1 = p_group / (1.0 + ex)
    g2 = p_group * ex / (1.0 + ex)
    e1 = i1 - N_EXPERT_GROUPS
    e2 = i2 - N_EXPERT_GROUPS

    hot1 = lane == e1
    hot2 = lane == e2
    hot = jnp.where(jnp.logical_or(hot1, hot2), 1.0, 0.0)
    rr = lax.broadcasted_iota(jnp.int32, (tile, tile), 0)
    cc = lax.broadcasted_iota(jnp.int32, (tile, tile), 1)
    tri = jnp.where(cc < rr, 1.0, 0.0).astype(BF16)
    before = jnp.dot(tri, hot.astype(BF16), preferred_element_type=F32) + carry[0:1, :]
    r1 = jnp.sum(jnp.where(hot1, before, 0.0), axis=-1, keepdims=True).astype(jnp.int32)
    r2 = jnp.sum(jnp.where(hot2, before, 0.0), axis=-1, keepdims=True).astype(jnp.int32)
    carry[...] = carry[...] + jnp.sum(hot, axis=0, keepdims=True)

    idx_ref[...] = jnp.where(lane == 0, e1, jnp.where(lane == 1, e2, jnp.where(lane == 2, r1, r2)))
    gate_ref[...] = jnp.where(lane == 0, g1, g2)
    cnt_ref[...] = carry[...]


def _router(h_all, w_r, b_r):
    m = h_all.shape[0]
    tile = ROUTER_TILE if m % ROUTER_TILE == 0 else ROW_TILE
    row = pl.BlockSpec((tile, LANES), lambda i: (i, 0))
    return pl.pallas_call(
        _router_kernel,
        grid=(m // tile,),
        in_specs=[pl.BlockSpec((tile, D_MODEL), lambda i: (i, 0)),
                  pl.BlockSpec((D_MODEL, LANES), lambda i: (0, 0)),
                  pl.BlockSpec((1, LANES), lambda i: (0, 0))],
        out_specs=[row, row, pl.BlockSpec((SUBLANES, LANES), lambda i: (0, 0))],
        out_shape=[jax.ShapeDtypeStruct((m, LANES), jnp.int32), jax.ShapeDtypeStruct((m, LANES), F32),
                   jax.ShapeDtypeStruct((SUBLANES, LANES), F32)],
        scratch_shapes=[pltpu.VMEM((SUBLANES, LANES), F32)],
        compiler_params=_params("arbitrary"),
        name="router",
    )(h_all, w_r, b_r)


def _slab(ref, row):
    return ref.at[pl.ds(pl.multiple_of(row * SLAB_ROWS, SLAB_ROWS), SLAB_ROWS), :]


def _dispatch_kernel(tile, dest_ref, h_ref, x_hbm, sem):
    base = pl.program_id(0) * tile

    def copy(r, k):
        return pltpu.make_async_copy(_slab(h_ref, r), _slab(x_hbm, dest_ref[(base + r) * TOP_K + k]), sem)

    def start(r, c):
        for k in range(TOP_K):
            copy(r, k).start()
        return c

    def wait(r, c):
        for k in range(TOP_K):
            copy(r, k).wait()
        return c

    lax.fori_loop(0, tile, start, 0, unroll=DMA_UNROLL)
    lax.fori_loop(0, tile, wait, 0, unroll=DMA_UNROLL)


def _dispatch(h_slabs, dest_flat, n_rows):
    m = h_slabs.shape[0] // SLAB_ROWS
    tile = DISPATCH_TILE if m % DISPATCH_TILE == 0 else ROW_TILE
    grid_spec = pltpu.PrefetchScalarGridSpec(
        num_scalar_prefetch=1,
        grid=(m // tile,),
        in_specs=[pl.BlockSpec((tile * SLAB_ROWS, LANES), lambda i, d: (i, 0))],
        out_specs=pl.BlockSpec(memory_space=pl.ANY),
        scratch_shapes=[pltpu.SemaphoreType.DMA(())],
    )
    return pl.pallas_call(
        functools.partial(_dispatch_kernel, tile),
        grid_spec=grid_spec,
        out_shape=jax.ShapeDtypeStruct((n_rows * SLAB_ROWS, LANES), jnp.uint32),
        compiler_params=pltpu.CompilerParams(dimension_semantics=("arbitrary",), has_side_effects=True),
        name="moe_dispatch",
    )(dest_flat, h_slabs)


def _expert_kernel(blk_e, blk_first, blk_valid, blk_ord, exp_list, n_used, n_exp,
                   x_ref, wg_hbm, wu_hbm, wd_hbm, y_ref, wg32, wu32, wd32, wg16, wu16, wd16, sems, rows32, x16):
    i = pl.program_id(0)

    def weight_copies(e, slot):
        return [pltpu.make_async_copy(hbm.at[e], buf.at[slot], sems.at[slot, n])
                for n, (hbm, buf) in enumerate(((wg_hbm, wg32), (wu_hbm, wu32), (wd_hbm, wd32)))]

    def start_weights(ordinal, slot):
        for c in weight_copies(exp_list[ordinal], slot):
            c.start()

    @pl.when(i == 0)
    def _():
        start_weights(0, 0)

        @pl.when(n_exp[0] > 1)
        def _():
            start_weights(1, 1)

    @pl.when(i < n_used[0])
    def _():
        @pl.when(blk_first[i] == 1)
        def _():
            ordinal = blk_ord[i]
            slot = ordinal & 1
            for c in weight_copies(blk_e[i], slot):
                c.wait()
            wg16[...] = wg32[slot].astype(BF16)
            wu16[...] = wu32[slot].astype(BF16)
            wd16[...] = wd32[slot].astype(BF16)

            @pl.when(ordinal + 2 < n_exp[0])
            def _():
                start_weights(ordinal + 2, slot)

        _unpack_slabs_bf16(x_ref, x16, MOE_BLK, blk_valid[i])
        x = x16[...]
        gate = jnp.dot(x, wg16[...], preferred_element_type=F32)
        up = jnp.dot(x, wu16[...], preferred_element_type=F32)
        hid = (jax.nn.silu(gate) * up).astype(BF16)
        rows32[...] = jnp.dot(hid, wd16[...], preferred_element_type=F32)
        _pack_slabs(rows32, y_ref, MOE_BLK)


def _experts(x_slabs, plan, w_gate, w_up, w_down):
    n_rows = x_slabs.shape[0] // SLAB_ROWS
    n_blocks = n_rows // MOE_BLK
    row_map = lambda i, be, bf, bv, bo, el, nu, ne: (jnp.minimum(i, nu[0] - 1), 0)
    hbm = pl.BlockSpec(memory_space=pl.ANY)
    grid_spec = pltpu.PrefetchScalarGridSpec(
        num_scalar_prefetch=7,
        grid=(n_blocks,),
        in_specs=[pl.BlockSpec((MOE_BLK * SLAB_ROWS, LANES), row_map), hbm, hbm, hbm],
        out_specs=pl.BlockSpec((MOE_BLK * SLAB_ROWS, LANES), row_map),
        scratch_shapes=[pltpu.VMEM((2, D_MODEL, D_EXPERT), F32), pltpu.VMEM((2, D_MODEL, D_EXPERT), F32),
                        pltpu.VMEM((2, D_EXPERT, D_MODEL), F32),
                        pltpu.VMEM((D_MODEL, D_EXPERT), BF16), pltpu.VMEM((D_MODEL, D_EXPERT), BF16),
                        pltpu.VMEM((D_EXPERT, D_MODEL), BF16),
                        pltpu.SemaphoreType.DMA((2, 3)),
                        pltpu.VMEM((MOE_BLK, D_MODEL), F32), pltpu.VMEM((MOE_BLK, D_MODEL), BF16)],
    )
    return pl.pallas_call(
        _expert_kernel,
        grid_spec=grid_spec,
        out_shape=jax.ShapeDtypeStruct((n_rows * SLAB_ROWS, LANES), jnp.uint32),
        compiler_params=_params("arbitrary"),
        name="moe_experts",
    )(*plan, x_slabs, w_gate, w_up, w_down)


def _combine_kernel(tok_off, tile, dest_ref, y_hbm, h_ref, gate_ref, g_ref, b_ref, out_ref, ybuf, y32, sems):
    i = pl.program_id(0)
    n_steps = pl.num_programs(0)

    def copy(step, r, k):
        slot = step & 1
        row = dest_ref[(tok_off + step * tile + r) * TOP_K + k]
        return pltpu.make_async_copy(_slab(y_hbm, row), _slab(ybuf.at[slot, k], r), sems.at[slot])

    def gather(step):
        def body(r, c):
            for k in range(TOP_K):
                copy(step, r, k).start()
            return c
        lax.fori_loop(0, tile, body, 0, unroll=DMA_UNROLL)

    @pl.when(i == 0)
    def _():
        gather(i)

    @pl.when(i + 1 < n_steps)
    def _():
        gather(i + 1)

    def wait(r, c):
        for k in range(TOP_K):
            copy(i, r, k).wait()
        return c

    lax.fori_loop(0, tile, wait, 0, unroll=DMA_UNROLL)
    slot = i & 1
    for k in range(TOP_K):
        _unpack_slabs(ybuf.at[slot, k], y32.at[k], tile)
    f = gate_ref[:, 0:1] * y32[0] + gate_ref[:, 1:2] * y32[1]
    out_ref[...] = _layer_norm(ALPHA * h_ref[...] + f, g_ref[...], b_ref[...])


def _combine(y_slabs, dest_flat, h_all, gates, ln_g, ln_b, tok_off, n_tok):
    tile = COMBINE_TILE if (n_tok % COMBINE_TILE == 0 and tok_off % COMBINE_TILE == 0) else ROW_TILE
    blk_off = tok_off // tile
    grid_spec = pltpu.PrefetchScalarGridSpec(
        num_scalar_prefetch=1,
        grid=(n_tok // tile,),
        in_specs=[pl.BlockSpec(memory_space=pl.ANY),
                  pl.BlockSpec((tile, D_MODEL), lambda i, d: (i + blk_off, 0)),
                  pl.BlockSpec((tile, LANES), lambda i, d: (i + blk_off, 0)),
                  pl.BlockSpec((1, D_MODEL), lambda i, d: (0, 0)),
                  pl.BlockSpec((1, D_MODEL), lambda i, d: (0, 0))],
        out_specs=pl.BlockSpec((tile, D_MODEL), lambda i, d: (i, 0)),
        scratch_shapes=[pltpu.VMEM((2, TOP_K, tile * SLAB_ROWS, LANES), jnp.uint32),
                        pltpu.VMEM((TOP_K, tile, D_MODEL), F32), pltpu.SemaphoreType.DMA((2,))],
    )
    return pl.pallas_call(
        functools.partial(_combine_kernel, tok_off, tile),
        grid_spec=grid_spec,
        out_shape=jax.ShapeDtypeStruct((n_tok, D_MODEL), F32),
        compiler_params=_params("arbitrary"),
        name="combine_ln2",
    )(dest_flat, y_slabs, h_all, gates, ln_g, ln_b)


def _moe_plan(idx, counts_f):
    m = idx.shape[0]
    n_asg = m * TOP_K
    experts = idx[:, 0:TOP_K]
    ranks = idx[:, TOP_K:2 * TOP_K]
    counts = counts_f[0, :N_EXPERTS].astype(jnp.int32)
    padded = (counts + MOE_BLK - 1) // MOE_BLK * MOE_BLK
    pad_end = jnp.cumsum(padded)
    pad_start = pad_end - padded
    hot = experts[:, :, None] == jnp.arange(N_EXPERTS, dtype=jnp.int32)
    dest = (jnp.sum(jnp.where(hot, pad_start, 0), axis=-1) + ranks).reshape(n_asg).astype(jnp.int32)
    n_blocks = -(-(n_asg + N_EXPERTS * (MOE_BLK - 1)) // MOE_BLK)
    n_used = jnp.maximum(pad_end[-1] // MOE_BLK, 1).astype(jnp.int32)
    blk = jnp.minimum(jnp.arange(n_blocks, dtype=jnp.int32), n_used - 1)
    blk_e = jnp.minimum(jnp.searchsorted(pad_end, blk * MOE_BLK, side='right'), N_EXPERTS - 1).astype(jnp.int32)
    blk_first = (blk * MOE_BLK == pad_start[blk_e]).astype(jnp.int32)
    blk_valid = jnp.clip(counts[blk_e] - (blk * MOE_BLK - pad_start[blk_e]), 0, MOE_BLK).astype(jnp.int32)
    used = counts > 0
    exp_list = jnp.argsort(jnp.logical_not(used), stable=True).astype(jnp.int32)
    blk_ord = (jnp.cumsum(used.astype(jnp.int32)) - 1)[blk_e].astype(jnp.int32)
    n_exp = jnp.sum(used.astype(jnp.int32)).reshape(1)
    plan = (blk_e, blk_first, blk_valid, blk_ord, exp_list, n_used.reshape(1), n_exp)
    return dest, n_blocks * MOE_BLK, plan


def kernel(x_prompt, x_sample, cache_kv_w128, cache_kv_w512, cache_kv_w2048, state_conv, w_in, conv_w, w_conv_out, w_att_out, w_o, ln1_g, ln1_b, w_router_group, b_router_group, w_router_expert, b_router_expert, w_expert_gate, w_expert_up, w_expert_down, ln2_g, ln2_b):
    bsz, seq, _ = x_prompt.shape
    bd, t_new, _ = x_sample.shape
    m_p = bsz * seq
    m_s = bd * t_new
    m_all = m_p + m_s
    caches = (cache_kv_w128, cache_kv_w512, cache_kv_w2048)
    lyr = 0

    w_in16 = w_in[lyr].astype(BF16)
    wc16 = w_conv_out[lyr].astype(BF16)
    wa16 = w_att_out[lyr].astype(BF16)
    wo16 = w_o[lyr].astype(BF16)
    cw = conv_w[lyr]
    g1, b1 = ln1_g[lyr][None, :], ln1_b[lyr][None, :]
    g2, b2 = ln2_g[lyr][None, :], ln2_b[lyr][None, :]
    w_re = jnp.transpose(w_router_expert[lyr], (1, 0, 2)).reshape(D_MODEL, N_EXPERTS)
    w_r = jnp.pad(jnp.concatenate([w_router_group[lyr], w_re], axis=1),
                  ((0, 0), (0, LANES - N_EXPERT_GROUPS - N_EXPERTS)))
    b_r = jnp.pad(jnp.concatenate([b_router_group[lyr], b_router_expert[lyr].reshape(N_EXPERTS)]),
                  (0, LANES - N_EXPERT_GROUPS - N_EXPERTS))[None, :]

    xp = x_prompt.reshape(m_p, D_MODEL)
    xs = x_sample.reshape(m_s, D_MODEL)

    caches3 = [c[lyr].reshape(bd, -1, HEAD_DIM) for c in caches]
    z_p, qkv_p, shifted = _proj_in(xp, w_in16, min(PROJ_TM, m_p),
                                   [c.reshape(-1, HEAD_DIM) for c in caches3], t_new * KV_ROWS)
    shifted = [s.reshape(c.shape) for s, c in zip(shifted, caches3)]
    p_p, conv_p = _conv_prompt(z_p, cw, bsz, seq)
    o_p = _attn_prompt(qkv_p, bsz, seq)
    h_buf = _mix(p_p, o_p, z_p, xp, wc16, wa16, wo16, g1, b1, min(MIX_TM, m_p), m_all)

    z_s, qkv_s, _ = _proj_in(xs, w_in16, m_s)
    p_s, conv_s = _conv_sample(z_s, cw, state_conv[lyr], t_new)
    q3 = qkv_s.reshape(bd, t_new, QKV_COLS)
    kv5 = qkv_s.reshape(bd, t_new, 3, N_DIL, HEADS_PER_GROUP, HEAD_DIM)
    os_, ls_, bufs_s = [], [], []
    for g, (window, _) in enumerate(DILATED_GROUPS):
        new_kv = kv5[:, :, 1:3, g].reshape(bd, t_new * KV_ROWS, HEAD_DIM)
        buf, o_g, lse_g = _attn_sample(q3, caches3[g], new_kv, shifted[g], g)
        bufs_s.append(buf.reshape(1, bd, window, 2, HEADS_PER_GROUP, HEAD_DIM))
        os_.append(o_g.reshape(m_s, D_GRP))
        ls_.append(lse_g.reshape(m_s, D_GRP))
    o_s = _merge_sample(os_, ls_)
    h_buf = _mix(p_s, o_s, z_s, xs, wc16, wa16, wo16, g1, b1, m_s, m_all, h_all=h_buf, row_off=m_p)

    h_all, h_slabs = h_buf
    idx, gates, counts = _router(h_all, w_r, b_r)
    dest, n_rows, plan = _moe_plan(idx, counts)
    x_slabs = _dispatch(h_slabs, dest, n_rows)
    y_slabs = _experts(x_slabs, plan, w_expert_gate[lyr], w_expert_up[lyr], w_expert_down[lyr])
    y_p = _combine(y_slabs, dest, h_all, gates, g2, b2, 0, m_p)
    y_s = _combine(y_slabs, dest, h_all, gates, g2, b2, m_p, m_s)

    qkv_p3 = qkv_p.reshape(bsz, seq, QKV_COLS)
    bufs_p = []
    for g, (w, _) in enumerate(DILATED_GROUPS):
        keep = min(w, seq)
        k_g = qkv_p3[:, seq - keep:, D_ATT + g * D_GRP:D_ATT + (g + 1) * D_GRP]
        v_g = qkv_p3[:, seq - keep:, 2 * D_ATT + g * D_GRP:2 * D_ATT + (g + 1) * D_GRP]
        bufs_p.append(jnp.stack([k_g, v_g], axis=2).reshape(1, bsz, keep, 2, HEADS_PER_GROUP, HEAD_DIM))
    return (y_p.reshape(bsz, seq, D_MODEL), y_s.reshape(bd, t_new, D_MODEL),
            bufs_p[0], bufs_p[1], bufs_p[2], conv_p[None],
            bufs_s[0], bufs_s[1], bufs_s[2], conv_s[None])
```

```python
import functools

import jax
import jax.numpy as jnp
from jax import lax
from jax.experimental import pallas as pl
from jax.experimental.pallas import tpu as pltpu

F32 = jnp.float32
BF16 = jnp.bfloat16

D_MODEL = 2048
D_CONV = D_MODEL // 2
CONV_W = 3
DILATED_GROUPS = ((128, 1), (512, 4), (2048, 16))
N_DIL = len(DILATED_GROUPS)
HEADS_PER_GROUP = 4
HEAD_DIM = 128
SPAN = 128
D_GRP = HEADS_PER_GROUP * HEAD_DIM
D_ATT = N_DIL * D_GRP
N_EXPERT_GROUPS = 8
EXPERTS_PER_GROUP = 8
N_EXPERTS = N_EXPERT_GROUPS * EXPERTS_PER_GROUP
TOP_K = 2
D_EXPERT = D_MODEL // 4
DEPTH = 1
LN_EPS = 1e-5
ALPHA = (2 * DEPTH) ** 0.25
NEG_INF = -1e30
ATT_SCALE = HEAD_DIM ** -0.5

OFF_B = 0
OFF_C = D_CONV
OFF_H = 2 * D_CONV
OFF_Q = 3 * D_CONV
OFF_K = OFF_Q + D_ATT
OFF_V = OFF_K + D_ATT
OFF_GC = OFF_V + D_ATT
OFF_GA = OFF_GC + D_MODEL
N_IN = OFF_GA + D_MODEL

LANES = 128
SUBLANES = 8
VMEM_LIMIT = 56 * 1024 * 1024

PROJ_TN = 512
PROJ_TM = 1024
Z_COLS = 2 * D_MODEL + 3 * D_CONV
QKV_COLS = 3 * D_ATT
NZ_TILES = Z_COLS // PROJ_TN
NQ_TILES = QKV_COLS // PROJ_TN
CONV_TM = 512
MIX_TM = 512
ROUTER_TILE = 640
ROW_TILE = 128
MOE_BLK = 256
ATTN_UNROLL = 4
DISPATCH_TILE = 640
COMBINE_TILE = 256
KV_PACK_TM = 512
SHIFT_CHUNK_ROWS = 4128
SHIFT_AHEAD = 2
SHIFT_SLOTS = 2 * SHIFT_AHEAD
DMA_UNROLL = 8


def _proj_tile_order():
    t = lambda off, width: list(range(off // PROJ_TN, (off + width) // PROJ_TN))
    z_tiles = t(OFF_GC, D_MODEL) + t(OFF_GA, D_MODEL) + t(OFF_B, D_CONV) + t(OFF_C, D_CONV) + t(OFF_H, D_CONV)
    q_tiles = t(OFF_Q, D_ATT) + t(OFF_K, D_ATT) + t(OFF_V, D_ATT)
    return z_tiles + q_tiles


def _params(*sem):
    return pltpu.CompilerParams(dimension_semantics=sem, vmem_limit_bytes=VMEM_LIMIT)


def _shift_chunks(rows, drop, max_rows):
    keep = rows - drop
    best = SUBLANES
    for c in range(SUBLANES, max_rows + 1, SUBLANES):
        if keep % c == 0:
            best = c
    return best, keep // best


def _proj_kernel(shift_plan, drop, order_ref, x_ref, w_ref, *rest):
    del order_ref
    n_shift = len(shift_plan)
    srcs = rest[:n_shift]
    z_ref, qkv_ref = rest[n_shift:n_shift + 2]
    dsts = rest[n_shift + 2:2 * n_shift + 2]
    x16 = rest[2 * n_shift + 2]
    j = pl.program_id(1)

    if n_shift:
        stage, sem_in, sem_out = rest[2 * n_shift + 3:]
        t = pl.program_id(0) * pl.num_programs(1) + j
        n_chunks = shift_plan[-1][2]

        def for_chunk(q, action):
            for n, (chunk, lo, hi) in enumerate(shift_plan):
                @pl.when(jnp.logical_and(q >= lo, q < hi))
                def _(n=n, chunk=chunk, lo=lo):
                    action(n, chunk, pl.multiple_of((q - lo) * chunk, SUBLANES), q & (SHIFT_SLOTS - 1))

        def load(n, chunk, start, slot):
            return pltpu.make_async_copy(srcs[n].at[pl.ds(start + drop, chunk), :],
                                         stage.at[slot, pl.ds(0, chunk), :], sem_in.at[slot])

        def store(n, chunk, start, slot):
            return pltpu.make_async_copy(stage.at[slot, pl.ds(0, chunk), :],
                                         dsts[n].at[pl.ds(start, chunk), :], sem_out.at[slot])

        @pl.when(t == 0)
        def _():
            for q in range(SHIFT_AHEAD):
                for_chunk(t + q, lambda *a: load(*a).start())

        @pl.when(t < n_chunks)
        def _():
            for_chunk(t, lambda *a: load(*a).wait())
            for_chunk(t, lambda *a: store(*a).start())

        @pl.when(jnp.logical_and(t >= SHIFT_AHEAD, t < n_chunks + SHIFT_AHEAD))
        def _():
            for_chunk(t - SHIFT_AHEAD, lambda *a: store(*a).wait())

        @pl.when(t + SHIFT_AHEAD < n_chunks)
        def _():
            for_chunk(t + SHIFT_AHEAD, lambda *a: load(*a).start())

    @pl.when(j == 0)
    def _():
        x16[...] = x_ref[...].astype(BF16)

    @pl.when(j < NZ_TILES)
    def _():
        z_ref[...] = jnp.dot(x16[...], w_ref[...], preferred_element_type=F32).astype(BF16)

    @pl.when(j >= NZ_TILES)
    def _():
        qkv_ref[...] = jnp.dot(x16[...], w_ref[...], preferred_element_type=F32)


def _proj_in(x, w16, tm, shift_srcs=(), drop=0):
    m = x.shape[0]
    n_shift = len(shift_srcs)
    n_steps = (m // tm) * (NZ_TILES + NQ_TILES)
    order = jnp.asarray(_proj_tile_order(), jnp.int32)
    hbm = pl.BlockSpec(memory_space=pl.ANY)
    shift_plan, first = [], 0
    for s in shift_srcs:
        chunk, count = _shift_chunks(s.shape[0], drop, SHIFT_CHUNK_ROWS)
        shift_plan.append((chunk, first, first + count))
        first += count
    assert first + SHIFT_AHEAD <= n_steps, "one shift chunk per grid step, plus the steps that drain"
    shift_scratch = [pltpu.VMEM((SHIFT_SLOTS, SHIFT_CHUNK_ROWS, LANES), F32),
                     pltpu.SemaphoreType.DMA((SHIFT_SLOTS,)), pltpu.SemaphoreType.DMA((SHIFT_SLOTS,))] if n_shift else []
    grid_spec = pltpu.PrefetchScalarGridSpec(
        num_scalar_prefetch=1,
        grid=(m // tm, NZ_TILES + NQ_TILES),
        in_specs=[
            pl.BlockSpec((tm, D_MODEL), lambda i, j, o: (i, 0)),
            pl.BlockSpec((D_MODEL, PROJ_TN), lambda i, j, o: (0, o[j])),
        ] + [hbm] * n_shift,
        out_specs=[
            pl.BlockSpec((tm, PROJ_TN), lambda i, j, o: (i, jnp.minimum(j, NZ_TILES - 1))),
            pl.BlockSpec((tm, PROJ_TN), lambda i, j, o: (i, jnp.maximum(j - NZ_TILES, 0))),
        ] + [hbm] * n_shift,
        scratch_shapes=[pltpu.VMEM((tm, D_MODEL), BF16)] + shift_scratch,
    )
    outs = pl.pallas_call(
        functools.partial(_proj_kernel, tuple(shift_plan), drop),
        grid_spec=grid_spec,
        out_shape=[jax.ShapeDtypeStruct((m, Z_COLS), BF16), jax.ShapeDtypeStruct((m, QKV_COLS), F32)]
        + [jax.ShapeDtypeStruct(s.shape, s.dtype) for s in shift_srcs],
        compiler_params=_params("arbitrary", "arbitrary"),
        name="proj_in",
    )(order, x, w16, *shift_srcs)
    return outs[0], outs[1], outs[2:]


ZB_BLK = (2 * D_MODEL) // D_CONV


def _conv_prompt_kernel(zb_ref, zc_ref, zh_ref, cw_ref, p_ref, st_ref, uext):
    tm = zb_ref.shape[0]

    @pl.when(pl.program_id(1) == 0)
    def _():
        uext[0:SUBLANES, :] = jnp.zeros((SUBLANES, D_CONV), F32)

    u = zc_ref[...].astype(F32) * zh_ref[...].astype(F32)
    uext[SUBLANES:SUBLANES + tm, :] = u
    conv = (cw_ref[0:1, :] * uext[SUBLANES - 2:SUBLANES - 2 + tm, :]
            + cw_ref[1:2, :] * uext[SUBLANES - 1:SUBLANES - 1 + tm, :]
            + cw_ref[2:3, :] * u)
    p_ref[...] = (zb_ref[...].astype(F32) * conv).astype(BF16)
    last = uext[tm:tm + SUBLANES, :]
    uext[0:SUBLANES, :] = last
    st_ref[0] = last[SUBLANES - 2:SUBLANES, :]


def _conv_prompt(z, conv_w, bsz, seq):
    tm = min(CONV_TM, seq)
    nsb = seq // tm
    zspec = lambda blk: pl.BlockSpec((tm, D_CONV), lambda b, s: (b * nsb + s, blk))
    return pl.pallas_call(
        _conv_prompt_kernel,
        grid=(bsz, nsb),
        in_specs=[zspec(ZB_BLK), zspec(ZB_BLK + 1), zspec(ZB_BLK + 2),
                  pl.BlockSpec((CONV_W, D_CONV), lambda b, s: (0, 0))],
        out_specs=[pl.BlockSpec((tm, D_CONV), lambda b, s: (b * nsb + s, 0)),
                   pl.BlockSpec((1, CONV_W - 1, D_CONV), lambda b, s: (b, 0, 0))],
        out_shape=[jax.ShapeDtypeStruct((bsz * seq, D_CONV), BF16),
                   jax.ShapeDtypeStruct((bsz, CONV_W - 1, D_CONV), F32)],
        scratch_shapes=[pltpu.VMEM((tm + SUBLANES, D_CONV), F32)],
        compiler_params=_params("arbitrary", "arbitrary"),
        name="conv_prompt",
    )(z, z, z, conv_w)


def _conv_sample_kernel(t_new, zb_ref, zc_ref, zh_ref, cw_ref, s1_ref, s2_ref, p_ref, u_ref, uext):
    m = zb_ref.shape[0]
    u = zc_ref[...].astype(F32) * zh_ref[...].astype(F32)
    u_ref[...] = u
    uext[0:SUBLANES, :] = jnp.zeros((SUBLANES, D_CONV), F32)
    uext[SUBLANES:SUBLANES + m, :] = u
    t = lax.broadcasted_iota(jnp.int32, (m, D_CONV), 0) % t_new
    um1 = jnp.where(t < 1, s1_ref[...], uext[SUBLANES - 1:SUBLANES - 1 + m, :])
    um2 = jnp.where(t < 2, s2_ref[...], uext[SUBLANES - 2:SUBLANES - 2 + m, :])
    conv = cw_ref[0:1, :] * um2 + cw_ref[1:2, :] * um1 + cw_ref[2:3, :] * u
    p_ref[...] = (zb_ref[...].astype(F32) * conv).astype(BF16)


def _conv_sample(z, conv_w, state, t_new):
    m = z.shape[0]
    bd = m // t_new
    s1 = jnp.pad(state[:, 1:2], ((0, 0), (0, t_new - 1), (0, 0))).reshape(m, D_CONV)
    s2 = jnp.pad(state, ((0, 0), (0, t_new - 2), (0, 0))).reshape(m, D_CONV)
    zspec = lambda blk: pl.BlockSpec((m, D_CONV), lambda i: (0, blk))
    full = lambda r: pl.BlockSpec((r, D_CONV), lambda i: (0, 0))
    p, u = pl.pallas_call(
        functools.partial(_conv_sample_kernel, t_new),
        grid=(1,),
        in_specs=[zspec(ZB_BLK), zspec(ZB_BLK + 1), zspec(ZB_BLK + 2), full(CONV_W), full(m), full(m)],
        out_specs=[full(m), full(m)],
        out_shape=[jax.ShapeDtypeStruct((m, D_CONV), BF16), jax.ShapeDtypeStruct((m, D_CONV), F32)],
        scratch_shapes=[pltpu.VMEM((m + SUBLANES, D_CONV), F32)],
        compiler_params=_params("arbitrary"),
        name="conv_sample",
    )(z, z, z, conv_w, s1, s2)
    new_state = u.reshape(bd, t_new, D_CONV)[:, t_new - (CONV_W - 1):]
    return p, new_state


def _merge_groups(o_list, lse_list):
    top = functools.reduce(jnp.maximum, lse_list)
    es = [jnp.exp(l - top) for l in lse_list]
    num = functools.reduce(lambda a, b: a + b, [e * o for e, o in zip(es, o_list)])
    den = functools.reduce(lambda a, b: a + b, es)
    return num / den


def _nt_dot(a, b):
    return lax.dot_general(a, b, (((1,), (1,)), ((), ())), preferred_element_type=F32)


def _attn_prompt_group(q_ref, k_ref, v_ref, o_sc, lse_sc, g, dil, seq):
    shift = dil.bit_length() - 1
    qi = lax.broadcasted_iota(jnp.int32, (SPAN, 2 * SPAN), 0)
    kj = lax.broadcasted_iota(jnp.int32, (SPAN, 2 * SPAN), 1)
    band = jnp.logical_and(kj >= qi, kj <= qi + SPAN)
    is_prev = lax.broadcasted_iota(jnp.int32, (1, 2 * SPAN), 1) < SPAN
    ones = jnp.ones((2 * SPAN, HEAD_DIM), BF16)

    def rows(ref, start):
        if dil == 1:
            return ref[pl.ds(start, SPAN), :]
        return ref[pl.ds(start, SPAN, stride=dil), :]

    def body(trip, carry):
        blocks = range(ATTN_UNROLL)
        idx = [trip * ATTN_UNROLL + u for u in blocks]
        c = [i >> shift for i in idx]
        base = [c[u] * (SPAN * dil) + (idx[u] & (dil - 1)) for u in blocks]
        pbase = [jnp.maximum(c[u] - 1, 0) * (SPAN * dil) + (idx[u] & (dil - 1)) for u in blocks]
        q = [rows(q_ref, base[u]).astype(BF16) for u in blocks]
        k = [jnp.concatenate([rows(k_ref, pbase[u]), rows(k_ref, base[u])], axis=0).astype(BF16) for u in blocks]
        s = [_nt_dot(q[u], k[u]) for u in blocks]
        pen = [jnp.where(jnp.logical_and(is_prev, c[u] == 0), NEG_INF, 0.0) for u in blocks]
        s = [jnp.where(band, s[u] * ATT_SCALE + pen[u], NEG_INF) for u in blocks]
        m = [jnp.max(jnp.maximum(s[u][:, :SPAN], s[u][:, SPAN:]), axis=-1, keepdims=True) for u in blocks]
        p = [jnp.exp(s[u] - m[u]).astype(BF16) for u in blocks]
        v = [jnp.concatenate(
            [jnp.concatenate([rows(v_ref, pbase[u]), rows(v_ref, base[u])], axis=0).astype(BF16), ones], axis=1)
            for u in blocks]
        acc = [jnp.dot(p[u], v[u], preferred_element_type=F32) for u in blocks]
        for u in blocks:
            den = acc[u][:, HEAD_DIM:]
            o = acc[u][:, :HEAD_DIM] / den
            lse = m[u] + jnp.log(den)
            if dil == 1:
                o_sc[g, pl.ds(base[u], SPAN), :] = o
                lse_sc[g, pl.ds(base[u], SPAN), :] = lse
            else:
                o_sc[g, pl.ds(base[u], SPAN, stride=dil), :] = o
                lse_sc[g, pl.ds(base[u], SPAN, stride=dil), :] = lse
        return carry

    lax.fori_loop(0, seq // (SPAN * ATTN_UNROLL), body, 0)


def _attn_prompt_kernel(q_ref, k_ref, v_ref, o_ref, o_sc, lse_sc):
    seq = q_ref.shape[0]
    g_id = pl.program_id(2)
    for g, (_, dil) in enumerate(DILATED_GROUPS):
        @pl.when(g_id == g)
        def _(g=g, dil=dil):
            _attn_prompt_group(q_ref, k_ref, v_ref, o_sc, lse_sc, g, dil, seq)

    @pl.when(g_id == N_DIL - 1)
    def _():
        chunk = 256

        def body(i, carry):
            sl = pl.ds(pl.multiple_of(i * chunk, chunk), chunk)
            o = _merge_groups([o_sc[g, sl, :] for g in range(N_DIL)], [lse_sc[g, sl, :] for g in range(N_DIL)])
            o_ref[sl, :] = o.astype(BF16)
            return carry

        lax.fori_loop(0, seq // chunk, body, 0)


def _attn_prompt(qkv, bsz, seq):
    qkv3 = qkv.reshape(bsz, seq, QKV_COLS)
    per_sel = D_ATT // HEAD_DIM
    spec = lambda sel: pl.BlockSpec((None, seq, HEAD_DIM),
                                    lambda b, j, g: (b, 0, sel * per_sel + g * HEADS_PER_GROUP + j))
    out = pl.pallas_call(
        _attn_prompt_kernel,
        grid=(bsz, HEADS_PER_GROUP, N_DIL),
        in_specs=[spec(0), spec(1), spec(2)],
        out_specs=pl.BlockSpec((None, seq, HEAD_DIM), lambda b, j, g: (b, 0, j)),
        out_shape=jax.ShapeDtypeStruct((bsz, seq, D_GRP), BF16),
        scratch_shapes=[pltpu.VMEM((N_DIL, seq, HEAD_DIM), F32), pltpu.VMEM((N_DIL, seq, HEAD_DIM), F32)],
        compiler_params=_params("arbitrary", "arbitrary", "arbitrary"),
        name="attn_prompt",
    )(qkv3, qkv3, qkv3)
    return out.reshape(bsz * seq, D_GRP)


KV_ROWS = 2 * HEADS_PER_GROUP


def _kv_pack_kernel(k_ref, v_ref, out_ref):
    tm = k_ref.shape[0]
    for h in range(HEADS_PER_GROUP):
        cols = slice(h * HEAD_DIM, (h + 1) * HEAD_DIM)
        out_ref[pl.ds(h, tm, stride=KV_ROWS), :] = k_ref[:, cols]
        out_ref[pl.ds(HEADS_PER_GROUP + h, tm, stride=KV_ROWS), :] = v_ref[:, cols]


def _kv_pack(qkv3, g, keep):
    bsz, seq, _ = qkv3.shape
    tm = min(KV_PACK_TM, keep)
    assert keep % tm == 0 and (seq - keep) % tm == 0
    first = (seq - keep) // tm
    col = lambda sel: (sel * D_ATT + g * D_GRP) // D_GRP
    spec = lambda sel: pl.BlockSpec((None, tm, D_GRP), lambda b, s: (b, first + s, col(sel)))
    return pl.pallas_call(
        _kv_pack_kernel,
        grid=(bsz, keep // tm),
        in_specs=[spec(1), spec(2)],
        out_specs=pl.BlockSpec((None, tm * KV_ROWS, HEAD_DIM), lambda b, s: (b, s, 0)),
        out_shape=jax.ShapeDtypeStruct((bsz, keep * KV_ROWS, HEAD_DIM), F32),
        compiler_params=_params("arbitrary", "arbitrary"),
        name="kv_pack",
    )(qkv3, qkv3)


def _attn_sample_kernel(dil, q_ref, cache_ref, new_ref, shifted_hbm, buf_ref, o_ref, lse_ref):
    del shifted_hbm
    t_new = q_ref.shape[0]
    n_buf = cache_ref.shape[0] // KV_ROWS
    tq = lax.broadcasted_iota(jnp.int32, (t_new, n_buf), 0)
    pos = lax.broadcasted_iota(jnp.int32, (t_new, n_buf), 1)
    back = n_buf + tq - pos
    cache_ok = jnp.logical_and(back <= dil * SPAN, (back & (dil - 1)) == 0)
    tq2 = lax.broadcasted_iota(jnp.int32, (t_new, t_new), 0)
    tk2 = lax.broadcasted_iota(jnp.int32, (t_new, t_new), 1)
    new_ok = jnp.logical_and(tk2 <= tq2, ((tq2 - tk2) & (dil - 1)) == 0)
    for h in range(HEADS_PER_GROUP):
        cols = slice(h * HEAD_DIM, (h + 1) * HEAD_DIM)
        q = q_ref[:, cols].astype(BF16)
        k_c = cache_ref[pl.ds(h, n_buf, stride=KV_ROWS), :].astype(BF16)
        v_c = cache_ref[pl.ds(HEADS_PER_GROUP + h, n_buf, stride=KV_ROWS), :].astype(BF16)
        k_n = new_ref[pl.ds(h, t_new, stride=KV_ROWS), :].astype(BF16)
        v_n = new_ref[pl.ds(HEADS_PER_GROUP + h, t_new, stride=KV_ROWS), :].astype(BF16)
        s_c = jnp.where(cache_ok, _nt_dot(q, k_c) * ATT_SCALE, NEG_INF)
        s_n = jnp.where(new_ok, _nt_dot(q, k_n) * ATT_SCALE, NEG_INF)
        m = jnp.maximum(jnp.max(s_c, axis=-1, keepdims=True), jnp.max(s_n, axis=-1, keepdims=True))
        p_c = jnp.exp(s_c - m)
        p_n = jnp.exp(s_n - m)
        den = jnp.sum(p_c, axis=-1, keepdims=True) + jnp.sum(p_n, axis=-1, keepdims=True)
        acc = (jnp.dot(p_c.astype(BF16), v_c, preferred_element_type=F32)
               + jnp.dot(p_n.astype(BF16), v_n, preferred_element_type=F32))
        o_ref[:, cols] = acc / den
        lse_ref[:, cols] = jnp.broadcast_to(m + jnp.log(den), (t_new, HEAD_DIM))
    buf_ref[...] = new_ref[...]


def _attn_sample(q3, cache, new_kv, shifted, g):
    window, dil = DILATED_GROUPS[g]
    bd, t_new, _ = q3.shape
    rows = cache.shape[1]
    new_rows = t_new * KV_ROWS
    assert rows == window * KV_ROWS, "the cache must hold a full window"
    assert rows % new_rows == 0
    row_spec = lambda r: pl.BlockSpec((None, r, HEAD_DIM), lambda b: (b, 0, 0))
    tok_spec = pl.BlockSpec((None, t_new, D_GRP), lambda b: (b, 0, 0))
    return pl.pallas_call(
        functools.partial(_attn_sample_kernel, dil),
        grid=(bd,),
        in_specs=[pl.BlockSpec((None, t_new, D_GRP), lambda b: (b, 0, g)), row_spec(rows),
                  row_spec(new_rows), pl.BlockSpec(memory_space=pl.ANY)],
        out_specs=[pl.BlockSpec((None, new_rows, HEAD_DIM), lambda b: (b, rows // new_rows - 1, 0)),
                   tok_spec, tok_spec],
        out_shape=[jax.ShapeDtypeStruct((bd, rows, HEAD_DIM), F32),
                   jax.ShapeDtypeStruct((bd, t_new, D_GRP), F32),
                   jax.ShapeDtypeStruct((bd, t_new, D_GRP), F32)],
        input_output_aliases={3: 0},
        compiler_params=_params("arbitrary"),
        name=f"attn_sample_w{window}",
    )(q3, cache, new_kv, shifted)


def _merge_sample_kernel(o0, o1, o2, l0, l1, l2, out_ref):
    out_ref[...] = _merge_groups([o0[...], o1[...], o2[...]], [l0[...], l1[...], l2[...]]).astype(BF16)


def _merge_sample(os_, ls_):
    m = os_[0].shape[0]
    spec = pl.BlockSpec((m, D_GRP), lambda i: (0, 0))
    return pl.pallas_call(
        _merge_sample_kernel,
        grid=(1,),
        in_specs=[spec] * 6,
        out_specs=spec,
        out_shape=jax.ShapeDtypeStruct((m, D_GRP), BF16),
        compiler_params=_params("arbitrary"),
        name="merge_sample",
    )(*os_, *ls_)


def _layer_norm(v, g, b):
    mu = jnp.mean(v, axis=-1, keepdims=True)
    d = v - mu
    var = jnp.mean(d * d, axis=-1, keepdims=True)
    return d * lax.rsqrt(var + LN_EPS) * g + b


SLAB_ROWS = SUBLANES
HALF = D_MODEL // 2
HI_MASK = 0xFFFF0000


def _bf16_bits(v):
    return pltpu.bitcast(v.astype(BF16).astype(F32), jnp.uint32)


def _pack_slabs(src_ref, slab_ref, n_tok):
    def body(tg, carry):
        t0 = pl.multiple_of(tg * SUBLANES, SUBLANES)
        for s in range(SLAB_ROWS):
            lo = src_ref[pl.ds(t0, SUBLANES), s * LANES:(s + 1) * LANES]
            hi = src_ref[pl.ds(t0, SUBLANES), HALF + s * LANES:HALF + (s + 1) * LANES]
            word = (_bf16_bits(lo) >> 16) | _bf16_bits(hi)
            slab_ref[pl.ds(t0 * SLAB_ROWS + s, SUBLANES, stride=SLAB_ROWS), :] = word
        return carry
    lax.fori_loop(0, n_tok // SUBLANES, body, 0)


def _unpack_slabs(slab_ref, dst_ref, n_tok):
    def body(tg, carry):
        t0 = pl.multiple_of(tg * SUBLANES, SUBLANES)
        for s in range(SLAB_ROWS):
            word = slab_ref[pl.ds(t0 * SLAB_ROWS + s, SUBLANES, stride=SLAB_ROWS), :]
            dst_ref[pl.ds(t0, SUBLANES), s * LANES:(s + 1) * LANES] = pltpu.bitcast(word << 16, F32)
            dst_ref[pl.ds(t0, SUBLANES), HALF + s * LANES:HALF + (s + 1) * LANES] = pltpu.bitcast(
                word & jnp.uint32(HI_MASK), F32)
        return carry
    lax.fori_loop(0, n_tok // SUBLANES, body, 0)


def _unpack_slabs_bf16(slab_ref, dst_ref, n_tok, n_valid):
    pair = 2 * SUBLANES
    sub = lax.broadcasted_iota(jnp.int32, (SUBLANES, LANES), 0)

    def body(tg, carry):
        t0 = pl.multiple_of(tg * pair, pair)
        for s in range(SLAB_ROWS):
            words = []
            for half in range(2):
                t1 = t0 + half * SUBLANES
                w = slab_ref[pl.ds(t1 * SLAB_ROWS + s, SUBLANES, stride=SLAB_ROWS), :]
                words.append(jnp.where(t1 + sub < n_valid, w, jnp.uint32(0)))
            lo = jnp.concatenate([pltpu.bitcast(w << 16, F32) for w in words], axis=0)
            hi = jnp.concatenate([pltpu.bitcast(w & jnp.uint32(HI_MASK), F32) for w in words], axis=0)
            dst_ref[pl.ds(t0, pair), s * LANES:(s + 1) * LANES] = lo.astype(BF16)
            dst_ref[pl.ds(t0, pair), HALF + s * LANES:HALF + (s + 1) * LANES] = hi.astype(BF16)
        return carry
    lax.fori_loop(0, n_tok // pair, body, 0)


def _mix_kernel(p_ref, o_ref, gc_ref, ga_ref, x_ref, wc_ref, wa_ref, wo_ref, g_ref, b_ref, *rest):
    h_ref, hs_ref = rest[-2:]
    y_conv = jnp.dot(p_ref[...], wc_ref[...], preferred_element_type=F32)
    y_att = jnp.dot(o_ref[...], wa_ref[...], preferred_element_type=F32)
    merged = (jax.nn.sigmoid(gc_ref[...].astype(F32)) * y_conv
              + jax.nn.sigmoid(ga_ref[...].astype(F32)) * y_att)
    mix = jnp.dot(merged.astype(BF16), wo_ref[...], preferred_element_type=F32)
    h_ref[...] = _layer_norm(ALPHA * x_ref[...] + mix, g_ref[...], b_ref[...])
    _pack_slabs(h_ref, hs_ref, h_ref.shape[0])


def _mix(p, o, z, x, wc16, wa16, wo16, ln_g, ln_b, tm, total, h_all=None, row_off=0):
    m = p.shape[0]
    assert row_off % tm == 0
    blk_off = row_off // tm
    row = lambda w: pl.BlockSpec((tm, w), lambda i: (i, 0))
    const = lambda a: pl.BlockSpec(a.shape, lambda i: (0, 0), pipeline_mode=pl.Buffered(1))
    in_specs = [row(D_CONV), row(D_GRP),
                pl.BlockSpec((tm, D_MODEL), lambda i: (i, 0)), pl.BlockSpec((tm, D_MODEL), lambda i: (i, 1)),
                row(D_MODEL), const(wc16), const(wa16), const(wo16), const(ln_g), const(ln_b)]
    args = [p, o, z, z, x, wc16, wa16, wo16, ln_g, ln_b]
    aliases = {}
    if h_all is not None:
        in_specs += [pl.BlockSpec(memory_space=pl.ANY)] * 2
        args += list(h_all)
        aliases = {len(args) - 2: 0, len(args) - 1: 1}
    return pl.pallas_call(
        _mix_kernel,
        grid=(m // tm,),
        in_specs=in_specs,
        out_specs=[pl.BlockSpec((tm, D_MODEL), lambda i: (i + blk_off, 0)),
                   pl.BlockSpec((tm * SLAB_ROWS, LANES), lambda i: (i + blk_off, 0))],
        out_shape=[jax.ShapeDtypeStruct((total, D_MODEL), F32),
                   jax.ShapeDtypeStruct((total * SLAB_ROWS, LANES), jnp.uint32)],
        input_output_aliases=aliases,
        compiler_params=_params("arbitrary"),
        name="mix_ln1",
    )(*args)


def _route(h, w_ref, b_ref, carry):
    tile = h.shape[0]
    logits = jnp.dot(h, w_ref[...], preferred_element_type=F32, precision=lax.Precision.HIGHEST) + b_ref[...]
    lane = lax.broadcasted_iota(jnp.int32, (tile, LANES), 1)
    lane_f = lane.astype(F32)
    first_max = lambda v, mx: jnp.min(jnp.where(v == mx, lane_f, float(LANES)), axis=-1, keepdims=True)

    is_grp = lane < N_EXPERT_GROUPS
    lg = jnp.where(is_grp, logits, -jnp.inf)
    mg = jnp.max(lg, axis=-1, keepdims=True)
    gsel = first_max(lg, mg).astype(jnp.int32)
    p_group = 1.0 / jnp.sum(jnp.where(is_grp, jnp.exp(logits - mg), 0.0), axis=-1, keepdims=True)

    e_lane = lane - N_EXPERT_GROUPS
    in_grp = jnp.logical_and(e_lane >= 0, (e_lane >> 3) == gsel)
    le = jnp.where(in_grp, logits, -jnp.inf)
    v1 = jnp.max(le, axis=-1, keepdims=True)
    i1 = first_max(le, v1).astype(jnp.int32)
    le2 = jnp.where(lane == i1, -jnp.inf, le)
    v2 = jnp.max(le2, axis=-1, keepdims=True)
    i2 = first_max(le2, v2).astype(jnp.int32)
    ex = jnp.exp(v2 - v1)
    g1 = p_group / (1.0 + ex)
    g2 = p_group * ex / (1.0 + ex)
    e1 = i1 - N_EXPERT_GROUPS
    e2 = i2 - N_EXPERT_GROUPS

    hot1 = lane == e1
    hot2 = lane == e2
    hot = jnp.where(jnp.logical_or(hot1, hot2), 1.0, 0.0)
    rr = lax.broadcasted_iota(jnp.int32, (tile, tile), 0)
    cc = lax.broadcasted_iota(jnp.int32, (tile, tile), 1)
    tri = jnp.where(cc < rr, 1.0, 0.0).astype(BF16)
    before = jnp.dot(tri, hot.astype(BF16), preferred_element_type=F32) + carry[0:1, :]
    r1 = jnp.sum(jnp.where(hot1, before, 0.0), axis=-1, keepdims=True).astype(jnp.int32)
    r2 = jnp.sum(jnp.where(hot2, before, 0.0), axis=-1, keepdims=True).astype(jnp.int32)
    carry[...] = carry[...] + jnp.sum(hot, axis=0, keepdims=True)

    idx = jnp.where(lane == 0, e1, jnp.where(lane == 1, e2, jnp.where(lane == 2, r1, r2)))
    return idx, jnp.where(lane == 0, g1, g2)


def _router_kernel(h_ref, w_ref, b_ref, idx_ref, gate_ref, cnt_ref, carry):
    @pl.when(pl.program_id(0) == 0)
    def _():
        carry[...] = jnp.zeros(carry.shape, F32)

    idx_ref[...], gate_ref[...] = _route(h_ref[...], w_ref, b_ref, carry)
    cnt_ref[...] = carry[...]


def _router(h_all, w_r, b_r):
    m = h_all.shape[0]
    tile = ROUTER_TILE if m % ROUTER_TILE == 0 else ROW_TILE
    row = pl.BlockSpec((tile, LANES), lambda i: (i, 0))
    return pl.pallas_call(
        _router_kernel,
        grid=(m // tile,),
        in_specs=[pl.BlockSpec((tile, D_MODEL), lambda i: (i, 0)),
                  pl.BlockSpec((D_MODEL, LANES), lambda i: (0, 0)),
                  pl.BlockSpec((1, LANES), lambda i: (0, 0))],
        out_specs=[row, row, pl.BlockSpec((SUBLANES, LANES), lambda i: (0, 0))],
        out_shape=[jax.ShapeDtypeStruct((m, LANES), jnp.int32), jax.ShapeDtypeStruct((m, LANES), F32),
                   jax.ShapeDtypeStruct((SUBLANES, LANES), F32)],
        scratch_shapes=[pltpu.VMEM((SUBLANES, LANES), F32)],
        compiler_params=_params("arbitrary"),
        name="router",
    )(h_all, w_r, b_r)


def _slab(ref, row):
    return ref.at[pl.ds(pl.multiple_of(row * SLAB_ROWS, SLAB_ROWS), SLAB_ROWS), :]


def _dispatch_kernel(tile, dest_ref, h_ref, x_hbm, sem):
    base = pl.program_id(0) * tile

    def copy(r, k):
        return pltpu.make_async_copy(_slab(h_ref, r), _slab(x_hbm, dest_ref[(base + r) * TOP_K + k]), sem)

    def start(r, c):
        for k in range(TOP_K):
            copy(r, k).start()
        return c

    def wait(r, c):
        for k in range(TOP_K):
            copy(r, k).wait()
        return c

    lax.fori_loop(0, tile, start, 0, unroll=DMA_UNROLL)
    lax.fori_loop(0, tile, wait, 0, unroll=DMA_UNROLL)


def _dispatch(h_slabs, dest_flat, n_rows):
    m = h_slabs.shape[0] // SLAB_ROWS
    tile = DISPATCH_TILE if m % DISPATCH_TILE == 0 else ROW_TILE
    grid_spec = pltpu.PrefetchScalarGridSpec(
        num_scalar_prefetch=1,
        grid=(m // tile,),
        in_specs=[pl.BlockSpec((tile * SLAB_ROWS, LANES), lambda i, d: (i, 0))],
        out_specs=pl.BlockSpec(memory_space=pl.ANY),
        scratch_shapes=[pltpu.SemaphoreType.DMA(())],
    )
    return pl.pallas_call(
        functools.partial(_dispatch_kernel, tile),
        grid_spec=grid_spec,
        out_shape=jax.ShapeDtypeStruct((n_rows * SLAB_ROWS, LANES), jnp.uint32),
        compiler_params=pltpu.CompilerParams(dimension_semantics=("arbitrary",), has_side_effects=True),
        name="moe_dispatch",
    )(dest_flat, h_slabs)


def _expert_kernel(blk_e, blk_first, blk_valid, blk_ord, exp_list, n_used, n_exp,
                   x_ref, wg_hbm, wu_hbm, wd_hbm, y_ref, wg32, wu32, wd32, wg16, wu16, wd16, sems, rows32, x16):
    i = pl.program_id(0)

    def weight_copies(e, slot):
        return [pltpu.make_async_copy(hbm.at[e], buf.at[slot], sems.at[slot, n])
                for n, (hbm, buf) in enumerate(((wg_hbm, wg32), (wu_hbm, wu32), (wd_hbm, wd32)))]

    def start_weights(ordinal, slot):
        for c in weight_copies(exp_list[ordinal], slot):
            c.start()

    @pl.when(i == 0)
    def _():
        start_weights(0, 0)

        @pl.when(n_exp[0] > 1)
        def _():
            start_weights(1, 1)

    @pl.when(i < n_used[0])
    def _():
        @pl.when(blk_first[i] == 1)
        def _():
            ordinal = blk_ord[i]
            slot = ordinal & 1
            for c in weight_copies(blk_e[i], slot):
                c.wait()
            wg16[...] = wg32[slot].astype(BF16)
            wu16[...] = wu32[slot].astype(BF16)
            wd16[...] = wd32[slot].astype(BF16)

            @pl.when(ordinal + 2 < n_exp[0])
            def _():
                start_weights(ordinal + 2, slot)

        _unpack_slabs_bf16(x_ref, x16, MOE_BLK, blk_valid[i])
        x = x16[...]
        gate = jnp.dot(x, wg16[...], preferred_element_type=F32)
        up = jnp.dot(x, wu16[...], preferred_element_type=F32)
        hid = (jax.nn.silu(gate) * up).astype(BF16)
        rows32[...] = jnp.dot(hid, wd16[...], preferred_element_type=F32)
        _pack_slabs(rows32, y_ref, MOE_BLK)


def _experts(x_slabs, plan, w_gate, w_up, w_down):
    n_rows = x_slabs.shape[0] // SLAB_ROWS
    n_blocks = n_rows // MOE_BLK
    row_map = lambda i, be, bf, bv, bo, el, nu, ne: (jnp.minimum(i, nu[0] - 1), 0)
    hbm = pl.BlockSpec(memory_space=pl.ANY)
    grid_spec = pltpu.PrefetchScalarGridSpec(
        num_scalar_prefetch=7,
        grid=(n_blocks,),
        in_specs=[pl.BlockSpec((MOE_BLK * SLAB_ROWS, LANES), row_map), hbm, hbm, hbm],
        out_specs=pl.BlockSpec((MOE_BLK * SLAB_ROWS, LANES), row_map),
        scratch_shapes=[pltpu.VMEM((2, D_MODEL, D_EXPERT), F32), pltpu.VMEM((2, D_MODEL, D_EXPERT), F32),
                        pltpu.VMEM((2, D_EXPERT, D_MODEL), F32),
                        pltpu.VMEM((D_MODEL, D_EXPERT), BF16), pltpu.VMEM((D_MODEL, D_EXPERT), BF16),
                        pltpu.VMEM((D_EXPERT, D_MODEL), BF16),
                        pltpu.SemaphoreType.DMA((2, 3)),
                        pltpu.VMEM((MOE_BLK, D_MODEL), F32), pltpu.VMEM((MOE_BLK, D_MODEL), BF16)],
    )
    return pl.pallas_call(
        _expert_kernel,
        grid_spec=grid_spec,
        out_shape=jax.ShapeDtypeStruct((n_rows * SLAB_ROWS, LANES), jnp.uint32),
        compiler_params=_params("arbitrary"),
        name="moe_experts",
    )(*plan, x_slabs, w_gate, w_up, w_down)


def _combine_kernel(tok_off, tile, dest_ref, y_hbm, h_ref, gate_ref, g_ref, b_ref, out_ref, ybuf, y32, sems):
    i = pl.program_id(0)
    n_steps = pl.num_programs(0)

    def copy(step, r, k):
        slot = step & 1
        row = dest_ref[(tok_off + step * tile + r) * TOP_K + k]
        return pltpu.make_async_copy(_slab(y_hbm, row), _slab(ybuf.at[slot, k], r), sems.at[slot])

    def gather(step):
        def body(r, c):
            for k in range(TOP_K):
                copy(step, r, k).start()
            return c
        lax.fori_loop(0, tile, body, 0, unroll=DMA_UNROLL)

    @pl.when(i == 0)
    def _():
        gather(i)

    @pl.when(i + 1 < n_steps)
    def _():
        gather(i + 1)

    def wait(r, c):
        for k in range(TOP_K):
            copy(i, r, k).wait()
        return c

    lax.fori_loop(0, tile, wait, 0, unroll=DMA_UNROLL)
    slot = i & 1
    for k in range(TOP_K):
        _unpack_slabs(ybuf.at[slot, k], y32.at[k], tile)
    f = gate_ref[:, 0:1] * y32[0] + gate_ref[:, 1:2] * y32[1]
    out_ref[...] = _layer_norm(ALPHA * h_ref[...] + f, g_ref[...], b_ref[...])


def _combine(y_slabs, dest_flat, h_all, gates, ln_g, ln_b, tok_off, n_tok):
    tile = COMBINE_TILE if (n_tok % COMBINE_TILE == 0 and tok_off % COMBINE_TILE == 0) else ROW_TILE
    blk_off = tok_off // tile
    grid_spec = pltpu.PrefetchScalarGridSpec(
        num_scalar_prefetch=1,
        grid=(n_tok // tile,),
        in_specs=[pl.BlockSpec(memory_space=pl.ANY),
                  pl.BlockSpec((tile, D_MODEL), lambda i, d: (i + blk_off, 0)),
                  pl.BlockSpec((tile, LANES), lambda i, d: (i + blk_off, 0)),
                  pl.BlockSpec((1, D_MODEL), lambda i, d: (0, 0)),
                  pl.BlockSpec((1, D_MODEL), lambda i, d: (0, 0))],
        out_specs=pl.BlockSpec((tile, D_MODEL), lambda i, d: (i, 0)),
        scratch_shapes=[pltpu.VMEM((2, TOP_K, tile * SLAB_ROWS, LANES), jnp.uint32),
                        pltpu.VMEM((TOP_K, tile, D_MODEL), F32), pltpu.SemaphoreType.DMA((2,))],
    )
    return pl.pallas_call(
        functools.partial(_combine_kernel, tok_off, tile),
        grid_spec=grid_spec,
        out_shape=jax.ShapeDtypeStruct((n_tok, D_MODEL), F32),
        compiler_params=_params("arbitrary"),
        name="combine_ln2",
    )(dest_flat, y_slabs, h_all, gates, ln_g, ln_b)


def _moe_plan(idx, counts_f):
    m = idx.shape[0]
    n_asg = m * TOP_K
    experts = idx[:, 0:TOP_K]
    ranks = idx[:, TOP_K:2 * TOP_K]
    counts = counts_f[0, :N_EXPERTS].astype(jnp.int32)
    padded = (counts + MOE_BLK - 1) // MOE_BLK * MOE_BLK
    pad_end = jnp.cumsum(padded)
    pad_start = pad_end - padded
    expert_ids = jnp.arange(N_EXPERTS, dtype=jnp.int32)
    lookup = lambda hot, table: jnp.sum(jnp.where(hot, table, 0), axis=-1)
    dest = (lookup(experts[:, :, None] == expert_ids, pad_start) + ranks).reshape(n_asg).astype(jnp.int32)
    n_blocks = -(-(n_asg + N_EXPERTS * (MOE_BLK - 1)) // MOE_BLK)
    n_used = jnp.maximum(pad_end[-1] // MOE_BLK, 1).astype(jnp.int32)
    blk = jnp.minimum(jnp.arange(n_blocks, dtype=jnp.int32), n_used - 1)
    blk_row = blk * MOE_BLK
    blk_e = jnp.minimum(jnp.sum((pad_end[None, :] <= blk_row[:, None]).astype(jnp.int32), axis=-1), N_EXPERTS - 1)
    blk_hot = blk_e[:, None] == expert_ids
    blk_off = blk_row - lookup(blk_hot, pad_start)
    blk_first = (blk_off == 0).astype(jnp.int32)
    blk_valid = jnp.clip(lookup(blk_hot, counts) - blk_off, 0, MOE_BLK).astype(jnp.int32)
    used = counts > 0
    exp_list = jnp.argsort(jnp.logical_not(used), stable=True).astype(jnp.int32)
    blk_ord = lookup(blk_hot, jnp.cumsum(used.astype(jnp.int32)) - 1).astype(jnp.int32)
    n_exp = jnp.sum(used.astype(jnp.int32)).reshape(1)
    plan = (blk_e, blk_first, blk_valid, blk_ord, exp_list, n_used.reshape(1), n_exp)
    return dest, n_blocks * MOE_BLK, plan


def kernel(x_prompt, x_sample, cache_kv_w128, cache_kv_w512, cache_kv_w2048, state_conv, w_in, conv_w, w_conv_out, w_att_out, w_o, ln1_g, ln1_b, w_router_group, b_router_group, w_router_expert, b_router_expert, w_expert_gate, w_expert_up, w_expert_down, ln2_g, ln2_b):
    bsz, seq, _ = x_prompt.shape
    bd, t_new, _ = x_sample.shape
    m_p = bsz * seq
    m_s = bd * t_new
    m_all = m_p + m_s
    caches = (cache_kv_w128, cache_kv_w512, cache_kv_w2048)
    lyr = 0

    w_in16 = w_in[lyr].astype(BF16)
    wc16 = w_conv_out[lyr].astype(BF16)
    wa16 = w_att_out[lyr].astype(BF16)
    wo16 = w_o[lyr].astype(BF16)
    cw = conv_w[lyr]
    g1, b1 = ln1_g[lyr][None, :], ln1_b[lyr][None, :]
    g2, b2 = ln2_g[lyr][None, :], ln2_b[lyr][None, :]
    w_re = jnp.transpose(w_router_expert[lyr], (1, 0, 2)).reshape(D_MODEL, N_EXPERTS)
    w_r = jnp.pad(jnp.concatenate([w_router_group[lyr], w_re], axis=1),
                  ((0, 0), (0, LANES - N_EXPERT_GROUPS - N_EXPERTS)))
    b_r = jnp.pad(jnp.concatenate([b_router_group[lyr], b_router_expert[lyr].reshape(N_EXPERTS)]),
                  (0, LANES - N_EXPERT_GROUPS - N_EXPERTS))[None, :]

    xp = x_prompt.reshape(m_p, D_MODEL)
    xs = x_sample.reshape(m_s, D_MODEL)

    caches3 = [c[lyr].reshape(bd, -1, HEAD_DIM) for c in caches]
    z_p, qkv_p, shifted = _proj_in(xp, w_in16, min(PROJ_TM, m_p),
                                   [c.reshape(-1, HEAD_DIM) for c in caches3], t_new * KV_ROWS)
    shifted = [s.reshape(c.shape) for s, c in zip(shifted, caches3)]
    p_p, conv_p = _conv_prompt(z_p, cw, bsz, seq)
    o_p = _attn_prompt(qkv_p, bsz, seq)
    h_buf = _mix(p_p, o_p, z_p, xp, wc16, wa16, wo16, g1, b1, min(MIX_TM, m_p), m_all)

    z_s, qkv_s, _ = _proj_in(xs, w_in16, m_s)
    p_s, conv_s = _conv_sample(z_s, cw, state_conv[lyr], t_new)
    q3 = qkv_s.reshape(bd, t_new, QKV_COLS)
    kv5 = qkv_s.reshape(bd, t_new, 3, N_DIL, HEADS_PER_GROUP, HEAD_DIM)
    os_, ls_, bufs_s = [], [], []
    for g, (window, _) in enumerate(DILATED_GROUPS):
        new_kv = kv5[:, :, 1:3, g].reshape(bd, t_new * KV_ROWS, HEAD_DIM)
        buf, o_g, lse_g = _attn_sample(q3, caches3[g], new_kv, shifted[g], g)
        bufs_s.append(buf.reshape(1, bd, window, 2, HEADS_PER_GROUP, HEAD_DIM))
        os_.append(o_g.reshape(m_s, D_GRP))
        ls_.append(lse_g.reshape(m_s, D_GRP))
    o_s = _merge_sample(os_, ls_)
    h_buf = _mix(p_s, o_s, z_s, xs, wc16, wa16, wo16, g1, b1, m_s, m_all, h_all=h_buf, row_off=m_p)

    h_all, h_slabs = h_buf
    idx, gates, counts = _router(h_all, w_r, b_r)
    dest, n_rows, plan = _moe_plan(idx, counts)
    x_slabs = _dispatch(h_slabs, dest, n_rows)
    y_slabs = _experts(x_slabs, plan, w_expert_gate[lyr], w_expert_up[lyr], w_expert_down[lyr])
    y_p = _combine(y_slabs, dest, h_all, gates, g2, b2, 0, m_p)
    y_s = _combine(y_slabs, dest, h_all, gates, g2, b2, m_p, m_s)

    qkv_p3 = qkv_p.reshape(bsz, seq, QKV_COLS)
    bufs_p = []
    for g, (w, _) in enumerate(DILATED_GROUPS):
        keep = min(w, seq)
        bufs_p.append(_kv_pack(qkv_p3, g, keep).reshape(1, bsz, keep, 2, HEADS_PER_GROUP, HEAD_DIM))
    return (y_p.reshape(bsz, seq, D_MODEL), y_s.reshape(bd, t_new, D_MODEL),
            bufs_p[0], bufs_p[1], bufs_p[2], conv_p[None],
            bufs_s[0], bufs_s[1], bufs_s[2], conv_s[None])
```

```python
import functools

import jax
import jax.numpy as jnp
from jax import lax
from jax.experimental import pallas as pl
from jax.experimental.pallas import tpu as pltpu

F32 = jnp.float32
BF16 = jnp.bfloat16

D_MODEL = 2048
D_CONV = D_MODEL // 2
CONV_W = 3
DILATED_GROUPS = ((128, 1), (512, 4), (2048, 16))
N_DIL = len(DILATED_GROUPS)
HEADS_PER_GROUP = 4
HEAD_DIM = 128
SPAN = 128
D_GRP = HEADS_PER_GROUP * HEAD_DIM
D_ATT = N_DIL * D_GRP
N_EXPERT_GROUPS = 8
EXPERTS_PER_GROUP = 8
N_EXPERTS = N_EXPERT_GROUPS * EXPERTS_PER_GROUP
TOP_K = 2
D_EXPERT = D_MODEL // 4
DEPTH = 1
LN_EPS = 1e-5
ALPHA = (2 * DEPTH) ** 0.25
NEG_INF = -1e30
ATT_SCALE = HEAD_DIM ** -0.5

OFF_B = 0
OFF_C = D_CONV
OFF_H = 2 * D_CONV
OFF_Q = 3 * D_CONV
OFF_K = OFF_Q + D_ATT
OFF_V = OFF_K + D_ATT
OFF_GC = OFF_V + D_ATT
OFF_GA = OFF_GC + D_MODEL
N_IN = OFF_GA + D_MODEL

LANES = 128
SUBLANES = 8
VMEM_LIMIT = 56 * 1024 * 1024

PROJ_TN = 512
PROJ_TM = 1024
Z_COLS = 2 * D_MODEL + 3 * D_CONV
QKV_COLS = 3 * D_ATT
NZ_TILES = Z_COLS // PROJ_TN
NQ_TILES = QKV_COLS // PROJ_TN
CONV_TM = 512
MIX_TM = 512
ROUTER_TILE = 640
ROW_TILE = 128
MOE_BLK = 256
ATTN_UNROLL = 4
DISPATCH_TILE = 640
COMBINE_TILE = 256
KV_PACK_TM = 512
SHIFT_CHUNK_ROWS = 4128
SHIFT_AHEAD = 2
SHIFT_SLOTS = 2 * SHIFT_AHEAD
DMA_UNROLL = 8


def _proj_tile_order():
    t = lambda off, width: list(range(off // PROJ_TN, (off + width) // PROJ_TN))
    z_tiles = t(OFF_GC, D_MODEL) + t(OFF_GA, D_MODEL) + t(OFF_B, D_CONV) + t(OFF_C, D_CONV) + t(OFF_H, D_CONV)
    q_tiles = t(OFF_Q, D_ATT) + t(OFF_K, D_ATT) + t(OFF_V, D_ATT)
    return z_tiles + q_tiles


def _params(*sem):
    return pltpu.CompilerParams(dimension_semantics=sem, vmem_limit_bytes=VMEM_LIMIT)


def _shift_chunks(rows, drop, max_rows):
    keep = rows - drop
    best = SUBLANES
    for c in range(SUBLANES, max_rows + 1, SUBLANES):
        if keep % c == 0:
            best = c
    return best, keep // best


def _proj_kernel(shift_plan, drop, order_ref, x_ref, w_ref, *rest):
    del order_ref
    n_shift = len(shift_plan)
    srcs = rest[:n_shift]
    z_ref, qkv_ref = rest[n_shift:n_shift + 2]
    dsts = rest[n_shift + 2:2 * n_shift + 2]
    x16 = rest[2 * n_shift + 2]
    j = pl.program_id(1)

    if n_shift:
        stage, sem_in, sem_out = rest[2 * n_shift + 3:]
        t = pl.program_id(0) * pl.num_programs(1) + j
        n_chunks = shift_plan[-1][2]

        def for_chunk(q, action):
            for n, (chunk, lo, hi) in enumerate(shift_plan):
                @pl.when(jnp.logical_and(q >= lo, q < hi))
                def _(n=n, chunk=chunk, lo=lo):
                    action(n, chunk, pl.multiple_of((q - lo) * chunk, SUBLANES), q & (SHIFT_SLOTS - 1))

        def load(n, chunk, start, slot):
            return pltpu.make_async_copy(srcs[n].at[pl.ds(start + drop, chunk), :],
                                         stage.at[slot, pl.ds(0, chunk), :], sem_in.at[slot])

        def store(n, chunk, start, slot):
            return pltpu.make_async_copy(stage.at[slot, pl.ds(0, chunk), :],
                                         dsts[n].at[pl.ds(start, chunk), :], sem_out.at[slot])

        @pl.when(t == 0)
        def _():
            for q in range(SHIFT_AHEAD):
                for_chunk(t + q, lambda *a: load(*a).start())

        @pl.when(t < n_chunks)
        def _():
            for_chunk(t, lambda *a: load(*a).wait())
            for_chunk(t, lambda *a: store(*a).start())

        @pl.when(jnp.logical_and(t >= SHIFT_AHEAD, t < n_chunks + SHIFT_AHEAD))
        def _():
            for_chunk(t - SHIFT_AHEAD, lambda *a: store(*a).wait())

        @pl.when(t + SHIFT_AHEAD < n_chunks)
        def _():
            for_chunk(t + SHIFT_AHEAD, lambda *a: load(*a).start())

    @pl.when(j == 0)
    def _():
        x16[...] = x_ref[...].astype(BF16)

    @pl.when(j < NZ_TILES)
    def _():
        z_ref[...] = jnp.dot(x16[...], w_ref[...], preferred_element_type=F32).astype(BF16)

    @pl.when(j >= NZ_TILES)
    def _():
        qkv_ref[...] = jnp.dot(x16[...], w_ref[...], preferred_element_type=F32)


def _proj_in(x, w16, tm, shift_srcs=(), drop=0):
    m = x.shape[0]
    n_shift = len(shift_srcs)
    n_steps = (m // tm) * (NZ_TILES + NQ_TILES)
    order = jnp.asarray(_proj_tile_order(), jnp.int32)
    hbm = pl.BlockSpec(memory_space=pl.ANY)
    shift_plan, first = [], 0
    for s in shift_srcs:
        chunk, count = _shift_chunks(s.shape[0], drop, SHIFT_CHUNK_ROWS)
        shift_plan.append((chunk, first, first + count))
        first += count
    assert first + SHIFT_AHEAD <= n_steps, "one shift chunk per grid step, plus the steps that drain"
    shift_scratch = [pltpu.VMEM((SHIFT_SLOTS, SHIFT_CHUNK_ROWS, LANES), F32),
                     pltpu.SemaphoreType.DMA((SHIFT_SLOTS,)), pltpu.SemaphoreType.DMA((SHIFT_SLOTS,))] if n_shift else []
    grid_spec = pltpu.PrefetchScalarGridSpec(
        num_scalar_prefetch=1,
        grid=(m // tm, NZ_TILES + NQ_TILES),
        in_specs=[
            pl.BlockSpec((tm, D_MODEL), lambda i, j, o: (i, 0)),
            pl.BlockSpec((D_MODEL, PROJ_TN), lambda i, j, o: (0, o[j])),
        ] + [hbm] * n_shift,
        out_specs=[
            pl.BlockSpec((tm, PROJ_TN), lambda i, j, o: (i, jnp.minimum(j, NZ_TILES - 1))),
            pl.BlockSpec((tm, PROJ_TN), lambda i, j, o: (i, jnp.maximum(j - NZ_TILES, 0))),
        ] + [hbm] * n_shift,
        scratch_shapes=[pltpu.VMEM((tm, D_MODEL), BF16)] + shift_scratch,
    )
    outs = pl.pallas_call(
        functools.partial(_proj_kernel, tuple(shift_plan), drop),
        grid_spec=grid_spec,
        out_shape=[jax.ShapeDtypeStruct((m, Z_COLS), BF16), jax.ShapeDtypeStruct((m, QKV_COLS), F32)]
        + [jax.ShapeDtypeStruct(s.shape, s.dtype) for s in shift_srcs],
        compiler_params=_params("arbitrary", "arbitrary"),
        name="proj_in",
    )(order, x, w16, *shift_srcs)
    return outs[0], outs[1], outs[2:]


ZB_BLK = (2 * D_MODEL) // D_CONV


def _conv_prompt_kernel(zb_ref, zc_ref, zh_ref, cw_ref, p_ref, st_ref, uext):
    tm = zb_ref.shape[0]

    @pl.when(pl.program_id(1) == 0)
    def _():
        uext[0:SUBLANES, :] = jnp.zeros((SUBLANES, D_CONV), F32)

    u = zc_ref[...].astype(F32) * zh_ref[...].astype(F32)
    uext[SUBLANES:SUBLANES + tm, :] = u
    conv = (cw_ref[0:1, :] * uext[SUBLANES - 2:SUBLANES - 2 + tm, :]
            + cw_ref[1:2, :] * uext[SUBLANES - 1:SUBLANES - 1 + tm, :]
            + cw_ref[2:3, :] * u)
    p_ref[...] = (zb_ref[...].astype(F32) * conv).astype(BF16)
    last = uext[tm:tm + SUBLANES, :]
    uext[0:SUBLANES, :] = last
    st_ref[0] = last[SUBLANES - 2:SUBLANES, :]


def _conv_prompt(z, conv_w, bsz, seq):
    tm = min(CONV_TM, seq)
    nsb = seq // tm
    zspec = lambda blk: pl.BlockSpec((tm, D_CONV), lambda b, s: (b * nsb + s, blk))
    return pl.pallas_call(
        _conv_prompt_kernel,
        grid=(bsz, nsb),
        in_specs=[zspec(ZB_BLK), zspec(ZB_BLK + 1), zspec(ZB_BLK + 2),
                  pl.BlockSpec((CONV_W, D_CONV), lambda b, s: (0, 0))],
        out_specs=[pl.BlockSpec((tm, D_CONV), lambda b, s: (b * nsb + s, 0)),
                   pl.BlockSpec((1, CONV_W - 1, D_CONV), lambda b, s: (b, 0, 0))],
        out_shape=[jax.ShapeDtypeStruct((bsz * seq, D_CONV), BF16),
                   jax.ShapeDtypeStruct((bsz, CONV_W - 1, D_CONV), F32)],
        scratch_shapes=[pltpu.VMEM((tm + SUBLANES, D_CONV), F32)],
        compiler_params=_params("arbitrary", "arbitrary"),
        name="conv_prompt",
    )(z, z, z, conv_w)


def _conv_sample_kernel(t_new, zb_ref, zc_ref, zh_ref, cw_ref, s1_ref, s2_ref, p_ref, u_ref, uext):
    m = zb_ref.shape[0]
    u = zc_ref[...].astype(F32) * zh_ref[...].astype(F32)
    u_ref[...] = u
    uext[0:SUBLANES, :] = jnp.zeros((SUBLANES, D_CONV), F32)
    uext[SUBLANES:SUBLANES + m, :] = u
    t = lax.broadcasted_iota(jnp.int32, (m, D_CONV), 0) % t_new
    um1 = jnp.where(t < 1, s1_ref[...], uext[SUBLANES - 1:SUBLANES - 1 + m, :])
    um2 = jnp.where(t < 2, s2_ref[...], uext[SUBLANES - 2:SUBLANES - 2 + m, :])
    conv = cw_ref[0:1, :] * um2 + cw_ref[1:2, :] * um1 + cw_ref[2:3, :] * u
    p_ref[...] = (zb_ref[...].astype(F32) * conv).astype(BF16)


def _conv_sample(z, conv_w, state, t_new):
    m = z.shape[0]
    bd = m // t_new
    s1 = jnp.pad(state[:, 1:2], ((0, 0), (0, t_new - 1), (0, 0))).reshape(m, D_CONV)
    s2 = jnp.pad(state, ((0, 0), (0, t_new - 2), (0, 0))).reshape(m, D_CONV)
    zspec = lambda blk: pl.BlockSpec((m, D_CONV), lambda i: (0, blk))
    full = lambda r: pl.BlockSpec((r, D_CONV), lambda i: (0, 0))
    p, u = pl.pallas_call(
        functools.partial(_conv_sample_kernel, t_new),
        grid=(1,),
        in_specs=[zspec(ZB_BLK), zspec(ZB_BLK + 1), zspec(ZB_BLK + 2), full(CONV_W), full(m), full(m)],
        out_specs=[full(m), full(m)],
        out_shape=[jax.ShapeDtypeStruct((m, D_CONV), BF16), jax.ShapeDtypeStruct((m, D_CONV), F32)],
        scratch_shapes=[pltpu.VMEM((m + SUBLANES, D_CONV), F32)],
        compiler_params=_params("arbitrary"),
        name="conv_sample",
    )(z, z, z, conv_w, s1, s2)
    new_state = u.reshape(bd, t_new, D_CONV)[:, t_new - (CONV_W - 1):]
    return p, new_state


def _merge_groups(o_list, lse_list):
    top = functools.reduce(jnp.maximum, lse_list)
    es = [jnp.exp(l - top) for l in lse_list]
    num = functools.reduce(lambda a, b: a + b, [e * o for e, o in zip(es, o_list)])
    den = functools.reduce(lambda a, b: a + b, es)
    return num / den


def _nt_dot(a, b):
    return lax.dot_general(a, b, (((1,), (1,)), ((), ())), preferred_element_type=F32)


def _attn_prompt_group(q_ref, k_ref, v_ref, o_sc, lse_sc, g, dil, seq):
    shift = dil.bit_length() - 1
    qi = lax.broadcasted_iota(jnp.int32, (SPAN, 2 * SPAN), 0)
    kj = lax.broadcasted_iota(jnp.int32, (SPAN, 2 * SPAN), 1)
    band = jnp.logical_and(kj >= qi, kj <= qi + SPAN)
    is_prev = lax.broadcasted_iota(jnp.int32, (1, 2 * SPAN), 1) < SPAN
    ones = jnp.ones((2 * SPAN, HEAD_DIM), BF16)

    def rows(ref, start):
        if dil == 1:
            return ref[pl.ds(start, SPAN), :]
        return ref[pl.ds(start, SPAN, stride=dil), :]

    def body(trip, carry):
        blocks = range(ATTN_UNROLL)
        idx = [trip * ATTN_UNROLL + u for u in blocks]
        c = [i >> shift for i in idx]
        base = [c[u] * (SPAN * dil) + (idx[u] & (dil - 1)) for u in blocks]
        pbase = [jnp.maximum(c[u] - 1, 0) * (SPAN * dil) + (idx[u] & (dil - 1)) for u in blocks]
        q = [rows(q_ref, base[u]).astype(BF16) for u in blocks]
        k = [jnp.concatenate([rows(k_ref, pbase[u]), rows(k_ref, base[u])], axis=0).astype(BF16) for u in blocks]
        s = [_nt_dot(q[u], k[u]) for u in blocks]
        pen = [jnp.where(jnp.logical_and(is_prev, c[u] == 0), NEG_INF, 0.0) for u in blocks]
        s = [jnp.where(band, s[u] * ATT_SCALE + pen[u], NEG_INF) for u in blocks]
        m = [jnp.max(jnp.maximum(s[u][:, :SPAN], s[u][:, SPAN:]), axis=-1, keepdims=True) for u in blocks]
        p = [jnp.exp(s[u] - m[u]).astype(BF16) for u in blocks]
        v = [jnp.concatenate(
            [jnp.concatenate([rows(v_ref, pbase[u]), rows(v_ref, base[u])], axis=0).astype(BF16), ones], axis=1)
            for u in blocks]
        acc = [jnp.dot(p[u], v[u], preferred_element_type=F32) for u in blocks]
        for u in blocks:
            den = acc[u][:, HEAD_DIM:]
            o = acc[u][:, :HEAD_DIM] / den
            lse = m[u] + jnp.log(den)
            if dil == 1:
                o_sc[g, pl.ds(base[u], SPAN), :] = o
                lse_sc[g, pl.ds(base[u], SPAN), :] = lse
            else:
                o_sc[g, pl.ds(base[u], SPAN, stride=dil), :] = o
                lse_sc[g, pl.ds(base[u], SPAN, stride=dil), :] = lse
        return carry

    lax.fori_loop(0, seq // (SPAN * ATTN_UNROLL), body, 0)


def _attn_prompt_kernel(q_ref, k_ref, v_ref, o_ref, o_sc, lse_sc):
    seq = q_ref.shape[0]
    g_id = pl.program_id(2)
    for g, (_, dil) in enumerate(DILATED_GROUPS):
        @pl.when(g_id == g)
        def _(g=g, dil=dil):
            _attn_prompt_group(q_ref, k_ref, v_ref, o_sc, lse_sc, g, dil, seq)

    @pl.when(g_id == N_DIL - 1)
    def _():
        chunk = 256

        def body(i, carry):
            sl = pl.ds(pl.multiple_of(i * chunk, chunk), chunk)
            o = _merge_groups([o_sc[g, sl, :] for g in range(N_DIL)], [lse_sc[g, sl, :] for g in range(N_DIL)])
            o_ref[sl, :] = o.astype(BF16)
            return carry

        lax.fori_loop(0, seq // chunk, body, 0)


def _attn_prompt(qkv, bsz, seq):
    qkv3 = qkv.reshape(bsz, seq, QKV_COLS)
    per_sel = D_ATT // HEAD_DIM
    spec = lambda sel: pl.BlockSpec((None, seq, HEAD_DIM),
                                    lambda b, j, g: (b, 0, sel * per_sel + g * HEADS_PER_GROUP + j))
    out = pl.pallas_call(
        _attn_prompt_kernel,
        grid=(bsz, HEADS_PER_GROUP, N_DIL),
        in_specs=[spec(0), spec(1), spec(2)],
        out_specs=pl.BlockSpec((None, seq, HEAD_DIM), lambda b, j, g: (b, 0, j)),
        out_shape=jax.ShapeDtypeStruct((bsz, seq, D_GRP), BF16),
        scratch_shapes=[pltpu.VMEM((N_DIL, seq, HEAD_DIM), F32), pltpu.VMEM((N_DIL, seq, HEAD_DIM), F32)],
        compiler_params=_params("arbitrary", "arbitrary", "arbitrary"),
        name="attn_prompt",
    )(qkv3, qkv3, qkv3)
    return out.reshape(bsz * seq, D_GRP)


KV_ROWS = 2 * HEADS_PER_GROUP


def _kv_pack_kernel(k_ref, v_ref, out_ref):
    tm = k_ref.shape[0]
    for h in range(HEADS_PER_GROUP):
        cols = slice(h * HEAD_DIM, (h + 1) * HEAD_DIM)
        out_ref[pl.ds(h, tm, stride=KV_ROWS), :] = k_ref[:, cols]
        out_ref[pl.ds(HEADS_PER_GROUP + h, tm, stride=KV_ROWS), :] = v_ref[:, cols]


def _kv_pack(qkv3, g, keep):
    bsz, seq, _ = qkv3.shape
    tm = min(KV_PACK_TM, keep)
    assert keep % tm == 0 and (seq - keep) % tm == 0
    first = (seq - keep) // tm
    col = lambda sel: (sel * D_ATT + g * D_GRP) // D_GRP
    spec = lambda sel: pl.BlockSpec((None, tm, D_GRP), lambda b, s: (b, first + s, col(sel)))
    return pl.pallas_call(
        _kv_pack_kernel,
        grid=(bsz, keep // tm),
        in_specs=[spec(1), spec(2)],
        out_specs=pl.BlockSpec((None, tm * KV_ROWS, HEAD_DIM), lambda b, s: (b, s, 0)),
        out_shape=jax.ShapeDtypeStruct((bsz, keep * KV_ROWS, HEAD_DIM), F32),
        compiler_params=_params("arbitrary", "arbitrary"),
        name="kv_pack",
    )(qkv3, qkv3)


def _attn_sample_kernel(dil, q_ref, cache_ref, new_ref, shifted_hbm, buf_ref, o_ref, lse_ref):
    del shifted_hbm
    t_new = q_ref.shape[0]
    n_buf = cache_ref.shape[0] // KV_ROWS
    tq = lax.broadcasted_iota(jnp.int32, (t_new, n_buf), 0)
    pos = lax.broadcasted_iota(jnp.int32, (t_new, n_buf), 1)
    back = n_buf + tq - pos
    cache_ok = jnp.logical_and(back <= dil * SPAN, (back & (dil - 1)) == 0)
    tq2 = lax.broadcasted_iota(jnp.int32, (t_new, t_new), 0)
    tk2 = lax.broadcasted_iota(jnp.int32, (t_new, t_new), 1)
    new_ok = jnp.logical_and(tk2 <= tq2, ((tq2 - tk2) & (dil - 1)) == 0)
    for h in range(HEADS_PER_GROUP):
        cols = slice(h * HEAD_DIM, (h + 1) * HEAD_DIM)
        q = q_ref[:, cols].astype(BF16)
        k_c = cache_ref[pl.ds(h, n_buf, stride=KV_ROWS), :].astype(BF16)
        v_c = cache_ref[pl.ds(HEADS_PER_GROUP + h, n_buf, stride=KV_ROWS), :].astype(BF16)
        k_n = new_ref[pl.ds(h, t_new, stride=KV_ROWS), :].astype(BF16)
        v_n = new_ref[pl.ds(HEADS_PER_GROUP + h, t_new, stride=KV_ROWS), :].astype(BF16)
        s_c = jnp.where(cache_ok, _nt_dot(q, k_c) * ATT_SCALE, NEG_INF)
        s_n = jnp.where(new_ok, _nt_dot(q, k_n) * ATT_SCALE, NEG_INF)
        m = jnp.maximum(jnp.max(s_c, axis=-1, keepdims=True), jnp.max(s_n, axis=-1, keepdims=True))
        p_c = jnp.exp(s_c - m)
        p_n = jnp.exp(s_n - m)
        den = jnp.sum(p_c, axis=-1, keepdims=True) + jnp.sum(p_n, axis=-1, keepdims=True)
        acc = (jnp.dot(p_c.astype(BF16), v_c, preferred_element_type=F32)
               + jnp.dot(p_n.astype(BF16), v_n, preferred_element_type=F32))
        o_ref[:, cols] = acc / den
        lse_ref[:, cols] = jnp.broadcast_to(m + jnp.log(den), (t_new, HEAD_DIM))
    buf_ref[...] = new_ref[...]


def _attn_sample(q3, cache, new_kv, shifted, g):
    window, dil = DILATED_GROUPS[g]
    bd, t_new, _ = q3.shape
    rows = cache.shape[1]
    new_rows = t_new * KV_ROWS
    assert rows == window * KV_ROWS, "the cache must hold a full window"
    assert rows % new_rows == 0
    row_spec = lambda r: pl.BlockSpec((None, r, HEAD_DIM), lambda b: (b, 0, 0))
    tok_spec = pl.BlockSpec((None, t_new, D_GRP), lambda b: (b, 0, 0))
    return pl.pallas_call(
        functools.partial(_attn_sample_kernel, dil),
        grid=(bd,),
        in_specs=[pl.BlockSpec((None, t_new, D_GRP), lambda b: (b, 0, g)), row_spec(rows),
                  row_spec(new_rows), pl.BlockSpec(memory_space=pl.ANY)],
        out_specs=[pl.BlockSpec((None, new_rows, HEAD_DIM), lambda b: (b, rows // new_rows - 1, 0)),
                   tok_spec, tok_spec],
        out_shape=[jax.ShapeDtypeStruct((bd, rows, HEAD_DIM), F32),
                   jax.ShapeDtypeStruct((bd, t_new, D_GRP), F32),
                   jax.ShapeDtypeStruct((bd, t_new, D_GRP), F32)],
        input_output_aliases={3: 0},
        compiler_params=_params("arbitrary"),
        name=f"attn_sample_w{window}",
    )(q3, cache, new_kv, shifted)


def _merge_sample_kernel(o0, o1, o2, l0, l1, l2, out_ref):
    out_ref[...] = _merge_groups([o0[...], o1[...], o2[...]], [l0[...], l1[...], l2[...]]).astype(BF16)


def _merge_sample(os_, ls_):
    m = os_[0].shape[0]
    spec = pl.BlockSpec((m, D_GRP), lambda i: (0, 0))
    return pl.pallas_call(
        _merge_sample_kernel,
        grid=(1,),
        in_specs=[spec] * 6,
        out_specs=spec,
        out_shape=jax.ShapeDtypeStruct((m, D_GRP), BF16),
        compiler_params=_params("arbitrary"),
        name="merge_sample",
    )(*os_, *ls_)


def _layer_norm(v, g, b):
    mu = jnp.mean(v, axis=-1, keepdims=True)
    d = v - mu
    var = jnp.mean(d * d, axis=-1, keepdims=True)
    return d * lax.rsqrt(var + LN_EPS) * g + b


SLAB_ROWS = SUBLANES
HALF = D_MODEL // 2
HI_MASK = 0xFFFF0000


def _bf16_bits(v):
    return pltpu.bitcast(v.astype(BF16).astype(F32), jnp.uint32)


def _pack_slabs(src_ref, slab_ref, n_tok):
    def body(tg, carry):
        t0 = pl.multiple_of(tg * SUBLANES, SUBLANES)
        for s in range(SLAB_ROWS):
            lo = src_ref[pl.ds(t0, SUBLANES), s * LANES:(s + 1) * LANES]
            hi = src_ref[pl.ds(t0, SUBLANES), HALF + s * LANES:HALF + (s + 1) * LANES]
            word = (_bf16_bits(lo) >> 16) | _bf16_bits(hi)
            slab_ref[pl.ds(t0 * SLAB_ROWS + s, SUBLANES, stride=SLAB_ROWS), :] = word
        return carry
    lax.fori_loop(0, n_tok // SUBLANES, body, 0)


def _unpack_slabs(slab_ref, dst_ref, n_tok):
    def body(tg, carry):
        t0 = pl.multiple_of(tg * SUBLANES, SUBLANES)
        for s in range(SLAB_ROWS):
            word = slab_ref[pl.ds(t0 * SLAB_ROWS + s, SUBLANES, stride=SLAB_ROWS), :]
            dst_ref[pl.ds(t0, SUBLANES), s * LANES:(s + 1) * LANES] = pltpu.bitcast(word << 16, F32)
            dst_ref[pl.ds(t0, SUBLANES), HALF + s * LANES:HALF + (s + 1) * LANES] = pltpu.bitcast(
                word & jnp.uint32(HI_MASK), F32)
        return carry
    lax.fori_loop(0, n_tok // SUBLANES, body, 0)


def _unpack_slabs_bf16(slab_ref, dst_ref, n_tok, n_valid):
    pair = 2 * SUBLANES
    sub = lax.broadcasted_iota(jnp.int32, (SUBLANES, LANES), 0)

    def body(tg, carry):
        t0 = pl.multiple_of(tg * pair, pair)
        for s in range(SLAB_ROWS):
            words = []
            for half in range(2):
                t1 = t0 + half * SUBLANES
                w = slab_ref[pl.ds(t1 * SLAB_ROWS + s, SUBLANES, stride=SLAB_ROWS), :]
                words.append(jnp.where(t1 + sub < n_valid, w, jnp.uint32(0)))
            lo = jnp.concatenate([pltpu.bitcast(w << 16, F32) for w in words], axis=0)
            hi = jnp.concatenate([pltpu.bitcast(w & jnp.uint32(HI_MASK), F32) for w in words], axis=0)
            dst_ref[pl.ds(t0, pair), s * LANES:(s + 1) * LANES] = lo.astype(BF16)
            dst_ref[pl.ds(t0, pair), HALF + s * LANES:HALF + (s + 1) * LANES] = hi.astype(BF16)
        return carry
    lax.fori_loop(0, n_tok // pair, body, 0)


def _mix_kernel(p_ref, o_ref, gc_ref, ga_ref, x_ref, wc_ref, wa_ref, wo_ref, g_ref, b_ref, *rest):
    h_ref, hs_ref = rest[-2:]
    y_conv = jnp.dot(p_ref[...], wc_ref[...], preferred_element_type=F32)
    y_att = jnp.dot(o_ref[...], wa_ref[...], preferred_element_type=F32)
    merged = (jax.nn.sigmoid(gc_ref[...].astype(F32)) * y_conv
              + jax.nn.sigmoid(ga_ref[...].astype(F32)) * y_att)
    mix = jnp.dot(merged.astype(BF16), wo_ref[...], preferred_element_type=F32)
    h_ref[...] = _layer_norm(ALPHA * x_ref[...] + mix, g_ref[...], b_ref[...])
    _pack_slabs(h_ref, hs_ref, h_ref.shape[0])


def _mix(p, o, z, x, wc16, wa16, wo16, ln_g, ln_b, tm, total, h_all=None, row_off=0):
    m = p.shape[0]
    assert row_off % tm == 0
    blk_off = row_off // tm
    row = lambda w: pl.BlockSpec((tm, w), lambda i: (i, 0))
    const = lambda a: pl.BlockSpec(a.shape, lambda i: (0, 0), pipeline_mode=pl.Buffered(1))
    in_specs = [row(D_CONV), row(D_GRP),
                pl.BlockSpec((tm, D_MODEL), lambda i: (i, 0)), pl.BlockSpec((tm, D_MODEL), lambda i: (i, 1)),
                row(D_MODEL), const(wc16), const(wa16), const(wo16), const(ln_g), const(ln_b)]
    args = [p, o, z, z, x, wc16, wa16, wo16, ln_g, ln_b]
    aliases = {}
    if h_all is not None:
        in_specs += [pl.BlockSpec(memory_space=pl.ANY)] * 2
        args += list(h_all)
        aliases = {len(args) - 2: 0, len(args) - 1: 1}
    return pl.pallas_call(
        _mix_kernel,
        grid=(m // tm,),
        in_specs=in_specs,
        out_specs=[pl.BlockSpec((tm, D_MODEL), lambda i: (i + blk_off, 0)),
                   pl.BlockSpec((tm * SLAB_ROWS, LANES), lambda i: (i + blk_off, 0))],
        out_shape=[jax.ShapeDtypeStruct((total, D_MODEL), F32),
                   jax.ShapeDtypeStruct((total * SLAB_ROWS, LANES), jnp.uint32)],
        input_output_aliases=aliases,
        compiler_params=_params("arbitrary"),
        name="mix_ln1",
    )(*args)


def _route(h, w_ref, b_ref, carry):
    tile = h.shape[0]
    h_hi = h.astype(BF16)
    h_lo = (h - h_hi.astype(F32)).astype(BF16)
    w_both = w_ref[...]
    by_hi = jnp.dot(h_hi, w_both, preferred_element_type=F32)
    by_lo = jnp.dot(h_lo, w_both[:, :LANES], preferred_element_type=F32)
    logits = by_hi[:, :LANES] + (by_hi[:, LANES:] + by_lo) + b_ref[...]
    lane = lax.broadcasted_iota(jnp.int32, (tile, LANES), 1)
    lane_f = lane.astype(F32)
    first_max = lambda v, mx: jnp.min(jnp.where(v == mx, lane_f, float(LANES)), axis=-1, keepdims=True)

    is_grp = lane < N_EXPERT_GROUPS
    lg = jnp.where(is_grp, logits, -jnp.inf)
    mg = jnp.max(lg, axis=-1, keepdims=True)
    gsel = first_max(lg, mg).astype(jnp.int32)
    p_group = 1.0 / jnp.sum(jnp.where(is_grp, jnp.exp(logits - mg), 0.0), axis=-1, keepdims=True)

    e_lane = lane - N_EXPERT_GROUPS
    in_grp = jnp.logical_and(e_lane >= 0, (e_lane >> 3) == gsel)
    le = jnp.where(in_grp, logits, -jnp.inf)
    v1 = jnp.max(le, axis=-1, keepdims=True)
    i1 = first_max(le, v1).astype(jnp.int32)
    le2 = jnp.where(lane == i1, -jnp.inf, le)
    v2 = jnp.max(le2, axis=-1, keepdims=True)
    i2 = first_max(le2, v2).astype(jnp.int32)
    ex = jnp.exp(v2 - v1)
    g1 = p_group / (1.0 + ex)
    g2 = p_group * ex / (1.0 + ex)
    e1 = i1 - N_EXPERT_GROUPS
    e2 = i2 - N_EXPERT_GROUPS

    hot1 = lane == e1
    hot2 = lane == e2
    hot = jnp.where(jnp.logical_or(hot1, hot2), 1.0, 0.0)
    rr = lax.broadcasted_iota(jnp.int32, (tile, tile), 0)
    cc = lax.broadcasted_iota(jnp.int32, (tile, tile), 1)
    tri = jnp.where(cc < rr, 1.0, 0.0).astype(BF16)
    before = jnp.dot(tri, hot.astype(BF16), preferred_element_type=F32) + carry[0:1, :]
    r1 = jnp.sum(jnp.where(hot1, before, 0.0), axis=-1, keepdims=True).astype(jnp.int32)
    r2 = jnp.sum(jnp.where(hot2, before, 0.0), axis=-1, keepdims=True).astype(jnp.int32)
    carry[...] = carry[...] + jnp.sum(hot, axis=0, keepdims=True)

    idx = jnp.where(lane == 0, e1, jnp.where(lane == 1, e2, jnp.where(lane == 2, r1, r2)))
    return idx, jnp.where(lane == 0, g1, g2)


def _router_kernel(h_ref, w_ref, b_ref, idx_ref, gate_ref, cnt_ref, carry):
    @pl.when(pl.program_id(0) == 0)
    def _():
        carry[...] = jnp.zeros(carry.shape, F32)

    idx_ref[...], gate_ref[...] = _route(h_ref[...], w_ref, b_ref, carry)
    cnt_ref[...] = carry[...]


def _router(h_all, w_r, b_r):
    m = h_all.shape[0]
    tile = ROUTER_TILE if m % ROUTER_TILE == 0 else ROW_TILE
    row = pl.BlockSpec((tile, LANES), lambda i: (i, 0))
    return pl.pallas_call(
        _router_kernel,
        grid=(m // tile,),
        in_specs=[pl.BlockSpec((tile, D_MODEL), lambda i: (i, 0)),
                  pl.BlockSpec((D_MODEL, 2 * LANES), lambda i: (0, 0)),
                  pl.BlockSpec((1, LANES), lambda i: (0, 0))],
        out_specs=[row, row, pl.BlockSpec((SUBLANES, LANES), lambda i: (0, 0))],
        out_shape=[jax.ShapeDtypeStruct((m, LANES), jnp.int32), jax.ShapeDtypeStruct((m, LANES), F32),
                   jax.ShapeDtypeStruct((SUBLANES, LANES), F32)],
        scratch_shapes=[pltpu.VMEM((SUBLANES, LANES), F32)],
        compiler_params=_params("arbitrary"),
        name="router",
    )(h_all, w_r, b_r)


def _slab(ref, row):
    return ref.at[pl.ds(pl.multiple_of(row * SLAB_ROWS, SLAB_ROWS), SLAB_ROWS), :]


def _dispatch_kernel(tile, dest_ref, h_ref, x_hbm, sem):
    base = pl.program_id(0) * tile

    def copy(r, k):
        return pltpu.make_async_copy(_slab(h_ref, r), _slab(x_hbm, dest_ref[(base + r) * TOP_K + k]), sem)

    def start(r, c):
        for k in range(TOP_K):
            copy(r, k).start()
        return c

    def wait(r, c):
        for k in range(TOP_K):
            copy(r, k).wait()
        return c

    lax.fori_loop(0, tile, start, 0, unroll=DMA_UNROLL)
    lax.fori_loop(0, tile, wait, 0, unroll=DMA_UNROLL)


def _dispatch(h_slabs, dest_flat, n_rows):
    m = h_slabs.shape[0] // SLAB_ROWS
    tile = DISPATCH_TILE if m % DISPATCH_TILE == 0 else ROW_TILE
    grid_spec = pltpu.PrefetchScalarGridSpec(
        num_scalar_prefetch=1,
        grid=(m // tile,),
        in_specs=[pl.BlockSpec((tile * SLAB_ROWS, LANES), lambda i, d: (i, 0))],
        out_specs=pl.BlockSpec(memory_space=pl.ANY),
        scratch_shapes=[pltpu.SemaphoreType.DMA(())],
    )
    return pl.pallas_call(
        functools.partial(_dispatch_kernel, tile),
        grid_spec=grid_spec,
        out_shape=jax.ShapeDtypeStruct((n_rows * SLAB_ROWS, LANES), jnp.uint32),
        compiler_params=pltpu.CompilerParams(dimension_semantics=("arbitrary",), has_side_effects=True),
        name="moe_dispatch",
    )(dest_flat, h_slabs)


def _expert_kernel(blk_e, blk_first, blk_valid, blk_ord, exp_list, n_used, n_exp,
                   x_ref, wg_hbm, wu_hbm, wd_hbm, y_ref, wg32, wu32, wd32, wg16, wu16, wd16, sems, rows32, x16):
    del blk_e
    i = pl.program_id(0)
    weights = ((wg_hbm, wg32, wg16), (wu_hbm, wu32, wu16), (wd_hbm, wd32, wd16))

    def weight_copies(e, slot):
        return [pltpu.make_async_copy(hbm.at[e], w32.at[slot], sems.at[slot, n])
                for n, (hbm, w32, _) in enumerate(weights)]

    def start_weights(ordinal, slot):
        for c in weight_copies(exp_list[ordinal], slot):
            c.start()

    def wait_weights(slot):
        for c in weight_copies(0, slot):
            c.wait()

    def cast_weights(slot):
        for _, w32, w16 in weights:
            w16[slot] = w32[slot].astype(BF16)

    def ffn(slot, cast_next):
        _unpack_slabs_bf16(x_ref, x16, MOE_BLK, blk_valid[i])
        if cast_next:
            cast_weights(1 - slot)
        x = x16[...]
        gate = jnp.dot(x, wg16[slot], preferred_element_type=F32)
        up = jnp.dot(x, wu16[slot], preferred_element_type=F32)
        hid = (jax.nn.silu(gate) * up).astype(BF16)
        rows32[...] = jnp.dot(hid, wd16[slot], preferred_element_type=F32)
        _pack_slabs(rows32, y_ref, MOE_BLK)

    @pl.when(i == 0)
    def _():
        start_weights(0, 0)
        start_weights(1, 1)
        wait_weights(0)
        cast_weights(0)

        @pl.when(n_exp[0] > 2)
        def _():
            start_weights(2, 0)

    @pl.when(i < n_used[0])
    def _():
        ordinal = blk_ord[i]
        first = blk_first[i] == 1
        for slot in (0, 1):
            mine = (ordinal & 1) == slot

            @pl.when(jnp.logical_and(mine, first))
            def _(slot=slot):
                @pl.when(ordinal + 1 < n_exp[0])
                def _():
                    wait_weights(1 - slot)

                ffn(slot, cast_next=True)

                @pl.when(ordinal + 3 < n_exp[0])
                def _():
                    start_weights(ordinal + 3, 1 - slot)

            @pl.when(jnp.logical_and(mine, jnp.logical_not(first)))
            def _(slot=slot):
                ffn(slot, cast_next=False)


def _experts(x_slabs, plan, w_gate, w_up, w_down):
    n_rows = x_slabs.shape[0] // SLAB_ROWS
    n_blocks = n_rows // MOE_BLK
    row_map = lambda i, be, bf, bv, bo, el, nu, ne: (jnp.minimum(i, nu[0] - 1), 0)
    hbm = pl.BlockSpec(memory_space=pl.ANY)
    grid_spec = pltpu.PrefetchScalarGridSpec(
        num_scalar_prefetch=7,
        grid=(n_blocks,),
        in_specs=[pl.BlockSpec((MOE_BLK * SLAB_ROWS, LANES), row_map), hbm, hbm, hbm],
        out_specs=pl.BlockSpec((MOE_BLK * SLAB_ROWS, LANES), row_map),
        scratch_shapes=[pltpu.VMEM((2, D_MODEL, D_EXPERT), F32), pltpu.VMEM((2, D_MODEL, D_EXPERT), F32),
                        pltpu.VMEM((2, D_EXPERT, D_MODEL), F32),
                        pltpu.VMEM((2, D_MODEL, D_EXPERT), BF16), pltpu.VMEM((2, D_MODEL, D_EXPERT), BF16),
                        pltpu.VMEM((2, D_EXPERT, D_MODEL), BF16),
                        pltpu.SemaphoreType.DMA((2, 3)),
                        pltpu.VMEM((MOE_BLK, D_MODEL), F32), pltpu.VMEM((MOE_BLK, D_MODEL), BF16)],
    )
    return pl.pallas_call(
        _expert_kernel,
        grid_spec=grid_spec,
        out_shape=jax.ShapeDtypeStruct((n_rows * SLAB_ROWS, LANES), jnp.uint32),
        compiler_params=_params("arbitrary"),
        name="moe_experts",
    )(*plan, x_slabs, w_gate, w_up, w_down)


def _combine_kernel(tok_off, tile, dest_ref, y_hbm, h_ref, gate_ref, g_ref, b_ref, out_ref, ybuf, y32, sems):
    i = pl.program_id(0)
    n_steps = pl.num_programs(0)

    def copy(step, r, k):
        slot = step & 1
        row = dest_ref[(tok_off + step * tile + r) * TOP_K + k]
        return pltpu.make_async_copy(_slab(y_hbm, row), _slab(ybuf.at[slot, k], r), sems.at[slot])

    def gather(step):
        def body(r, c):
            for k in range(TOP_K):
                copy(step, r, k).start()
            return c
        lax.fori_loop(0, tile, body, 0, unroll=DMA_UNROLL)

    @pl.when(i == 0)
    def _():
        gather(i)

    @pl.when(i + 1 < n_steps)
    def _():
        gather(i + 1)

    def wait(r, c):
        for k in range(TOP_K):
            copy(i, r, k).wait()
        return c

    lax.fori_loop(0, tile, wait, 0, unroll=DMA_UNROLL)
    slot = i & 1
    for k in range(TOP_K):
        _unpack_slabs(ybuf.at[slot, k], y32.at[k], tile)
    f = gate_ref[:, 0:1] * y32[0] + gate_ref[:, 1:2] * y32[1]
    out_ref[...] = _layer_norm(ALPHA * h_ref[...] + f, g_ref[...], b_ref[...])


def _combine(y_slabs, dest_flat, h_all, gates, ln_g, ln_b, tok_off, n_tok):
    tile = COMBINE_TILE if (n_tok % COMBINE_TILE == 0 and tok_off % COMBINE_TILE == 0) else ROW_TILE
    blk_off = tok_off // tile
    grid_spec = pltpu.PrefetchScalarGridSpec(
        num_scalar_prefetch=1,
        grid=(n_tok // tile,),
        in_specs=[pl.BlockSpec(memory_space=pl.ANY),
                  pl.BlockSpec((tile, D_MODEL), lambda i, d: (i + blk_off, 0)),
                  pl.BlockSpec((tile, LANES), lambda i, d: (i + blk_off, 0)),
                  pl.BlockSpec((1, D_MODEL), lambda i, d: (0, 0)),
                  pl.BlockSpec((1, D_MODEL), lambda i, d: (0, 0))],
        out_specs=pl.BlockSpec((tile, D_MODEL), lambda i, d: (i, 0)),
        scratch_shapes=[pltpu.VMEM((2, TOP_K, tile * SLAB_ROWS, LANES), jnp.uint32),
                        pltpu.VMEM((TOP_K, tile, D_MODEL), F32), pltpu.SemaphoreType.DMA((2,))],
    )
    return pl.pallas_call(
        functools.partial(_combine_kernel, tok_off, tile),
        grid_spec=grid_spec,
        out_shape=jax.ShapeDtypeStruct((n_tok, D_MODEL), F32),
        compiler_params=_params("arbitrary"),
        name="combine_ln2",
    )(dest_flat, y_slabs, h_all, gates, ln_g, ln_b)


def _moe_plan(idx, counts_f):
    m = idx.shape[0]
    n_asg = m * TOP_K
    experts = idx[:, 0:TOP_K]
    ranks = idx[:, TOP_K:2 * TOP_K]
    counts = counts_f[0, :N_EXPERTS].astype(jnp.int32)
    padded = (counts + MOE_BLK - 1) // MOE_BLK * MOE_BLK
    pad_end = jnp.cumsum(padded)
    pad_start = pad_end - padded
    expert_ids = jnp.arange(N_EXPERTS, dtype=jnp.int32)
    lookup = lambda hot, table: jnp.sum(jnp.where(hot, table, 0), axis=-1)
    dest = (lookup(experts[:, :, None] == expert_ids, pad_start) + ranks).reshape(n_asg).astype(jnp.int32)
    n_blocks = -(-(n_asg + N_EXPERTS * (MOE_BLK - 1)) // MOE_BLK)
    n_used = jnp.maximum(pad_end[-1] // MOE_BLK, 1).astype(jnp.int32)
    blk = jnp.minimum(jnp.arange(n_blocks, dtype=jnp.int32), n_used - 1)
    blk_row = blk * MOE_BLK
    blk_e = jnp.minimum(jnp.sum((pad_end[None, :] <= blk_row[:, None]).astype(jnp.int32), axis=-1), N_EXPERTS - 1)
    blk_hot = blk_e[:, None] == expert_ids
    blk_off = blk_row - lookup(blk_hot, pad_start)
    blk_first = (blk_off == 0).astype(jnp.int32)
    blk_valid = jnp.clip(lookup(blk_hot, counts) - blk_off, 0, MOE_BLK).astype(jnp.int32)
    used = counts > 0
    exp_list = jnp.argsort(jnp.logical_not(used), stable=True).astype(jnp.int32)
    blk_ord = lookup(blk_hot, jnp.cumsum(used.astype(jnp.int32)) - 1).astype(jnp.int32)
    n_exp = jnp.sum(used.astype(jnp.int32)).reshape(1)
    plan = (blk_e, blk_first, blk_valid, blk_ord, exp_list, n_used.reshape(1), n_exp)
    return dest, n_blocks * MOE_BLK, plan


def kernel(x_prompt, x_sample, cache_kv_w128, cache_kv_w512, cache_kv_w2048, state_conv, w_in, conv_w, w_conv_out, w_att_out, w_o, ln1_g, ln1_b, w_router_group, b_router_group, w_router_expert, b_router_expert, w_expert_gate, w_expert_up, w_expert_down, ln2_g, ln2_b):
    bsz, seq, _ = x_prompt.shape
    bd, t_new, _ = x_sample.shape
    m_p = bsz * seq
    m_s = bd * t_new
    m_all = m_p + m_s
    caches = (cache_kv_w128, cache_kv_w512, cache_kv_w2048)
    lyr = 0

    w_in16 = w_in[lyr].astype(BF16)
    wc16 = w_conv_out[lyr].astype(BF16)
    wa16 = w_att_out[lyr].astype(BF16)
    wo16 = w_o[lyr].astype(BF16)
    cw = conv_w[lyr]
    g1, b1 = ln1_g[lyr][None, :], ln1_b[lyr][None, :]
    g2, b2 = ln2_g[lyr][None, :], ln2_b[lyr][None, :]
    w_re = jnp.transpose(w_router_expert[lyr], (1, 0, 2)).reshape(D_MODEL, N_EXPERTS)
    w_r = jnp.pad(jnp.concatenate([w_router_group[lyr], w_re], axis=1),
                  ((0, 0), (0, LANES - N_EXPERT_GROUPS - N_EXPERTS)))
    w_r_hi = w_r.astype(BF16)
    w_r = jnp.concatenate([w_r_hi, (w_r - w_r_hi.astype(F32)).astype(BF16)], axis=1)
    b_r = jnp.pad(jnp.concatenate([b_router_group[lyr], b_router_expert[lyr].reshape(N_EXPERTS)]),
                  (0, LANES - N_EXPERT_GROUPS - N_EXPERTS))[None, :]

    xp = x_prompt.reshape(m_p, D_MODEL)
    xs = x_sample.reshape(m_s, D_MODEL)

    caches3 = [c[lyr].reshape(bd, -1, HEAD_DIM) for c in caches]
    z_p, qkv_p, shifted = _proj_in(xp, w_in16, min(PROJ_TM, m_p),
                                   [c.reshape(-1, HEAD_DIM) for c in caches3], t_new * KV_ROWS)
    shifted = [s.reshape(c.shape) for s, c in zip(shifted, caches3)]
    p_p, conv_p = _conv_prompt(z_p, cw, bsz, seq)
    o_p = _attn_prompt(qkv_p, bsz, seq)
    h_buf = _mix(p_p, o_p, z_p, xp, wc16, wa16, wo16, g1, b1, min(MIX_TM, m_p), m_all)

    z_s, qkv_s, _ = _proj_in(xs, w_in16, m_s)
    p_s, conv_s = _conv_sample(z_s, cw, state_conv[lyr], t_new)
    q3 = qkv_s.reshape(bd, t_new, QKV_COLS)
    kv5 = qkv_s.reshape(bd, t_new, 3, N_DIL, HEADS_PER_GROUP, HEAD_DIM)
    os_, ls_, bufs_s = [], [], []
    for g, (window, _) in enumerate(DILATED_GROUPS):
        new_kv = kv5[:, :, 1:3, g].reshape(bd, t_new * KV_ROWS, HEAD_DIM)
        buf, o_g, lse_g = _attn_sample(q3, caches3[g], new_kv, shifted[g], g)
        bufs_s.append(buf.reshape(1, bd, window, 2, HEADS_PER_GROUP, HEAD_DIM))
        os_.append(o_g.reshape(m_s, D_GRP))
        ls_.append(lse_g.reshape(m_s, D_GRP))
    o_s = _merge_sample(os_, ls_)
    h_buf = _mix(p_s, o_s, z_s, xs, wc16, wa16, wo16, g1, b1, m_s, m_all, h_all=h_buf, row_off=m_p)

    h_all, h_slabs = h_buf
    idx, gates, counts = _router(h_all, w_r, b_r)
    dest, n_rows, plan = _moe_plan(idx, counts)
    x_slabs = _dispatch(h_slabs, dest, n_rows)
    y_slabs = _experts(x_slabs, plan, w_expert_gate[lyr], w_expert_up[lyr], w_expert_down[lyr])
    y_p = _combine(y_slabs, dest, h_all, gates, g2, b2, 0, m_p)
    y_s = _combine(y_slabs, dest, h_all, gates, g2, b2, m_p, m_s)

    qkv_p3 = qkv_p.reshape(bsz, seq, QKV_COLS)
    bufs_p = []
    for g, (w, _) in enumerate(DILATED_GROUPS):
        keep = min(w, seq)
        bufs_p.append(_kv_pack(qkv_p3, g, keep).reshape(1, bsz, keep, 2, HEADS_PER_GROUP, HEAD_DIM))
    return (y_p.reshape(bsz, seq, D_MODEL), y_s.reshape(bd, t_new, D_MODEL),
            bufs_p[0], bufs_p[1], bufs_p[2], conv_p[None],
            bufs_s[0], bufs_s[1], bufs_s[2], conv_s[None])
```

```python
import functools

import jax
import jax.numpy as jnp
from jax import lax
from jax.experimental import pallas as pl
from jax.experimental.pallas import tpu as pltpu

F32 = jnp.float32
BF16 = jnp.bfloat16

D_MODEL = 2048
D_CONV = D_MODEL // 2
CONV_W = 3
DILATED_GROUPS = ((128, 1), (512, 4), (2048, 16))
N_DIL = len(DILATED_GROUPS)
HEADS_PER_GROUP = 4
HEAD_DIM = 128
SPAN = 128
D_GRP = HEADS_PER_GROUP * HEAD_DIM
D_ATT = N_DIL * D_GRP
N_EXPERT_GROUPS = 8
EXPERTS_PER_GROUP = 8
N_EXPERTS = N_EXPERT_GROUPS * EXPERTS_PER_GROUP
TOP_K = 2
D_EXPERT = D_MODEL // 4
DEPTH = 1
LN_EPS = 1e-5
ALPHA = (2 * DEPTH) ** 0.25
NEG_INF = -1e30
ATT_SCALE = HEAD_DIM ** -0.5

OFF_B = 0
OFF_C = D_CONV
OFF_H = 2 * D_CONV
OFF_Q = 3 * D_CONV
OFF_K = OFF_Q + D_ATT
OFF_V = OFF_K + D_ATT
OFF_GC = OFF_V + D_ATT
OFF_GA = OFF_GC + D_MODEL
N_IN = OFF_GA + D_MODEL

LANES = 128
SUBLANES = 8
VMEM_LIMIT = 56 * 1024 * 1024

PROJ_TN = 512
PROJ_TM = 1024
Z_COLS = 2 * D_MODEL + 3 * D_CONV
QKV_COLS = 3 * D_ATT
NZ_TILES = Z_COLS // PROJ_TN
NQ_TILES = QKV_COLS // PROJ_TN
CONV_TM = 512
MIX_TM = 512
ROUTER_TILE = 640
ROW_TILE = 128
MOE_BLK = 256
W_SLOTS = 3
ATTN_UNROLL = 4
DISPATCH_TILE = 640
DISPATCH_SLOTS = 3
COMBINE_TILE = 256
KV_PACK_TM = 512
SHIFT_CHUNK_ROWS = 4128
SHIFT_AHEAD = 2
SHIFT_SLOTS = 2 * SHIFT_AHEAD
DMA_UNROLL = 8


def _proj_tile_order():
    t = lambda off, width: list(range(off // PROJ_TN, (off + width) // PROJ_TN))
    z_tiles = t(OFF_GC, D_MODEL) + t(OFF_GA, D_MODEL) + t(OFF_B, D_CONV) + t(OFF_C, D_CONV) + t(OFF_H, D_CONV)
    q_tiles = t(OFF_Q, D_ATT) + t(OFF_K, D_ATT) + t(OFF_V, D_ATT)
    return z_tiles + q_tiles


def _params(*sem):
    return pltpu.CompilerParams(dimension_semantics=sem, vmem_limit_bytes=VMEM_LIMIT)


def _shift_chunks(rows, drop, max_rows):
    keep = rows - drop
    best = SUBLANES
    for c in range(SUBLANES, max_rows + 1, SUBLANES):
        if keep % c == 0:
            best = c
    return best, keep // best


def _proj_kernel(shift_plan, drop, order_ref, x_ref, w_ref, *rest):
    del order_ref
    n_shift = len(shift_plan)
    srcs = rest[:n_shift]
    z_ref, qkv_ref = rest[n_shift:n_shift + 2]
    dsts = rest[n_shift + 2:2 * n_shift + 2]
    x16 = rest[2 * n_shift + 2]
    j = pl.program_id(1)

    if n_shift:
        stage, sem_in, sem_out = rest[2 * n_shift + 3:]
        t = pl.program_id(0) * pl.num_programs(1) + j
        n_chunks = shift_plan[-1][2]

        def for_chunk(q, action):
            for n, (chunk, lo, hi) in enumerate(shift_plan):
                @pl.when(jnp.logical_and(q >= lo, q < hi))
                def _(n=n, chunk=chunk, lo=lo):
                    action(n, chunk, pl.multiple_of((q - lo) * chunk, SUBLANES), q & (SHIFT_SLOTS - 1))

        def load(n, chunk, start, slot):
            return pltpu.make_async_copy(srcs[n].at[pl.ds(start + drop, chunk), :],
                                         stage.at[slot, pl.ds(0, chunk), :], sem_in.at[slot])

        def store(n, chunk, start, slot):
            return pltpu.make_async_copy(stage.at[slot, pl.ds(0, chunk), :],
                                         dsts[n].at[pl.ds(start, chunk), :], sem_out.at[slot])

        @pl.when(t == 0)
        def _():
            for q in range(SHIFT_AHEAD):
                for_chunk(t + q, lambda *a: load(*a).start())

        @pl.when(t < n_chunks)
        def _():
            for_chunk(t, lambda *a: load(*a).wait())
            for_chunk(t, lambda *a: store(*a).start())

        @pl.when(jnp.logical_and(t >= SHIFT_AHEAD, t < n_chunks + SHIFT_AHEAD))
        def _():
            for_chunk(t - SHIFT_AHEAD, lambda *a: store(*a).wait())

        @pl.when(t + SHIFT_AHEAD < n_chunks)
        def _():
            for_chunk(t + SHIFT_AHEAD, lambda *a: load(*a).start())

    @pl.when(j == 0)
    def _():
        x16[...] = x_ref[...].astype(BF16)

    @pl.when(j < NZ_TILES)
    def _():
        z_ref[...] = jnp.dot(x16[...], w_ref[...], preferred_element_type=F32).astype(BF16)

    @pl.when(j >= NZ_TILES)
    def _():
        qkv_ref[...] = jnp.dot(x16[...], w_ref[...], preferred_element_type=F32)


def _proj_in(x, w16, tm, shift_srcs=(), drop=0):
    m = x.shape[0]
    n_shift = len(shift_srcs)
    n_steps = (m // tm) * (NZ_TILES + NQ_TILES)
    order = jnp.asarray(_proj_tile_order(), jnp.int32)
    hbm = pl.BlockSpec(memory_space=pl.ANY)
    shift_plan, first = [], 0
    for s in shift_srcs:
        chunk, count = _shift_chunks(s.shape[0], drop, SHIFT_CHUNK_ROWS)
        shift_plan.append((chunk, first, first + count))
        first += count
    assert first + SHIFT_AHEAD <= n_steps, "one shift chunk per grid step, plus the steps that drain"
    shift_scratch = [pltpu.VMEM((SHIFT_SLOTS, SHIFT_CHUNK_ROWS, LANES), F32),
                     pltpu.SemaphoreType.DMA((SHIFT_SLOTS,)), pltpu.SemaphoreType.DMA((SHIFT_SLOTS,))] if n_shift else []
    grid_spec = pltpu.PrefetchScalarGridSpec(
        num_scalar_prefetch=1,
        grid=(m // tm, NZ_TILES + NQ_TILES),
        in_specs=[
            pl.BlockSpec((tm, D_MODEL), lambda i, j, o: (i, 0)),
            pl.BlockSpec((D_MODEL, PROJ_TN), lambda i, j, o: (0, o[j])),
        ] + [hbm] * n_shift,
        out_specs=[
            pl.BlockSpec((tm, PROJ_TN), lambda i, j, o: (i, jnp.minimum(j, NZ_TILES - 1))),
            pl.BlockSpec((tm, PROJ_TN), lambda i, j, o: (i, jnp.maximum(j - NZ_TILES, 0))),
        ] + [hbm] * n_shift,
        scratch_shapes=[pltpu.VMEM((tm, D_MODEL), BF16)] + shift_scratch,
    )
    outs = pl.pallas_call(
        functools.partial(_proj_kernel, tuple(shift_plan), drop),
        grid_spec=grid_spec,
        out_shape=[jax.ShapeDtypeStruct((m, Z_COLS), BF16), jax.ShapeDtypeStruct((m, QKV_COLS), F32)]
        + [jax.ShapeDtypeStruct(s.shape, s.dtype) for s in shift_srcs],
        compiler_params=_params("arbitrary", "arbitrary"),
        name="proj_in",
    )(order, x, w16, *shift_srcs)
    return outs[0], outs[1], outs[2:]


ZB_BLK = (2 * D_MODEL) // D_CONV


def _conv_prompt_kernel(zb_ref, zc_ref, zh_ref, cw_ref, p_ref, st_ref, uext):
    tm = zb_ref.shape[0]

    @pl.when(pl.program_id(1) == 0)
    def _():
        uext[0:SUBLANES, :] = jnp.zeros((SUBLANES, D_CONV), F32)

    u = zc_ref[...].astype(F32) * zh_ref[...].astype(F32)
    uext[SUBLANES:SUBLANES + tm, :] = u
    conv = (cw_ref[0:1, :] * uext[SUBLANES - 2:SUBLANES - 2 + tm, :]
            + cw_ref[1:2, :] * uext[SUBLANES - 1:SUBLANES - 1 + tm, :]
            + cw_ref[2:3, :] * u)
    p_ref[...] = (zb_ref[...].astype(F32) * conv).astype(BF16)
    last = uext[tm:tm + SUBLANES, :]
    uext[0:SUBLANES, :] = last
    st_ref[0] = last[SUBLANES - 2:SUBLANES, :]


def _conv_prompt(z, conv_w, bsz, seq):
    tm = min(CONV_TM, seq)
    nsb = seq // tm
    zspec = lambda blk: pl.BlockSpec((tm, D_CONV), lambda b, s: (b * nsb + s, blk))
    return pl.pallas_call(
        _conv_prompt_kernel,
        grid=(bsz, nsb),
        in_specs=[zspec(ZB_BLK), zspec(ZB_BLK + 1), zspec(ZB_BLK + 2),
                  pl.BlockSpec((CONV_W, D_CONV), lambda b, s: (0, 0))],
        out_specs=[pl.BlockSpec((tm, D_CONV), lambda b, s: (b * nsb + s, 0)),
                   pl.BlockSpec((1, CONV_W - 1, D_CONV), lambda b, s: (b, 0, 0))],
        out_shape=[jax.ShapeDtypeStruct((bsz * seq, D_CONV), BF16),
                   jax.ShapeDtypeStruct((bsz, CONV_W - 1, D_CONV), F32)],
        scratch_shapes=[pltpu.VMEM((tm + SUBLANES, D_CONV), F32)],
        compiler_params=_params("arbitrary", "arbitrary"),
        name="conv_prompt",
    )(z, z, z, conv_w)


def _conv_sample_kernel(t_new, zb_ref, zc_ref, zh_ref, cw_ref, s1_ref, s2_ref, p_ref, u_ref, uext):
    m = zb_ref.shape[0]
    u = zc_ref[...].astype(F32) * zh_ref[...].astype(F32)
    u_ref[...] = u
    uext[0:SUBLANES, :] = jnp.zeros((SUBLANES, D_CONV), F32)
    uext[SUBLANES:SUBLANES + m, :] = u
    t = lax.broadcasted_iota(jnp.int32, (m, D_CONV), 0) % t_new
    um1 = jnp.where(t < 1, s1_ref[...], uext[SUBLANES - 1:SUBLANES - 1 + m, :])
    um2 = jnp.where(t < 2, s2_ref[...], uext[SUBLANES - 2:SUBLANES - 2 + m, :])
    conv = cw_ref[0:1, :] * um2 + cw_ref[1:2, :] * um1 + cw_ref[2:3, :] * u
    p_ref[...] = (zb_ref[...].astype(F32) * conv).astype(BF16)


def _conv_sample(z, conv_w, state, t_new):
    m = z.shape[0]
    bd = m // t_new
    s1 = jnp.pad(state[:, 1:2], ((0, 0), (0, t_new - 1), (0, 0))).reshape(m, D_CONV)
    s2 = jnp.pad(state, ((0, 0), (0, t_new - 2), (0, 0))).reshape(m, D_CONV)
    zspec = lambda blk: pl.BlockSpec((m, D_CONV), lambda i: (0, blk))
    full = lambda r: pl.BlockSpec((r, D_CONV), lambda i: (0, 0))
    p, u = pl.pallas_call(
        functools.partial(_conv_sample_kernel, t_new),
        grid=(1,),
        in_specs=[zspec(ZB_BLK), zspec(ZB_BLK + 1), zspec(ZB_BLK + 2), full(CONV_W), full(m), full(m)],
        out_specs=[full(m), full(m)],
        out_shape=[jax.ShapeDtypeStruct((m, D_CONV), BF16), jax.ShapeDtypeStruct((m, D_CONV), F32)],
        scratch_shapes=[pltpu.VMEM((m + SUBLANES, D_CONV), F32)],
        compiler_params=_params("arbitrary"),
        name="conv_sample",
    )(z, z, z, conv_w, s1, s2)
    new_state = u.reshape(bd, t_new, D_CONV)[:, t_new - (CONV_W - 1):]
    return p, new_state


def _merge_groups(o_list, lse_list):
    top = functools.reduce(jnp.maximum, lse_list)
    es = [jnp.exp(l - top) for l in lse_list]
    num = functools.reduce(lambda a, b: a + b, [e * o for e, o in zip(es, o_list)])
    den = functools.reduce(lambda a, b: a + b, es)
    return num / den


def _nt_dot(a, b):
    return lax.dot_general(a, b, (((1,), (1,)), ((), ())), preferred_element_type=F32)


def _attn_prompt_group(q_ref, k_ref, v_ref, o_sc, lse_sc, g, dil, seq):
    shift = dil.bit_length() - 1
    qi = lax.broadcasted_iota(jnp.int32, (SPAN, 2 * SPAN), 0)
    kj = lax.broadcasted_iota(jnp.int32, (SPAN, 2 * SPAN), 1)
    band = jnp.logical_and(kj >= qi, kj <= qi + SPAN)
    is_prev = lax.broadcasted_iota(jnp.int32, (1, 2 * SPAN), 1) < SPAN
    ones = jnp.ones((2 * SPAN, HEAD_DIM), BF16)

    def rows(ref, start):
        if dil == 1:
            return ref[pl.ds(start, SPAN), :]
        return ref[pl.ds(start, SPAN, stride=dil), :]

    def body(trip, carry):
        blocks = range(ATTN_UNROLL)
        idx = [trip * ATTN_UNROLL + u for u in blocks]
        c = [i >> shift for i in idx]
        base = [c[u] * (SPAN * dil) + (idx[u] & (dil - 1)) for u in blocks]
        pbase = [jnp.maximum(c[u] - 1, 0) * (SPAN * dil) + (idx[u] & (dil - 1)) for u in blocks]
        q = [rows(q_ref, base[u]).astype(BF16) for u in blocks]
        k = [jnp.concatenate([rows(k_ref, pbase[u]), rows(k_ref, base[u])], axis=0).astype(BF16) for u in blocks]
        s = [_nt_dot(q[u], k[u]) for u in blocks]
        pen = [jnp.where(jnp.logical_and(is_prev, c[u] == 0), NEG_INF, 0.0) for u in blocks]
        s = [jnp.where(band, s[u] * ATT_SCALE + pen[u], NEG_INF) for u in blocks]
        m = [jnp.max(jnp.maximum(s[u][:, :SPAN], s[u][:, SPAN:]), axis=-1, keepdims=True) for u in blocks]
        p = [jnp.exp(s[u] - m[u]).astype(BF16) for u in blocks]
        v = [jnp.concatenate(
            [jnp.concatenate([rows(v_ref, pbase[u]), rows(v_ref, base[u])], axis=0).astype(BF16), ones], axis=1)
            for u in blocks]
        acc = [jnp.dot(p[u], v[u], preferred_element_type=F32) for u in blocks]
        for u in blocks:
            den = acc[u][:, HEAD_DIM:]
            o = acc[u][:, :HEAD_DIM] / den
            lse = m[u] + jnp.log(den)
            if dil == 1:
                o_sc[g, pl.ds(base[u], SPAN), :] = o
                lse_sc[g, pl.ds(base[u], SPAN), :] = lse
            else:
                o_sc[g, pl.ds(base[u], SPAN, stride=dil), :] = o
                lse_sc[g, pl.ds(base[u], SPAN, stride=dil), :] = lse
        return carry

    lax.fori_loop(0, seq // (SPAN * ATTN_UNROLL), body, 0)


def _attn_prompt_kernel(q_ref, k_ref, v_ref, o_ref, o_sc, lse_sc):
    seq = q_ref.shape[0]
    g_id = pl.program_id(2)
    for g, (_, dil) in enumerate(DILATED_GROUPS):
        @pl.when(g_id == g)
        def _(g=g, dil=dil):
            _attn_prompt_group(q_ref, k_ref, v_ref, o_sc, lse_sc, g, dil, seq)

    @pl.when(g_id == N_DIL - 1)
    def _():
        chunk = 256

        def body(i, carry):
            sl = pl.ds(pl.multiple_of(i * chunk, chunk), chunk)
            o = _merge_groups([o_sc[g, sl, :] for g in range(N_DIL)], [lse_sc[g, sl, :] for g in range(N_DIL)])
            o_ref[sl, :] = o.astype(BF16)
            return carry

        lax.fori_loop(0, seq // chunk, body, 0)


def _attn_prompt(qkv, bsz, seq):
    qkv3 = qkv.reshape(bsz, seq, QKV_COLS)
    per_sel = D_ATT // HEAD_DIM
    spec = lambda sel: pl.BlockSpec((None, seq, HEAD_DIM),
                                    lambda b, j, g: (b, 0, sel * per_sel + g * HEADS_PER_GROUP + j))
    out = pl.pallas_call(
        _attn_prompt_kernel,
        grid=(bsz, HEADS_PER_GROUP, N_DIL),
        in_specs=[spec(0), spec(1), spec(2)],
        out_specs=pl.BlockSpec((None, seq, HEAD_DIM), lambda b, j, g: (b, 0, j)),
        out_shape=jax.ShapeDtypeStruct((bsz, seq, D_GRP), BF16),
        scratch_shapes=[pltpu.VMEM((N_DIL, seq, HEAD_DIM), F32), pltpu.VMEM((N_DIL, seq, HEAD_DIM), F32)],
        compiler_params=_params("arbitrary", "arbitrary", "arbitrary"),
        name="attn_prompt",
    )(qkv3, qkv3, qkv3)
    return out.reshape(bsz * seq, D_GRP)


KV_ROWS = 2 * HEADS_PER_GROUP


def _kv_pack_kernel(k_ref, v_ref, out_ref):
    tm = k_ref.shape[0]
    for h in range(HEADS_PER_GROUP):
        cols = slice(h * HEAD_DIM, (h + 1) * HEAD_DIM)
        out_ref[pl.ds(h, tm, stride=KV_ROWS), :] = k_ref[:, cols]
        out_ref[pl.ds(HEADS_PER_GROUP + h, tm, stride=KV_ROWS), :] = v_ref[:, cols]


def _kv_pack(qkv3, g, keep):
    bsz, seq, _ = qkv3.shape
    tm = min(KV_PACK_TM, keep)
    assert keep % tm == 0 and (seq - keep) % tm == 0
    first = (seq - keep) // tm
    col = lambda sel: (sel * D_ATT + g * D_GRP) // D_GRP
    spec = lambda sel: pl.BlockSpec((None, tm, D_GRP), lambda b, s: (b, first + s, col(sel)))
    return pl.pallas_call(
        _kv_pack_kernel,
        grid=(bsz, keep // tm),
        in_specs=[spec(1), spec(2)],
        out_specs=pl.BlockSpec((None, tm * KV_ROWS, HEAD_DIM), lambda b, s: (b, s, 0)),
        out_shape=jax.ShapeDtypeStruct((bsz, keep * KV_ROWS, HEAD_DIM), F32),
        compiler_params=_params("arbitrary", "arbitrary"),
        name="kv_pack",
    )(qkv3, qkv3)


def _attn_sample_kernel(dil, q_ref, cache_ref, new_ref, shifted_hbm, buf_ref, o_ref, lse_ref):
    del shifted_hbm
    t_new = q_ref.shape[0]
    n_buf = cache_ref.shape[0] // KV_ROWS
    tq = lax.broadcasted_iota(jnp.int32, (t_new, n_buf), 0)
    pos = lax.broadcasted_iota(jnp.int32, (t_new, n_buf), 1)
    back = n_buf + tq - pos
    cache_ok = jnp.logical_and(back <= dil * SPAN, (back & (dil - 1)) == 0)
    tq2 = lax.broadcasted_iota(jnp.int32, (t_new, t_new), 0)
    tk2 = lax.broadcasted_iota(jnp.int32, (t_new, t_new), 1)
    new_ok = jnp.logical_and(tk2 <= tq2, ((tq2 - tk2) & (dil - 1)) == 0)
    for h in range(HEADS_PER_GROUP):
        cols = slice(h * HEAD_DIM, (h + 1) * HEAD_DIM)
        q = q_ref[:, cols].astype(BF16)
        k_c = cache_ref[pl.ds(h, n_buf, stride=KV_ROWS), :].astype(BF16)
        v_c = cache_ref[pl.ds(HEADS_PER_GROUP + h, n_buf, stride=KV_ROWS), :].astype(BF16)
        k_n = new_ref[pl.ds(h, t_new, stride=KV_ROWS), :].astype(BF16)
        v_n = new_ref[pl.ds(HEADS_PER_GROUP + h, t_new, stride=KV_ROWS), :].astype(BF16)
        s_c = jnp.where(cache_ok, _nt_dot(q, k_c) * ATT_SCALE, NEG_INF)
        s_n = jnp.where(new_ok, _nt_dot(q, k_n) * ATT_SCALE, NEG_INF)
        m = jnp.maximum(jnp.max(s_c, axis=-1, keepdims=True), jnp.max(s_n, axis=-1, keepdims=True))
        p_c = jnp.exp(s_c - m)
        p_n = jnp.exp(s_n - m)
        den = jnp.sum(p_c, axis=-1, keepdims=True) + jnp.sum(p_n, axis=-1, keepdims=True)
        acc = (jnp.dot(p_c.astype(BF16), v_c, preferred_element_type=F32)
               + jnp.dot(p_n.astype(BF16), v_n, preferred_element_type=F32))
        o_ref[:, cols] = acc / den
        lse_ref[:, cols] = jnp.broadcast_to(m + jnp.log(den), (t_new, HEAD_DIM))
    buf_ref[...] = new_ref[...]


def _attn_sample(q3, cache, new_kv, shifted, g):
    window, dil = DILATED_GROUPS[g]
    bd, t_new, _ = q3.shape
    rows = cache.shape[1]
    new_rows = t_new * KV_ROWS
    assert rows == window * KV_ROWS, "the cache must hold a full window"
    assert rows % new_rows == 0
    row_spec = lambda r: pl.BlockSpec((None, r, HEAD_DIM), lambda b: (b, 0, 0))
    tok_spec = pl.BlockSpec((None, t_new, D_GRP), lambda b: (b, 0, 0))
    return pl.pallas_call(
        functools.partial(_attn_sample_kernel, dil),
        grid=(bd,),
        in_specs=[pl.BlockSpec((None, t_new, D_GRP), lambda b: (b, 0, g)), row_spec(rows),
                  row_spec(new_rows), pl.BlockSpec(memory_space=pl.ANY)],
        out_specs=[pl.BlockSpec((None, new_rows, HEAD_DIM), lambda b: (b, rows // new_rows - 1, 0)),
                   tok_spec, tok_spec],
        out_shape=[jax.ShapeDtypeStruct((bd, rows, HEAD_DIM), F32),
                   jax.ShapeDtypeStruct((bd, t_new, D_GRP), F32),
                   jax.ShapeDtypeStruct((bd, t_new, D_GRP), F32)],
        input_output_aliases={3: 0},
        compiler_params=_params("arbitrary"),
        name=f"attn_sample_w{window}",
    )(q3, cache, new_kv, shifted)


def _merge_sample_kernel(o0, o1, o2, l0, l1, l2, out_ref):
    out_ref[...] = _merge_groups([o0[...], o1[...], o2[...]], [l0[...], l1[...], l2[...]]).astype(BF16)


def _merge_sample(os_, ls_):
    m = os_[0].shape[0]
    spec = pl.BlockSpec((m, D_GRP), lambda i: (0, 0))
    return pl.pallas_call(
        _merge_sample_kernel,
        grid=(1,),
        in_specs=[spec] * 6,
        out_specs=spec,
        out_shape=jax.ShapeDtypeStruct((m, D_GRP), BF16),
        compiler_params=_params("arbitrary"),
        name="merge_sample",
    )(*os_, *ls_)


def _layer_norm(v, g, b):
    mu = jnp.mean(v, axis=-1, keepdims=True)
    d = v - mu
    var = jnp.mean(d * d, axis=-1, keepdims=True)
    return d * lax.rsqrt(var + LN_EPS) * g + b


SLAB_ROWS = SUBLANES
HALF = D_MODEL // 2
HI_MASK = 0xFFFF0000


def _bf16_bits(v):
    return pltpu.bitcast(v.astype(BF16).astype(F32), jnp.uint32)


def _pack_slabs(src_ref, slab_ref, n_tok):
    def body(tg, carry):
        t0 = pl.multiple_of(tg * SUBLANES, SUBLANES)
        for s in range(SLAB_ROWS):
            lo = src_ref[pl.ds(t0, SUBLANES), s * LANES:(s + 1) * LANES]
            hi = src_ref[pl.ds(t0, SUBLANES), HALF + s * LANES:HALF + (s + 1) * LANES]
            word = (_bf16_bits(lo) >> 16) | _bf16_bits(hi)
            slab_ref[pl.ds(t0 * SLAB_ROWS + s, SUBLANES, stride=SLAB_ROWS), :] = word
        return carry
    lax.fori_loop(0, n_tok // SUBLANES, body, 0)


def _unpack_slabs(slab_ref, dst_ref, n_tok):
    def body(tg, carry):
        t0 = pl.multiple_of(tg * SUBLANES, SUBLANES)
        for s in range(SLAB_ROWS):
            word = slab_ref[pl.ds(t0 * SLAB_ROWS + s, SUBLANES, stride=SLAB_ROWS), :]
            dst_ref[pl.ds(t0, SUBLANES), s * LANES:(s + 1) * LANES] = pltpu.bitcast(word << 16, F32)
            dst_ref[pl.ds(t0, SUBLANES), HALF + s * LANES:HALF + (s + 1) * LANES] = pltpu.bitcast(
                word & jnp.uint32(HI_MASK), F32)
        return carry
    lax.fori_loop(0, n_tok // SUBLANES, body, 0)


def _unpack_slabs_bf16(slab_ref, dst_ref, n_tok, n_valid):
    pair = 2 * SUBLANES
    sub = lax.broadcasted_iota(jnp.int32, (SUBLANES, LANES), 0)

    def body(tg, carry):
        t0 = pl.multiple_of(tg * pair, pair)
        for s in range(SLAB_ROWS):
            words = []
            for half in range(2):
                t1 = t0 + half * SUBLANES
                w = slab_ref[pl.ds(t1 * SLAB_ROWS + s, SUBLANES, stride=SLAB_ROWS), :]
                words.append(jnp.where(t1 + sub < n_valid, w, jnp.uint32(0)))
            lo = jnp.concatenate([pltpu.bitcast(w << 16, F32) for w in words], axis=0)
            hi = jnp.concatenate([pltpu.bitcast(w & jnp.uint32(HI_MASK), F32) for w in words], axis=0)
            dst_ref[pl.ds(t0, pair), s * LANES:(s + 1) * LANES] = lo.astype(BF16)
            dst_ref[pl.ds(t0, pair), HALF + s * LANES:HALF + (s + 1) * LANES] = hi.astype(BF16)
        return carry
    lax.fori_loop(0, n_tok // pair, body, 0)


def _mix_kernel(p_ref, o_ref, gc_ref, ga_ref, x_ref, wc_ref, wa_ref, wo_ref, g_ref, b_ref, *rest):
    h_ref, hs_ref = rest[-2:]
    y_conv = jnp.dot(p_ref[...], wc_ref[...], preferred_element_type=F32)
    y_att = jnp.dot(o_ref[...], wa_ref[...], preferred_element_type=F32)
    merged = (jax.nn.sigmoid(gc_ref[...].astype(F32)) * y_conv
              + jax.nn.sigmoid(ga_ref[...].astype(F32)) * y_att)
    mix = jnp.dot(merged.astype(BF16), wo_ref[...], preferred_element_type=F32)
    h_ref[...] = _layer_norm(ALPHA * x_ref[...] + mix, g_ref[...], b_ref[...])
    _pack_slabs(h_ref, hs_ref, h_ref.shape[0])


def _mix(p, o, z, x, wc16, wa16, wo16, ln_g, ln_b, tm, total, h_all=None, row_off=0):
    m = p.shape[0]
    assert row_off % tm == 0
    blk_off = row_off // tm
    row = lambda w: pl.BlockSpec((tm, w), lambda i: (i, 0))
    const = lambda a: pl.BlockSpec(a.shape, lambda i: (0, 0), pipeline_mode=pl.Buffered(1))
    in_specs = [row(D_CONV), row(D_GRP),
                pl.BlockSpec((tm, D_MODEL), lambda i: (i, 0)), pl.BlockSpec((tm, D_MODEL), lambda i: (i, 1)),
                row(D_MODEL), const(wc16), const(wa16), const(wo16), const(ln_g), const(ln_b)]
    args = [p, o, z, z, x, wc16, wa16, wo16, ln_g, ln_b]
    aliases = {}
    if h_all is not None:
        in_specs += [pl.BlockSpec(memory_space=pl.ANY)] * 2
        args += list(h_all)
        aliases = {len(args) - 2: 0, len(args) - 1: 1}
    return pl.pallas_call(
        _mix_kernel,
        grid=(m // tm,),
        in_specs=in_specs,
        out_specs=[pl.BlockSpec((tm, D_MODEL), lambda i: (i + blk_off, 0)),
                   pl.BlockSpec((tm * SLAB_ROWS, LANES), lambda i: (i + blk_off, 0))],
        out_shape=[jax.ShapeDtypeStruct((total, D_MODEL), F32),
                   jax.ShapeDtypeStruct((total * SLAB_ROWS, LANES), jnp.uint32)],
        input_output_aliases=aliases,
        compiler_params=_params("arbitrary"),
        name="mix_ln1",
    )(*args)


def _route(h, w_ref, b_ref, carry):
    tile = h.shape[0]
    h_hi = h.astype(BF16)
    h_lo = (h - h_hi.astype(F32)).astype(BF16)
    w_both = w_ref[...]
    by_hi = jnp.dot(h_hi, w_both, preferred_element_type=F32)
    by_lo = jnp.dot(h_lo, w_both[:, :LANES], preferred_element_type=F32)
    logits = by_hi[:, :LANES] + (by_hi[:, LANES:] + by_lo) + b_ref[...]
    lane = lax.broadcasted_iota(jnp.int32, (tile, LANES), 1)
    lane_f = lane.astype(F32)
    first_max = lambda v, mx: jnp.min(jnp.where(v == mx, lane_f, float(LANES)), axis=-1, keepdims=True)

    is_grp = lane < N_EXPERT_GROUPS
    lg = jnp.where(is_grp, logits, -jnp.inf)
    mg = jnp.max(lg, axis=-1, keepdims=True)
    gsel = first_max(lg, mg).astype(jnp.int32)
    p_group = 1.0 / jnp.sum(jnp.where(is_grp, jnp.exp(logits - mg), 0.0), axis=-1, keepdims=True)

    e_lane = lane - N_EXPERT_GROUPS
    in_grp = jnp.logical_and(e_lane >= 0, (e_lane >> 3) == gsel)
    le = jnp.where(in_grp, logits, -jnp.inf)
    v1 = jnp.max(le, axis=-1, keepdims=True)
    i1 = first_max(le, v1).astype(jnp.int32)
    le2 = jnp.where(lane == i1, -jnp.inf, le)
    v2 = jnp.max(le2, axis=-1, keepdims=True)
    i2 = first_max(le2, v2).astype(jnp.int32)
    ex = jnp.exp(v2 - v1)
    g1 = p_group / (1.0 + ex)
    g2 = p_group * ex / (1.0 + ex)
    e1 = i1 - N_EXPERT_GROUPS
    e2 = i2 - N_EXPERT_GROUPS

    hot1 = lane == e1
    hot2 = lane == e2
    hot = jnp.where(jnp.logical_or(hot1, hot2), 1.0, 0.0)
    rr = lax.broadcasted_iota(jnp.int32, (tile, tile), 0)
    cc = lax.broadcasted_iota(jnp.int32, (tile, tile), 1)
    tri = jnp.where(cc < rr, 1.0, 0.0).astype(BF16)
    before = jnp.dot(tri, hot.astype(BF16), preferred_element_type=F32) + carry[0:1, :]
    r1 = jnp.sum(jnp.where(hot1, before, 0.0), axis=-1, keepdims=True).astype(jnp.int32)
    r2 = jnp.sum(jnp.where(hot2, before, 0.0), axis=-1, keepdims=True).astype(jnp.int32)
    carry[...] = carry[...] + jnp.sum(hot, axis=0, keepdims=True)

    idx = jnp.where(lane == 0, e1, jnp.where(lane == 1, e2, jnp.where(lane == 2, r1, r2)))
    return idx, jnp.where(lane == 0, g1, g2)


def _router_kernel(h_ref, w_ref, b_ref, idx_ref, gate_ref, cnt_ref, carry):
    @pl.when(pl.program_id(0) == 0)
    def _():
        carry[...] = jnp.zeros(carry.shape, F32)

    idx_ref[...], gate_ref[...] = _route(h_ref[...], w_ref, b_ref, carry)
    cnt_ref[...] = carry[...]


def _router(h_all, w_r, b_r):
    m = h_all.shape[0]
    tile = ROUTER_TILE if m % ROUTER_TILE == 0 else ROW_TILE
    row = pl.BlockSpec((tile, LANES), lambda i: (i, 0))
    return pl.pallas_call(
        _router_kernel,
        grid=(m // tile,),
        in_specs=[pl.BlockSpec((tile, D_MODEL), lambda i: (i, 0)),
                  pl.BlockSpec((D_MODEL, 2 * LANES), lambda i: (0, 0)),
                  pl.BlockSpec((1, LANES), lambda i: (0, 0))],
        out_specs=[row, row, pl.BlockSpec((SUBLANES, LANES), lambda i: (0, 0))],
        out_shape=[jax.ShapeDtypeStruct((m, LANES), jnp.int32), jax.ShapeDtypeStruct((m, LANES), F32),
                   jax.ShapeDtypeStruct((SUBLANES, LANES), F32)],
        scratch_shapes=[pltpu.VMEM((SUBLANES, LANES), F32)],
        compiler_params=_params("arbitrary"),
        name="router",
    )(h_all, w_r, b_r)


def _slab(ref, row):
    return ref.at[pl.ds(pl.multiple_of(row * SLAB_ROWS, SLAB_ROWS), SLAB_ROWS), :]


def _dispatch_kernel(tile, dest_ref, h_hbm, x_hbm, stage, sem_in, sem_rows):
    i = pl.program_id(0)
    n_steps = pl.num_programs(0)
    slot_of = lambda step: lax.rem(step, DISPATCH_SLOTS)

    def load(step):
        rows = pl.ds(pl.multiple_of(step * (tile * SLAB_ROWS), SLAB_ROWS), tile * SLAB_ROWS)
        return pltpu.make_async_copy(h_hbm.at[rows, :], stage.at[slot_of(step)], sem_in.at[slot_of(step)])

    def each_row(step, action):
        slot = slot_of(step)

        def body(r, c):
            for k in range(TOP_K):
                action(pltpu.make_async_copy(_slab(stage.at[slot], r),
                                             _slab(x_hbm, dest_ref[(step * tile + r) * TOP_K + k]),
                                             sem_rows.at[slot]))
            return c
        lax.fori_loop(0, tile, body, 0, unroll=DMA_UNROLL)

    @pl.when(i == 0)
    def _():
        load(i).start()

    @pl.when(i + 1 < n_steps)
    def _():
        load(i + 1).start()

    load(i).wait()
    each_row(i, lambda c: c.start())

    @pl.when(i > 0)
    def _():
        each_row(i - 1, lambda c: c.wait())

    @pl.when(i == n_steps - 1)
    def _():
        each_row(i, lambda c: c.wait())


def _dispatch(h_slabs, dest_flat, n_rows):
    m = h_slabs.shape[0] // SLAB_ROWS
    tile = DISPATCH_TILE if m % DISPATCH_TILE == 0 else ROW_TILE
    grid_spec = pltpu.PrefetchScalarGridSpec(
        num_scalar_prefetch=1,
        grid=(m // tile,),
        in_specs=[pl.BlockSpec(memory_space=pl.ANY)],
        out_specs=pl.BlockSpec(memory_space=pl.ANY),
        scratch_shapes=[pltpu.VMEM((DISPATCH_SLOTS, tile * SLAB_ROWS, LANES), jnp.uint32),
                        pltpu.SemaphoreType.DMA((DISPATCH_SLOTS,)), pltpu.SemaphoreType.DMA((DISPATCH_SLOTS,))],
    )
    return pl.pallas_call(
        functools.partial(_dispatch_kernel, tile),
        grid_spec=grid_spec,
        out_shape=jax.ShapeDtypeStruct((n_rows * SLAB_ROWS, LANES), jnp.uint32),
        compiler_params=pltpu.CompilerParams(dimension_semantics=("arbitrary",), has_side_effects=True),
        name="moe_dispatch",
    )(dest_flat, h_slabs)


def _expert_kernel(blk_e, blk_first, blk_valid, blk_ord, exp_list, n_used, n_exp,
                   x_ref, wg_hbm, wu_hbm, wd_hbm, y_ref, wg32, wu32, wd32, wg16, wu16, wd16, sems, rows32, x16):
    i = pl.program_id(0)

    def weight_copies(e, slot):
        return [pltpu.make_async_copy(hbm.at[e], buf.at[slot], sems.at[slot, n])
                for n, (hbm, buf) in enumerate(((wg_hbm, wg32), (wu_hbm, wu32), (wd_hbm, wd32)))]

    def start_weights(ordinal, slot):
        for c in weight_copies(exp_list[ordinal], slot):
            c.start()

    @pl.when(i == 0)
    def _():
        for ahead in range(W_SLOTS):
            @pl.when(ahead < n_exp[0])
            def _(ahead=ahead):
                start_weights(ahead, ahead)

    @pl.when(i < n_used[0])
    def _():
        @pl.when(blk_first[i] == 1)
        def _():
            ordinal = blk_ord[i]
            slot = lax.rem(ordinal, W_SLOTS)
            for c in weight_copies(blk_e[i], slot):
                c.wait()
            wg16[...] = wg32[slot].astype(BF16)
            wu16[...] = wu32[slot].astype(BF16)
            wd16[...] = wd32[slot].astype(BF16)

            @pl.when(ordinal + W_SLOTS < n_exp[0])
            def _():
                start_weights(ordinal + W_SLOTS, slot)

        _unpack_slabs_bf16(x_ref, x16, MOE_BLK, blk_valid[i])
        x = x16[...]
        gate = jnp.dot(x, wg16[...], preferred_element_type=F32)
        up = jnp.dot(x, wu16[...], preferred_element_type=F32)
        hid = (jax.nn.silu(gate) * up).astype(BF16)
        rows32[...] = jnp.dot(hid, wd16[...], preferred_element_type=F32)
        _pack_slabs(rows32, y_ref, MOE_BLK)


def _experts(x_slabs, plan, w_gate, w_up, w_down):
    n_rows = x_slabs.shape[0] // SLAB_ROWS
    n_blocks = n_rows // MOE_BLK
    row_map = lambda i, be, bf, bv, bo, el, nu, ne: (jnp.minimum(i, nu[0] - 1), 0)
    hbm = pl.BlockSpec(memory_space=pl.ANY)
    grid_spec = pltpu.PrefetchScalarGridSpec(
        num_scalar_prefetch=7,
        grid=(n_blocks,),
        in_specs=[pl.BlockSpec((MOE_BLK * SLAB_ROWS, LANES), row_map), hbm, hbm, hbm],
        out_specs=pl.BlockSpec((MOE_BLK * SLAB_ROWS, LANES), row_map),
        scratch_shapes=[pltpu.VMEM((W_SLOTS, D_MODEL, D_EXPERT), F32), pltpu.VMEM((W_SLOTS, D_MODEL, D_EXPERT), F32),
                        pltpu.VMEM((W_SLOTS, D_EXPERT, D_MODEL), F32),
                        pltpu.VMEM((D_MODEL, D_EXPERT), BF16), pltpu.VMEM((D_MODEL, D_EXPERT), BF16),
                        pltpu.VMEM((D_EXPERT, D_MODEL), BF16),
                        pltpu.SemaphoreType.DMA((W_SLOTS, 3)),
                        pltpu.VMEM((MOE_BLK, D_MODEL), F32), pltpu.VMEM((MOE_BLK, D_MODEL), BF16)],
    )
    return pl.pallas_call(
        _expert_kernel,
        grid_spec=grid_spec,
        out_shape=jax.ShapeDtypeStruct((n_rows * SLAB_ROWS, LANES), jnp.uint32),
        compiler_params=_params("arbitrary"),
        name="moe_experts",
    )(*plan, x_slabs, w_gate, w_up, w_down)


def _combine_kernel(tok_off, tile, dest_ref, y_hbm, h_ref, gate_ref, g_ref, b_ref, out_ref, ybuf, y32, sems):
    i = pl.program_id(0)
    n_steps = pl.num_programs(0)

    def copy(step, r, k):
        slot = step & 1
        row = dest_ref[(tok_off + step * tile + r) * TOP_K + k]
        return pltpu.make_async_copy(_slab(y_hbm, row), _slab(ybuf.at[slot, k], r), sems.at[slot])

    def gather(step):
        def body(r, c):
            for k in range(TOP_K):
                copy(step, r, k).start()
            return c
        lax.fori_loop(0, tile, body, 0, unroll=DMA_UNROLL)

    @pl.when(i == 0)
    def _():
        gather(i)

    @pl.when(i + 1 < n_steps)
    def _():
        gather(i + 1)

    def wait(r, c):
        for k in range(TOP_K):
            copy(i, r, k).wait()
        return c

    lax.fori_loop(0, tile, wait, 0, unroll=DMA_UNROLL)
    slot = i & 1
    for k in range(TOP_K):
        _unpack_slabs(ybuf.at[slot, k], y32.at[k], tile)
    f = gate_ref[:, 0:1] * y32[0] + gate_ref[:, 1:2] * y32[1]
    out_ref[...] = _layer_norm(ALPHA * h_ref[...] + f, g_ref[...], b_ref[...])


def _combine(y_slabs, dest_flat, h_all, gates, ln_g, ln_b, tok_off, n_tok):
    tile = COMBINE_TILE if (n_tok % COMBINE_TILE == 0 and tok_off % COMBINE_TILE == 0) else ROW_TILE
    blk_off = tok_off // tile
    grid_spec = pltpu.PrefetchScalarGridSpec(
        num_scalar_prefetch=1,
        grid=(n_tok // tile,),
        in_specs=[pl.BlockSpec(memory_space=pl.ANY),
                  pl.BlockSpec((tile, D_MODEL), lambda i, d: (i + blk_off, 0)),
                  pl.BlockSpec((tile, LANES), lambda i, d: (i + blk_off, 0)),
                  pl.BlockSpec((1, D_MODEL), lambda i, d: (0, 0)),
                  pl.BlockSpec((1, D_MODEL), lambda i, d: (0, 0))],
        out_specs=pl.BlockSpec((tile, D_MODEL), lambda i, d: (i, 0)),
        scratch_shapes=[pltpu.VMEM((2, TOP_K, tile * SLAB_ROWS, LANES), jnp.uint32),
                        pltpu.VMEM((TOP_K, tile, D_MODEL), F32), pltpu.SemaphoreType.DMA((2,))],
    )
    return pl.pallas_call(
        functools.partial(_combine_kernel, tok_off, tile),
        grid_spec=grid_spec,
        out_shape=jax.ShapeDtypeStruct((n_tok, D_MODEL), F32),
        compiler_params=_params("arbitrary"),
        name="combine_ln2",
    )(dest_flat, y_slabs, h_all, gates, ln_g, ln_b)


def _moe_plan(idx, counts_f):
    m = idx.shape[0]
    n_asg = m * TOP_K
    experts = idx[:, 0:TOP_K]
    ranks = idx[:, TOP_K:2 * TOP_K]
    counts = counts_f[0, :N_EXPERTS].astype(jnp.int32)
    padded = (counts + MOE_BLK - 1) // MOE_BLK * MOE_BLK
    pad_end = jnp.cumsum(padded)
    pad_start = pad_end - padded
    expert_ids = jnp.arange(N_EXPERTS, dtype=jnp.int32)
    lookup = lambda hot, table: jnp.sum(jnp.where(hot, table, 0), axis=-1)
    dest = (lookup(experts[:, :, None] == expert_ids, pad_start) + ranks).reshape(n_asg).astype(jnp.int32)
    n_blocks = -(-(n_asg + N_EXPERTS * (MOE_BLK - 1)) // MOE_BLK)
    n_used = jnp.maximum(pad_end[-1] // MOE_BLK, 1).astype(jnp.int32)
    blk = jnp.minimum(jnp.arange(n_blocks, dtype=jnp.int32), n_used - 1)
    blk_row = blk * MOE_BLK
    blk_e = jnp.minimum(jnp.sum((pad_end[None, :] <= blk_row[:, None]).astype(jnp.int32), axis=-1), N_EXPERTS - 1)
    blk_hot = blk_e[:, None] == expert_ids
    blk_off = blk_row - lookup(blk_hot, pad_start)
    blk_first = (blk_off == 0).astype(jnp.int32)
    blk_valid = jnp.clip(lookup(blk_hot, counts) - blk_off, 0, MOE_BLK).astype(jnp.int32)
    used = counts > 0
    exp_list = jnp.argsort(jnp.logical_not(used), stable=True).astype(jnp.int32)
    blk_ord = lookup(blk_hot, jnp.cumsum(used.astype(jnp.int32)) - 1).astype(jnp.int32)
    n_exp = jnp.sum(used.astype(jnp.int32)).reshape(1)
    plan = (blk_e, blk_first, blk_valid, blk_ord, exp_list, n_used.reshape(1), n_exp)
    return dest, n_blocks * MOE_BLK, plan


def kernel(x_prompt, x_sample, cache_kv_w128, cache_kv_w512, cache_kv_w2048, state_conv, w_in, conv_w, w_conv_out, w_att_out, w_o, ln1_g, ln1_b, w_router_group, b_router_group, w_router_expert, b_router_expert, w_expert_gate, w_expert_up, w_expert_down, ln2_g, ln2_b):
    bsz, seq, _ = x_prompt.shape
    bd, t_new, _ = x_sample.shape
    m_p = bsz * seq
    m_s = bd * t_new
    m_all = m_p + m_s
    caches = (cache_kv_w128, cache_kv_w512, cache_kv_w2048)
    lyr = 0

    w_in16 = w_in[lyr].astype(BF16)
    wc16 = w_conv_out[lyr].astype(BF16)
    wa16 = w_att_out[lyr].astype(BF16)
    wo16 = w_o[lyr].astype(BF16)
    cw = conv_w[lyr]
    g1, b1 = ln1_g[lyr][None, :], ln1_b[lyr][None, :]
    g2, b2 = ln2_g[lyr][None, :], ln2_b[lyr][None, :]
    w_re = jnp.transpose(w_router_expert[lyr], (1, 0, 2)).reshape(D_MODEL, N_EXPERTS)
    w_r = jnp.pad(jnp.concatenate([w_router_group[lyr], w_re], axis=1),
                  ((0, 0), (0, LANES - N_EXPERT_GROUPS - N_EXPERTS)))
    w_r_hi = w_r.astype(BF16)
    w_r = jnp.concatenate([w_r_hi, (w_r - w_r_hi.astype(F32)).astype(BF16)], axis=1)
    b_r = jnp.pad(jnp.concatenate([b_router_group[lyr], b_router_expert[lyr].reshape(N_EXPERTS)]),
                  (0, LANES - N_EXPERT_GROUPS - N_EXPERTS))[None, :]

    xp = x_prompt.reshape(m_p, D_MODEL)
    xs = x_sample.reshape(m_s, D_MODEL)

    caches3 = [c[lyr].reshape(bd, -1, HEAD_DIM) for c in caches]
    z_p, qkv_p, shifted = _proj_in(xp, w_in16, min(PROJ_TM, m_p),
                                   [c.reshape(-1, HEAD_DIM) for c in caches3], t_new * KV_ROWS)
    shifted = [s.reshape(c.shape) for s, c in zip(shifted, caches3)]
    p_p, conv_p = _conv_prompt(z_p, cw, bsz, seq)
    o_p = _attn_prompt(qkv_p, bsz, seq)
    h_buf = _mix(p_p, o_p, z_p, xp, wc16, wa16, wo16, g1, b1, min(MIX_TM, m_p), m_all)

    z_s, qkv_s, _ = _proj_in(xs, w_in16, m_s)
    p_s, conv_s = _conv_sample(z_s, cw, state_conv[lyr], t_new)
    q3 = qkv_s.reshape(bd, t_new, QKV_COLS)
    kv5 = qkv_s.reshape(bd, t_new, 3, N_DIL, HEADS_PER_GROUP, HEAD_DIM)
    os_, ls_, bufs_s = [], [], []
    for g, (window, _) in enumerate(DILATED_GROUPS):
        new_kv = kv5[:, :, 1:3, g].reshape(bd, t_new * KV_ROWS, HEAD_DIM)
        buf, o_g, lse_g = _attn_sample(q3, caches3[g], new_kv, shifted[g], g)
        bufs_s.append(buf.reshape(1, bd, window, 2, HEADS_PER_GROUP, HEAD_DIM))
        os_.append(o_g.reshape(m_s, D_GRP))
        ls_.append(lse_g.reshape(m_s, D_GRP))
    o_s = _merge_sample(os_, ls_)
    h_buf = _mix(p_s, o_s, z_s, xs, wc16, wa16, wo16, g1, b1, m_s, m_all, h_all=h_buf, row_off=m_p)

    h_all, h_slabs = h_buf
    idx, gates, counts = _router(h_all, w_r, b_r)
    dest, n_rows, plan = _moe_plan(idx, counts)
    x_slabs = _dispatch(h_slabs, dest, n_rows)
    y_slabs = _experts(x_slabs, plan, w_expert_gate[lyr], w_expert_up[lyr], w_expert_down[lyr])
    y_p = _combine(y_slabs, dest, h_all, gates, g2, b2, 0, m_p)
    y_s = _combine(y_slabs, dest, h_all, gates, g2, b2, m_p, m_s)

    qkv_p3 = qkv_p.reshape(bsz, seq, QKV_COLS)
    bufs_p = []
    for g, (w, _) in enumerate(DILATED_GROUPS):
        keep = min(w, seq)
        bufs_p.append(_kv_pack(qkv_p3, g, keep).reshape(1, bsz, keep, 2, HEADS_PER_GROUP, HEAD_DIM))
    return (y_p.reshape(bsz, seq, D_MODEL), y_s.reshape(bd, t_new, D_MODEL),
            bufs_p[0], bufs_p[1], bufs_p[2], conv_p[None],
            bufs_s[0], bufs_s[1], bufs_s[2], conv_s[None])
```

```python
import functools

import jax
import jax.numpy as jnp
from jax import lax
from jax.experimental import pallas as pl
from jax.experimental.pallas import tpu as pltpu

F32 = jnp.float32
BF16 = jnp.bfloat16

D_MODEL = 2048
D_CONV = D_MODEL // 2
CONV_W = 3
DILATED_GROUPS = ((128, 1), (512, 4), (2048, 16))
N_DIL = len(DILATED_GROUPS)
HEADS_PER_GROUP = 4
HEAD_DIM = 128
SPAN = 128
D_GRP = HEADS_PER_GROUP * HEAD_DIM
D_ATT = N_DIL * D_GRP
N_EXPERT_GROUPS = 8
EXPERTS_PER_GROUP = 8
N_EXPERTS = N_EXPERT_GROUPS * EXPERTS_PER_GROUP
TOP_K = 2
D_EXPERT = D_MODEL // 4
DEPTH = 1
LN_EPS = 1e-5
ALPHA = (2 * DEPTH) ** 0.25
NEG_INF = -1e30
ATT_SCALE = HEAD_DIM ** -0.5

OFF_B = 0
OFF_C = D_CONV
OFF_H = 2 * D_CONV
OFF_Q = 3 * D_CONV
OFF_K = OFF_Q + D_ATT
OFF_V = OFF_K + D_ATT
OFF_GC = OFF_V + D_ATT
OFF_GA = OFF_GC + D_MODEL
N_IN = OFF_GA + D_MODEL

LANES = 128
SUBLANES = 8
VMEM_LIMIT = 56 * 1024 * 1024

PROJ_TN = 512
PROJ_TM = 1024
Z_COLS = 2 * D_MODEL + 3 * D_CONV
QKV_COLS = 3 * D_ATT
NZ_TILES = Z_COLS // PROJ_TN
NQ_TILES = QKV_COLS // PROJ_TN
MIX_TM = 256
ROUTER_TILE = 640
ROW_TILE = 128
MOE_BLK = 256
W_SLOTS = 3
ATTN_UNROLL = 4
DISPATCH_TILE = 640
DISPATCH_SLOTS = 3
COMBINE_TILE = 256
KV_PACK_TM = 512
SHIFT_CHUNK_ROWS = 4128
SHIFT_AHEAD = 2
SHIFT_SLOTS = 2 * SHIFT_AHEAD
DMA_UNROLL = 8


def _proj_tile_order():
    t = lambda off, width: list(range(off // PROJ_TN, (off + width) // PROJ_TN))
    z_tiles = t(OFF_GC, D_MODEL) + t(OFF_GA, D_MODEL) + t(OFF_B, D_CONV) + t(OFF_C, D_CONV) + t(OFF_H, D_CONV)
    q_tiles = t(OFF_Q, D_ATT) + t(OFF_K, D_ATT) + t(OFF_V, D_ATT)
    return z_tiles + q_tiles


def _params(*sem):
    return pltpu.CompilerParams(dimension_semantics=sem, vmem_limit_bytes=VMEM_LIMIT)


def _shift_chunks(rows, drop, max_rows):
    keep = rows - drop
    best = SUBLANES
    for c in range(SUBLANES, max_rows + 1, SUBLANES):
        if keep % c == 0:
            best = c
    return best, keep // best


def _proj_kernel(shift_plan, drop, order_ref, x_ref, w_ref, *rest):
    del order_ref
    n_shift = len(shift_plan)
    srcs = rest[:n_shift]
    z_ref, qkv_ref = rest[n_shift:n_shift + 2]
    dsts = rest[n_shift + 2:2 * n_shift + 2]
    x16 = rest[2 * n_shift + 2]
    j = pl.program_id(1)

    if n_shift:
        stage, sem_in, sem_out = rest[2 * n_shift + 3:]
        t = pl.program_id(0) * pl.num_programs(1) + j
        n_chunks = shift_plan[-1][2]

        def for_chunk(q, action):
            for n, (chunk, lo, hi) in enumerate(shift_plan):
                @pl.when(jnp.logical_and(q >= lo, q < hi))
                def _(n=n, chunk=chunk, lo=lo):
                    action(n, chunk, pl.multiple_of((q - lo) * chunk, SUBLANES), q & (SHIFT_SLOTS - 1))

        def load(n, chunk, start, slot):
            return pltpu.make_async_copy(srcs[n].at[pl.ds(start + drop, chunk), :],
                                         stage.at[slot, pl.ds(0, chunk), :], sem_in.at[slot])

        def store(n, chunk, start, slot):
            return pltpu.make_async_copy(stage.at[slot, pl.ds(0, chunk), :],
                                         dsts[n].at[pl.ds(start, chunk), :], sem_out.at[slot])

        @pl.when(t == 0)
        def _():
            for q in range(SHIFT_AHEAD):
                for_chunk(t + q, lambda *a: load(*a).start())

        @pl.when(t < n_chunks)
        def _():
            for_chunk(t, lambda *a: load(*a).wait())
            for_chunk(t, lambda *a: store(*a).start())

        @pl.when(jnp.logical_and(t >= SHIFT_AHEAD, t < n_chunks + SHIFT_AHEAD))
        def _():
            for_chunk(t - SHIFT_AHEAD, lambda *a: store(*a).wait())

        @pl.when(t + SHIFT_AHEAD < n_chunks)
        def _():
            for_chunk(t + SHIFT_AHEAD, lambda *a: load(*a).start())

    @pl.when(j == 0)
    def _():
        x16[...] = x_ref[...].astype(BF16)

    @pl.when(j < NZ_TILES)
    def _():
        z_ref[...] = jnp.dot(x16[...], w_ref[...], preferred_element_type=F32).astype(BF16)

    @pl.when(j >= NZ_TILES)
    def _():
        qkv_ref[...] = jnp.dot(x16[...], w_ref[...], preferred_element_type=F32)


def _proj_in(x, w16, tm, shift_srcs=(), drop=0):
    m = x.shape[0]
    n_shift = len(shift_srcs)
    n_steps = (m // tm) * (NZ_TILES + NQ_TILES)
    order = jnp.asarray(_proj_tile_order(), jnp.int32)
    hbm = pl.BlockSpec(memory_space=pl.ANY)
    shift_plan, first = [], 0
    for s in shift_srcs:
        chunk, count = _shift_chunks(s.shape[0], drop, SHIFT_CHUNK_ROWS)
        shift_plan.append((chunk, first, first + count))
        first += count
    assert first + SHIFT_AHEAD <= n_steps, "one shift chunk per grid step, plus the steps that drain"
    shift_scratch = [pltpu.VMEM((SHIFT_SLOTS, SHIFT_CHUNK_ROWS, LANES), F32),
                     pltpu.SemaphoreType.DMA((SHIFT_SLOTS,)), pltpu.SemaphoreType.DMA((SHIFT_SLOTS,))] if n_shift else []
    grid_spec = pltpu.PrefetchScalarGridSpec(
        num_scalar_prefetch=1,
        grid=(m // tm, NZ_TILES + NQ_TILES),
        in_specs=[
            pl.BlockSpec((tm, D_MODEL), lambda i, j, o: (i, 0)),
            pl.BlockSpec((D_MODEL, PROJ_TN), lambda i, j, o: (0, o[j])),
        ] + [hbm] * n_shift,
        out_specs=[
            pl.BlockSpec((tm, PROJ_TN), lambda i, j, o: (i, jnp.minimum(j, NZ_TILES - 1))),
            pl.BlockSpec((tm, PROJ_TN), lambda i, j, o: (i, jnp.maximum(j - NZ_TILES, 0))),
        ] + [hbm] * n_shift,
        scratch_shapes=[pltpu.VMEM((tm, D_MODEL), BF16)] + shift_scratch,
    )
    outs = pl.pallas_call(
        functools.partial(_proj_kernel, tuple(shift_plan), drop),
        grid_spec=grid_spec,
        out_shape=[jax.ShapeDtypeStruct((m, Z_COLS), BF16), jax.ShapeDtypeStruct((m, QKV_COLS), F32)]
        + [jax.ShapeDtypeStruct(s.shape, s.dtype) for s in shift_srcs],
        compiler_params=_params("arbitrary", "arbitrary"),
        name="proj_in",
    )(order, x, w16, *shift_srcs)
    return outs[0], outs[1], outs[2:]


ZB_BLK = (2 * D_MODEL) // D_CONV


def _gated_conv_tile(zb_ref, zc_ref, zh_ref, cw_ref, st_ref, uext, seq_start):
    tm = zb_ref.shape[0]

    @pl.when(seq_start)
    def _():
        uext[0:SUBLANES, :] = jnp.zeros((SUBLANES, D_CONV), F32)

    u = zc_ref[...].astype(F32) * zh_ref[...].astype(F32)
    uext[SUBLANES:SUBLANES + tm, :] = u
    conv = (cw_ref[0:1, :] * uext[SUBLANES - 2:SUBLANES - 2 + tm, :]
            + cw_ref[1:2, :] * uext[SUBLANES - 1:SUBLANES - 1 + tm, :]
            + cw_ref[2:3, :] * u)
    last = uext[tm:tm + SUBLANES, :]
    uext[0:SUBLANES, :] = last
    st_ref[0] = last[SUBLANES - 2:SUBLANES, :]
    return (zb_ref[...].astype(F32) * conv).astype(BF16)


def _conv_sample_kernel(t_new, zb_ref, zc_ref, zh_ref, cw_ref, s1_ref, s2_ref, p_ref, u_ref, uext):
    m = zb_ref.shape[0]
    u = zc_ref[...].astype(F32) * zh_ref[...].astype(F32)
    u_ref[...] = u
    uext[0:SUBLANES, :] = jnp.zeros((SUBLANES, D_CONV), F32)
    uext[SUBLANES:SUBLANES + m, :] = u
    t = lax.broadcasted_iota(jnp.int32, (m, D_CONV), 0) % t_new
    um1 = jnp.where(t < 1, s1_ref[...], uext[SUBLANES - 1:SUBLANES - 1 + m, :])
    um2 = jnp.where(t < 2, s2_ref[...], uext[SUBLANES - 2:SUBLANES - 2 + m, :])
    conv = cw_ref[0:1, :] * um2 + cw_ref[1:2, :] * um1 + cw_ref[2:3, :] * u
    p_ref[...] = (zb_ref[...].astype(F32) * conv).astype(BF16)


def _conv_sample(z, conv_w, state, t_new):
    m = z.shape[0]
    bd = m // t_new
    s1 = jnp.pad(state[:, 1:2], ((0, 0), (0, t_new - 1), (0, 0))).reshape(m, D_CONV)
    s2 = jnp.pad(state, ((0, 0), (0, t_new - 2), (0, 0))).reshape(m, D_CONV)
    zspec = lambda blk: pl.BlockSpec((m, D_CONV), lambda i: (0, blk))
    full = lambda r: pl.BlockSpec((r, D_CONV), lambda i: (0, 0))
    p, u = pl.pallas_call(
        functools.partial(_conv_sample_kernel, t_new),
        grid=(1,),
        in_specs=[zspec(ZB_BLK), zspec(ZB_BLK + 1), zspec(ZB_BLK + 2), full(CONV_W), full(m), full(m)],
        out_specs=[full(m), full(m)],
        out_shape=[jax.ShapeDtypeStruct((m, D_CONV), BF16), jax.ShapeDtypeStruct((m, D_CONV), F32)],
        scratch_shapes=[pltpu.VMEM((m + SUBLANES, D_CONV), F32)],
        compiler_params=_params("arbitrary"),
        name="conv_sample",
    )(z, z, z, conv_w, s1, s2)
    new_state = u.reshape(bd, t_new, D_CONV)[:, t_new - (CONV_W - 1):]
    return p, new_state


def _merge_groups(o_list, lse_list):
    top = functools.reduce(jnp.maximum, lse_list)
    es = [jnp.exp(l - top) for l in lse_list]
    num = functools.reduce(lambda a, b: a + b, [e * o for e, o in zip(es, o_list)])
    den = functools.reduce(lambda a, b: a + b, es)
    return num / den


def _nt_dot(a, b):
    return lax.dot_general(a, b, (((1,), (1,)), ((), ())), preferred_element_type=F32)


def _attn_prompt_group(q_ref, k_ref, v_ref, o_sc, lse_sc, g, dil, seq):
    shift = dil.bit_length() - 1
    qi = lax.broadcasted_iota(jnp.int32, (SPAN, 2 * SPAN), 0)
    kj = lax.broadcasted_iota(jnp.int32, (SPAN, 2 * SPAN), 1)
    band = jnp.logical_and(kj >= qi, kj <= qi + SPAN)
    is_prev = lax.broadcasted_iota(jnp.int32, (1, 2 * SPAN), 1) < SPAN
    ones = jnp.ones((2 * SPAN, HEAD_DIM), BF16)

    def rows(ref, start):
        if dil == 1:
            return ref[pl.ds(start, SPAN), :]
        return ref[pl.ds(start, SPAN, stride=dil), :]

    def body(trip, carry):
        blocks = range(ATTN_UNROLL)
        idx = [trip * ATTN_UNROLL + u for u in blocks]
        c = [i >> shift for i in idx]
        base = [c[u] * (SPAN * dil) + (idx[u] & (dil - 1)) for u in blocks]
        pbase = [jnp.maximum(c[u] - 1, 0) * (SPAN * dil) + (idx[u] & (dil - 1)) for u in blocks]
        q = [rows(q_ref, base[u]).astype(BF16) for u in blocks]
        k = [jnp.concatenate([rows(k_ref, pbase[u]), rows(k_ref, base[u])], axis=0).astype(BF16) for u in blocks]
        s = [_nt_dot(q[u], k[u]) for u in blocks]
        pen = [jnp.where(jnp.logical_and(is_prev, c[u] == 0), NEG_INF, 0.0) for u in blocks]
        s = [jnp.where(band, s[u] * ATT_SCALE + pen[u], NEG_INF) for u in blocks]
        m = [jnp.max(jnp.maximum(s[u][:, :SPAN], s[u][:, SPAN:]), axis=-1, keepdims=True) for u in blocks]
        p = [jnp.exp(s[u] - m[u]).astype(BF16) for u in blocks]
        v = [jnp.concatenate(
            [jnp.concatenate([rows(v_ref, pbase[u]), rows(v_ref, base[u])], axis=0).astype(BF16), ones], axis=1)
            for u in blocks]
        acc = [jnp.dot(p[u], v[u], preferred_element_type=F32) for u in blocks]
        for u in blocks:
            den = acc[u][:, HEAD_DIM:]
            o = acc[u][:, :HEAD_DIM] / den
            lse = m[u] + jnp.log(den)
            if dil == 1:
                o_sc[g, pl.ds(base[u], SPAN), :] = o
                lse_sc[g, pl.ds(base[u], SPAN), :] = lse
            else:
                o_sc[g, pl.ds(base[u], SPAN, stride=dil), :] = o
                lse_sc[g, pl.ds(base[u], SPAN, stride=dil), :] = lse
        return carry

    lax.fori_loop(0, seq // (SPAN * ATTN_UNROLL), body, 0)


def _attn_prompt_kernel(q_ref, k_ref, v_ref, o_ref, o_sc, lse_sc):
    seq = q_ref.shape[0]
    g_id = pl.program_id(2)
    for g, (_, dil) in enumerate(DILATED_GROUPS):
        @pl.when(g_id == g)
        def _(g=g, dil=dil):
            _attn_prompt_group(q_ref, k_ref, v_ref, o_sc, lse_sc, g, dil, seq)

    @pl.when(g_id == N_DIL - 1)
    def _():
        chunk = 256

        def body(i, carry):
            sl = pl.ds(pl.multiple_of(i * chunk, chunk), chunk)
            o = _merge_groups([o_sc[g, sl, :] for g in range(N_DIL)], [lse_sc[g, sl, :] for g in range(N_DIL)])
            o_ref[sl, :] = o.astype(BF16)
            return carry

        lax.fori_loop(0, seq // chunk, body, 0)


def _attn_prompt(qkv, bsz, seq):
    qkv3 = qkv.reshape(bsz, seq, QKV_COLS)
    per_sel = D_ATT // HEAD_DIM
    spec = lambda sel: pl.BlockSpec((None, seq, HEAD_DIM),
                                    lambda b, j, g: (b, 0, sel * per_sel + g * HEADS_PER_GROUP + j))
    out = pl.pallas_call(
        _attn_prompt_kernel,
        grid=(bsz, HEADS_PER_GROUP, N_DIL),
        in_specs=[spec(0), spec(1), spec(2)],
        out_specs=pl.BlockSpec((None, seq, HEAD_DIM), lambda b, j, g: (b, 0, j)),
        out_shape=jax.ShapeDtypeStruct((bsz, seq, D_GRP), BF16),
        scratch_shapes=[pltpu.VMEM((N_DIL, seq, HEAD_DIM), F32), pltpu.VMEM((N_DIL, seq, HEAD_DIM), F32)],
        compiler_params=_params("arbitrary", "arbitrary", "arbitrary"),
        name="attn_prompt",
    )(qkv3, qkv3, qkv3)
    return out.reshape(bsz * seq, D_GRP)


KV_ROWS = 2 * HEADS_PER_GROUP


def _kv_pack_kernel(k_ref, v_ref, out_ref):
    tm = k_ref.shape[0]
    for h in range(HEADS_PER_GROUP):
        cols = slice(h * HEAD_DIM, (h + 1) * HEAD_DIM)
        out_ref[pl.ds(h, tm, stride=KV_ROWS), :] = k_ref[:, cols]
        out_ref[pl.ds(HEADS_PER_GROUP + h, tm, stride=KV_ROWS), :] = v_ref[:, cols]


def _kv_pack(qkv3, g, keep):
    bsz, seq, _ = qkv3.shape
    tm = min(KV_PACK_TM, keep)
    assert keep % tm == 0 and (seq - keep) % tm == 0
    first = (seq - keep) // tm
    col = lambda sel: (sel * D_ATT + g * D_GRP) // D_GRP
    spec = lambda sel: pl.BlockSpec((None, tm, D_GRP), lambda b, s: (b, first + s, col(sel)))
    return pl.pallas_call(
        _kv_pack_kernel,
        grid=(bsz, keep // tm),
        in_specs=[spec(1), spec(2)],
        out_specs=pl.BlockSpec((None, tm * KV_ROWS, HEAD_DIM), lambda b, s: (b, s, 0)),
        out_shape=jax.ShapeDtypeStruct((bsz, keep * KV_ROWS, HEAD_DIM), F32),
        compiler_params=_params("arbitrary", "arbitrary"),
        name="kv_pack",
    )(qkv3, qkv3)


def _attn_sample_group(dil, g, q_ref, cache_ref, new_ref, o_sc, lse_sc):
    t_new = q_ref.shape[0]
    n_buf = cache_ref.shape[0] // KV_ROWS
    tq = lax.broadcasted_iota(jnp.int32, (t_new, n_buf), 0)
    pos = lax.broadcasted_iota(jnp.int32, (t_new, n_buf), 1)
    back = n_buf + tq - pos
    cache_ok = jnp.logical_and(back <= dil * SPAN, (back & (dil - 1)) == 0)
    tq2 = lax.broadcasted_iota(jnp.int32, (t_new, t_new), 0)
    tk2 = lax.broadcasted_iota(jnp.int32, (t_new, t_new), 1)
    new_ok = jnp.logical_and(tk2 <= tq2, ((tq2 - tk2) & (dil - 1)) == 0)
    for h in range(HEADS_PER_GROUP):
        cols = slice(h * HEAD_DIM, (h + 1) * HEAD_DIM)
        q = q_ref[:, g * D_GRP + h * HEAD_DIM:g * D_GRP + (h + 1) * HEAD_DIM].astype(BF16)
        k_c = cache_ref[pl.ds(h, n_buf, stride=KV_ROWS), :].astype(BF16)
        v_c = cache_ref[pl.ds(HEADS_PER_GROUP + h, n_buf, stride=KV_ROWS), :].astype(BF16)
        k_n = new_ref[pl.ds(h, t_new, stride=KV_ROWS), :].astype(BF16)
        v_n = new_ref[pl.ds(HEADS_PER_GROUP + h, t_new, stride=KV_ROWS), :].astype(BF16)
        s_c = jnp.where(cache_ok, _nt_dot(q, k_c) * ATT_SCALE, NEG_INF)
        s_n = jnp.where(new_ok, _nt_dot(q, k_n) * ATT_SCALE, NEG_INF)
        m = jnp.maximum(jnp.max(s_c, axis=-1, keepdims=True), jnp.max(s_n, axis=-1, keepdims=True))
        p_c = jnp.exp(s_c - m)
        p_n = jnp.exp(s_n - m)
        den = jnp.sum(p_c, axis=-1, keepdims=True) + jnp.sum(p_n, axis=-1, keepdims=True)
        acc = (jnp.dot(p_c.astype(BF16), v_c, preferred_element_type=F32)
               + jnp.dot(p_n.astype(BF16), v_n, preferred_element_type=F32))
        o_sc[g, :, cols] = acc / den
        lse_sc[g, :, cols] = jnp.broadcast_to(m + jnp.log(den), (t_new, HEAD_DIM))


def _attn_sample_kernel(q_ref, *refs):
    caches, news = refs[:N_DIL], refs[N_DIL:2 * N_DIL]
    bufs = refs[3 * N_DIL:4 * N_DIL]
    o_ref, o_sc, lse_sc = refs[4 * N_DIL:]
    for g, (_, dil) in enumerate(DILATED_GROUPS):
        _attn_sample_group(dil, g, q_ref, caches[g], news[g], o_sc, lse_sc)
        bufs[g][...] = news[g][...]
    o_ref[...] = _merge_groups([o_sc[g] for g in range(N_DIL)], [lse_sc[g] for g in range(N_DIL)]).astype(BF16)


def _attn_sample(q3, caches, new_kvs, shifted):
    bd, t_new, _ = q3.shape
    new_rows = t_new * KV_ROWS
    row_spec = lambda r: pl.BlockSpec((None, r, HEAD_DIM), lambda b: (b, 0, 0))
    for (window, _), cache in zip(DILATED_GROUPS, caches):
        assert cache.shape[1] == window * KV_ROWS, "the cache must hold a full window"
        assert cache.shape[1] % new_rows == 0
    hbm = pl.BlockSpec(memory_space=pl.ANY)
    tail = lambda c: pl.BlockSpec((None, new_rows, HEAD_DIM), lambda b: (b, c.shape[1] // new_rows - 1, 0))
    outs = pl.pallas_call(
        _attn_sample_kernel,
        grid=(bd,),
        in_specs=[pl.BlockSpec((None, t_new, D_ATT), lambda b: (b, 0, 0))]
        + [row_spec(c.shape[1]) for c in caches] + [row_spec(new_rows)] * N_DIL + [hbm] * N_DIL,
        out_specs=[tail(c) for c in caches] + [pl.BlockSpec((None, t_new, D_GRP), lambda b: (b, 0, 0))],
        out_shape=[jax.ShapeDtypeStruct(c.shape, F32) for c in caches]
        + [jax.ShapeDtypeStruct((bd, t_new, D_GRP), BF16)],
        scratch_shapes=[pltpu.VMEM((N_DIL, t_new, D_GRP), F32), pltpu.VMEM((N_DIL, t_new, D_GRP), F32)],
        input_output_aliases={1 + 2 * N_DIL + g: g for g in range(N_DIL)},
        compiler_params=_params("arbitrary"),
        name="attn_sample",
    )(q3, *caches, *new_kvs, *shifted)
    return outs[:N_DIL], outs[N_DIL]


def _layer_norm(v, g, b):
    mu = jnp.mean(v, axis=-1, keepdims=True)
    d = v - mu
    var = jnp.mean(d * d, axis=-1, keepdims=True)
    return d * lax.rsqrt(var + LN_EPS) * g + b


SLAB_ROWS = SUBLANES
HALF = D_MODEL // 2
HI_MASK = 0xFFFF0000


def _bf16_bits(v):
    return pltpu.bitcast(v.astype(BF16).astype(F32), jnp.uint32)


def _pack_slabs(src_ref, slab_ref, n_tok):
    def body(tg, carry):
        t0 = pl.multiple_of(tg * SUBLANES, SUBLANES)
        for s in range(SLAB_ROWS):
            lo = src_ref[pl.ds(t0, SUBLANES), s * LANES:(s + 1) * LANES]
            hi = src_ref[pl.ds(t0, SUBLANES), HALF + s * LANES:HALF + (s + 1) * LANES]
            word = (_bf16_bits(lo) >> 16) | _bf16_bits(hi)
            slab_ref[pl.ds(t0 * SLAB_ROWS + s, SUBLANES, stride=SLAB_ROWS), :] = word
        return carry
    lax.fori_loop(0, n_tok // SUBLANES, body, 0)


def _unpack_gated_sum(slab_refs, gate_ref, dst_ref, n_tok):
    def body(tg, carry):
        t0 = pl.multiple_of(tg * SUBLANES, SUBLANES)
        gates = [gate_ref[pl.ds(t0, SUBLANES), k:k + 1] for k in range(len(slab_refs))]
        for s in range(SLAB_ROWS):
            words = [ref[pl.ds(t0 * SLAB_ROWS + s, SUBLANES, stride=SLAB_ROWS), :] for ref in slab_refs]
            lo = [g * pltpu.bitcast(w << 16, F32) for g, w in zip(gates, words)]
            hi = [g * pltpu.bitcast(w & jnp.uint32(HI_MASK), F32) for g, w in zip(gates, words)]
            dst_ref[pl.ds(t0, SUBLANES), s * LANES:(s + 1) * LANES] = functools.reduce(lambda a, b: a + b, lo)
            dst_ref[pl.ds(t0, SUBLANES), HALF + s * LANES:HALF + (s + 1) * LANES] = functools.reduce(
                lambda a, b: a + b, hi)
        return carry
    lax.fori_loop(0, n_tok // SUBLANES, body, 0, unroll=4)


def _unpack_slabs_bf16(slab_ref, dst_ref, n_tok, n_valid):
    pair = 2 * SUBLANES
    sub = lax.broadcasted_iota(jnp.int32, (SUBLANES, LANES), 0)

    def body(tg, carry):
        t0 = pl.multiple_of(tg * pair, pair)
        for s in range(SLAB_ROWS):
            words = []
            for half in range(2):
                t1 = t0 + half * SUBLANES
                w = slab_ref[pl.ds(t1 * SLAB_ROWS + s, SUBLANES, stride=SLAB_ROWS), :]
                words.append(jnp.where(t1 + sub < n_valid, w, jnp.uint32(0)))
            lo = jnp.concatenate([pltpu.bitcast(w << 16, F32) for w in words], axis=0)
            hi = jnp.concatenate([pltpu.bitcast(w & jnp.uint32(HI_MASK), F32) for w in words], axis=0)
            dst_ref[pl.ds(t0, pair), s * LANES:(s + 1) * LANES] = lo.astype(BF16)
            dst_ref[pl.ds(t0, pair), HALF + s * LANES:HALF + (s + 1) * LANES] = hi.astype(BF16)
        return carry
    lax.fori_loop(0, n_tok // pair, body, 0)


def _mix_kernel(conv_steps, *refs):
    if conv_steps:
        conv_in, refs = refs[:4], refs[4:]
    else:
        p_ref, refs = refs[0], refs[1:]
    o_ref, gc_ref, ga_ref, x_ref, wc_ref, wa_ref, wo_ref, g_ref, b_ref = refs[:9]
    if conv_steps:
        h_ref, hs_ref, st_ref, uext = refs[-4:]
        p = _gated_conv_tile(*conv_in, st_ref, uext, pl.program_id(0) % conv_steps == 0)
    else:
        h_ref, hs_ref = refs[-2:]
        p = p_ref[...]
    y_conv = jnp.dot(p, wc_ref[...], preferred_element_type=F32)
    y_att = jnp.dot(o_ref[...], wa_ref[...], preferred_element_type=F32)
    merged = (jax.nn.sigmoid(gc_ref[...].astype(F32)) * y_conv
              + jax.nn.sigmoid(ga_ref[...].astype(F32)) * y_att)
    mix = jnp.dot(merged.astype(BF16), wo_ref[...], preferred_element_type=F32)
    h_ref[...] = _layer_norm(ALPHA * x_ref[...] + mix, g_ref[...], b_ref[...])
    _pack_slabs(h_ref, hs_ref, h_ref.shape[0])


def _mix(p, o, z, x, wc16, wa16, wo16, ln_g, ln_b, tm, total, h_all=None, row_off=0, conv=None):
    m = x.shape[0]
    assert row_off % tm == 0
    blk_off = row_off // tm
    row = lambda w: pl.BlockSpec((tm, w), lambda i: (i, 0))
    const = lambda a: pl.BlockSpec(a.shape, lambda i: (0, 0), pipeline_mode=pl.Buffered(1))
    out_specs = [pl.BlockSpec((tm, D_MODEL), lambda i: (i + blk_off, 0)),
                 pl.BlockSpec((tm * SLAB_ROWS, LANES), lambda i: (i + blk_off, 0))]
    out_shape = [jax.ShapeDtypeStruct((total, D_MODEL), F32),
                 jax.ShapeDtypeStruct((total * SLAB_ROWS, LANES), jnp.uint32)]
    scratch, conv_steps = [], 0
    if conv is None:
        in_specs, args = [row(D_CONV)], [p]
    else:
        conv_w, seq = conv
        assert seq % tm == 0
        conv_steps = seq // tm
        zspec = lambda blk: pl.BlockSpec((tm, D_CONV), lambda i: (i, blk))
        in_specs, args = [zspec(ZB_BLK), zspec(ZB_BLK + 1), zspec(ZB_BLK + 2), const(conv_w)], [z, z, z, conv_w]
        out_specs.append(pl.BlockSpec((1, CONV_W - 1, D_CONV), lambda i: (i // conv_steps, 0, 0)))
        out_shape.append(jax.ShapeDtypeStruct((m // seq, CONV_W - 1, D_CONV), F32))
        scratch = [pltpu.VMEM((tm + SUBLANES, D_CONV), F32)]
    in_specs += [row(D_GRP),
                 pl.BlockSpec((tm, D_MODEL), lambda i: (i, 0)), pl.BlockSpec((tm, D_MODEL), lambda i: (i, 1)),
                 row(D_MODEL), const(wc16), const(wa16), const(wo16), const(ln_g), const(ln_b)]
    args += [o, z, z, x, wc16, wa16, wo16, ln_g, ln_b]
    aliases = {}
    if h_all is not None:
        in_specs += [pl.BlockSpec(memory_space=pl.ANY)] * 2
        args += list(h_all)
        aliases = {len(args) - 2: 0, len(args) - 1: 1}
    outs = pl.pallas_call(
        functools.partial(_mix_kernel, conv_steps),
        grid=(m // tm,),
        in_specs=in_specs,
        out_specs=out_specs,
        out_shape=out_shape,
        scratch_shapes=scratch,
        input_output_aliases=aliases,
        compiler_params=_params("arbitrary"),
        name="mix_ln1",
    )(*args)
    return (tuple(outs[:2]), outs[2]) if conv is not None else tuple(outs)


def _route(h, w_ref, b_ref, carry):
    tile = h.shape[0]
    h_hi = h.astype(BF16)
    h_lo = (h - h_hi.astype(F32)).astype(BF16)
    w_both = w_ref[...]
    by_hi = jnp.dot(h_hi, w_both, preferred_element_type=F32)
    by_lo = jnp.dot(h_lo, w_both[:, :LANES], preferred_element_type=F32)
    logits = by_hi[:, :LANES] + (by_hi[:, LANES:] + by_lo) + b_ref[...]
    lane = lax.broadcasted_iota(jnp.int32, (tile, LANES), 1)
    lane_f = lane.astype(F32)
    first_max = lambda v, mx: jnp.min(jnp.where(v == mx, lane_f, float(LANES)), axis=-1, keepdims=True)

    is_grp = lane < N_EXPERT_GROUPS
    lg = jnp.where(is_grp, logits, -jnp.inf)
    mg = jnp.max(lg, axis=-1, keepdims=True)
    gsel = first_max(lg, mg).astype(jnp.int32)
    p_group = 1.0 / jnp.sum(jnp.where(is_grp, jnp.exp(logits - mg), 0.0), axis=-1, keepdims=True)

    e_lane = lane - N_EXPERT_GROUPS
    in_grp = jnp.logical_and(e_lane >= 0, (e_lane >> 3) == gsel)
    le = jnp.where(in_grp, logits, -jnp.inf)
    v1 = jnp.max(le, axis=-1, keepdims=True)
    i1 = first_max(le, v1).astype(jnp.int32)
    le2 = jnp.where(lane == i1, -jnp.inf, le)
    v2 = jnp.max(le2, axis=-1, keepdims=True)
    i2 = first_max(le2, v2).astype(jnp.int32)
    ex = jnp.exp(v2 - v1)
    g1 = p_group / (1.0 + ex)
    g2 = p_group * ex / (1.0 + ex)
    e1 = i1 - N_EXPERT_GROUPS
    e2 = i2 - N_EXPERT_GROUPS

    hot1 = lane == e1
    hot2 = lane == e2
    hot = jnp.where(jnp.logical_or(hot1, hot2), 1.0, 0.0)
    rr = lax.broadcasted_iota(jnp.int32, (tile, tile), 0)
    cc = lax.broadcasted_iota(jnp.int32, (tile, tile), 1)
    tri = jnp.where(cc < rr, 1.0, 0.0).astype(BF16)
    before = jnp.dot(tri, hot.astype(BF16), preferred_element_type=F32) + carry[0:1, :]
    r1 = jnp.sum(jnp.where(hot1, before, 0.0), axis=-1, keepdims=True).astype(jnp.int32)
    r2 = jnp.sum(jnp.where(hot2, before, 0.0), axis=-1, keepdims=True).astype(jnp.int32)
    carry[...] = carry[...] + jnp.sum(hot, axis=0, keepdims=True)

    idx = jnp.where(lane == 0, e1, jnp.where(lane == 1, e2, jnp.where(lane == 2, r1, r2)))
    return idx, jnp.where(lane == 0, g1, g2)


def _router_kernel(h_ref, w_ref, b_ref, idx_ref, gate_ref, cnt_ref, carry):
    @pl.when(pl.program_id(0) == 0)
    def _():
        carry[...] = jnp.zeros(carry.shape, F32)

    idx_ref[...], gate_ref[...] = _route(h_ref[...], w_ref, b_ref, carry)
    cnt_ref[...] = carry[...]


def _router(h_all, w_r, b_r):
    m = h_all.shape[0]
    tile = ROUTER_TILE if m % ROUTER_TILE == 0 else ROW_TILE
    row = pl.BlockSpec((tile, LANES), lambda i: (i, 0))
    return pl.pallas_call(
        _router_kernel,
        grid=(m // tile,),
        in_specs=[pl.BlockSpec((tile, D_MODEL), lambda i: (i, 0)),
                  pl.BlockSpec((D_MODEL, 2 * LANES), lambda i: (0, 0)),
                  pl.BlockSpec((1, LANES), lambda i: (0, 0))],
        out_specs=[row, row, pl.BlockSpec((SUBLANES, LANES), lambda i: (0, 0))],
        out_shape=[jax.ShapeDtypeStruct((m, LANES), jnp.int32), jax.ShapeDtypeStruct((m, LANES), F32),
                   jax.ShapeDtypeStruct((SUBLANES, LANES), F32)],
        scratch_shapes=[pltpu.VMEM((SUBLANES, LANES), F32)],
        compiler_params=_params("arbitrary"),
        name="router",
    )(h_all, w_r, b_r)


def _slab(ref, row):
    return ref.at[pl.ds(pl.multiple_of(row * SLAB_ROWS, SLAB_ROWS), SLAB_ROWS), :]


def _dispatch_kernel(tile, dest_ref, h_hbm, x_hbm, stage, sem_in, sem_rows):
    i = pl.program_id(0)
    n_steps = pl.num_programs(0)
    slot_of = lambda step: lax.rem(step, DISPATCH_SLOTS)

    def load(step):
        rows = pl.ds(pl.multiple_of(step * (tile * SLAB_ROWS), SLAB_ROWS), tile * SLAB_ROWS)
        return pltpu.make_async_copy(h_hbm.at[rows, :], stage.at[slot_of(step)], sem_in.at[slot_of(step)])

    def each_row(step, action):
        slot = slot_of(step)

        def body(r, c):
            for k in range(TOP_K):
                action(pltpu.make_async_copy(_slab(stage.at[slot], r),
                                             _slab(x_hbm, dest_ref[(step * tile + r) * TOP_K + k]),
                                             sem_rows.at[slot]))
            return c
        lax.fori_loop(0, tile, body, 0, unroll=DMA_UNROLL)

    @pl.when(i == 0)
    def _():
        load(i).start()

    @pl.when(i + 1 < n_steps)
    def _():
        load(i + 1).start()

    load(i).wait()
    each_row(i, lambda c: c.start())

    @pl.when(i > 0)
    def _():
        each_row(i - 1, lambda c: c.wait())

    @pl.when(i == n_steps - 1)
    def _():
        each_row(i, lambda c: c.wait())


def _dispatch(h_slabs, dest_flat, n_rows):
    m = h_slabs.shape[0] // SLAB_ROWS
    tile = DISPATCH_TILE if m % DISPATCH_TILE == 0 else ROW_TILE
    grid_spec = pltpu.PrefetchScalarGridSpec(
        num_scalar_prefetch=1,
        grid=(m // tile,),
        in_specs=[pl.BlockSpec(memory_space=pl.ANY)],
        out_specs=pl.BlockSpec(memory_space=pl.ANY),
        scratch_shapes=[pltpu.VMEM((DISPATCH_SLOTS, tile * SLAB_ROWS, LANES), jnp.uint32),
                        pltpu.SemaphoreType.DMA((DISPATCH_SLOTS,)), pltpu.SemaphoreType.DMA((DISPATCH_SLOTS,))],
    )
    return pl.pallas_call(
        functools.partial(_dispatch_kernel, tile),
        grid_spec=grid_spec,
        out_shape=jax.ShapeDtypeStruct((n_rows * SLAB_ROWS, LANES), jnp.uint32),
        compiler_params=pltpu.CompilerParams(dimension_semantics=("arbitrary",), has_side_effects=True),
        name="moe_dispatch",
    )(dest_flat, h_slabs)


def _expert_kernel(blk_e, blk_first, blk_valid, blk_ord, exp_list, n_used, n_exp,
                   x_ref, wg_hbm, wu_hbm, wd_hbm, y_ref, wg32, wu32, wd32, wg16, wu16, wd16, sems, rows32, x16):
    i = pl.program_id(0)

    def weight_copies(e, slot):
        return [pltpu.make_async_copy(hbm.at[e], buf.at[slot], sems.at[slot, n])
                for n, (hbm, buf) in enumerate(((wg_hbm, wg32), (wu_hbm, wu32), (wd_hbm, wd32)))]

    def start_weights(ordinal, slot):
        for c in weight_copies(exp_list[ordinal], slot):
            c.start()

    @pl.when(i == 0)
    def _():
        for ahead in range(W_SLOTS):
            @pl.when(ahead < n_exp[0])
            def _(ahead=ahead):
                start_weights(ahead, ahead)

    @pl.when(i < n_used[0])
    def _():
        @pl.when(blk_first[i] == 1)
        def _():
            ordinal = blk_ord[i]
            slot = lax.rem(ordinal, W_SLOTS)
            for c in weight_copies(blk_e[i], slot):
                c.wait()
            wg16[...] = wg32[slot].astype(BF16)
            wu16[...] = wu32[slot].astype(BF16)
            wd16[...] = wd32[slot].astype(BF16)

            @pl.when(ordinal + W_SLOTS < n_exp[0])
            def _():
                start_weights(ordinal + W_SLOTS, slot)

        _unpack_slabs_bf16(x_ref, x16, MOE_BLK, blk_valid[i])
        x = x16[...]
        gate = jnp.dot(x, wg16[...], preferred_element_type=F32)
        up = jnp.dot(x, wu16[...], preferred_element_type=F32)
        hid = (jax.nn.silu(gate) * up).astype(BF16)
        rows32[...] = jnp.dot(hid, wd16[...], preferred_element_type=F32)
        _pack_slabs(rows32, y_ref, MOE_BLK)


def _experts(x_slabs, plan, w_gate, w_up, w_down):
    n_rows = x_slabs.shape[0] // SLAB_ROWS
    n_blocks = n_rows // MOE_BLK
    row_map = lambda i, be, bf, bv, bo, el, nu, ne: (jnp.minimum(i, nu[0] - 1), 0)
    hbm = pl.BlockSpec(memory_space=pl.ANY)
    grid_spec = pltpu.PrefetchScalarGridSpec(
        num_scalar_prefetch=7,
        grid=(n_blocks,),
        in_specs=[pl.BlockSpec((MOE_BLK * SLAB_ROWS, LANES), row_map), hbm, hbm, hbm],
        out_specs=pl.BlockSpec((MOE_BLK * SLAB_ROWS, LANES), row_map),
        scratch_shapes=[pltpu.VMEM((W_SLOTS, D_MODEL, D_EXPERT), F32), pltpu.VMEM((W_SLOTS, D_MODEL, D_EXPERT), F32),
                        pltpu.VMEM((W_SLOTS, D_EXPERT, D_MODEL), F32),
                        pltpu.VMEM((D_MODEL, D_EXPERT), BF16), pltpu.VMEM((D_MODEL, D_EXPERT), BF16),
                        pltpu.VMEM((D_EXPERT, D_MODEL), BF16),
                        pltpu.SemaphoreType.DMA((W_SLOTS, 3)),
                        pltpu.VMEM((MOE_BLK, D_MODEL), F32), pltpu.VMEM((MOE_BLK, D_MODEL), BF16)],
    )
    return pl.pallas_call(
        _expert_kernel,
        grid_spec=grid_spec,
        out_shape=jax.ShapeDtypeStruct((n_rows * SLAB_ROWS, LANES), jnp.uint32),
        compiler_params=_params("arbitrary"),
        name="moe_experts",
    )(*plan, x_slabs, w_gate, w_up, w_down)


def _combine_kernel(tok_off, tile, dest_ref, y_hbm, h_ref, gate_ref, g_ref, b_ref, out_ref, ybuf, ffn_sum, sems):
    i = pl.program_id(0)
    n_steps = pl.num_programs(0)

    def copy(step, r, k):
        slot = step & 1
        row = dest_ref[(tok_off + step * tile + r) * TOP_K + k]
        return pltpu.make_async_copy(_slab(y_hbm, row), _slab(ybuf.at[slot, k], r), sems.at[slot])

    def gather(step):
        def body(r, c):
            for k in range(TOP_K):
                copy(step, r, k).start()
            return c
        lax.fori_loop(0, tile, body, 0, unroll=DMA_UNROLL)

    @pl.when(i == 0)
    def _():
        gather(i)

    @pl.when(i + 1 < n_steps)
    def _():
        gather(i + 1)

    def wait(r, c):
        for k in range(TOP_K):
            copy(i, r, k).wait()
        return c

    lax.fori_loop(0, tile, wait, 0, unroll=DMA_UNROLL)
    slot = i & 1
    _unpack_gated_sum([ybuf.at[slot, k] for k in range(TOP_K)], gate_ref, ffn_sum, tile)
    out_ref[...] = _layer_norm(ALPHA * h_ref[...] + ffn_sum[...], g_ref[...], b_ref[...])


def _combine(y_slabs, dest_flat, h_all, gates, ln_g, ln_b, tok_off, n_tok):
    tile = COMBINE_TILE if (n_tok % COMBINE_TILE == 0 and tok_off % COMBINE_TILE == 0) else ROW_TILE
    blk_off = tok_off // tile
    grid_spec = pltpu.PrefetchScalarGridSpec(
        num_scalar_prefetch=1,
        grid=(n_tok // tile,),
        in_specs=[pl.BlockSpec(memory_space=pl.ANY),
                  pl.BlockSpec((tile, D_MODEL), lambda i, d: (i + blk_off, 0)),
                  pl.BlockSpec((tile, LANES), lambda i, d: (i + blk_off, 0)),
                  pl.BlockSpec((1, D_MODEL), lambda i, d: (0, 0)),
                  pl.BlockSpec((1, D_MODEL), lambda i, d: (0, 0))],
        out_specs=pl.BlockSpec((tile, D_MODEL), lambda i, d: (i, 0)),
        scratch_shapes=[pltpu.VMEM((2, TOP_K, tile * SLAB_ROWS, LANES), jnp.uint32),
                        pltpu.VMEM((tile, D_MODEL), F32), pltpu.SemaphoreType.DMA((2,))],
    )
    return pl.pallas_call(
        functools.partial(_combine_kernel, tok_off, tile),
        grid_spec=grid_spec,
        out_shape=jax.ShapeDtypeStruct((n_tok, D_MODEL), F32),
        compiler_params=_params("arbitrary"),
        name="combine_ln2",
    )(dest_flat, y_slabs, h_all, gates, ln_g, ln_b)


def _moe_plan(idx, counts_f):
    m = idx.shape[0]
    n_asg = m * TOP_K
    experts = idx[:, 0:TOP_K]
    ranks = idx[:, TOP_K:2 * TOP_K]
    counts = counts_f[0, :N_EXPERTS].astype(jnp.int32)
    padded = (counts + MOE_BLK - 1) // MOE_BLK * MOE_BLK
    pad_end = jnp.cumsum(padded)
    pad_start = pad_end - padded
    expert_ids = jnp.arange(N_EXPERTS, dtype=jnp.int32)
    lookup = lambda hot, table: jnp.sum(jnp.where(hot, table, 0), axis=-1)
    dest = (lookup(experts[:, :, None] == expert_ids, pad_start) + ranks).reshape(n_asg).astype(jnp.int32)
    n_blocks = -(-(n_asg + N_EXPERTS * (MOE_BLK - 1)) // MOE_BLK)
    n_used = jnp.maximum(pad_end[-1] // MOE_BLK, 1).astype(jnp.int32)
    blk = jnp.minimum(jnp.arange(n_blocks, dtype=jnp.int32), n_used - 1)
    blk_row = blk * MOE_BLK
    blk_e = jnp.minimum(jnp.sum((pad_end[None, :] <= blk_row[:, None]).astype(jnp.int32), axis=-1), N_EXPERTS - 1)
    blk_hot = blk_e[:, None] == expert_ids
    blk_off = blk_row - lookup(blk_hot, pad_start)
    blk_first = (blk_off == 0).astype(jnp.int32)
    blk_valid = jnp.clip(lookup(blk_hot, counts) - blk_off, 0, MOE_BLK).astype(jnp.int32)
    used = counts > 0
    exp_list = jnp.argsort(jnp.logical_not(used), stable=True).astype(jnp.int32)
    blk_ord = lookup(blk_hot, jnp.cumsum(used.astype(jnp.int32)) - 1).astype(jnp.int32)
    n_exp = jnp.sum(used.astype(jnp.int32)).reshape(1)
    plan = (blk_e, blk_first, blk_valid, blk_ord, exp_list, n_used.reshape(1), n_exp)
    return dest, n_blocks * MOE_BLK, plan


def kernel(x_prompt, x_sample, cache_kv_w128, cache_kv_w512, cache_kv_w2048, state_conv, w_in, conv_w, w_conv_out, w_att_out, w_o, ln1_g, ln1_b, w_router_group, b_router_group, w_router_expert, b_router_expert, w_expert_gate, w_expert_up, w_expert_down, ln2_g, ln2_b):
    bsz, seq, _ = x_prompt.shape
    bd, t_new, _ = x_sample.shape
    m_p = bsz * seq
    m_s = bd * t_new
    m_all = m_p + m_s
    caches = (cache_kv_w128, cache_kv_w512, cache_kv_w2048)
    lyr = 0

    w_in16 = w_in[lyr].astype(BF16)
    wc16 = w_conv_out[lyr].astype(BF16)
    wa16 = w_att_out[lyr].astype(BF16)
    wo16 = w_o[lyr].astype(BF16)
    cw = conv_w[lyr]
    g1, b1 = ln1_g[lyr][None, :], ln1_b[lyr][None, :]
    g2, b2 = ln2_g[lyr][None, :], ln2_b[lyr][None, :]
    w_re = jnp.transpose(w_router_expert[lyr], (1, 0, 2)).reshape(D_MODEL, N_EXPERTS)
    w_r = jnp.pad(jnp.concatenate([w_router_group[lyr], w_re], axis=1),
                  ((0, 0), (0, LANES - N_EXPERT_GROUPS - N_EXPERTS)))
    w_r_hi = w_r.astype(BF16)
    w_r = jnp.concatenate([w_r_hi, (w_r - w_r_hi.astype(F32)).astype(BF16)], axis=1)
    b_r = jnp.pad(jnp.concatenate([b_router_group[lyr], b_router_expert[lyr].reshape(N_EXPERTS)]),
                  (0, LANES - N_EXPERT_GROUPS - N_EXPERTS))[None, :]

    xp = x_prompt.reshape(m_p, D_MODEL)
    xs = x_sample.reshape(m_s, D_MODEL)

    caches3 = [c[lyr].reshape(bd, -1, HEAD_DIM) for c in caches]
    z_p, qkv_p, shifted = _proj_in(xp, w_in16, min(PROJ_TM, m_p),
                                   [c.reshape(-1, HEAD_DIM) for c in caches3], t_new * KV_ROWS)
    shifted = [s.reshape(c.shape) for s, c in zip(shifted, caches3)]
    o_p = _attn_prompt(qkv_p, bsz, seq)
    h_buf, conv_p = _mix(None, o_p, z_p, xp, wc16, wa16, wo16, g1, b1, min(MIX_TM, seq), m_all, conv=(cw, seq))

    z_s, qkv_s, _ = _proj_in(xs, w_in16, m_s)
    p_s, conv_s = _conv_sample(z_s, cw, state_conv[lyr], t_new)
    q3 = qkv_s.reshape(bd, t_new, QKV_COLS)
    kv5 = qkv_s.reshape(bd, t_new, 3, N_DIL, HEADS_PER_GROUP, HEAD_DIM)
    new_kvs = [kv5[:, :, 1:3, g].reshape(bd, t_new * KV_ROWS, HEAD_DIM) for g in range(N_DIL)]
    bufs_s, o_s = _attn_sample(q3, caches3, new_kvs, shifted)
    bufs_s = [buf.reshape(1, bd, window, 2, HEADS_PER_GROUP, HEAD_DIM)
              for buf, (window, _) in zip(bufs_s, DILATED_GROUPS)]
    h_buf = _mix(p_s, o_s.reshape(m_s, D_GRP), z_s, xs, wc16, wa16, wo16, g1, b1, m_s, m_all, h_all=h_buf, row_off=m_p)

    h_all, h_slabs = h_buf
    idx, gates, counts = _router(h_all, w_r, b_r)
    dest, n_rows, plan = _moe_plan(idx, counts)
    x_slabs = _dispatch(h_slabs, dest, n_rows)
    y_slabs = _experts(x_slabs, plan, w_expert_gate[lyr], w_expert_up[lyr], w_expert_down[lyr])
    y_p = _combine(y_slabs, dest, h_all, gates, g2, b2, 0, m_p)
    y_s = _combine(y_slabs, dest, h_all, gates, g2, b2, m_p, m_s)

    qkv_p3 = qkv_p.reshape(bsz, seq, QKV_COLS)
    bufs_p = []
    for g, (w, _) in enumerate(DILATED_GROUPS):
        keep = min(w, seq)
        bufs_p.append(_kv_pack(qkv_p3, g, keep).reshape(1, bsz, keep, 2, HEADS_PER_GROUP, HEAD_DIM))
    return (y_p.reshape(bsz, seq, D_MODEL), y_s.reshape(bd, t_new, D_MODEL),
            bufs_p[0], bufs_p[1], bufs_p[2], conv_p[None],
            bufs_s[0], bufs_s[1], bufs_s[2], conv_s[None])
```

```python
import functools

import jax
import jax.numpy as jnp
from jax import lax
from jax.experimental import pallas as pl
from jax.experimental.pallas import tpu as pltpu

F32 = jnp.float32
BF16 = jnp.bfloat16

D_MODEL = 2048
D_CONV = D_MODEL // 2
CONV_W = 3
DILATED_GROUPS = ((128, 1), (512, 4), (2048, 16))
N_DIL = len(DILATED_GROUPS)
HEADS_PER_GROUP = 4
HEAD_DIM = 128
SPAN = 128
D_GRP = HEADS_PER_GROUP * HEAD_DIM
D_ATT = N_DIL * D_GRP
N_EXPERT_GROUPS = 8
EXPERTS_PER_GROUP = 8
N_EXPERTS = N_EXPERT_GROUPS * EXPERTS_PER_GROUP
TOP_K = 2
D_EXPERT = D_MODEL // 4
DEPTH = 1
LN_EPS = 1e-5
ALPHA = (2 * DEPTH) ** 0.25
NEG_INF = -1e30
ATT_SCALE = HEAD_DIM ** -0.5

OFF_B = 0
OFF_C = D_CONV
OFF_H = 2 * D_CONV
OFF_Q = 3 * D_CONV
OFF_K = OFF_Q + D_ATT
OFF_V = OFF_K + D_ATT
OFF_GC = OFF_V + D_ATT
OFF_GA = OFF_GC + D_MODEL
N_IN = OFF_GA + D_MODEL

LANES = 128
SUBLANES = 8
VMEM_LIMIT = 56 * 1024 * 1024

PROJ_TN = 512
PROJ_TM = 1024
Z_COLS = 2 * D_MODEL + 3 * D_CONV
QKV_COLS = 3 * D_ATT
NZ_TILES = Z_COLS // PROJ_TN
NQ_TILES = QKV_COLS // PROJ_TN
MIX_TM = 256
ROUTER_TILE = 640
ROW_TILE = 128
MOE_BLK = 256
W_SLOTS = 3
ATTN_UNROLL = 4
DISPATCH_TILE = 640
DISPATCH_SLOTS = 3
COMBINE_TILE = 256
KV_PACK_TM = 512
SHIFT_CHUNK_ROWS = 4128
SHIFT_AHEAD = 2
SHIFT_SLOTS = 2 * SHIFT_AHEAD
DMA_UNROLL = 8


def _proj_tile_order():
    t = lambda off, width: list(range(off // PROJ_TN, (off + width) // PROJ_TN))
    z_tiles = t(OFF_GC, D_MODEL) + t(OFF_GA, D_MODEL) + t(OFF_B, D_CONV) + t(OFF_C, D_CONV) + t(OFF_H, D_CONV)
    q_tiles = t(OFF_Q, D_ATT) + t(OFF_K, D_ATT) + t(OFF_V, D_ATT)
    return z_tiles + q_tiles


def _params(*sem):
    return pltpu.CompilerParams(dimension_semantics=sem, vmem_limit_bytes=VMEM_LIMIT)


def _shift_chunks(rows, drop, max_rows):
    keep = rows - drop
    best = SUBLANES
    for c in range(SUBLANES, max_rows + 1, SUBLANES):
        if keep % c == 0:
            best = c
    return best, keep // best


def _proj_kernel(shift_plan, drop, order_ref, x_ref, w_ref, *rest):
    del order_ref
    n_shift = len(shift_plan)
    srcs = rest[:n_shift]
    z_ref, qkv_ref = rest[n_shift:n_shift + 2]
    dsts = rest[n_shift + 2:2 * n_shift + 2]
    x16 = rest[2 * n_shift + 2]
    j = pl.program_id(1)

    if n_shift:
        stage, sem_in, sem_out = rest[2 * n_shift + 3:]
        t = pl.program_id(0) * pl.num_programs(1) + j
        n_chunks = shift_plan[-1][2]

        def for_chunk(q, action):
            for n, (chunk, lo, hi) in enumerate(shift_plan):
                @pl.when(jnp.logical_and(q >= lo, q < hi))
                def _(n=n, chunk=chunk, lo=lo):
                    action(n, chunk, pl.multiple_of((q - lo) * chunk, SUBLANES), q & (SHIFT_SLOTS - 1))

        def load(n, chunk, start, slot):
            return pltpu.make_async_copy(srcs[n].at[pl.ds(start + drop, chunk), :],
                                         stage.at[slot, pl.ds(0, chunk), :], sem_in.at[slot])

        def store(n, chunk, start, slot):
            return pltpu.make_async_copy(stage.at[slot, pl.ds(0, chunk), :],
                                         dsts[n].at[pl.ds(start, chunk), :], sem_out.at[slot])

        @pl.when(t == 0)
        def _():
            for q in range(SHIFT_AHEAD):
                for_chunk(t + q, lambda *a: load(*a).start())

        @pl.when(t < n_chunks)
        def _():
            for_chunk(t, lambda *a: load(*a).wait())
            for_chunk(t, lambda *a: store(*a).start())

        @pl.when(jnp.logical_and(t >= SHIFT_AHEAD, t < n_chunks + SHIFT_AHEAD))
        def _():
            for_chunk(t - SHIFT_AHEAD, lambda *a: store(*a).wait())

        @pl.when(t + SHIFT_AHEAD < n_chunks)
        def _():
            for_chunk(t + SHIFT_AHEAD, lambda *a: load(*a).start())

    @pl.when(j == 0)
    def _():
        x16[...] = x_ref[...].astype(BF16)

    @pl.when(j < NZ_TILES)
    def _():
        z_ref[...] = jnp.dot(x16[...], w_ref[...], preferred_element_type=F32).astype(BF16)

    @pl.when(j >= NZ_TILES)
    def _():
        qkv_ref[...] = jnp.dot(x16[...], w_ref[...], preferred_element_type=F32)


def _proj_in(x, w16, tm, shift_srcs=(), drop=0):
    m = x.shape[0]
    n_shift = len(shift_srcs)
    n_steps = (m // tm) * (NZ_TILES + NQ_TILES)
    order = jnp.asarray(_proj_tile_order(), jnp.int32)
    hbm = pl.BlockSpec(memory_space=pl.ANY)
    shift_plan, first = [], 0
    for s in shift_srcs:
        chunk, count = _shift_chunks(s.shape[0], drop, SHIFT_CHUNK_ROWS)
        shift_plan.append((chunk, first, first + count))
        first += count
    assert first + SHIFT_AHEAD <= n_steps, "one shift chunk per grid step, plus the steps that drain"
    shift_scratch = [pltpu.VMEM((SHIFT_SLOTS, SHIFT_CHUNK_ROWS, LANES), F32),
                     pltpu.SemaphoreType.DMA((SHIFT_SLOTS,)), pltpu.SemaphoreType.DMA((SHIFT_SLOTS,))] if n_shift else []
    grid_spec = pltpu.PrefetchScalarGridSpec(
        num_scalar_prefetch=1,
        grid=(m // tm, NZ_TILES + NQ_TILES),
        in_specs=[
            pl.BlockSpec((tm, D_MODEL), lambda i, j, o: (i, 0)),
            pl.BlockSpec((D_MODEL, PROJ_TN), lambda i, j, o: (0, o[j])),
        ] + [hbm] * n_shift,
        out_specs=[
            pl.BlockSpec((tm, PROJ_TN), lambda i, j, o: (i, jnp.minimum(j, NZ_TILES - 1))),
            pl.BlockSpec((tm, PROJ_TN), lambda i, j, o: (i, jnp.maximum(j - NZ_TILES, 0))),
        ] + [hbm] * n_shift,
        scratch_shapes=[pltpu.VMEM((tm, D_MODEL), BF16)] + shift_scratch,
    )
    outs = pl.pallas_call(
        functools.partial(_proj_kernel, tuple(shift_plan), drop),
        grid_spec=grid_spec,
        out_shape=[jax.ShapeDtypeStruct((m, Z_COLS), BF16), jax.ShapeDtypeStruct((m, QKV_COLS), F32)]
        + [jax.ShapeDtypeStruct(s.shape, s.dtype) for s in shift_srcs],
        compiler_params=_params("arbitrary", "arbitrary"),
        name="proj_in",
    )(order, x, w16, *shift_srcs)
    return outs[0], outs[1], outs[2:]


ZB_BLK = (2 * D_MODEL) // D_CONV


def _gated_conv_tile(zb_ref, zc_ref, zh_ref, cw_ref, st_ref, uext, seq_start):
    tm = zb_ref.shape[0]

    @pl.when(seq_start)
    def _():
        uext[0:SUBLANES, :] = jnp.zeros((SUBLANES, D_CONV), F32)

    u = zc_ref[...].astype(F32) * zh_ref[...].astype(F32)
    uext[SUBLANES:SUBLANES + tm, :] = u
    conv = (cw_ref[0:1, :] * uext[SUBLANES - 2:SUBLANES - 2 + tm, :]
            + cw_ref[1:2, :] * uext[SUBLANES - 1:SUBLANES - 1 + tm, :]
            + cw_ref[2:3, :] * u)
    last = uext[tm:tm + SUBLANES, :]
    uext[0:SUBLANES, :] = last
    st_ref[0] = last[SUBLANES - 2:SUBLANES, :]
    return (zb_ref[...].astype(F32) * conv).astype(BF16)


def _conv_sample_kernel(t_new, zb_ref, zc_ref, zh_ref, cw_ref, s1_ref, s2_ref, p_ref, u_ref, uext):
    m = zb_ref.shape[0]
    u = zc_ref[...].astype(F32) * zh_ref[...].astype(F32)
    u_ref[...] = u
    uext[0:SUBLANES, :] = jnp.zeros((SUBLANES, D_CONV), F32)
    uext[SUBLANES:SUBLANES + m, :] = u
    t = lax.broadcasted_iota(jnp.int32, (m, D_CONV), 0) % t_new
    um1 = jnp.where(t < 1, s1_ref[...], uext[SUBLANES - 1:SUBLANES - 1 + m, :])
    um2 = jnp.where(t < 2, s2_ref[...], uext[SUBLANES - 2:SUBLANES - 2 + m, :])
    conv = cw_ref[0:1, :] * um2 + cw_ref[1:2, :] * um1 + cw_ref[2:3, :] * u
    p_ref[...] = (zb_ref[...].astype(F32) * conv).astype(BF16)


def _conv_sample(z, conv_w, state, t_new):
    m = z.shape[0]
    bd = m // t_new
    s1 = jnp.pad(state[:, 1:2], ((0, 0), (0, t_new - 1), (0, 0))).reshape(m, D_CONV)
    s2 = jnp.pad(state, ((0, 0), (0, t_new - 2), (0, 0))).reshape(m, D_CONV)
    zspec = lambda blk: pl.BlockSpec((m, D_CONV), lambda i: (0, blk))
    full = lambda r: pl.BlockSpec((r, D_CONV), lambda i: (0, 0))
    p, u = pl.pallas_call(
        functools.partial(_conv_sample_kernel, t_new),
        grid=(1,),
        in_specs=[zspec(ZB_BLK), zspec(ZB_BLK + 1), zspec(ZB_BLK + 2), full(CONV_W), full(m), full(m)],
        out_specs=[full(m), full(m)],
        out_shape=[jax.ShapeDtypeStruct((m, D_CONV), BF16), jax.ShapeDtypeStruct((m, D_CONV), F32)],
        scratch_shapes=[pltpu.VMEM((m + SUBLANES, D_CONV), F32)],
        compiler_params=_params("arbitrary"),
        name="conv_sample",
    )(z, z, z, conv_w, s1, s2)
    new_state = u.reshape(bd, t_new, D_CONV)[:, t_new - (CONV_W - 1):]
    return p, new_state


def _merge_groups(o_list, lse_list):
    top = functools.reduce(jnp.maximum, lse_list)
    es = [jnp.exp(l - top) for l in lse_list]
    num = functools.reduce(lambda a, b: a + b, [e * o for e, o in zip(es, o_list)])
    den = functools.reduce(lambda a, b: a + b, es)
    return num / den


def _nt_dot(a, b):
    return lax.dot_general(a, b, (((1,), (1,)), ((), ())), preferred_element_type=F32)


def _attn_prompt_group(q_ref, k_ref, v_ref, o_sc, lse_sc, g, dil, seq):
    shift = dil.bit_length() - 1
    qi = lax.broadcasted_iota(jnp.int32, (SPAN, 2 * SPAN), 0)
    kj = lax.broadcasted_iota(jnp.int32, (SPAN, 2 * SPAN), 1)
    band = jnp.logical_and(kj >= qi, kj <= qi + SPAN)
    is_prev = lax.broadcasted_iota(jnp.int32, (1, 2 * SPAN), 1) < SPAN
    ones = jnp.ones((2 * SPAN, HEAD_DIM), BF16)

    def rows(ref, start):
        if dil == 1:
            return ref[pl.ds(start, SPAN), :]
        return ref[pl.ds(start, SPAN, stride=dil), :]

    def body(trip, carry):
        blocks = range(ATTN_UNROLL)
        idx = [trip * ATTN_UNROLL + u for u in blocks]
        c = [i >> shift for i in idx]
        base = [c[u] * (SPAN * dil) + (idx[u] & (dil - 1)) for u in blocks]
        pbase = [jnp.maximum(c[u] - 1, 0) * (SPAN * dil) + (idx[u] & (dil - 1)) for u in blocks]
        q = [rows(q_ref, base[u]).astype(BF16) for u in blocks]
        k = [jnp.concatenate([rows(k_ref, pbase[u]), rows(k_ref, base[u])], axis=0).astype(BF16) for u in blocks]
        s = [_nt_dot(q[u], k[u]) for u in blocks]
        pen = [jnp.where(jnp.logical_and(is_prev, c[u] == 0), NEG_INF, 0.0) for u in blocks]
        s = [jnp.where(band, s[u] * ATT_SCALE + pen[u], NEG_INF) for u in blocks]
        m = [jnp.max(jnp.maximum(s[u][:, :SPAN], s[u][:, SPAN:]), axis=-1, keepdims=True) for u in blocks]
        p = [jnp.exp(s[u] - m[u]).astype(BF16) for u in blocks]
        v = [jnp.concatenate(
            [jnp.concatenate([rows(v_ref, pbase[u]), rows(v_ref, base[u])], axis=0).astype(BF16), ones], axis=1)
            for u in blocks]
        acc = [jnp.dot(p[u], v[u], preferred_element_type=F32) for u in blocks]
        for u in blocks:
            den = acc[u][:, HEAD_DIM:]
            o = acc[u][:, :HEAD_DIM] / den
            lse = m[u] + jnp.log(den)
            if dil == 1:
                o_sc[g, pl.ds(base[u], SPAN), :] = o
                lse_sc[g, pl.ds(base[u], SPAN), :] = lse
            else:
                o_sc[g, pl.ds(base[u], SPAN, stride=dil), :] = o
                lse_sc[g, pl.ds(base[u], SPAN, stride=dil), :] = lse
        return carry

    lax.fori_loop(0, seq // (SPAN * ATTN_UNROLL), body, 0)


def _attn_prompt_kernel(q_ref, k_ref, v_ref, o_ref, o_sc, lse_sc):
    seq = q_ref.shape[0]
    g_id = pl.program_id(2)
    for g, (_, dil) in enumerate(DILATED_GROUPS):
        @pl.when(g_id == g)
        def _(g=g, dil=dil):
            _attn_prompt_group(q_ref, k_ref, v_ref, o_sc, lse_sc, g, dil, seq)

    @pl.when(g_id == N_DIL - 1)
    def _():
        chunk = 256

        def body(i, carry):
            sl = pl.ds(pl.multiple_of(i * chunk, chunk), chunk)
            o = _merge_groups([o_sc[g, sl, :] for g in range(N_DIL)], [lse_sc[g, sl, :] for g in range(N_DIL)])
            o_ref[sl, :] = o.astype(BF16)
            return carry

        lax.fori_loop(0, seq // chunk, body, 0)


def _attn_prompt(qkv, bsz, seq):
    qkv3 = qkv.reshape(bsz, seq, QKV_COLS)
    per_sel = D_ATT // HEAD_DIM
    spec = lambda sel: pl.BlockSpec((None, seq, HEAD_DIM),
                                    lambda b, j, g: (b, 0, sel * per_sel + g * HEADS_PER_GROUP + j))
    out = pl.pallas_call(
        _attn_prompt_kernel,
        grid=(bsz, HEADS_PER_GROUP, N_DIL),
        in_specs=[spec(0), spec(1), spec(2)],
        out_specs=pl.BlockSpec((None, seq, HEAD_DIM), lambda b, j, g: (b, 0, j)),
        out_shape=jax.ShapeDtypeStruct((bsz, seq, D_GRP), BF16),
        scratch_shapes=[pltpu.VMEM((N_DIL, seq, HEAD_DIM), F32), pltpu.VMEM((N_DIL, seq, HEAD_DIM), F32)],
        compiler_params=_params("arbitrary", "arbitrary", "arbitrary"),
        name="attn_prompt",
    )(qkv3, qkv3, qkv3)
    return out.reshape(bsz * seq, D_GRP)


KV_ROWS = 2 * HEADS_PER_GROUP


def _kv_pack_kernel(k_ref, v_ref, out_ref):
    tm = k_ref.shape[0]
    for h in range(HEADS_PER_GROUP):
        cols = slice(h * HEAD_DIM, (h + 1) * HEAD_DIM)
        out_ref[pl.ds(h, tm, stride=KV_ROWS), :] = k_ref[:, cols]
        out_ref[pl.ds(HEADS_PER_GROUP + h, tm, stride=KV_ROWS), :] = v_ref[:, cols]


def _kv_pack(qkv3, g, keep):
    bsz, seq, _ = qkv3.shape
    tm = min(KV_PACK_TM, keep)
    assert keep % tm == 0 and (seq - keep) % tm == 0
    first = (seq - keep) // tm
    col = lambda sel: (sel * D_ATT + g * D_GRP) // D_GRP
    spec = lambda sel: pl.BlockSpec((None, tm, D_GRP), lambda b, s: (b, first + s, col(sel)))
    return pl.pallas_call(
        _kv_pack_kernel,
        grid=(bsz, keep // tm),
        in_specs=[spec(1), spec(2)],
        out_specs=pl.BlockSpec((None, tm * KV_ROWS, HEAD_DIM), lambda b, s: (b, s, 0)),
        out_shape=jax.ShapeDtypeStruct((bsz, keep * KV_ROWS, HEAD_DIM), F32),
        compiler_params=_params("arbitrary", "arbitrary"),
        name="kv_pack",
    )(qkv3, qkv3)


def _sample_key_bias(dil, t_new, n_rows, n_buf):
    shape = (HEADS_PER_GROUP * t_new, n_rows)
    rr = lax.broadcasted_iota(jnp.int32, shape, 0)
    cc = lax.broadcasted_iota(jnp.int32, shape, 1)
    head = rr // t_new
    back = n_buf + (rr - head * t_new) - cc // KV_ROWS
    ok = jnp.logical_and(cc % KV_ROWS == HEADS_PER_GROUP + head,
                         jnp.logical_and(jnp.logical_and(back >= 0, back <= dil * SPAN), back % dil == 0))
    return jnp.where(ok, 0.0, NEG_INF)


def _attn_sample_group(g, q_ref, cache_ref, new_ref, bias_c, bias_n, o_sc, lse_sc):
    t_new = q_ref.shape[0]
    q = jnp.concatenate([q_ref[:, g * D_GRP + h * HEAD_DIM:g * D_GRP + (h + 1) * HEAD_DIM]
                         for h in range(HEADS_PER_GROUP)], axis=0).astype(BF16)

    def keys_values(ref):
        rows = ref[...]
        tiles = rows.reshape(-1, KV_ROWS, HEAD_DIM)
        return pltpu.roll(tiles, HEADS_PER_GROUP, 1).reshape(rows.shape).astype(BF16), rows.astype(BF16)

    k_c, v_c = keys_values(cache_ref)
    k_n, v_n = keys_values(new_ref)
    s_c = _nt_dot(q, k_c) * ATT_SCALE + bias_c[...]
    s_n = _nt_dot(q, k_n) * ATT_SCALE + bias_n[...]
    m = jnp.maximum(jnp.max(s_c, axis=-1, keepdims=True), jnp.max(s_n, axis=-1, keepdims=True))
    p_c = jnp.exp(s_c - m)
    p_n = jnp.exp(s_n - m)
    den = jnp.sum(p_c, axis=-1, keepdims=True) + jnp.sum(p_n, axis=-1, keepdims=True)
    acc = (jnp.dot(p_c.astype(BF16), v_c, preferred_element_type=F32)
           + jnp.dot(p_n.astype(BF16), v_n, preferred_element_type=F32))
    o = acc / den
    lse = jnp.broadcast_to(m + jnp.log(den), o.shape)
    for h in range(HEADS_PER_GROUP):
        cols = slice(h * HEAD_DIM, (h + 1) * HEAD_DIM)
        o_sc[g, :, cols] = o[h * t_new:(h + 1) * t_new, :]
        lse_sc[g, :, cols] = lse[h * t_new:(h + 1) * t_new, :]


def _attn_sample_kernel(q_ref, *refs):
    caches, news = refs[:N_DIL], refs[N_DIL:2 * N_DIL]
    bufs = refs[3 * N_DIL:4 * N_DIL]
    o_ref = refs[4 * N_DIL]
    bias_cs = refs[4 * N_DIL + 1:5 * N_DIL + 1]
    bias_n, o_sc, lse_sc = refs[5 * N_DIL + 1:]
    t_new = q_ref.shape[0]

    @pl.when(pl.program_id(0) == 0)
    def _():
        for g, (_, dil) in enumerate(DILATED_GROUPS):
            n_rows = caches[g].shape[0]
            n_buf = n_rows // KV_ROWS
            bias_cs[g][...] = _sample_key_bias(dil, t_new, n_rows, n_buf)
            bias_n[g] = _sample_key_bias(dil, t_new, t_new * KV_ROWS, 0)

    for g in range(N_DIL):
        _attn_sample_group(g, q_ref, caches[g], news[g], bias_cs[g], bias_n.at[g], o_sc, lse_sc)
        bufs[g][...] = news[g][...]
    o_ref[...] = _merge_groups([o_sc[g] for g in range(N_DIL)], [lse_sc[g] for g in range(N_DIL)]).astype(BF16)


def _attn_sample(q3, caches, new_kvs, shifted):
    bd, t_new, _ = q3.shape
    new_rows = t_new * KV_ROWS
    row_spec = lambda r: pl.BlockSpec((None, r, HEAD_DIM), lambda b: (b, 0, 0))
    for (window, _), cache in zip(DILATED_GROUPS, caches):
        assert cache.shape[1] == window * KV_ROWS, "the cache must hold a full window"
        assert cache.shape[1] % new_rows == 0
    hbm = pl.BlockSpec(memory_space=pl.ANY)
    tail = lambda c: pl.BlockSpec((None, new_rows, HEAD_DIM), lambda b: (b, c.shape[1] // new_rows - 1, 0))
    outs = pl.pallas_call(
        _attn_sample_kernel,
        grid=(bd,),
        in_specs=[pl.BlockSpec((None, t_new, D_ATT), lambda b: (b, 0, 0))]
        + [row_spec(c.shape[1]) for c in caches] + [row_spec(new_rows)] * N_DIL + [hbm] * N_DIL,
        out_specs=[tail(c) for c in caches] + [pl.BlockSpec((None, t_new, D_GRP), lambda b: (b, 0, 0))],
        out_shape=[jax.ShapeDtypeStruct(c.shape, F32) for c in caches]
        + [jax.ShapeDtypeStruct((bd, t_new, D_GRP), BF16)],
        scratch_shapes=[pltpu.VMEM((HEADS_PER_GROUP * t_new, c.shape[1]), F32) for c in caches]
        + [pltpu.VMEM((N_DIL, HEADS_PER_GROUP * t_new, new_rows), F32),
           pltpu.VMEM((N_DIL, t_new, D_GRP), F32), pltpu.VMEM((N_DIL, t_new, D_GRP), F32)],
        input_output_aliases={1 + 2 * N_DIL + g: g for g in range(N_DIL)},
        compiler_params=_params("arbitrary"),
        name="attn_sample",
    )(q3, *caches, *new_kvs, *shifted)
    return outs[:N_DIL], outs[N_DIL]


def _layer_norm(v, g, b):
    mu = jnp.mean(v, axis=-1, keepdims=True)
    d = v - mu
    var = jnp.mean(d * d, axis=-1, keepdims=True)
    return d * lax.rsqrt(var + LN_EPS) * g + b


SLAB_ROWS = SUBLANES
HALF = D_MODEL // 2
HI_MASK = 0xFFFF0000


def _bf16_bits(v):
    return pltpu.bitcast(v.astype(BF16).astype(F32), jnp.uint32)


def _pack_slabs(src_ref, slab_ref, n_tok):
    def body(tg, carry):
        t0 = pl.multiple_of(tg * SUBLANES, SUBLANES)
        for s in range(SLAB_ROWS):
            lo = src_ref[pl.ds(t0, SUBLANES), s * LANES:(s + 1) * LANES]
            hi = src_ref[pl.ds(t0, SUBLANES), HALF + s * LANES:HALF + (s + 1) * LANES]
            word = (_bf16_bits(lo) >> 16) | _bf16_bits(hi)
            slab_ref[pl.ds(t0 * SLAB_ROWS + s, SUBLANES, stride=SLAB_ROWS), :] = word
        return carry
    lax.fori_loop(0, n_tok // SUBLANES, body, 0)


def _unpack_gated_sum(slab_refs, gate_ref, dst_ref, n_tok):
    def body(tg, carry):
        t0 = pl.multiple_of(tg * SUBLANES, SUBLANES)
        gates = [gate_ref[pl.ds(t0, SUBLANES), k:k + 1] for k in range(len(slab_refs))]
        for s in range(SLAB_ROWS):
            words = [ref[pl.ds(t0 * SLAB_ROWS + s, SUBLANES, stride=SLAB_ROWS), :] for ref in slab_refs]
            lo = [g * pltpu.bitcast(w << 16, F32) for g, w in zip(gates, words)]
            hi = [g * pltpu.bitcast(w & jnp.uint32(HI_MASK), F32) for g, w in zip(gates, words)]
            dst_ref[pl.ds(t0, SUBLANES), s * LANES:(s + 1) * LANES] = functools.reduce(lambda a, b: a + b, lo)
            dst_ref[pl.ds(t0, SUBLANES), HALF + s * LANES:HALF + (s + 1) * LANES] = functools.reduce(
                lambda a, b: a + b, hi)
        return carry
    lax.fori_loop(0, n_tok // SUBLANES, body, 0, unroll=4)


def _unpack_slabs_bf16(slab_ref, dst_ref, n_tok, n_valid):
    pair = 2 * SUBLANES
    sub = lax.broadcasted_iota(jnp.int32, (SUBLANES, LANES), 0)

    def body(tg, carry):
        t0 = pl.multiple_of(tg * pair, pair)
        for s in range(SLAB_ROWS):
            words = []
            for half in range(2):
                t1 = t0 + half * SUBLANES
                w = slab_ref[pl.ds(t1 * SLAB_ROWS + s, SUBLANES, stride=SLAB_ROWS), :]
                words.append(jnp.where(t1 + sub < n_valid, w, jnp.uint32(0)))
            lo = jnp.concatenate([pltpu.bitcast(w << 16, F32) for w in words], axis=0)
            hi = jnp.concatenate([pltpu.bitcast(w & jnp.uint32(HI_MASK), F32) for w in words], axis=0)
            dst_ref[pl.ds(t0, pair), s * LANES:(s + 1) * LANES] = lo.astype(BF16)
            dst_ref[pl.ds(t0, pair), HALF + s * LANES:HALF + (s + 1) * LANES] = hi.astype(BF16)
        return carry
    lax.fori_loop(0, n_tok // pair, body, 0)


def _mix_kernel(conv_steps, *refs):
    if conv_steps:
        conv_in, refs = refs[:4], refs[4:]
    else:
        p_ref, refs = refs[0], refs[1:]
    o_ref, gc_ref, ga_ref, x_ref, wc_ref, wa_ref, wo_ref, g_ref, b_ref = refs[:9]
    if conv_steps:
        h_ref, hs_ref, st_ref, uext = refs[-4:]
        p = _gated_conv_tile(*conv_in, st_ref, uext, pl.program_id(0) % conv_steps == 0)
    else:
        h_ref, hs_ref = refs[-2:]
        p = p_ref[...]
    y_conv = jnp.dot(p, wc_ref[...], preferred_element_type=F32)
    y_att = jnp.dot(o_ref[...], wa_ref[...], preferred_element_type=F32)
    merged = (jax.nn.sigmoid(gc_ref[...].astype(F32)) * y_conv
              + jax.nn.sigmoid(ga_ref[...].astype(F32)) * y_att)
    mix = jnp.dot(merged.astype(BF16), wo_ref[...], preferred_element_type=F32)
    h_ref[...] = _layer_norm(ALPHA * x_ref[...] + mix, g_ref[...], b_ref[...])
    _pack_slabs(h_ref, hs_ref, h_ref.shape[0])


def _mix(p, o, z, x, wc16, wa16, wo16, ln_g, ln_b, tm, total, h_all=None, row_off=0, conv=None):
    m = x.shape[0]
    assert row_off % tm == 0
    blk_off = row_off // tm
    row = lambda w: pl.BlockSpec((tm, w), lambda i: (i, 0))
    const = lambda a: pl.BlockSpec(a.shape, lambda i: (0, 0), pipeline_mode=pl.Buffered(1))
    out_specs = [pl.BlockSpec((tm, D_MODEL), lambda i: (i + blk_off, 0)),
                 pl.BlockSpec((tm * SLAB_ROWS, LANES), lambda i: (i + blk_off, 0))]
    out_shape = [jax.ShapeDtypeStruct((total, D_MODEL), F32),
                 jax.ShapeDtypeStruct((total * SLAB_ROWS, LANES), jnp.uint32)]
    scratch, conv_steps = [], 0
    if conv is None:
        in_specs, args = [row(D_CONV)], [p]
    else:
        conv_w, seq = conv
        assert seq % tm == 0
        conv_steps = seq // tm
        zspec = lambda blk: pl.BlockSpec((tm, D_CONV), lambda i: (i, blk))
        in_specs, args = [zspec(ZB_BLK), zspec(ZB_BLK + 1), zspec(ZB_BLK + 2), const(conv_w)], [z, z, z, conv_w]
        out_specs.append(pl.BlockSpec((1, CONV_W - 1, D_CONV), lambda i: (i // conv_steps, 0, 0)))
        out_shape.append(jax.ShapeDtypeStruct((m // seq, CONV_W - 1, D_CONV), F32))
        scratch = [pltpu.VMEM((tm + SUBLANES, D_CONV), F32)]
    in_specs += [row(D_GRP),
                 pl.BlockSpec((tm, D_MODEL), lambda i: (i, 0)), pl.BlockSpec((tm, D_MODEL), lambda i: (i, 1)),
                 row(D_MODEL), const(wc16), const(wa16), const(wo16), const(ln_g), const(ln_b)]
    args += [o, z, z, x, wc16, wa16, wo16, ln_g, ln_b]
    aliases = {}
    if h_all is not None:
        in_specs += [pl.BlockSpec(memory_space=pl.ANY)] * 2
        args += list(h_all)
        aliases = {len(args) - 2: 0, len(args) - 1: 1}
    outs = pl.pallas_call(
        functools.partial(_mix_kernel, conv_steps),
        grid=(m // tm,),
        in_specs=in_specs,
        out_specs=out_specs,
        out_shape=out_shape,
        scratch_shapes=scratch,
        input_output_aliases=aliases,
        compiler_params=_params("arbitrary"),
        name="mix_ln1",
    )(*args)
    return (tuple(outs[:2]), outs[2]) if conv is not None else tuple(outs)


def _route(h, w_ref, b_ref, carry):
    tile = h.shape[0]
    h_hi = h.astype(BF16)
    h_lo = (h - h_hi.astype(F32)).astype(BF16)
    w_both = w_ref[...]
    by_hi = jnp.dot(h_hi, w_both, preferred_element_type=F32)
    by_lo = jnp.dot(h_lo, w_both[:, :LANES], preferred_element_type=F32)
    logits = by_hi[:, :LANES] + (by_hi[:, LANES:] + by_lo) + b_ref[...]
    lane = lax.broadcasted_iota(jnp.int32, (tile, LANES), 1)
    lane_f = lane.astype(F32)
    first_max = lambda v, mx: jnp.min(jnp.where(v == mx, lane_f, float(LANES)), axis=-1, keepdims=True)

    is_grp = lane < N_EXPERT_GROUPS
    lg = jnp.where(is_grp, logits, -jnp.inf)
    mg = jnp.max(lg, axis=-1, keepdims=True)
    gsel = first_max(lg, mg).astype(jnp.int32)
    p_group = 1.0 / jnp.sum(jnp.where(is_grp, jnp.exp(logits - mg), 0.0), axis=-1, keepdims=True)

    e_lane = lane - N_EXPERT_GROUPS
    in_grp = jnp.logical_and(e_lane >= 0, (e_lane >> 3) == gsel)
    le = jnp.where(in_grp, logits, -jnp.inf)
    v1 = jnp.max(le, axis=-1, keepdims=True)
    i1 = first_max(le, v1).astype(jnp.int32)
    le2 = jnp.where(lane == i1, -jnp.inf, le)
    v2 = jnp.max(le2, axis=-1, keepdims=True)
    i2 = first_max(le2, v2).astype(jnp.int32)
    ex = jnp.exp(v2 - v1)
    g1 = p_group / (1.0 + ex)
    g2 = p_group * ex / (1.0 + ex)
    e1 = i1 - N_EXPERT_GROUPS
    e2 = i2 - N_EXPERT_GROUPS

    hot1 = lane == e1
    hot2 = lane == e2
    hot = jnp.where(jnp.logical_or(hot1, hot2), 1.0, 0.0)
    rr = lax.broadcasted_iota(jnp.int32, (tile, tile), 0)
    cc = lax.broadcasted_iota(jnp.int32, (tile, tile), 1)
    tri = jnp.where(cc < rr, 1.0, 0.0).astype(BF16)
    before = jnp.dot(tri, hot.astype(BF16), preferred_element_type=F32) + carry[0:1, :]
    r1 = jnp.sum(jnp.where(hot1, before, 0.0), axis=-1, keepdims=True).astype(jnp.int32)
    r2 = jnp.sum(jnp.where(hot2, before, 0.0), axis=-1, keepdims=True).astype(jnp.int32)
    carry[...] = carry[...] + jnp.sum(hot, axis=0, keepdims=True)

    idx = jnp.where(lane == 0, e1, jnp.where(lane == 1, e2, jnp.where(lane == 2, r1, r2)))
    return idx, jnp.where(lane == 0, g1, g2)


def _router_kernel(h_ref, w_ref, b_ref, idx_ref, gate_ref, cnt_ref, carry):
    @pl.when(pl.program_id(0) == 0)
    def _():
        carry[...] = jnp.zeros(carry.shape, F32)

    idx_ref[...], gate_ref[...] = _route(h_ref[...], w_ref, b_ref, carry)
    cnt_ref[...] = carry[...]


def _router(h_all, w_r, b_r):
    m = h_all.shape[0]
    tile = ROUTER_TILE if m % ROUTER_TILE == 0 else ROW_TILE
    row = pl.BlockSpec((tile, LANES), lambda i: (i, 0))
    return pl.pallas_call(
        _router_kernel,
        grid=(m // tile,),
        in_specs=[pl.BlockSpec((tile, D_MODEL), lambda i: (i, 0)),
                  pl.BlockSpec((D_MODEL, 2 * LANES), lambda i: (0, 0)),
                  pl.BlockSpec((1, LANES), lambda i: (0, 0))],
        out_specs=[row, row, pl.BlockSpec((SUBLANES, LANES), lambda i: (0, 0))],
        out_shape=[jax.ShapeDtypeStruct((m, LANES), jnp.int32), jax.ShapeDtypeStruct((m, LANES), F32),
                   jax.ShapeDtypeStruct((SUBLANES, LANES), F32)],
        scratch_shapes=[pltpu.VMEM((SUBLANES, LANES), F32)],
        compiler_params=_params("arbitrary"),
        name="router",
    )(h_all, w_r, b_r)


def _slab(ref, row):
    return ref.at[pl.ds(pl.multiple_of(row * SLAB_ROWS, SLAB_ROWS), SLAB_ROWS), :]


def _dispatch_kernel(tile, dest_ref, h_hbm, x_hbm, stage, sem_in, sem_rows):
    i = pl.program_id(0)
    n_steps = pl.num_programs(0)
    slot_of = lambda step: lax.rem(step, DISPATCH_SLOTS)

    def load(step):
        rows = pl.ds(pl.multiple_of(step * (tile * SLAB_ROWS), SLAB_ROWS), tile * SLAB_ROWS)
        return pltpu.make_async_copy(h_hbm.at[rows, :], stage.at[slot_of(step)], sem_in.at[slot_of(step)])

    def each_row(step, action):
        slot = slot_of(step)

        def body(r, c):
            for k in range(TOP_K):
                action(pltpu.make_async_copy(_slab(stage.at[slot], r),
                                             _slab(x_hbm, dest_ref[(step * tile + r) * TOP_K + k]),
                                             sem_rows.at[slot]))
            return c
        lax.fori_loop(0, tile, body, 0, unroll=DMA_UNROLL)

    @pl.when(i == 0)
    def _():
        load(i).start()

    @pl.when(i + 1 < n_steps)
    def _():
        load(i + 1).start()

    load(i).wait()
    each_row(i, lambda c: c.start())

    @pl.when(i > 0)
    def _():
        each_row(i - 1, lambda c: c.wait())

    @pl.when(i == n_steps - 1)
    def _():
        each_row(i, lambda c: c.wait())


def _dispatch(h_slabs, dest_flat, n_rows):
    m = h_slabs.shape[0] // SLAB_ROWS
    tile = DISPATCH_TILE if m % DISPATCH_TILE == 0 else ROW_TILE
    grid_spec = pltpu.PrefetchScalarGridSpec(
        num_scalar_prefetch=1,
        grid=(m // tile,),
        in_specs=[pl.BlockSpec(memory_space=pl.ANY)],
        out_specs=pl.BlockSpec(memory_space=pl.ANY),
        scratch_shapes=[pltpu.VMEM((DISPATCH_SLOTS, tile * SLAB_ROWS, LANES), jnp.uint32),
                        pltpu.SemaphoreType.DMA((DISPATCH_SLOTS,)), pltpu.SemaphoreType.DMA((DISPATCH_SLOTS,))],
    )
    return pl.pallas_call(
        functools.partial(_dispatch_kernel, tile),
        grid_spec=grid_spec,
        out_shape=jax.ShapeDtypeStruct((n_rows * SLAB_ROWS, LANES), jnp.uint32),
        compiler_params=pltpu.CompilerParams(dimension_semantics=("arbitrary",), has_side_effects=True),
        name="moe_dispatch",
    )(dest_flat, h_slabs)


def _expert_kernel(blk_e, blk_first, blk_valid, blk_ord, exp_list, n_used, n_exp,
                   x_ref, wg_hbm, wu_hbm, wd_hbm, y_ref, wg32, wu32, wd32, wg16, wu16, wd16, sems, rows32, x16):
    i = pl.program_id(0)

    def weight_copies(e, slot):
        return [pltpu.make_async_copy(hbm.at[e], buf.at[slot], sems.at[slot, n])
                for n, (hbm, buf) in enumerate(((wg_hbm, wg32), (wu_hbm, wu32), (wd_hbm, wd32)))]

    def start_weights(ordinal, slot):
        for c in weight_copies(exp_list[ordinal], slot):
            c.start()

    @pl.when(i == 0)
    def _():
        for ahead in range(W_SLOTS):
            @pl.when(ahead < n_exp[0])
            def _(ahead=ahead):
                start_weights(ahead, ahead)

    @pl.when(i < n_used[0])
    def _():
        @pl.when(blk_first[i] == 1)
        def _():
            ordinal = blk_ord[i]
            slot = lax.rem(ordinal, W_SLOTS)
            for c in weight_copies(blk_e[i], slot):
                c.wait()
            wg16[...] = wg32[slot].astype(BF16)
            wu16[...] = wu32[slot].astype(BF16)
            wd16[...] = wd32[slot].astype(BF16)

            @pl.when(ordinal + W_SLOTS < n_exp[0])
            def _():
                start_weights(ordinal + W_SLOTS, slot)

        _unpack_slabs_bf16(x_ref, x16, MOE_BLK, blk_valid[i])
        x = x16[...]
        gate = jnp.dot(x, wg16[...], preferred_element_type=F32)
        up = jnp.dot(x, wu16[...], preferred_element_type=F32)
        hid = (jax.nn.silu(gate) * up).astype(BF16)
        rows32[...] = jnp.dot(hid, wd16[...], preferred_element_type=F32)
        _pack_slabs(rows32, y_ref, MOE_BLK)


def _experts(x_slabs, plan, w_gate, w_up, w_down):
    n_rows = x_slabs.shape[0] // SLAB_ROWS
    n_blocks = n_rows // MOE_BLK
    row_map = lambda i, be, bf, bv, bo, el, nu, ne: (jnp.minimum(i, nu[0] - 1), 0)
    hbm = pl.BlockSpec(memory_space=pl.ANY)
    grid_spec = pltpu.PrefetchScalarGridSpec(
        num_scalar_prefetch=7,
        grid=(n_blocks,),
        in_specs=[pl.BlockSpec((MOE_BLK * SLAB_ROWS, LANES), row_map), hbm, hbm, hbm],
        out_specs=pl.BlockSpec((MOE_BLK * SLAB_ROWS, LANES), row_map),
        scratch_shapes=[pltpu.VMEM((W_SLOTS, D_MODEL, D_EXPERT), F32), pltpu.VMEM((W_SLOTS, D_MODEL, D_EXPERT), F32),
                        pltpu.VMEM((W_SLOTS, D_EXPERT, D_MODEL), F32),
                        pltpu.VMEM((D_MODEL, D_EXPERT), BF16), pltpu.VMEM((D_MODEL, D_EXPERT), BF16),
                        pltpu.VMEM((D_EXPERT, D_MODEL), BF16),
                        pltpu.SemaphoreType.DMA((W_SLOTS, 3)),
                        pltpu.VMEM((MOE_BLK, D_MODEL), F32), pltpu.VMEM((MOE_BLK, D_MODEL), BF16)],
    )
    return pl.pallas_call(
        _expert_kernel,
        grid_spec=grid_spec,
        out_shape=jax.ShapeDtypeStruct((n_rows * SLAB_ROWS, LANES), jnp.uint32),
        compiler_params=_params("arbitrary"),
        name="moe_experts",
    )(*plan, x_slabs, w_gate, w_up, w_down)


def _combine_kernel(tok_off, tile, dest_ref, y_hbm, h_ref, gate_ref, g_ref, b_ref, out_ref, ybuf, ffn_sum, sems):
    i = pl.program_id(0)
    n_steps = pl.num_programs(0)

    def copy(step, r, k):
        slot = step & 1
        row = dest_ref[(tok_off + step * tile + r) * TOP_K + k]
        return pltpu.make_async_copy(_slab(y_hbm, row), _slab(ybuf.at[slot, k], r), sems.at[slot])

    def gather(step):
        def body(r, c):
            for k in range(TOP_K):
                copy(step, r, k).start()
            return c
        lax.fori_loop(0, tile, body, 0, unroll=DMA_UNROLL)

    @pl.when(i == 0)
    def _():
        gather(i)

    @pl.when(i + 1 < n_steps)
    def _():
        gather(i + 1)

    def wait(r, c):
        for k in range(TOP_K):
            copy(i, r, k).wait()
        return c

    lax.fori_loop(0, tile, wait, 0, unroll=DMA_UNROLL)
    slot = i & 1
    _unpack_gated_sum([ybuf.at[slot, k] for k in range(TOP_K)], gate_ref, ffn_sum, tile)
    out_ref[...] = _layer_norm(ALPHA * h_ref[...] + ffn_sum[...], g_ref[...], b_ref[...])


def _combine(y_slabs, dest_flat, h_all, gates, ln_g, ln_b, tok_off, n_tok):
    tile = COMBINE_TILE if (n_tok % COMBINE_TILE == 0 and tok_off % COMBINE_TILE == 0) else ROW_TILE
    blk_off = tok_off // tile
    grid_spec = pltpu.PrefetchScalarGridSpec(
        num_scalar_prefetch=1,
        grid=(n_tok // tile,),
        in_specs=[pl.BlockSpec(memory_space=pl.ANY),
                  pl.BlockSpec((tile, D_MODEL), lambda i, d: (i + blk_off, 0)),
                  pl.BlockSpec((tile, LANES), lambda i, d: (i + blk_off, 0)),
                  pl.BlockSpec((1, D_MODEL), lambda i, d: (0, 0)),
                  pl.BlockSpec((1, D_MODEL), lambda i, d: (0, 0))],
        out_specs=pl.BlockSpec((tile, D_MODEL), lambda i, d: (i, 0)),
        scratch_shapes=[pltpu.VMEM((2, TOP_K, tile * SLAB_ROWS, LANES), jnp.uint32),
                        pltpu.VMEM((tile, D_MODEL), F32), pltpu.SemaphoreType.DMA((2,))],
    )
    return pl.pallas_call(
        functools.partial(_combine_kernel, tok_off, tile),
        grid_spec=grid_spec,
        out_shape=jax.ShapeDtypeStruct((n_tok, D_MODEL), F32),
        compiler_params=_params("arbitrary"),
        name="combine_ln2",
    )(dest_flat, y_slabs, h_all, gates, ln_g, ln_b)


def _moe_plan(idx, counts_f):
    m = idx.shape[0]
    n_asg = m * TOP_K
    experts = idx[:, 0:TOP_K]
    ranks = idx[:, TOP_K:2 * TOP_K]
    counts = counts_f[0, :N_EXPERTS].astype(jnp.int32)
    padded = (counts + MOE_BLK - 1) // MOE_BLK * MOE_BLK
    pad_end = jnp.cumsum(padded)
    pad_start = pad_end - padded
    expert_ids = jnp.arange(N_EXPERTS, dtype=jnp.int32)
    lookup = lambda hot, table: jnp.sum(jnp.where(hot, table, 0), axis=-1)
    dest = (lookup(experts[:, :, None] == expert_ids, pad_start) + ranks).reshape(n_asg).astype(jnp.int32)
    n_blocks = -(-(n_asg + N_EXPERTS * (MOE_BLK - 1)) // MOE_BLK)
    n_used = jnp.maximum(pad_end[-1] // MOE_BLK, 1).astype(jnp.int32)
    blk = jnp.minimum(jnp.arange(n_blocks, dtype=jnp.int32), n_used - 1)
    blk_row = blk * MOE_BLK
    blk_e = jnp.minimum(jnp.sum((pad_end[None, :] <= blk_row[:, None]).astype(jnp.int32), axis=-1), N_EXPERTS - 1)
    blk_hot = blk_e[:, None] == expert_ids
    blk_off = blk_row - lookup(blk_hot, pad_start)
    blk_first = (blk_off == 0).astype(jnp.int32)
    blk_valid = jnp.clip(lookup(blk_hot, counts) - blk_off, 0, MOE_BLK).astype(jnp.int32)
    used = counts > 0
    exp_list = jnp.argsort(jnp.logical_not(used), stable=True).astype(jnp.int32)
    blk_ord = lookup(blk_hot, jnp.cumsum(used.astype(jnp.int32)) - 1).astype(jnp.int32)
    n_exp = jnp.sum(used.astype(jnp.int32)).reshape(1)
    plan = (blk_e, blk_first, blk_valid, blk_ord, exp_list, n_used.reshape(1), n_exp)
    return dest, n_blocks * MOE_BLK, plan


def kernel(x_prompt, x_sample, cache_kv_w128, cache_kv_w512, cache_kv_w2048, state_conv, w_in, conv_w, w_conv_out, w_att_out, w_o, ln1_g, ln1_b, w_router_group, b_router_group, w_router_expert, b_router_expert, w_expert_gate, w_expert_up, w_expert_down, ln2_g, ln2_b):
    bsz, seq, _ = x_prompt.shape
    bd, t_new, _ = x_sample.shape
    m_p = bsz * seq
    m_s = bd * t_new
    m_all = m_p + m_s
    caches = (cache_kv_w128, cache_kv_w512, cache_kv_w2048)
    lyr = 0

    w_in16 = w_in[lyr].astype(BF16)
    wc16 = w_conv_out[lyr].astype(BF16)
    wa16 = w_att_out[lyr].astype(BF16)
    wo16 = w_o[lyr].astype(BF16)
    cw = conv_w[lyr]
    g1, b1 = ln1_g[lyr][None, :], ln1_b[lyr][None, :]
    g2, b2 = ln2_g[lyr][None, :], ln2_b[lyr][None, :]
    w_re = jnp.transpose(w_router_expert[lyr], (1, 0, 2)).reshape(D_MODEL, N_EXPERTS)
    w_r = jnp.pad(jnp.concatenate([w_router_group[lyr], w_re], axis=1),
                  ((0, 0), (0, LANES - N_EXPERT_GROUPS - N_EXPERTS)))
    w_r_hi = w_r.astype(BF16)
    w_r = jnp.concatenate([w_r_hi, (w_r - w_r_hi.astype(F32)).astype(BF16)], axis=1)
    b_r = jnp.pad(jnp.concatenate([b_router_group[lyr], b_router_expert[lyr].reshape(N_EXPERTS)]),
                  (0, LANES - N_EXPERT_GROUPS - N_EXPERTS))[None, :]

    xp = x_prompt.reshape(m_p, D_MODEL)
    xs = x_sample.reshape(m_s, D_MODEL)

    caches3 = [c[lyr].reshape(bd, -1, HEAD_DIM) for c in caches]
    z_p, qkv_p, shifted = _proj_in(xp, w_in16, min(PROJ_TM, m_p),
                                   [c.reshape(-1, HEAD_DIM) for c in caches3], t_new * KV_ROWS)
    shifted = [s.reshape(c.shape) for s, c in zip(shifted, caches3)]
    o_p = _attn_prompt(qkv_p, bsz, seq)
    h_buf, conv_p = _mix(None, o_p, z_p, xp, wc16, wa16, wo16, g1, b1, min(MIX_TM, seq), m_all, conv=(cw, seq))

    z_s, qkv_s, _ = _proj_in(xs, w_in16, m_s)
    p_s, conv_s = _conv_sample(z_s, cw, state_conv[lyr], t_new)
    q3 = qkv_s.reshape(bd, t_new, QKV_COLS)
    kv5 = qkv_s.reshape(bd, t_new, 3, N_DIL, HEADS_PER_GROUP, HEAD_DIM)
    new_kvs = [kv5[:, :, 1:3, g].reshape(bd, t_new * KV_ROWS, HEAD_DIM) for g in range(N_DIL)]
    bufs_s, o_s = _attn_sample(q3, caches3, new_kvs, shifted)
    bufs_s = [buf.reshape(1, bd, window, 2, HEADS_PER_GROUP, HEAD_DIM)
              for buf, (window, _) in zip(bufs_s, DILATED_GROUPS)]
    h_buf = _mix(p_s, o_s.reshape(m_s, D_GRP), z_s, xs, wc16, wa16, wo16, g1, b1, m_s, m_all, h_all=h_buf, row_off=m_p)

    h_all, h_slabs = h_buf
    idx, gates, counts = _router(h_all, w_r, b_r)
    dest, n_rows, plan = _moe_plan(idx, counts)
    x_slabs = _dispatch(h_slabs, dest, n_rows)
    y_slabs = _experts(x_slabs, plan, w_expert_gate[lyr], w_expert_up[lyr], w_expert_down[lyr])
    y_p = _combine(y_slabs, dest, h_all, gates, g2, b2, 0, m_p)
    y_s = _combine(y_slabs, dest, h_all, gates, g2, b2, m_p, m_s)

    qkv_p3 = qkv_p.reshape(bsz, seq, QKV_COLS)
    bufs_p = []
    for g, (w, _) in enumerate(DILATED_GROUPS):
        keep = min(w, seq)
        bufs_p.append(_kv_pack(qkv_p3, g, keep).reshape(1, bsz, keep, 2, HEADS_PER_GROUP, HEAD_DIM))
    return (y_p.reshape(bsz, seq, D_MODEL), y_s.reshape(bd, t_new, D_MODEL),
            bufs_p[0], bufs_p[1], bufs_p[2], conv_p[None],
            bufs_s[0], bufs_s[1], bufs_s[2], conv_s[None])
```

```python
import functools

import jax
import jax.numpy as jnp
from jax import lax
from jax.experimental import pallas as pl
from jax.experimental.pallas import tpu as pltpu

F32 = jnp.float32
BF16 = jnp.bfloat16

D_MODEL = 2048
D_CONV = D_MODEL // 2
CONV_W = 3
DILATED_GROUPS = ((128, 1), (512, 4), (2048, 16))
N_DIL = len(DILATED_GROUPS)
HEADS_PER_GROUP = 4
HEAD_DIM = 128
SPAN = 128
D_GRP = HEADS_PER_GROUP * HEAD_DIM
D_ATT = N_DIL * D_GRP
N_EXPERT_GROUPS = 8
EXPERTS_PER_GROUP = 8
N_EXPERTS = N_EXPERT_GROUPS * EXPERTS_PER_GROUP
TOP_K = 2
D_EXPERT = D_MODEL // 4
DEPTH = 1
LN_EPS = 1e-5
ALPHA = (2 * DEPTH) ** 0.25
NEG_INF = -1e30
ATT_SCALE = HEAD_DIM ** -0.5

OFF_B = 0
OFF_C = D_CONV
OFF_H = 2 * D_CONV
OFF_Q = 3 * D_CONV
OFF_K = OFF_Q + D_ATT
OFF_V = OFF_K + D_ATT
OFF_GC = OFF_V + D_ATT
OFF_GA = OFF_GC + D_MODEL
N_IN = OFF_GA + D_MODEL

LANES = 128
SUBLANES = 8
VMEM_LIMIT = 56 * 1024 * 1024

PROJ_TN = 512
PROJ_TM = 1024
Z_COLS = 2 * D_MODEL + 3 * D_CONV
QKV_COLS = 3 * D_ATT
NZ_TILES = Z_COLS // PROJ_TN
NQ_TILES = QKV_COLS // PROJ_TN
MIX_TM = 256
ROUTER_TILE = 640
ROW_TILE = 128
MOE_BLK = 304
W_SLOTS = 3
ATTN_UNROLL = 8
DISPATCH_TILE = 640
DISPATCH_SLOTS = 3
COMBINE_TILE = 256
KV_PACK_TM = 512
SHIFT_CHUNK_ROWS = 4128
SHIFT_AHEAD = 2
SHIFT_SLOTS = 2 * SHIFT_AHEAD
DMA_UNROLL = 8


def _proj_tile_order():
    t = lambda off, width: list(range(off // PROJ_TN, (off + width) // PROJ_TN))
    z_tiles = t(OFF_GC, D_MODEL) + t(OFF_GA, D_MODEL) + t(OFF_B, D_CONV) + t(OFF_C, D_CONV) + t(OFF_H, D_CONV)
    q_tiles = t(OFF_Q, D_ATT) + t(OFF_K, D_ATT) + t(OFF_V, D_ATT)
    return z_tiles + q_tiles


def _params(*sem):
    return pltpu.CompilerParams(dimension_semantics=sem, vmem_limit_bytes=VMEM_LIMIT)


def _shift_chunks(rows, drop, max_rows):
    keep = rows - drop
    best = SUBLANES
    for c in range(SUBLANES, max_rows + 1, SUBLANES):
        if keep % c == 0:
            best = c
    return best, keep // best


def _proj_kernel(shift_plan, drop, order_ref, x_ref, w_ref, *rest):
    del order_ref
    n_shift = len(shift_plan)
    srcs = rest[:n_shift]
    z_ref, qkv_ref = rest[n_shift:n_shift + 2]
    dsts = rest[n_shift + 2:2 * n_shift + 2]
    x16 = rest[2 * n_shift + 2]
    j = pl.program_id(1)

    if n_shift:
        stage, sem_in, sem_out = rest[2 * n_shift + 3:]
        t = pl.program_id(0) * pl.num_programs(1) + j
        n_chunks = shift_plan[-1][2]

        def for_chunk(q, action):
            for n, (chunk, lo, hi) in enumerate(shift_plan):
                @pl.when(jnp.logical_and(q >= lo, q < hi))
                def _(n=n, chunk=chunk, lo=lo):
                    action(n, chunk, pl.multiple_of((q - lo) * chunk, SUBLANES), q & (SHIFT_SLOTS - 1))

        def load(n, chunk, start, slot):
            return pltpu.make_async_copy(srcs[n].at[pl.ds(start + drop, chunk), :],
                                         stage.at[slot, pl.ds(0, chunk), :], sem_in.at[slot])

        def store(n, chunk, start, slot):
            return pltpu.make_async_copy(stage.at[slot, pl.ds(0, chunk), :],
                                         dsts[n].at[pl.ds(start, chunk), :], sem_out.at[slot])

        @pl.when(t == 0)
        def _():
            for q in range(SHIFT_AHEAD):
                for_chunk(t + q, lambda *a: load(*a).start())

        @pl.when(t < n_chunks)
        def _():
            for_chunk(t, lambda *a: load(*a).wait())
            for_chunk(t, lambda *a: store(*a).start())

        @pl.when(jnp.logical_and(t >= SHIFT_AHEAD, t < n_chunks + SHIFT_AHEAD))
        def _():
            for_chunk(t - SHIFT_AHEAD, lambda *a: store(*a).wait())

        @pl.when(t + SHIFT_AHEAD < n_chunks)
        def _():
            for_chunk(t + SHIFT_AHEAD, lambda *a: load(*a).start())

    @pl.when(j == 0)
    def _():
        x16[...] = x_ref[...].astype(BF16)

    @pl.when(j < NZ_TILES)
    def _():
        z_ref[...] = jnp.dot(x16[...], w_ref[...], preferred_element_type=F32).astype(BF16)

    @pl.when(j >= NZ_TILES)
    def _():
        qkv_ref[...] = jnp.dot(x16[...], w_ref[...], preferred_element_type=F32)


def _proj_in(x, w16, tm, shift_srcs=(), drop=0):
    m = x.shape[0]
    n_shift = len(shift_srcs)
    n_steps = (m // tm) * (NZ_TILES + NQ_TILES)
    order = jnp.asarray(_proj_tile_order(), jnp.int32)
    hbm = pl.BlockSpec(memory_space=pl.ANY)
    shift_plan, first = [], 0
    for s in shift_srcs:
        chunk, count = _shift_chunks(s.shape[0], drop, SHIFT_CHUNK_ROWS)
        shift_plan.append((chunk, first, first + count))
        first += count
    assert first + SHIFT_AHEAD <= n_steps, "one shift chunk per grid step, plus the steps that drain"
    shift_scratch = [pltpu.VMEM((SHIFT_SLOTS, SHIFT_CHUNK_ROWS, LANES), F32),
                     pltpu.SemaphoreType.DMA((SHIFT_SLOTS,)), pltpu.SemaphoreType.DMA((SHIFT_SLOTS,))] if n_shift else []
    grid_spec = pltpu.PrefetchScalarGridSpec(
        num_scalar_prefetch=1,
        grid=(m // tm, NZ_TILES + NQ_TILES),
        in_specs=[
            pl.BlockSpec((tm, D_MODEL), lambda i, j, o: (i, 0)),
            pl.BlockSpec((D_MODEL, PROJ_TN), lambda i, j, o: (0, o[j])),
        ] + [hbm] * n_shift,
        out_specs=[
            pl.BlockSpec((tm, PROJ_TN), lambda i, j, o: (i, jnp.minimum(j, NZ_TILES - 1))),
            pl.BlockSpec((tm, PROJ_TN), lambda i, j, o: (i, jnp.maximum(j - NZ_TILES, 0))),
        ] + [hbm] * n_shift,
        scratch_shapes=[pltpu.VMEM((tm, D_MODEL), BF16)] + shift_scratch,
    )
    outs = pl.pallas_call(
        functools.partial(_proj_kernel, tuple(shift_plan), drop),
        grid_spec=grid_spec,
        out_shape=[jax.ShapeDtypeStruct((m, Z_COLS), BF16), jax.ShapeDtypeStruct((m, QKV_COLS), F32)]
        + [jax.ShapeDtypeStruct(s.shape, s.dtype) for s in shift_srcs],
        compiler_params=_params("arbitrary", "arbitrary"),
        name="proj_in",
    )(order, x, w16, *shift_srcs)
    return outs[0], outs[1], outs[2:]


ZB_BLK = (2 * D_MODEL) // D_CONV


def _gated_conv_tile(zb_ref, zc_ref, zh_ref, cw_ref, st_ref, uext, seq_start):
    tm = zb_ref.shape[0]

    @pl.when(seq_start)
    def _():
        uext[0:SUBLANES, :] = jnp.zeros((SUBLANES, D_CONV), F32)

    u = zc_ref[...].astype(F32) * zh_ref[...].astype(F32)
    uext[SUBLANES:SUBLANES + tm, :] = u
    conv = (cw_ref[0:1, :] * uext[SUBLANES - 2:SUBLANES - 2 + tm, :]
            + cw_ref[1:2, :] * uext[SUBLANES - 1:SUBLANES - 1 + tm, :]
            + cw_ref[2:3, :] * u)
    last = uext[tm:tm + SUBLANES, :]
    uext[0:SUBLANES, :] = last
    st_ref[0] = last[SUBLANES - 2:SUBLANES, :]
    return (zb_ref[...].astype(F32) * conv).astype(BF16)


def _conv_sample_kernel(t_new, zb_ref, zc_ref, zh_ref, cw_ref, s1_ref, s2_ref, p_ref, u_ref, uext):
    m = zb_ref.shape[0]
    u = zc_ref[...].astype(F32) * zh_ref[...].astype(F32)
    u_ref[...] = u
    uext[0:SUBLANES, :] = jnp.zeros((SUBLANES, D_CONV), F32)
    uext[SUBLANES:SUBLANES + m, :] = u
    t = lax.broadcasted_iota(jnp.int32, (m, D_CONV), 0) % t_new
    um1 = jnp.where(t < 1, s1_ref[...], uext[SUBLANES - 1:SUBLANES - 1 + m, :])
    um2 = jnp.where(t < 2, s2_ref[...], uext[SUBLANES - 2:SUBLANES - 2 + m, :])
    conv = cw_ref[0:1, :] * um2 + cw_ref[1:2, :] * um1 + cw_ref[2:3, :] * u
    p_ref[...] = (zb_ref[...].astype(F32) * conv).astype(BF16)


def _conv_sample(z, conv_w, state, t_new):
    m = z.shape[0]
    bd = m // t_new
    s1 = jnp.pad(state[:, 1:2], ((0, 0), (0, t_new - 1), (0, 0))).reshape(m, D_CONV)
    s2 = jnp.pad(state, ((0, 0), (0, t_new - 2), (0, 0))).reshape(m, D_CONV)
    zspec = lambda blk: pl.BlockSpec((m, D_CONV), lambda i: (0, blk))
    full = lambda r: pl.BlockSpec((r, D_CONV), lambda i: (0, 0))
    p, u = pl.pallas_call(
        functools.partial(_conv_sample_kernel, t_new),
        grid=(1,),
        in_specs=[zspec(ZB_BLK), zspec(ZB_BLK + 1), zspec(ZB_BLK + 2), full(CONV_W), full(m), full(m)],
        out_specs=[full(m), full(m)],
        out_shape=[jax.ShapeDtypeStruct((m, D_CONV), BF16), jax.ShapeDtypeStruct((m, D_CONV), F32)],
        scratch_shapes=[pltpu.VMEM((m + SUBLANES, D_CONV), F32)],
        compiler_params=_params("arbitrary"),
        name="conv_sample",
    )(z, z, z, conv_w, s1, s2)
    new_state = u.reshape(bd, t_new, D_CONV)[:, t_new - (CONV_W - 1):]
    return p, new_state


def _merge_groups(o_list, lse_list):
    top = functools.reduce(jnp.maximum, lse_list)
    es = [jnp.exp(l - top) for l in lse_list]
    num = functools.reduce(lambda a, b: a + b, [e * o for e, o in zip(es, o_list)])
    den = functools.reduce(lambda a, b: a + b, es)
    return num / den


def _nt_dot(a, b):
    return lax.dot_general(a, b, (((1,), (1,)), ((), ())), preferred_element_type=F32)


def _attn_prompt_group(q_ref, k_ref, v_ref, o_sc, lse_sc, g, dil, seq):
    shift = dil.bit_length() - 1
    qi = lax.broadcasted_iota(jnp.int32, (SPAN, 2 * SPAN), 0)
    kj = lax.broadcasted_iota(jnp.int32, (SPAN, 2 * SPAN), 1)
    band = jnp.logical_and(kj >= qi, kj <= qi + SPAN)
    is_prev = lax.broadcasted_iota(jnp.int32, (1, 2 * SPAN), 1) < SPAN
    ones = jnp.ones((2 * SPAN, HEAD_DIM), BF16)

    def rows(ref, start):
        if dil == 1:
            return ref[pl.ds(start, SPAN), :]
        return ref[pl.ds(start, SPAN, stride=dil), :]

    def body(trip, carry):
        blocks = range(ATTN_UNROLL)
        idx = [trip * ATTN_UNROLL + u for u in blocks]
        c = [i >> shift for i in idx]
        base = [c[u] * (SPAN * dil) + (idx[u] & (dil - 1)) for u in blocks]
        pbase = [jnp.maximum(c[u] - 1, 0) * (SPAN * dil) + (idx[u] & (dil - 1)) for u in blocks]
        q = [rows(q_ref, base[u]).astype(BF16) for u in blocks]
        k = [jnp.concatenate([rows(k_ref, pbase[u]), rows(k_ref, base[u])], axis=0).astype(BF16) for u in blocks]
        s = [_nt_dot(q[u], k[u]) for u in blocks]
        pen = [jnp.where(jnp.logical_and(is_prev, c[u] == 0), NEG_INF, 0.0) for u in blocks]
        s = [jnp.where(band, s[u] * ATT_SCALE + pen[u], NEG_INF) for u in blocks]
        m = [jnp.max(jnp.maximum(s[u][:, :SPAN], s[u][:, SPAN:]), axis=-1, keepdims=True) for u in blocks]
        p = [jnp.exp(s[u] - m[u]).astype(BF16) for u in blocks]
        v = [jnp.concatenate(
            [jnp.concatenate([rows(v_ref, pbase[u]), rows(v_ref, base[u])], axis=0).astype(BF16), ones], axis=1)
            for u in blocks]
        acc = [jnp.dot(p[u], v[u], preferred_element_type=F32) for u in blocks]
        for u in blocks:
            den = acc[u][:, HEAD_DIM:]
            o = acc[u][:, :HEAD_DIM] / den
            lse = m[u] + jnp.log(den)
            if dil == 1:
                o_sc[g, pl.ds(base[u], SPAN), :] = o
                lse_sc[g, pl.ds(base[u], SPAN), :] = lse
            else:
                o_sc[g, pl.ds(base[u], SPAN, stride=dil), :] = o
                lse_sc[g, pl.ds(base[u], SPAN, stride=dil), :] = lse
        return carry

    lax.fori_loop(0, seq // (SPAN * ATTN_UNROLL), body, 0)


def _attn_prompt_kernel(q_ref, k_ref, v_ref, o_ref, o_sc, lse_sc):
    seq = q_ref.shape[0]
    g_id = pl.program_id(2)
    for g, (_, dil) in enumerate(DILATED_GROUPS):
        @pl.when(g_id == g)
        def _(g=g, dil=dil):
            _attn_prompt_group(q_ref, k_ref, v_ref, o_sc, lse_sc, g, dil, seq)

    @pl.when(g_id == N_DIL - 1)
    def _():
        chunk = 256

        def body(i, carry):
            sl = pl.ds(pl.multiple_of(i * chunk, chunk), chunk)
            o = _merge_groups([o_sc[g, sl, :] for g in range(N_DIL)], [lse_sc[g, sl, :] for g in range(N_DIL)])
            o_ref[sl, :] = o.astype(BF16)
            return carry

        lax.fori_loop(0, seq // chunk, body, 0)


def _attn_prompt(qkv, bsz, seq):
    qkv3 = qkv.reshape(bsz, seq, QKV_COLS)
    per_sel = D_ATT // HEAD_DIM
    spec = lambda sel: pl.BlockSpec((None, seq, HEAD_DIM),
                                    lambda b, j, g: (b, 0, sel * per_sel + g * HEADS_PER_GROUP + j))
    out = pl.pallas_call(
        _attn_prompt_kernel,
        grid=(bsz, HEADS_PER_GROUP, N_DIL),
        in_specs=[spec(0), spec(1), spec(2)],
        out_specs=pl.BlockSpec((None, seq, HEAD_DIM), lambda b, j, g: (b, 0, j)),
        out_shape=jax.ShapeDtypeStruct((bsz, seq, D_GRP), BF16),
        scratch_shapes=[pltpu.VMEM((N_DIL, seq, HEAD_DIM), F32), pltpu.VMEM((N_DIL, seq, HEAD_DIM), F32)],
        compiler_params=_params("arbitrary", "arbitrary", "arbitrary"),
        name="attn_prompt",
    )(qkv3, qkv3, qkv3)
    return out.reshape(bsz * seq, D_GRP)


KV_ROWS = 2 * HEADS_PER_GROUP


def _kv_pack_kernel(k_ref, v_ref, out_ref):
    tm = k_ref.shape[0]
    for h in range(HEADS_PER_GROUP):
        cols = slice(h * HEAD_DIM, (h + 1) * HEAD_DIM)
        out_ref[pl.ds(h, tm, stride=KV_ROWS), :] = k_ref[:, cols]
        out_ref[pl.ds(HEADS_PER_GROUP + h, tm, stride=KV_ROWS), :] = v_ref[:, cols]


def _kv_pack(qkv3, g, keep):
    bsz, seq, _ = qkv3.shape
    tm = min(KV_PACK_TM, keep)
    assert keep % tm == 0 and (seq - keep) % tm == 0
    first = (seq - keep) // tm
    col = lambda sel: (sel * D_ATT + g * D_GRP) // D_GRP
    spec = lambda sel: pl.BlockSpec((None, tm, D_GRP), lambda b, s: (b, first + s, col(sel)))
    return pl.pallas_call(
        _kv_pack_kernel,
        grid=(bsz, keep // tm),
        in_specs=[spec(1), spec(2)],
        out_specs=pl.BlockSpec((None, tm * KV_ROWS, HEAD_DIM), lambda b, s: (b, s, 0)),
        out_shape=jax.ShapeDtypeStruct((bsz, keep * KV_ROWS, HEAD_DIM), F32),
        compiler_params=_params("arbitrary", "arbitrary"),
        name="kv_pack",
    )(qkv3, qkv3)


def _sample_key_bias(dil, t_new, n_rows, n_buf):
    shape = (HEADS_PER_GROUP * t_new, n_rows)
    rr = lax.broadcasted_iota(jnp.int32, shape, 0)
    cc = lax.broadcasted_iota(jnp.int32, shape, 1)
    head = rr // t_new
    back = n_buf + (rr - head * t_new) - cc // KV_ROWS
    ok = jnp.logical_and(cc % KV_ROWS == HEADS_PER_GROUP + head,
                         jnp.logical_and(jnp.logical_and(back >= 0, back <= dil * SPAN), back % dil == 0))
    return jnp.where(ok, 0.0, NEG_INF)


def _attn_sample_group(g, q_ref, cache_ref, new_ref, bias_c, bias_n, o_sc, lse_sc):
    t_new = q_ref.shape[0]
    q = jnp.concatenate([q_ref[:, g * D_GRP + h * HEAD_DIM:g * D_GRP + (h + 1) * HEAD_DIM]
                         for h in range(HEADS_PER_GROUP)], axis=0).astype(BF16)

    def keys_values(ref):
        rows = ref[...]
        tiles = rows.reshape(-1, KV_ROWS, HEAD_DIM)
        return pltpu.roll(tiles, HEADS_PER_GROUP, 1).reshape(rows.shape).astype(BF16), rows.astype(BF16)

    k_c, v_c = keys_values(cache_ref)
    k_n, v_n = keys_values(new_ref)
    s_c = _nt_dot(q, k_c) * ATT_SCALE + bias_c[...]
    s_n = _nt_dot(q, k_n) * ATT_SCALE + bias_n[...]
    m = jnp.maximum(jnp.max(s_c, axis=-1, keepdims=True), jnp.max(s_n, axis=-1, keepdims=True))
    p_c = jnp.exp(s_c - m)
    p_n = jnp.exp(s_n - m)
    den = jnp.sum(p_c, axis=-1, keepdims=True) + jnp.sum(p_n, axis=-1, keepdims=True)
    acc = (jnp.dot(p_c.astype(BF16), v_c, preferred_element_type=F32)
           + jnp.dot(p_n.astype(BF16), v_n, preferred_element_type=F32))
    o = acc / den
    lse = jnp.broadcast_to(m + jnp.log(den), o.shape)
    for h in range(HEADS_PER_GROUP):
        cols = slice(h * HEAD_DIM, (h + 1) * HEAD_DIM)
        o_sc[g, :, cols] = o[h * t_new:(h + 1) * t_new, :]
        lse_sc[g, :, cols] = lse[h * t_new:(h + 1) * t_new, :]


def _attn_sample_kernel(q_ref, *refs):
    caches, news = refs[:N_DIL], refs[N_DIL:2 * N_DIL]
    bufs = refs[3 * N_DIL:4 * N_DIL]
    o_ref = refs[4 * N_DIL]
    bias_cs = refs[4 * N_DIL + 1:5 * N_DIL + 1]
    bias_n, o_sc, lse_sc = refs[5 * N_DIL + 1:]
    t_new = q_ref.shape[0]

    @pl.when(pl.program_id(0) == 0)
    def _():
        for g, (_, dil) in enumerate(DILATED_GROUPS):
            n_rows = caches[g].shape[0]
            n_buf = n_rows // KV_ROWS
            bias_cs[g][...] = _sample_key_bias(dil, t_new, n_rows, n_buf)
            bias_n[g] = _sample_key_bias(dil, t_new, t_new * KV_ROWS, 0)

    for g in range(N_DIL):
        _attn_sample_group(g, q_ref, caches[g], news[g], bias_cs[g], bias_n.at[g], o_sc, lse_sc)
        bufs[g][...] = news[g][...]
    o_ref[...] = _merge_groups([o_sc[g] for g in range(N_DIL)], [lse_sc[g] for g in range(N_DIL)]).astype(BF16)


def _attn_sample(q3, caches, new_kvs, shifted):
    bd, t_new, _ = q3.shape
    new_rows = t_new * KV_ROWS
    row_spec = lambda r: pl.BlockSpec((None, r, HEAD_DIM), lambda b: (b, 0, 0))
    for (window, _), cache in zip(DILATED_GROUPS, caches):
        assert cache.shape[1] == window * KV_ROWS, "the cache must hold a full window"
        assert cache.shape[1] % new_rows == 0
    hbm = pl.BlockSpec(memory_space=pl.ANY)
    tail = lambda c: pl.BlockSpec((None, new_rows, HEAD_DIM), lambda b: (b, c.shape[1] // new_rows - 1, 0))
    outs = pl.pallas_call(
        _attn_sample_kernel,
        grid=(bd,),
        in_specs=[pl.BlockSpec((None, t_new, D_ATT), lambda b: (b, 0, 0))]
        + [row_spec(c.shape[1]) for c in caches] + [row_spec(new_rows)] * N_DIL + [hbm] * N_DIL,
        out_specs=[tail(c) for c in caches] + [pl.BlockSpec((None, t_new, D_GRP), lambda b: (b, 0, 0))],
        out_shape=[jax.ShapeDtypeStruct(c.shape, F32) for c in caches]
        + [jax.ShapeDtypeStruct((bd, t_new, D_GRP), BF16)],
        scratch_shapes=[pltpu.VMEM((HEADS_PER_GROUP * t_new, c.shape[1]), F32) for c in caches]
        + [pltpu.VMEM((N_DIL, HEADS_PER_GROUP * t_new, new_rows), F32),
           pltpu.VMEM((N_DIL, t_new, D_GRP), F32), pltpu.VMEM((N_DIL, t_new, D_GRP), F32)],
        input_output_aliases={1 + 2 * N_DIL + g: g for g in range(N_DIL)},
        compiler_params=_params("arbitrary"),
        name="attn_sample",
    )(q3, *caches, *new_kvs, *shifted)
    return outs[:N_DIL], outs[N_DIL]


def _layer_norm(v, g, b):
    mu = jnp.mean(v, axis=-1, keepdims=True)
    d = v - mu
    var = jnp.mean(d * d, axis=-1, keepdims=True)
    return d * lax.rsqrt(var + LN_EPS) * g + b


SLAB_ROWS = SUBLANES
HALF = D_MODEL // 2
HI_MASK = 0xFFFF0000


def _bf16_bits(v):
    return pltpu.bitcast(v.astype(BF16).astype(F32), jnp.uint32)


def _pack_slabs(src_ref, slab_ref, n_tok):
    def body(tg, carry):
        t0 = pl.multiple_of(tg * SUBLANES, SUBLANES)
        for s in range(SLAB_ROWS):
            lo = src_ref[pl.ds(t0, SUBLANES), s * LANES:(s + 1) * LANES]
            hi = src_ref[pl.ds(t0, SUBLANES), HALF + s * LANES:HALF + (s + 1) * LANES]
            word = (_bf16_bits(lo) >> 16) | _bf16_bits(hi)
            slab_ref[pl.ds(t0 * SLAB_ROWS + s, SUBLANES, stride=SLAB_ROWS), :] = word
        return carry
    lax.fori_loop(0, n_tok // SUBLANES, body, 0)


def _unpack_gated_sum(slab_refs, gate_ref, dst_ref, n_tok):
    def body(tg, carry):
        t0 = pl.multiple_of(tg * SUBLANES, SUBLANES)
        gates = [gate_ref[pl.ds(t0, SUBLANES), k:k + 1] for k in range(len(slab_refs))]
        for s in range(SLAB_ROWS):
            words = [ref[pl.ds(t0 * SLAB_ROWS + s, SUBLANES, stride=SLAB_ROWS), :] for ref in slab_refs]
            lo = [g * pltpu.bitcast(w << 16, F32) for g, w in zip(gates, words)]
            hi = [g * pltpu.bitcast(w & jnp.uint32(HI_MASK), F32) for g, w in zip(gates, words)]
            dst_ref[pl.ds(t0, SUBLANES), s * LANES:(s + 1) * LANES] = functools.reduce(lambda a, b: a + b, lo)
            dst_ref[pl.ds(t0, SUBLANES), HALF + s * LANES:HALF + (s + 1) * LANES] = functools.reduce(
                lambda a, b: a + b, hi)
        return carry
    lax.fori_loop(0, n_tok // SUBLANES, body, 0, unroll=4)


def _unpack_slabs_bf16(slab_ref, dst_ref, n_tok, n_valid):
    pair = 2 * SUBLANES
    sub = lax.broadcasted_iota(jnp.int32, (SUBLANES, LANES), 0)

    def body(tg, carry):
        t0 = pl.multiple_of(tg * pair, pair)
        for s in range(SLAB_ROWS):
            words = []
            for half in range(2):
                t1 = t0 + half * SUBLANES
                w = slab_ref[pl.ds(t1 * SLAB_ROWS + s, SUBLANES, stride=SLAB_ROWS), :]
                words.append(jnp.where(t1 + sub < n_valid, w, jnp.uint32(0)))
            lo = jnp.concatenate([pltpu.bitcast(w << 16, F32) for w in words], axis=0)
            hi = jnp.concatenate([pltpu.bitcast(w & jnp.uint32(HI_MASK), F32) for w in words], axis=0)
            dst_ref[pl.ds(t0, pair), s * LANES:(s + 1) * LANES] = lo.astype(BF16)
            dst_ref[pl.ds(t0, pair), HALF + s * LANES:HALF + (s + 1) * LANES] = hi.astype(BF16)
        return carry
    lax.fori_loop(0, n_tok // pair, body, 0)


def _mix_kernel(conv_steps, *refs):
    if conv_steps:
        conv_in, refs = refs[:4], refs[4:]
    else:
        p_ref, refs = refs[0], refs[1:]
    o_ref, gc_ref, ga_ref, x_ref, wc_ref, wa_ref, wo_ref, g_ref, b_ref = refs[:9]
    if conv_steps:
        h_ref, hs_ref, st_ref, uext = refs[-4:]
        p = _gated_conv_tile(*conv_in, st_ref, uext, pl.program_id(0) % conv_steps == 0)
    else:
        h_ref, hs_ref = refs[-2:]
        p = p_ref[...]
    y_conv = jnp.dot(p, wc_ref[...], preferred_element_type=F32)
    y_att = jnp.dot(o_ref[...], wa_ref[...], preferred_element_type=F32)
    merged = (jax.nn.sigmoid(gc_ref[...].astype(F32)) * y_conv
              + jax.nn.sigmoid(ga_ref[...].astype(F32)) * y_att)
    mix = jnp.dot(merged.astype(BF16), wo_ref[...], preferred_element_type=F32)
    h_ref[...] = _layer_norm(ALPHA * x_ref[...] + mix, g_ref[...], b_ref[...])
    _pack_slabs(h_ref, hs_ref, h_ref.shape[0])


def _mix(p, o, z, x, wc16, wa16, wo16, ln_g, ln_b, tm, total, h_all=None, row_off=0, conv=None):
    m = x.shape[0]
    assert row_off % tm == 0
    blk_off = row_off // tm
    row = lambda w: pl.BlockSpec((tm, w), lambda i: (i, 0))
    const = lambda a: pl.BlockSpec(a.shape, lambda i: (0, 0), pipeline_mode=pl.Buffered(1))
    out_specs = [pl.BlockSpec((tm, D_MODEL), lambda i: (i + blk_off, 0)),
                 pl.BlockSpec((tm * SLAB_ROWS, LANES), lambda i: (i + blk_off, 0))]
    out_shape = [jax.ShapeDtypeStruct((total, D_MODEL), F32),
                 jax.ShapeDtypeStruct((total * SLAB_ROWS, LANES), jnp.uint32)]
    scratch, conv_steps = [], 0
    if conv is None:
        in_specs, args = [row(D_CONV)], [p]
    else:
        conv_w, seq = conv
        assert seq % tm == 0
        conv_steps = seq // tm
        zspec = lambda blk: pl.BlockSpec((tm, D_CONV), lambda i: (i, blk))
        in_specs, args = [zspec(ZB_BLK), zspec(ZB_BLK + 1), zspec(ZB_BLK + 2), const(conv_w)], [z, z, z, conv_w]
        out_specs.append(pl.BlockSpec((1, CONV_W - 1, D_CONV), lambda i: (i // conv_steps, 0, 0)))
        out_shape.append(jax.ShapeDtypeStruct((m // seq, CONV_W - 1, D_CONV), F32))
        scratch = [pltpu.VMEM((tm + SUBLANES, D_CONV), F32)]
    in_specs += [row(D_GRP),
                 pl.BlockSpec((tm, D_MODEL), lambda i: (i, 0)), pl.BlockSpec((tm, D_MODEL), lambda i: (i, 1)),
                 row(D_MODEL), const(wc16), const(wa16), const(wo16), const(ln_g), const(ln_b)]
    args += [o, z, z, x, wc16, wa16, wo16, ln_g, ln_b]
    aliases = {}
    if h_all is not None:
        in_specs += [pl.BlockSpec(memory_space=pl.ANY)] * 2
        args += list(h_all)
        aliases = {len(args) - 2: 0, len(args) - 1: 1}
    outs = pl.pallas_call(
        functools.partial(_mix_kernel, conv_steps),
        grid=(m // tm,),
        in_specs=in_specs,
        out_specs=out_specs,
        out_shape=out_shape,
        scratch_shapes=scratch,
        input_output_aliases=aliases,
        compiler_params=_params("arbitrary"),
        name="mix_ln1",
    )(*args)
    return (tuple(outs[:2]), outs[2]) if conv is not None else tuple(outs)


def _route(h, w_ref, b_ref, carry):
    tile = h.shape[0]
    h_hi = h.astype(BF16)
    h_lo = (h - h_hi.astype(F32)).astype(BF16)
    w_both = w_ref[...]
    by_hi = jnp.dot(h_hi, w_both, preferred_element_type=F32)
    by_lo = jnp.dot(h_lo, w_both[:, :LANES], preferred_element_type=F32)
    logits = by_hi[:, :LANES] + (by_hi[:, LANES:] + by_lo) + b_ref[...]
    lane = lax.broadcasted_iota(jnp.int32, (tile, LANES), 1)
    lane_f = lane.astype(F32)
    first_max = lambda v, mx: jnp.min(jnp.where(v == mx, lane_f, float(LANES)), axis=-1, keepdims=True)

    is_grp = lane < N_EXPERT_GROUPS
    lg = jnp.where(is_grp, logits, -jnp.inf)
    mg = jnp.max(lg, axis=-1, keepdims=True)
    gsel = first_max(lg, mg).astype(jnp.int32)
    p_group = 1.0 / jnp.sum(jnp.where(is_grp, jnp.exp(logits - mg), 0.0), axis=-1, keepdims=True)

    e_lane = lane - N_EXPERT_GROUPS
    in_grp = jnp.logical_and(e_lane >= 0, (e_lane >> 3) == gsel)
    le = jnp.where(in_grp, logits, -jnp.inf)
    v1 = jnp.max(le, axis=-1, keepdims=True)
    i1 = first_max(le, v1).astype(jnp.int32)
    le2 = jnp.where(lane == i1, -jnp.inf, le)
    v2 = jnp.max(le2, axis=-1, keepdims=True)
    i2 = first_max(le2, v2).astype(jnp.int32)
    ex = jnp.exp(v2 - v1)
    g1 = p_group / (1.0 + ex)
    g2 = p_group * ex / (1.0 + ex)
    e1 = i1 - N_EXPERT_GROUPS
    e2 = i2 - N_EXPERT_GROUPS

    hot1 = lane == e1
    hot2 = lane == e2
    hot = jnp.where(jnp.logical_or(hot1, hot2), 1.0, 0.0)
    rr = lax.broadcasted_iota(jnp.int32, (tile, tile), 0)
    cc = lax.broadcasted_iota(jnp.int32, (tile, tile), 1)
    tri = jnp.where(cc < rr, 1.0, 0.0).astype(BF16)
    before = jnp.dot(tri, hot.astype(BF16), preferred_element_type=F32) + carry[0:1, :]
    r1 = jnp.sum(jnp.where(hot1, before, 0.0), axis=-1, keepdims=True).astype(jnp.int32)
    r2 = jnp.sum(jnp.where(hot2, before, 0.0), axis=-1, keepdims=True).astype(jnp.int32)
    carry[...] = carry[...] + jnp.sum(hot, axis=0, keepdims=True)

    idx = jnp.where(lane == 0, e1, jnp.where(lane == 1, e2, jnp.where(lane == 2, r1, r2)))
    return idx, jnp.where(lane == 0, g1, g2)


def _router_kernel(h_ref, w_ref, b_ref, idx_ref, gate_ref, cnt_ref, carry):
    @pl.when(pl.program_id(0) == 0)
    def _():
        carry[...] = jnp.zeros(carry.shape, F32)

    idx_ref[...], gate_ref[...] = _route(h_ref[...], w_ref, b_ref, carry)
    cnt_ref[...] = carry[...]


def _router(h_all, w_r, b_r):
    m = h_all.shape[0]
    tile = ROUTER_TILE if m % ROUTER_TILE == 0 else ROW_TILE
    row = pl.BlockSpec((tile, LANES), lambda i: (i, 0))
    return pl.pallas_call(
        _router_kernel,
        grid=(m // tile,),
        in_specs=[pl.BlockSpec((tile, D_MODEL), lambda i: (i, 0)),
                  pl.BlockSpec((D_MODEL, 2 * LANES), lambda i: (0, 0)),
                  pl.BlockSpec((1, LANES), lambda i: (0, 0))],
        out_specs=[row, row, pl.BlockSpec((SUBLANES, LANES), lambda i: (0, 0))],
        out_shape=[jax.ShapeDtypeStruct((m, LANES), jnp.int32), jax.ShapeDtypeStruct((m, LANES), F32),
                   jax.ShapeDtypeStruct((SUBLANES, LANES), F32)],
        scratch_shapes=[pltpu.VMEM((SUBLANES, LANES), F32)],
        compiler_params=_params("arbitrary"),
        name="router",
    )(h_all, w_r, b_r)


def _slab(ref, row):
    return ref.at[pl.ds(pl.multiple_of(row * SLAB_ROWS, SLAB_ROWS), SLAB_ROWS), :]


def _dispatch_kernel(tile, dest_ref, h_hbm, x_hbm, stage, sem_in, sem_rows):
    i = pl.program_id(0)
    n_steps = pl.num_programs(0)
    slot_of = lambda step: lax.rem(step, DISPATCH_SLOTS)

    def load(step):
        rows = pl.ds(pl.multiple_of(step * (tile * SLAB_ROWS), SLAB_ROWS), tile * SLAB_ROWS)
        return pltpu.make_async_copy(h_hbm.at[rows, :], stage.at[slot_of(step)], sem_in.at[slot_of(step)])

    def each_row(step, action):
        slot = slot_of(step)

        def body(r, c):
            for k in range(TOP_K):
                action(pltpu.make_async_copy(_slab(stage.at[slot], r),
                                             _slab(x_hbm, dest_ref[(step * tile + r) * TOP_K + k]),
                                             sem_rows.at[slot]))
            return c
        lax.fori_loop(0, tile, body, 0, unroll=DMA_UNROLL)

    @pl.when(i == 0)
    def _():
        load(i).start()

    @pl.when(i + 1 < n_steps)
    def _():
        load(i + 1).start()

    load(i).wait()
    each_row(i, lambda c: c.start())

    @pl.when(i > 0)
    def _():
        each_row(i - 1, lambda c: c.wait())

    @pl.when(i == n_steps - 1)
    def _():
        each_row(i, lambda c: c.wait())


def _dispatch(h_slabs, dest_flat, n_rows):
    m = h_slabs.shape[0] // SLAB_ROWS
    tile = DISPATCH_TILE if m % DISPATCH_TILE == 0 else ROW_TILE
    grid_spec = pltpu.PrefetchScalarGridSpec(
        num_scalar_prefetch=1,
        grid=(m // tile,),
        in_specs=[pl.BlockSpec(memory_space=pl.ANY)],
        out_specs=pl.BlockSpec(memory_space=pl.ANY),
        scratch_shapes=[pltpu.VMEM((DISPATCH_SLOTS, tile * SLAB_ROWS, LANES), jnp.uint32),
                        pltpu.SemaphoreType.DMA((DISPATCH_SLOTS,)), pltpu.SemaphoreType.DMA((DISPATCH_SLOTS,))],
    )
    return pl.pallas_call(
        functools.partial(_dispatch_kernel, tile),
        grid_spec=grid_spec,
        out_shape=jax.ShapeDtypeStruct((n_rows * SLAB_ROWS, LANES), jnp.uint32),
        compiler_params=pltpu.CompilerParams(dimension_semantics=("arbitrary",), has_side_effects=True),
        name="moe_dispatch",
    )(dest_flat, h_slabs)


def _expert_kernel(blk_e, blk_first, blk_valid, blk_ord, exp_list, n_used, n_exp,
                   x_ref, wg_hbm, wu_hbm, wd_hbm, y_ref, wg32, wu32, wd32, wg16, wu16, wd16, sems, rows32, x16):
    i = pl.program_id(0)

    def weight_copies(e, slot):
        return [pltpu.make_async_copy(hbm.at[e], buf.at[slot], sems.at[slot, n])
                for n, (hbm, buf) in enumerate(((wg_hbm, wg32), (wu_hbm, wu32), (wd_hbm, wd32)))]

    def start_weights(ordinal, slot):
        for c in weight_copies(exp_list[ordinal], slot):
            c.start()

    @pl.when(i == 0)
    def _():
        for ahead in range(W_SLOTS):
            @pl.when(ahead < n_exp[0])
            def _(ahead=ahead):
                start_weights(ahead, ahead)

    @pl.when(i < n_used[0])
    def _():
        @pl.when(blk_first[i] == 1)
        def _():
            ordinal = blk_ord[i]
            slot = lax.rem(ordinal, W_SLOTS)
            for c in weight_copies(blk_e[i], slot):
                c.wait()
            wg16[...] = wg32[slot].astype(BF16)
            wu16[...] = wu32[slot].astype(BF16)
            wd16[...] = wd32[slot].astype(BF16)

            @pl.when(ordinal + W_SLOTS < n_exp[0])
            def _():
                start_weights(ordinal + W_SLOTS, slot)

        _unpack_slabs_bf16(x_ref, x16, MOE_BLK, blk_valid[i])
        x = x16[...]
        gate = jnp.dot(x, wg16[...], preferred_element_type=F32)
        up = jnp.dot(x, wu16[...], preferred_element_type=F32)
        hid = (jax.nn.silu(gate) * up).astype(BF16)
        rows32[...] = jnp.dot(hid, wd16[...], preferred_element_type=F32)
        _pack_slabs(rows32, y_ref, MOE_BLK)


def _experts(x_slabs, plan, w_gate, w_up, w_down):
    n_rows = x_slabs.shape[0] // SLAB_ROWS
    n_blocks = n_rows // MOE_BLK
    row_map = lambda i, be, bf, bv, bo, el, nu, ne: (jnp.minimum(i, nu[0] - 1), 0)
    hbm = pl.BlockSpec(memory_space=pl.ANY)
    grid_spec = pltpu.PrefetchScalarGridSpec(
        num_scalar_prefetch=7,
        grid=(n_blocks,),
        in_specs=[pl.BlockSpec((MOE_BLK * SLAB_ROWS, LANES), row_map), hbm, hbm, hbm],
        out_specs=pl.BlockSpec((MOE_BLK * SLAB_ROWS, LANES), row_map),
        scratch_shapes=[pltpu.VMEM((W_SLOTS, D_MODEL, D_EXPERT), F32), pltpu.VMEM((W_SLOTS, D_MODEL, D_EXPERT), F32),
                        pltpu.VMEM((W_SLOTS, D_EXPERT, D_MODEL), F32),
                        pltpu.VMEM((D_MODEL, D_EXPERT), BF16), pltpu.VMEM((D_MODEL, D_EXPERT), BF16),
                        pltpu.VMEM((D_EXPERT, D_MODEL), BF16),
                        pltpu.SemaphoreType.DMA((W_SLOTS, 3)),
                        pltpu.VMEM((MOE_BLK, D_MODEL), F32), pltpu.VMEM((MOE_BLK, D_MODEL), BF16)],
    )
    return pl.pallas_call(
        _expert_kernel,
        grid_spec=grid_spec,
        out_shape=jax.ShapeDtypeStruct((n_rows * SLAB_ROWS, LANES), jnp.uint32),
        compiler_params=_params("arbitrary"),
        name="moe_experts",
    )(*plan, x_slabs, w_gate, w_up, w_down)


def _combine_kernel(tok_off, tile, dest_ref, y_hbm, h_ref, gate_ref, g_ref, b_ref, out_ref, ybuf, ffn_sum, sems):
    i = pl.program_id(0)
    n_steps = pl.num_programs(0)

    def copy(step, r, k):
        slot = step & 1
        row = dest_ref[(tok_off + step * tile + r) * TOP_K + k]
        return pltpu.make_async_copy(_slab(y_hbm, row), _slab(ybuf.at[slot, k], r), sems.at[slot])

    def gather(step):
        def body(r, c):
            for k in range(TOP_K):
                copy(step, r, k).start()
            return c
        lax.fori_loop(0, tile, body, 0, unroll=DMA_UNROLL)

    @pl.when(i == 0)
    def _():
        gather(i)

    @pl.when(i + 1 < n_steps)
    def _():
        gather(i + 1)

    def wait(r, c):
        for k in range(TOP_K):
            copy(i, r, k).wait()
        return c

    lax.fori_loop(0, tile, wait, 0, unroll=DMA_UNROLL)
    slot = i & 1
    _unpack_gated_sum([ybuf.at[slot, k] for k in range(TOP_K)], gate_ref, ffn_sum, tile)
    out_ref[...] = _layer_norm(ALPHA * h_ref[...] + ffn_sum[...], g_ref[...], b_ref[...])


def _combine(y_slabs, dest_flat, h_all, gates, ln_g, ln_b, tok_off, n_tok):
    tile = COMBINE_TILE if (n_tok % COMBINE_TILE == 0 and tok_off % COMBINE_TILE == 0) else ROW_TILE
    blk_off = tok_off // tile
    grid_spec = pltpu.PrefetchScalarGridSpec(
        num_scalar_prefetch=1,
        grid=(n_tok // tile,),
        in_specs=[pl.BlockSpec(memory_space=pl.ANY),
                  pl.BlockSpec((tile, D_MODEL), lambda i, d: (i + blk_off, 0)),
                  pl.BlockSpec((tile, LANES), lambda i, d: (i + blk_off, 0)),
                  pl.BlockSpec((1, D_MODEL), lambda i, d: (0, 0)),
                  pl.BlockSpec((1, D_MODEL), lambda i, d: (0, 0))],
        out_specs=pl.BlockSpec((tile, D_MODEL), lambda i, d: (i, 0)),
        scratch_shapes=[pltpu.VMEM((2, TOP_K, tile * SLAB_ROWS, LANES), jnp.uint32),
                        pltpu.VMEM((tile, D_MODEL), F32), pltpu.SemaphoreType.DMA((2,))],
    )
    return pl.pallas_call(
        functools.partial(_combine_kernel, tok_off, tile),
        grid_spec=grid_spec,
        out_shape=jax.ShapeDtypeStruct((n_tok, D_MODEL), F32),
        compiler_params=_params("arbitrary"),
        name="combine_ln2",
    )(dest_flat, y_slabs, h_all, gates, ln_g, ln_b)


def _moe_plan(idx, counts_f):
    m = idx.shape[0]
    n_asg = m * TOP_K
    experts = idx[:, 0:TOP_K]
    ranks = idx[:, TOP_K:2 * TOP_K]
    counts = counts_f[0, :N_EXPERTS].astype(jnp.int32)
    padded = (counts + MOE_BLK - 1) // MOE_BLK * MOE_BLK
    pad_end = jnp.cumsum(padded)
    pad_start = pad_end - padded
    expert_ids = jnp.arange(N_EXPERTS, dtype=jnp.int32)
    lookup = lambda hot, table: jnp.sum(jnp.where(hot, table, 0), axis=-1)
    dest = (lookup(experts[:, :, None] == expert_ids, pad_start) + ranks).reshape(n_asg).astype(jnp.int32)
    n_blocks = -(-(n_asg + N_EXPERTS * (MOE_BLK - 1)) // MOE_BLK)
    n_used = jnp.maximum(pad_end[-1] // MOE_BLK, 1).astype(jnp.int32)
    blk = jnp.minimum(jnp.arange(n_blocks, dtype=jnp.int32), n_used - 1)
    blk_row = blk * MOE_BLK
    blk_e = jnp.minimum(jnp.sum((pad_end[None, :] <= blk_row[:, None]).astype(jnp.int32), axis=-1), N_EXPERTS - 1)
    blk_hot = blk_e[:, None] == expert_ids
    blk_off = blk_row - lookup(blk_hot, pad_start)
    blk_first = (blk_off == 0).astype(jnp.int32)
    blk_valid = jnp.clip(lookup(blk_hot, counts) - blk_off, 0, MOE_BLK).astype(jnp.int32)
    used = counts > 0
    exp_list = jnp.argsort(jnp.logical_not(used), stable=True).astype(jnp.int32)
    blk_ord = lookup(blk_hot, jnp.cumsum(used.astype(jnp.int32)) - 1).astype(jnp.int32)
    n_exp = jnp.sum(used.astype(jnp.int32)).reshape(1)
    plan = (blk_e, blk_first, blk_valid, blk_ord, exp_list, n_used.reshape(1), n_exp)
    return dest, n_blocks * MOE_BLK, plan


def kernel(x_prompt, x_sample, cache_kv_w128, cache_kv_w512, cache_kv_w2048, state_conv, w_in, conv_w, w_conv_out, w_att_out, w_o, ln1_g, ln1_b, w_router_group, b_router_group, w_router_expert, b_router_expert, w_expert_gate, w_expert_up, w_expert_down, ln2_g, ln2_b):
    bsz, seq, _ = x_prompt.shape
    bd, t_new, _ = x_sample.shape
    m_p = bsz * seq
    m_s = bd * t_new
    m_all = m_p + m_s
    caches = (cache_kv_w128, cache_kv_w512, cache_kv_w2048)
    lyr = 0

    w_in16 = w_in[lyr].astype(BF16)
    wc16 = w_conv_out[lyr].astype(BF16)
    wa16 = w_att_out[lyr].astype(BF16)
    wo16 = w_o[lyr].astype(BF16)
    cw = conv_w[lyr]
    g1, b1 = ln1_g[lyr][None, :], ln1_b[lyr][None, :]
    g2, b2 = ln2_g[lyr][None, :], ln2_b[lyr][None, :]
    w_re = jnp.transpose(w_router_expert[lyr], (1, 0, 2)).reshape(D_MODEL, N_EXPERTS)
    w_r = jnp.pad(jnp.concatenate([w_router_group[lyr], w_re], axis=1),
                  ((0, 0), (0, LANES - N_EXPERT_GROUPS - N_EXPERTS)))
    w_r_hi = w_r.astype(BF16)
    w_r = jnp.concatenate([w_r_hi, (w_r - w_r_hi.astype(F32)).astype(BF16)], axis=1)
    b_r = jnp.pad(jnp.concatenate([b_router_group[lyr], b_router_expert[lyr].reshape(N_EXPERTS)]),
                  (0, LANES - N_EXPERT_GROUPS - N_EXPERTS))[None, :]

    xp = x_prompt.reshape(m_p, D_MODEL)
    xs = x_sample.reshape(m_s, D_MODEL)

    caches3 = [c[lyr].reshape(bd, -1, HEAD_DIM) for c in caches]
    z_p, qkv_p, shifted = _proj_in(xp, w_in16, min(PROJ_TM, m_p),
                                   [c.reshape(-1, HEAD_DIM) for c in caches3], t_new * KV_ROWS)
    shifted = [s.reshape(c.shape) for s, c in zip(shifted, caches3)]
    o_p = _attn_prompt(qkv_p, bsz, seq)
    h_buf, conv_p = _mix(None, o_p, z_p, xp, wc16, wa16, wo16, g1, b1, min(MIX_TM, seq), m_all, conv=(cw, seq))

    z_s, qkv_s, _ = _proj_in(xs, w_in16, m_s)
    p_s, conv_s = _conv_sample(z_s, cw, state_conv[lyr], t_new)
    q3 = qkv_s.reshape(bd, t_new, QKV_COLS)
    kv5 = qkv_s.reshape(bd, t_new, 3, N_DIL, HEADS_PER_GROUP, HEAD_DIM)
    new_kvs = [kv5[:, :, 1:3, g].reshape(bd, t_new * KV_ROWS, HEAD_DIM) for g in range(N_DIL)]
    bufs_s, o_s = _attn_sample(q3, caches3, new_kvs, shifted)
    bufs_s = [buf.reshape(1, bd, window, 2, HEADS_PER_GROUP, HEAD_DIM)
              for buf, (window, _) in zip(bufs_s, DILATED_GROUPS)]
    h_buf = _mix(p_s, o_s.reshape(m_s, D_GRP), z_s, xs, wc16, wa16, wo16, g1, b1, m_s, m_all, h_all=h_buf, row_off=m_p)

    h_all, h_slabs = h_buf
    idx, gates, counts = _router(h_all, w_r, b_r)
    dest, n_rows, plan = _moe_plan(idx, counts)
    x_slabs = _dispatch(h_slabs, dest, n_rows)
    y_slabs = _experts(x_slabs, plan, w_expert_gate[lyr], w_expert_up[lyr], w_expert_down[lyr])
    y_p = _combine(y_slabs, dest, h_all, gates, g2, b2, 0, m_p)
    y_s = _combine(y_slabs, dest, h_all, gates, g2, b2, m_p, m_s)

    qkv_p3 = qkv_p.reshape(bsz, seq, QKV_COLS)
    bufs_p = []
    for g, (w, _) in enumerate(DILATED_GROUPS):
        keep = min(w, seq)
        bufs_p.append(_kv_pack(qkv_p3, g, keep).reshape(1, bsz, keep, 2, HEADS_PER_GROUP, HEAD_DIM))
    return (y_p.reshape(bsz, seq, D_MODEL), y_s.reshape(bd, t_new, D_MODEL),
            bufs_p[0], bufs_p[1], bufs_p[2], conv_p[None],
            bufs_s[0], bufs_s[1], bufs_s[2], conv_s[None])
```

```python
import functools

import jax
import jax.numpy as jnp
from jax import lax
from jax.experimental import pallas as pl
from jax.experimental.pallas import tpu as pltpu

F32 = jnp.float32
BF16 = jnp.bfloat16

D_MODEL = 2048
D_CONV = D_MODEL // 2
CONV_W = 3
DILATED_GROUPS = ((128, 1), (512, 4), (2048, 16))
N_DIL = len(DILATED_GROUPS)
HEADS_PER_GROUP = 4
HEAD_DIM = 128
SPAN = 128
D_GRP = HEADS_PER_GROUP * HEAD_DIM
D_ATT = N_DIL * D_GRP
N_EXPERT_GROUPS = 8
EXPERTS_PER_GROUP = 8
N_EXPERTS = N_EXPERT_GROUPS * EXPERTS_PER_GROUP
TOP_K = 2
D_EXPERT = D_MODEL // 4
DEPTH = 1
LN_EPS = 1e-5
ALPHA = (2 * DEPTH) ** 0.25
NEG_INF = -1e30
ATT_SCALE = HEAD_DIM ** -0.5

OFF_B = 0
OFF_C = D_CONV
OFF_H = 2 * D_CONV
OFF_Q = 3 * D_CONV
OFF_K = OFF_Q + D_ATT
OFF_V = OFF_K + D_ATT
OFF_GC = OFF_V + D_ATT
OFF_GA = OFF_GC + D_MODEL
N_IN = OFF_GA + D_MODEL

LANES = 128
SUBLANES = 8
VMEM_LIMIT = 56 * 1024 * 1024

PROJ_TN = 512
PROJ_TM = 1024
Z_COLS = 2 * D_MODEL + 3 * D_CONV
QKV_COLS = 3 * D_ATT
NZ_TILES = Z_COLS // PROJ_TN
NQ_TILES = QKV_COLS // PROJ_TN
MIX_TM = 256
ROUTER_TILE = 640
ROW_TILE = 128
MOE_BLK = 304
W_SLOTS = 3
ATTN_UNROLL = 8
DISPATCH_TILE = 640
DISPATCH_SLOTS = 3
COMBINE_TILE = 512
KV_PACK_TM = 512
SHIFT_CHUNK_ROWS = 4128
SHIFT_AHEAD = 2
SHIFT_SLOTS = 2 * SHIFT_AHEAD
DMA_UNROLL = 8
DMA_PRIORITIES = 2


def _proj_tile_order():
    t = lambda off, width: list(range(off // PROJ_TN, (off + width) // PROJ_TN))
    z_tiles = t(OFF_GC, D_MODEL) + t(OFF_GA, D_MODEL) + t(OFF_B, D_CONV) + t(OFF_C, D_CONV) + t(OFF_H, D_CONV)
    q_tiles = t(OFF_Q, D_ATT) + t(OFF_K, D_ATT) + t(OFF_V, D_ATT)
    return z_tiles + q_tiles


def _params(*sem):
    return pltpu.CompilerParams(dimension_semantics=sem, vmem_limit_bytes=VMEM_LIMIT)


def _shift_chunks(rows, drop, max_rows):
    keep = rows - drop
    best = SUBLANES
    for c in range(SUBLANES, max_rows + 1, SUBLANES):
        if keep % c == 0:
            best = c
    return best, keep // best


def _proj_kernel(shift_plan, drop, order_ref, x_ref, w_ref, *rest):
    del order_ref
    n_shift = len(shift_plan)
    srcs = rest[:n_shift]
    z_ref, qkv_ref = rest[n_shift:n_shift + 2]
    dsts = rest[n_shift + 2:2 * n_shift + 2]
    x16 = rest[2 * n_shift + 2]
    j = pl.program_id(1)

    if n_shift:
        stage, sem_in, sem_out = rest[2 * n_shift + 3:]
        t = pl.program_id(0) * pl.num_programs(1) + j
        n_chunks = shift_plan[-1][2]

        def for_chunk(q, action):
            for n, (chunk, lo, hi) in enumerate(shift_plan):
                @pl.when(jnp.logical_and(q >= lo, q < hi))
                def _(n=n, chunk=chunk, lo=lo):
                    action(n, chunk, pl.multiple_of((q - lo) * chunk, SUBLANES), q & (SHIFT_SLOTS - 1))

        def load(n, chunk, start, slot):
            return pltpu.make_async_copy(srcs[n].at[pl.ds(start + drop, chunk), :],
                                         stage.at[slot, pl.ds(0, chunk), :], sem_in.at[slot])

        def store(n, chunk, start, slot):
            return pltpu.make_async_copy(stage.at[slot, pl.ds(0, chunk), :],
                                         dsts[n].at[pl.ds(start, chunk), :], sem_out.at[slot])

        @pl.when(t == 0)
        def _():
            for q in range(SHIFT_AHEAD):
                for_chunk(t + q, lambda *a: load(*a).start())

        @pl.when(t < n_chunks)
        def _():
            for_chunk(t, lambda *a: load(*a).wait())
            for_chunk(t, lambda *a: store(*a).start())

        @pl.when(jnp.logical_and(t >= SHIFT_AHEAD, t < n_chunks + SHIFT_AHEAD))
        def _():
            for_chunk(t - SHIFT_AHEAD, lambda *a: store(*a).wait())

        @pl.when(t + SHIFT_AHEAD < n_chunks)
        def _():
            for_chunk(t + SHIFT_AHEAD, lambda *a: load(*a).start())

    @pl.when(j == 0)
    def _():
        x16[...] = x_ref[...].astype(BF16)

    @pl.when(j < NZ_TILES)
    def _():
        z_ref[...] = jnp.dot(x16[...], w_ref[...], preferred_element_type=F32).astype(BF16)

    @pl.when(j >= NZ_TILES)
    def _():
        qkv_ref[...] = jnp.dot(x16[...], w_ref[...], preferred_element_type=F32)


def _proj_in(x, w16, tm, shift_srcs=(), drop=0):
    m = x.shape[0]
    n_shift = len(shift_srcs)
    n_steps = (m // tm) * (NZ_TILES + NQ_TILES)
    order = jnp.asarray(_proj_tile_order(), jnp.int32)
    hbm = pl.BlockSpec(memory_space=pl.ANY)
    shift_plan, first = [], 0
    for s in shift_srcs:
        chunk, count = _shift_chunks(s.shape[0], drop, SHIFT_CHUNK_ROWS)
        shift_plan.append((chunk, first, first + count))
        first += count
    assert first + SHIFT_AHEAD <= n_steps, "one shift chunk per grid step, plus the steps that drain"
    shift_scratch = [pltpu.VMEM((SHIFT_SLOTS, SHIFT_CHUNK_ROWS, LANES), F32),
                     pltpu.SemaphoreType.DMA((SHIFT_SLOTS,)), pltpu.SemaphoreType.DMA((SHIFT_SLOTS,))] if n_shift else []
    grid_spec = pltpu.PrefetchScalarGridSpec(
        num_scalar_prefetch=1,
        grid=(m // tm, NZ_TILES + NQ_TILES),
        in_specs=[
            pl.BlockSpec((tm, D_MODEL), lambda i, j, o: (i, 0)),
            pl.BlockSpec((D_MODEL, PROJ_TN), lambda i, j, o: (0, o[j])),
        ] + [hbm] * n_shift,
        out_specs=[
            pl.BlockSpec((tm, PROJ_TN), lambda i, j, o: (i, jnp.minimum(j, NZ_TILES - 1))),
            pl.BlockSpec((tm, PROJ_TN), lambda i, j, o: (i, jnp.maximum(j - NZ_TILES, 0))),
        ] + [hbm] * n_shift,
        scratch_shapes=[pltpu.VMEM((tm, D_MODEL), BF16)] + shift_scratch,
    )
    outs = pl.pallas_call(
        functools.partial(_proj_kernel, tuple(shift_plan), drop),
        grid_spec=grid_spec,
        out_shape=[jax.ShapeDtypeStruct((m, Z_COLS), BF16), jax.ShapeDtypeStruct((m, QKV_COLS), F32)]
        + [jax.ShapeDtypeStruct(s.shape, s.dtype) for s in shift_srcs],
        compiler_params=_params("arbitrary", "arbitrary"),
        name="proj_in",
    )(order, x, w16, *shift_srcs)
    return outs[0], outs[1], outs[2:]


ZB_BLK = (2 * D_MODEL) // D_CONV


def _gated_conv_tile(zb_ref, zc_ref, zh_ref, cw_ref, st_ref, uext, seq_start):
    tm = zb_ref.shape[0]

    @pl.when(seq_start)
    def _():
        uext[0:SUBLANES, :] = jnp.zeros((SUBLANES, D_CONV), F32)

    u = zc_ref[...].astype(F32) * zh_ref[...].astype(F32)
    uext[SUBLANES:SUBLANES + tm, :] = u
    conv = (cw_ref[0:1, :] * uext[SUBLANES - 2:SUBLANES - 2 + tm, :]
            + cw_ref[1:2, :] * uext[SUBLANES - 1:SUBLANES - 1 + tm, :]
            + cw_ref[2:3, :] * u)
    last = uext[tm:tm + SUBLANES, :]
    uext[0:SUBLANES, :] = last
    st_ref[0] = last[SUBLANES - 2:SUBLANES, :]
    return (zb_ref[...].astype(F32) * conv).astype(BF16)


def _conv_sample_kernel(t_new, zb_ref, zc_ref, zh_ref, cw_ref, s1_ref, s2_ref, p_ref, u_ref, uext):
    m = zb_ref.shape[0]
    u = zc_ref[...].astype(F32) * zh_ref[...].astype(F32)
    u_ref[...] = u
    uext[0:SUBLANES, :] = jnp.zeros((SUBLANES, D_CONV), F32)
    uext[SUBLANES:SUBLANES + m, :] = u
    t = lax.broadcasted_iota(jnp.int32, (m, D_CONV), 0) % t_new
    um1 = jnp.where(t < 1, s1_ref[...], uext[SUBLANES - 1:SUBLANES - 1 + m, :])
    um2 = jnp.where(t < 2, s2_ref[...], uext[SUBLANES - 2:SUBLANES - 2 + m, :])
    conv = cw_ref[0:1, :] * um2 + cw_ref[1:2, :] * um1 + cw_ref[2:3, :] * u
    p_ref[...] = (zb_ref[...].astype(F32) * conv).astype(BF16)


def _conv_sample(z, conv_w, state, t_new):
    m = z.shape[0]
    bd = m // t_new
    s1 = jnp.pad(state[:, 1:2], ((0, 0), (0, t_new - 1), (0, 0))).reshape(m, D_CONV)
    s2 = jnp.pad(state, ((0, 0), (0, t_new - 2), (0, 0))).reshape(m, D_CONV)
    zspec = lambda blk: pl.BlockSpec((m, D_CONV), lambda i: (0, blk))
    full = lambda r: pl.BlockSpec((r, D_CONV), lambda i: (0, 0))
    p, u = pl.pallas_call(
        functools.partial(_conv_sample_kernel, t_new),
        grid=(1,),
        in_specs=[zspec(ZB_BLK), zspec(ZB_BLK + 1), zspec(ZB_BLK + 2), full(CONV_W), full(m), full(m)],
        out_specs=[full(m), full(m)],
        out_shape=[jax.ShapeDtypeStruct((m, D_CONV), BF16), jax.ShapeDtypeStruct((m, D_CONV), F32)],
        scratch_shapes=[pltpu.VMEM((m + SUBLANES, D_CONV), F32)],
        compiler_params=_params("arbitrary"),
        name="conv_sample",
    )(z, z, z, conv_w, s1, s2)
    new_state = u.reshape(bd, t_new, D_CONV)[:, t_new - (CONV_W - 1):]
    return p, new_state


def _merge_groups(o_list, lse_list):
    top = functools.reduce(jnp.maximum, lse_list)
    es = [jnp.exp(l - top) for l in lse_list]
    num = functools.reduce(lambda a, b: a + b, [e * o for e, o in zip(es, o_list)])
    den = functools.reduce(lambda a, b: a + b, es)
    return num / den


def _nt_dot(a, b):
    return lax.dot_general(a, b, (((1,), (1,)), ((), ())), preferred_element_type=F32)


def _attn_prompt_group(q_ref, k_ref, v_ref, o_sc, lse_sc, g, dil, seq):
    shift = dil.bit_length() - 1
    qi = lax.broadcasted_iota(jnp.int32, (SPAN, 2 * SPAN), 0)
    kj = lax.broadcasted_iota(jnp.int32, (SPAN, 2 * SPAN), 1)
    band = jnp.logical_and(kj >= qi, kj <= qi + SPAN)
    is_prev = lax.broadcasted_iota(jnp.int32, (1, 2 * SPAN), 1) < SPAN
    ones = jnp.ones((2 * SPAN, HEAD_DIM), BF16)

    def rows(ref, start):
        if dil == 1:
            return ref[pl.ds(start, SPAN), :]
        return ref[pl.ds(start, SPAN, stride=dil), :]

    def body(trip, carry):
        blocks = range(ATTN_UNROLL)
        idx = [trip * ATTN_UNROLL + u for u in blocks]
        c = [i >> shift for i in idx]
        base = [c[u] * (SPAN * dil) + (idx[u] & (dil - 1)) for u in blocks]
        pbase = [jnp.maximum(c[u] - 1, 0) * (SPAN * dil) + (idx[u] & (dil - 1)) for u in blocks]
        q = [rows(q_ref, base[u]).astype(BF16) for u in blocks]
        k = [jnp.concatenate([rows(k_ref, pbase[u]), rows(k_ref, base[u])], axis=0).astype(BF16) for u in blocks]
        s = [_nt_dot(q[u], k[u]) for u in blocks]
        pen = [jnp.where(jnp.logical_and(is_prev, c[u] == 0), NEG_INF, 0.0) for u in blocks]
        s = [jnp.where(band, s[u] * ATT_SCALE + pen[u], NEG_INF) for u in blocks]
        m = [jnp.max(jnp.maximum(s[u][:, :SPAN], s[u][:, SPAN:]), axis=-1, keepdims=True) for u in blocks]
        p = [jnp.exp(s[u] - m[u]).astype(BF16) for u in blocks]
        v = [jnp.concatenate(
            [jnp.concatenate([rows(v_ref, pbase[u]), rows(v_ref, base[u])], axis=0).astype(BF16), ones], axis=1)
            for u in blocks]
        acc = [jnp.dot(p[u], v[u], preferred_element_type=F32) for u in blocks]
        for u in blocks:
            den = acc[u][:, HEAD_DIM:]
            o = acc[u][:, :HEAD_DIM] / den
            lse = m[u] + jnp.log(den)
            if dil == 1:
                o_sc[g, pl.ds(base[u], SPAN), :] = o
                lse_sc[g, pl.ds(base[u], SPAN), :] = lse
            else:
                o_sc[g, pl.ds(base[u], SPAN, stride=dil), :] = o
                lse_sc[g, pl.ds(base[u], SPAN, stride=dil), :] = lse
        return carry

    lax.fori_loop(0, seq // (SPAN * ATTN_UNROLL), body, 0)


def _attn_prompt_kernel(q_ref, k_ref, v_ref, o_ref, o_sc, lse_sc):
    seq = q_ref.shape[0]
    g_id = pl.program_id(2)
    for g, (_, dil) in enumerate(DILATED_GROUPS):
        @pl.when(g_id == g)
        def _(g=g, dil=dil):
            _attn_prompt_group(q_ref, k_ref, v_ref, o_sc, lse_sc, g, dil, seq)

    @pl.when(g_id == N_DIL - 1)
    def _():
        chunk = 256

        def body(i, carry):
            sl = pl.ds(pl.multiple_of(i * chunk, chunk), chunk)
            o = _merge_groups([o_sc[g, sl, :] for g in range(N_DIL)], [lse_sc[g, sl, :] for g in range(N_DIL)])
            o_ref[sl, :] = o.astype(BF16)
            return carry

        lax.fori_loop(0, seq // chunk, body, 0)


def _attn_prompt(qkv, bsz, seq):
    qkv3 = qkv.reshape(bsz, seq, QKV_COLS)
    per_sel = D_ATT // HEAD_DIM
    spec = lambda sel: pl.BlockSpec((None, seq, HEAD_DIM),
                                    lambda b, j, g: (b, 0, sel * per_sel + g * HEADS_PER_GROUP + j))
    out = pl.pallas_call(
        _attn_prompt_kernel,
        grid=(bsz, HEADS_PER_GROUP, N_DIL),
        in_specs=[spec(0), spec(1), spec(2)],
        out_specs=pl.BlockSpec((None, seq, HEAD_DIM), lambda b, j, g: (b, 0, j)),
        out_shape=jax.ShapeDtypeStruct((bsz, seq, D_GRP), BF16),
        scratch_shapes=[pltpu.VMEM((N_DIL, seq, HEAD_DIM), F32), pltpu.VMEM((N_DIL, seq, HEAD_DIM), F32)],
        compiler_params=_params("arbitrary", "arbitrary", "arbitrary"),
        name="attn_prompt",
    )(qkv3, qkv3, qkv3)
    return out.reshape(bsz * seq, D_GRP)


KV_ROWS = 2 * HEADS_PER_GROUP


def _kv_pack_kernel(k_ref, v_ref, out_ref):
    tm = k_ref.shape[0]
    for h in range(HEADS_PER_GROUP):
        cols = slice(h * HEAD_DIM, (h + 1) * HEAD_DIM)
        out_ref[pl.ds(h, tm, stride=KV_ROWS), :] = k_ref[:, cols]
        out_ref[pl.ds(HEADS_PER_GROUP + h, tm, stride=KV_ROWS), :] = v_ref[:, cols]


def _kv_pack(qkv3, g, keep):
    bsz, seq, _ = qkv3.shape
    tm = min(KV_PACK_TM, keep)
    assert keep % tm == 0 and (seq - keep) % tm == 0
    first = (seq - keep) // tm
    col = lambda sel: (sel * D_ATT + g * D_GRP) // D_GRP
    spec = lambda sel: pl.BlockSpec((None, tm, D_GRP), lambda b, s: (b, first + s, col(sel)))
    return pl.pallas_call(
        _kv_pack_kernel,
        grid=(bsz, keep // tm),
        in_specs=[spec(1), spec(2)],
        out_specs=pl.BlockSpec((None, tm * KV_ROWS, HEAD_DIM), lambda b, s: (b, s, 0)),
        out_shape=jax.ShapeDtypeStruct((bsz, keep * KV_ROWS, HEAD_DIM), F32),
        compiler_params=_params("arbitrary", "arbitrary"),
        name="kv_pack",
    )(qkv3, qkv3)


def _sample_key_bias(dil, t_new, n_rows, n_buf):
    shape = (HEADS_PER_GROUP * t_new, n_rows)
    rr = lax.broadcasted_iota(jnp.int32, shape, 0)
    cc = lax.broadcasted_iota(jnp.int32, shape, 1)
    head = rr // t_new
    back = n_buf + (rr - head * t_new) - cc // KV_ROWS
    ok = jnp.logical_and(cc % KV_ROWS == HEADS_PER_GROUP + head,
                         jnp.logical_and(jnp.logical_and(back >= 0, back <= dil * SPAN), back % dil == 0))
    return jnp.where(ok, 0.0, NEG_INF)


def _attn_sample_group(g, q_ref, cache_ref, new_ref, bias_c, bias_n, o_sc, lse_sc):
    t_new = q_ref.shape[0]
    q = jnp.concatenate([q_ref[:, g * D_GRP + h * HEAD_DIM:g * D_GRP + (h + 1) * HEAD_DIM]
                         for h in range(HEADS_PER_GROUP)], axis=0).astype(BF16)

    def keys_values(ref):
        rows = ref[...]
        tiles = rows.reshape(-1, KV_ROWS, HEAD_DIM)
        return pltpu.roll(tiles, HEADS_PER_GROUP, 1).reshape(rows.shape).astype(BF16), rows.astype(BF16)

    k_c, v_c = keys_values(cache_ref)
    k_n, v_n = keys_values(new_ref)
    s_c = _nt_dot(q, k_c) * ATT_SCALE + bias_c[...]
    s_n = _nt_dot(q, k_n) * ATT_SCALE + bias_n[...]
    m = jnp.maximum(jnp.max(s_c, axis=-1, keepdims=True), jnp.max(s_n, axis=-1, keepdims=True))
    p_c = jnp.exp(s_c - m)
    p_n = jnp.exp(s_n - m)
    den = jnp.sum(p_c, axis=-1, keepdims=True) + jnp.sum(p_n, axis=-1, keepdims=True)
    acc = (jnp.dot(p_c.astype(BF16), v_c, preferred_element_type=F32)
           + jnp.dot(p_n.astype(BF16), v_n, preferred_element_type=F32))
    o = acc / den
    lse = jnp.broadcast_to(m + jnp.log(den), o.shape)
    for h in range(HEADS_PER_GROUP):
        cols = slice(h * HEAD_DIM, (h + 1) * HEAD_DIM)
        o_sc[g, :, cols] = o[h * t_new:(h + 1) * t_new, :]
        lse_sc[g, :, cols] = lse[h * t_new:(h + 1) * t_new, :]


def _attn_sample_kernel(q_ref, *refs):
    caches, news = refs[:N_DIL], refs[N_DIL:2 * N_DIL]
    bufs = refs[3 * N_DIL:4 * N_DIL]
    o_ref = refs[4 * N_DIL]
    bias_cs = refs[4 * N_DIL + 1:5 * N_DIL + 1]
    bias_n, o_sc, lse_sc = refs[5 * N_DIL + 1:]
    t_new = q_ref.shape[0]

    @pl.when(pl.program_id(0) == 0)
    def _():
        for g, (_, dil) in enumerate(DILATED_GROUPS):
            n_rows = caches[g].shape[0]
            n_buf = n_rows // KV_ROWS
            bias_cs[g][...] = _sample_key_bias(dil, t_new, n_rows, n_buf)
            bias_n[g] = _sample_key_bias(dil, t_new, t_new * KV_ROWS, 0)

    for g in range(N_DIL):
        _attn_sample_group(g, q_ref, caches[g], news[g], bias_cs[g], bias_n.at[g], o_sc, lse_sc)
        bufs[g][...] = news[g][...]
    o_ref[...] = _merge_groups([o_sc[g] for g in range(N_DIL)], [lse_sc[g] for g in range(N_DIL)]).astype(BF16)


def _attn_sample(q3, caches, new_kvs, shifted):
    bd, t_new, _ = q3.shape
    new_rows = t_new * KV_ROWS
    row_spec = lambda r: pl.BlockSpec((None, r, HEAD_DIM), lambda b: (b, 0, 0))
    for (window, _), cache in zip(DILATED_GROUPS, caches):
        assert cache.shape[1] == window * KV_ROWS, "the cache must hold a full window"
        assert cache.shape[1] % new_rows == 0
    hbm = pl.BlockSpec(memory_space=pl.ANY)
    tail = lambda c: pl.BlockSpec((None, new_rows, HEAD_DIM), lambda b: (b, c.shape[1] // new_rows - 1, 0))
    outs = pl.pallas_call(
        _attn_sample_kernel,
        grid=(bd,),
        in_specs=[pl.BlockSpec((None, t_new, D_ATT), lambda b: (b, 0, 0))]
        + [row_spec(c.shape[1]) for c in caches] + [row_spec(new_rows)] * N_DIL + [hbm] * N_DIL,
        out_specs=[tail(c) for c in caches] + [pl.BlockSpec((None, t_new, D_GRP), lambda b: (b, 0, 0))],
        out_shape=[jax.ShapeDtypeStruct(c.shape, F32) for c in caches]
        + [jax.ShapeDtypeStruct((bd, t_new, D_GRP), BF16)],
        scratch_shapes=[pltpu.VMEM((HEADS_PER_GROUP * t_new, c.shape[1]), F32) for c in caches]
        + [pltpu.VMEM((N_DIL, HEADS_PER_GROUP * t_new, new_rows), F32),
           pltpu.VMEM((N_DIL, t_new, D_GRP), F32), pltpu.VMEM((N_DIL, t_new, D_GRP), F32)],
        input_output_aliases={1 + 2 * N_DIL + g: g for g in range(N_DIL)},
        compiler_params=_params("arbitrary"),
        name="attn_sample",
    )(q3, *caches, *new_kvs, *shifted)
    return outs[:N_DIL], outs[N_DIL]


def _layer_norm(v, g, b):
    mu = jnp.mean(v, axis=-1, keepdims=True)
    d = v - mu
    var = jnp.mean(d * d, axis=-1, keepdims=True)
    return d * lax.rsqrt(var + LN_EPS) * g + b


SLAB_ROWS = SUBLANES
HALF = D_MODEL // 2
HI_MASK = 0xFFFF0000


def _bf16_bits(v):
    return pltpu.bitcast(v.astype(BF16).astype(F32), jnp.uint32)


def _pack_slabs(src_ref, slab_ref, n_tok):
    def body(tg, carry):
        t0 = pl.multiple_of(tg * SUBLANES, SUBLANES)
        for s in range(SLAB_ROWS):
            lo = src_ref[pl.ds(t0, SUBLANES), s * LANES:(s + 1) * LANES]
            hi = src_ref[pl.ds(t0, SUBLANES), HALF + s * LANES:HALF + (s + 1) * LANES]
            word = (_bf16_bits(lo) >> 16) | _bf16_bits(hi)
            slab_ref[pl.ds(t0 * SLAB_ROWS + s, SUBLANES, stride=SLAB_ROWS), :] = word
        return carry
    lax.fori_loop(0, n_tok // SUBLANES, body, 0)


def _unpack_gated_sum(slab_refs, gate_ref, dst_ref, n_tok):
    def body(tg, carry):
        t0 = pl.multiple_of(tg * SUBLANES, SUBLANES)
        gates = [gate_ref[pl.ds(t0, SUBLANES), k:k + 1] for k in range(len(slab_refs))]
        for s in range(SLAB_ROWS):
            words = [ref[pl.ds(t0 * SLAB_ROWS + s, SUBLANES, stride=SLAB_ROWS), :] for ref in slab_refs]
            lo = [g * pltpu.bitcast(w << 16, F32) for g, w in zip(gates, words)]
            hi = [g * pltpu.bitcast(w & jnp.uint32(HI_MASK), F32) for g, w in zip(gates, words)]
            dst_ref[pl.ds(t0, SUBLANES), s * LANES:(s + 1) * LANES] = functools.reduce(lambda a, b: a + b, lo)
            dst_ref[pl.ds(t0, SUBLANES), HALF + s * LANES:HALF + (s + 1) * LANES] = functools.reduce(
                lambda a, b: a + b, hi)
        return carry
    lax.fori_loop(0, n_tok // SUBLANES, body, 0, unroll=4)


def _unpack_slabs_bf16(slab_ref, dst_ref, n_tok, n_valid):
    pair = 2 * SUBLANES
    sub = lax.broadcasted_iota(jnp.int32, (SUBLANES, LANES), 0)

    def body(tg, carry):
        t0 = pl.multiple_of(tg * pair, pair)
        for s in range(SLAB_ROWS):
            words = []
            for half in range(2):
                t1 = t0 + half * SUBLANES
                w = slab_ref[pl.ds(t1 * SLAB_ROWS + s, SUBLANES, stride=SLAB_ROWS), :]
                words.append(jnp.where(t1 + sub < n_valid, w, jnp.uint32(0)))
            lo = jnp.concatenate([pltpu.bitcast(w << 16, F32) for w in words], axis=0)
            hi = jnp.concatenate([pltpu.bitcast(w & jnp.uint32(HI_MASK), F32) for w in words], axis=0)
            dst_ref[pl.ds(t0, pair), s * LANES:(s + 1) * LANES] = lo.astype(BF16)
            dst_ref[pl.ds(t0, pair), HALF + s * LANES:HALF + (s + 1) * LANES] = hi.astype(BF16)
        return carry
    lax.fori_loop(0, n_tok // pair, body, 0)


def _mix_kernel(conv_steps, *refs):
    if conv_steps:
        conv_in, refs = refs[:4], refs[4:]
    else:
        p_ref, refs = refs[0], refs[1:]
    o_ref, gc_ref, ga_ref, x_ref, wc_ref, wa_ref, wo_ref, g_ref, b_ref = refs[:9]
    if conv_steps:
        h_ref, hs_ref, st_ref, uext = refs[-4:]
        p = _gated_conv_tile(*conv_in, st_ref, uext, pl.program_id(0) % conv_steps == 0)
    else:
        h_ref, hs_ref = refs[-2:]
        p = p_ref[...]
    y_conv = jnp.dot(p, wc_ref[...], preferred_element_type=F32)
    y_att = jnp.dot(o_ref[...], wa_ref[...], preferred_element_type=F32)
    merged = (jax.nn.sigmoid(gc_ref[...].astype(F32)) * y_conv
              + jax.nn.sigmoid(ga_ref[...].astype(F32)) * y_att)
    mix = jnp.dot(merged.astype(BF16), wo_ref[...], preferred_element_type=F32)
    h_ref[...] = _layer_norm(ALPHA * x_ref[...] + mix, g_ref[...], b_ref[...])
    _pack_slabs(h_ref, hs_ref, h_ref.shape[0])


def _mix(p, o, z, x, wc16, wa16, wo16, ln_g, ln_b, tm, total, h_all=None, row_off=0, conv=None):
    m = x.shape[0]
    assert row_off % tm == 0
    blk_off = row_off // tm
    row = lambda w: pl.BlockSpec((tm, w), lambda i: (i, 0))
    const = lambda a: pl.BlockSpec(a.shape, lambda i: (0, 0), pipeline_mode=pl.Buffered(1))
    out_specs = [pl.BlockSpec((tm, D_MODEL), lambda i: (i + blk_off, 0)),
                 pl.BlockSpec((tm * SLAB_ROWS, LANES), lambda i: (i + blk_off, 0))]
    out_shape = [jax.ShapeDtypeStruct((total, D_MODEL), F32),
                 jax.ShapeDtypeStruct((total * SLAB_ROWS, LANES), jnp.uint32)]
    scratch, conv_steps = [], 0
    if conv is None:
        in_specs, args = [row(D_CONV)], [p]
    else:
        conv_w, seq = conv
        assert seq % tm == 0
        conv_steps = seq // tm
        zspec = lambda blk: pl.BlockSpec((tm, D_CONV), lambda i: (i, blk))
        in_specs, args = [zspec(ZB_BLK), zspec(ZB_BLK + 1), zspec(ZB_BLK + 2), const(conv_w)], [z, z, z, conv_w]
        out_specs.append(pl.BlockSpec((1, CONV_W - 1, D_CONV), lambda i: (i // conv_steps, 0, 0)))
        out_shape.append(jax.ShapeDtypeStruct((m // seq, CONV_W - 1, D_CONV), F32))
        scratch = [pltpu.VMEM((tm + SUBLANES, D_CONV), F32)]
    in_specs += [row(D_GRP),
                 pl.BlockSpec((tm, D_MODEL), lambda i: (i, 0)), pl.BlockSpec((tm, D_MODEL), lambda i: (i, 1)),
                 row(D_MODEL), const(wc16), const(wa16), const(wo16), const(ln_g), const(ln_b)]
    args += [o, z, z, x, wc16, wa16, wo16, ln_g, ln_b]
    aliases = {}
    if h_all is not None:
        in_specs += [pl.BlockSpec(memory_space=pl.ANY)] * 2
        args += list(h_all)
        aliases = {len(args) - 2: 0, len(args) - 1: 1}
    outs = pl.pallas_call(
        functools.partial(_mix_kernel, conv_steps),
        grid=(m // tm,),
        in_specs=in_specs,
        out_specs=out_specs,
        out_shape=out_shape,
        scratch_shapes=scratch,
        input_output_aliases=aliases,
        compiler_params=_params("arbitrary"),
        name="mix_ln1",
    )(*args)
    return (tuple(outs[:2]), outs[2]) if conv is not None else tuple(outs)


def _route(h, w_ref, b_ref, carry):
    tile = h.shape[0]
    h_hi = h.astype(BF16)
    h_lo = (h - h_hi.astype(F32)).astype(BF16)
    w_both = w_ref[...]
    by_hi = jnp.dot(h_hi, w_both, preferred_element_type=F32)
    by_lo = jnp.dot(h_lo, w_both[:, :LANES], preferred_element_type=F32)
    logits = by_hi[:, :LANES] + (by_hi[:, LANES:] + by_lo) + b_ref[...]
    lane = lax.broadcasted_iota(jnp.int32, (tile, LANES), 1)
    lane_f = lane.astype(F32)
    first_max = lambda v, mx: jnp.min(jnp.where(v == mx, lane_f, float(LANES)), axis=-1, keepdims=True)

    is_grp = lane < N_EXPERT_GROUPS
    lg = jnp.where(is_grp, logits, -jnp.inf)
    mg = jnp.max(lg, axis=-1, keepdims=True)
    gsel = first_max(lg, mg).astype(jnp.int32)
    p_group = 1.0 / jnp.sum(jnp.where(is_grp, jnp.exp(logits - mg), 0.0), axis=-1, keepdims=True)

    e_lane = lane - N_EXPERT_GROUPS
    in_grp = jnp.logical_and(e_lane >= 0, (e_lane >> 3) == gsel)
    le = jnp.where(in_grp, logits, -jnp.inf)
    v1 = jnp.max(le, axis=-1, keepdims=True)
    i1 = first_max(le, v1).astype(jnp.int32)
    le2 = jnp.where(lane == i1, -jnp.inf, le)
    v2 = jnp.max(le2, axis=-1, keepdims=True)
    i2 = first_max(le2, v2).astype(jnp.int32)
    ex = jnp.exp(v2 - v1)
    g1 = p_group / (1.0 + ex)
    g2 = p_group * ex / (1.0 + ex)
    e1 = i1 - N_EXPERT_GROUPS
    e2 = i2 - N_EXPERT_GROUPS

    hot1 = lane == e1
    hot2 = lane == e2
    hot = jnp.where(jnp.logical_or(hot1, hot2), 1.0, 0.0)
    rr = lax.broadcasted_iota(jnp.int32, (tile, tile), 0)
    cc = lax.broadcasted_iota(jnp.int32, (tile, tile), 1)
    tri = jnp.where(cc < rr, 1.0, 0.0).astype(BF16)
    before = jnp.dot(tri, hot.astype(BF16), preferred_element_type=F32) + carry[0:1, :]
    r1 = jnp.sum(jnp.where(hot1, before, 0.0), axis=-1, keepdims=True).astype(jnp.int32)
    r2 = jnp.sum(jnp.where(hot2, before, 0.0), axis=-1, keepdims=True).astype(jnp.int32)
    carry[...] = carry[...] + jnp.sum(hot, axis=0, keepdims=True)

    idx = jnp.where(lane == 0, e1, jnp.where(lane == 1, e2, jnp.where(lane == 2, r1, r2)))
    return idx, jnp.where(lane == 0, g1, g2)


def _router_kernel(h_ref, w_ref, b_ref, idx_ref, gate_ref, cnt_ref, carry):
    @pl.when(pl.program_id(0) == 0)
    def _():
        carry[...] = jnp.zeros(carry.shape, F32)

    idx_ref[...], gate_ref[...] = _route(h_ref[...], w_ref, b_ref, carry)
    cnt_ref[...] = carry[...]


def _router(h_all, w_r, b_r):
    m = h_all.shape[0]
    tile = ROUTER_TILE if m % ROUTER_TILE == 0 else ROW_TILE
    row = pl.BlockSpec((tile, LANES), lambda i: (i, 0))
    return pl.pallas_call(
        _router_kernel,
        grid=(m // tile,),
        in_specs=[pl.BlockSpec((tile, D_MODEL), lambda i: (i, 0)),
                  pl.BlockSpec((D_MODEL, 2 * LANES), lambda i: (0, 0)),
                  pl.BlockSpec((1, LANES), lambda i: (0, 0))],
        out_specs=[row, row, pl.BlockSpec((SUBLANES, LANES), lambda i: (0, 0))],
        out_shape=[jax.ShapeDtypeStruct((m, LANES), jnp.int32), jax.ShapeDtypeStruct((m, LANES), F32),
                   jax.ShapeDtypeStruct((SUBLANES, LANES), F32)],
        scratch_shapes=[pltpu.VMEM((SUBLANES, LANES), F32)],
        compiler_params=_params("arbitrary"),
        name="router",
    )(h_all, w_r, b_r)


def _slab(ref, row):
    return ref.at[pl.ds(pl.multiple_of(row * SLAB_ROWS, SLAB_ROWS), SLAB_ROWS), :]


def _dispatch_kernel(tile, dest_ref, h_hbm, x_hbm, stage, sem_in, sem_rows):
    i = pl.program_id(0)
    n_steps = pl.num_programs(0)
    slot_of = lambda step: lax.rem(step, DISPATCH_SLOTS)

    def load(step):
        rows = pl.ds(pl.multiple_of(step * (tile * SLAB_ROWS), SLAB_ROWS), tile * SLAB_ROWS)
        return pltpu.make_async_copy(h_hbm.at[rows, :], stage.at[slot_of(step)], sem_in.at[slot_of(step)])

    def each_row(step, action):
        slot = slot_of(step)

        def body(r, c):
            for k in range(TOP_K):
                action(pltpu.make_async_copy(_slab(stage.at[slot], r),
                                             _slab(x_hbm, dest_ref[(step * tile + r) * TOP_K + k]),
                                             sem_rows.at[slot]), k)
            return c
        lax.fori_loop(0, tile, body, 0, unroll=DMA_UNROLL)

    @pl.when(i == 0)
    def _():
        load(i).start()

    @pl.when(i + 1 < n_steps)
    def _():
        load(i + 1).start()

    load(i).wait()
    each_row(i, lambda c, k: c.start(priority=k % DMA_PRIORITIES))

    @pl.when(i > 0)
    def _():
        each_row(i - 1, lambda c, k: c.wait())

    @pl.when(i == n_steps - 1)
    def _():
        each_row(i, lambda c, k: c.wait())


def _dispatch(h_slabs, dest_flat, n_rows):
    m = h_slabs.shape[0] // SLAB_ROWS
    tile = DISPATCH_TILE if m % DISPATCH_TILE == 0 else ROW_TILE
    grid_spec = pltpu.PrefetchScalarGridSpec(
        num_scalar_prefetch=1,
        grid=(m // tile,),
        in_specs=[pl.BlockSpec(memory_space=pl.ANY)],
        out_specs=pl.BlockSpec(memory_space=pl.ANY),
        scratch_shapes=[pltpu.VMEM((DISPATCH_SLOTS, tile * SLAB_ROWS, LANES), jnp.uint32),
                        pltpu.SemaphoreType.DMA((DISPATCH_SLOTS,)), pltpu.SemaphoreType.DMA((DISPATCH_SLOTS,))],
    )
    return pl.pallas_call(
        functools.partial(_dispatch_kernel, tile),
        grid_spec=grid_spec,
        out_shape=jax.ShapeDtypeStruct((n_rows * SLAB_ROWS, LANES), jnp.uint32),
        compiler_params=pltpu.CompilerParams(dimension_semantics=("arbitrary",), has_side_effects=True),
        name="moe_dispatch",
    )(dest_flat, h_slabs)


def _expert_kernel(blk_e, blk_first, blk_valid, blk_ord, exp_list, n_used, n_exp,
                   x_ref, wg_hbm, wu_hbm, wd_hbm, y_ref, wg32, wu32, wd32, wg16, wu16, wd16, sems, rows32, x16):
    i = pl.program_id(0)

    def weight_copies(e, slot):
        return [pltpu.make_async_copy(hbm.at[e], buf.at[slot], sems.at[slot, n])
                for n, (hbm, buf) in enumerate(((wg_hbm, wg32), (wu_hbm, wu32), (wd_hbm, wd32)))]

    def start_weights(ordinal, slot):
        for c in weight_copies(exp_list[ordinal], slot):
            c.start()

    @pl.when(i == 0)
    def _():
        for ahead in range(W_SLOTS):
            @pl.when(ahead < n_exp[0])
            def _(ahead=ahead):
                start_weights(ahead, ahead)

    @pl.when(i < n_used[0])
    def _():
        @pl.when(blk_first[i] == 1)
        def _():
            ordinal = blk_ord[i]
            slot = lax.rem(ordinal, W_SLOTS)
            for c in weight_copies(blk_e[i], slot):
                c.wait()
            wg16[...] = wg32[slot].astype(BF16)
            wu16[...] = wu32[slot].astype(BF16)
            wd16[...] = wd32[slot].astype(BF16)

            @pl.when(ordinal + W_SLOTS < n_exp[0])
            def _():
                start_weights(ordinal + W_SLOTS, slot)

        _unpack_slabs_bf16(x_ref, x16, MOE_BLK, blk_valid[i])
        x = x16[...]
        gate = jnp.dot(x, wg16[...], preferred_element_type=F32)
        up = jnp.dot(x, wu16[...], preferred_element_type=F32)
        hid = (jax.nn.silu(gate) * up).astype(BF16)
        rows32[...] = jnp.dot(hid, wd16[...], preferred_element_type=F32)
        _pack_slabs(rows32, y_ref, MOE_BLK)


def _experts(x_slabs, plan, w_gate, w_up, w_down):
    n_rows = x_slabs.shape[0] // SLAB_ROWS
    n_blocks = n_rows // MOE_BLK
    row_map = lambda i, be, bf, bv, bo, el, nu, ne: (jnp.minimum(i, nu[0] - 1), 0)
    hbm = pl.BlockSpec(memory_space=pl.ANY)
    grid_spec = pltpu.PrefetchScalarGridSpec(
        num_scalar_prefetch=7,
        grid=(n_blocks,),
        in_specs=[pl.BlockSpec((MOE_BLK * SLAB_ROWS, LANES), row_map), hbm, hbm, hbm],
        out_specs=pl.BlockSpec((MOE_BLK * SLAB_ROWS, LANES), row_map),
        scratch_shapes=[pltpu.VMEM((W_SLOTS, D_MODEL, D_EXPERT), F32), pltpu.VMEM((W_SLOTS, D_MODEL, D_EXPERT), F32),
                        pltpu.VMEM((W_SLOTS, D_EXPERT, D_MODEL), F32),
                        pltpu.VMEM((D_MODEL, D_EXPERT), BF16), pltpu.VMEM((D_MODEL, D_EXPERT), BF16),
                        pltpu.VMEM((D_EXPERT, D_MODEL), BF16),
                        pltpu.SemaphoreType.DMA((W_SLOTS, 3)),
                        pltpu.VMEM((MOE_BLK, D_MODEL), F32), pltpu.VMEM((MOE_BLK, D_MODEL), BF16)],
    )
    return pl.pallas_call(
        _expert_kernel,
        grid_spec=grid_spec,
        out_shape=jax.ShapeDtypeStruct((n_rows * SLAB_ROWS, LANES), jnp.uint32),
        compiler_params=_params("arbitrary"),
        name="moe_experts",
    )(*plan, x_slabs, w_gate, w_up, w_down)


def _combine_kernel(tok_off, tile, dest_ref, y_hbm, h_ref, gate_ref, g_ref, b_ref, out_ref, ybuf, ffn_sum, sems):
    i = pl.program_id(0)
    n_steps = pl.num_programs(0)

    def copy(step, r, k):
        slot = step & 1
        row = dest_ref[(tok_off + step * tile + r) * TOP_K + k]
        return pltpu.make_async_copy(_slab(y_hbm, row), _slab(ybuf.at[slot, k], r), sems.at[slot])

    def gather(step):
        def body(r, c):
            for k in range(TOP_K):
                copy(step, r, k).start(priority=k % DMA_PRIORITIES)
            return c
        lax.fori_loop(0, tile, body, 0, unroll=DMA_UNROLL)

    @pl.when(i == 0)
    def _():
        gather(i)

    @pl.when(i + 1 < n_steps)
    def _():
        gather(i + 1)

    def wait(r, c):
        for k in range(TOP_K):
            copy(i, r, k).wait()
        return c

    lax.fori_loop(0, tile, wait, 0, unroll=DMA_UNROLL)
    slot = i & 1
    _unpack_gated_sum([ybuf.at[slot, k] for k in range(TOP_K)], gate_ref, ffn_sum, tile)
    out_ref[...] = _layer_norm(ALPHA * h_ref[...] + ffn_sum[...], g_ref[...], b_ref[...])


def _combine(y_slabs, dest_flat, h_all, gates, ln_g, ln_b, tok_off, n_tok):
    tile = COMBINE_TILE if (n_tok % COMBINE_TILE == 0 and tok_off % COMBINE_TILE == 0) else ROW_TILE
    blk_off = tok_off // tile
    grid_spec = pltpu.PrefetchScalarGridSpec(
        num_scalar_prefetch=1,
        grid=(n_tok // tile,),
        in_specs=[pl.BlockSpec(memory_space=pl.ANY),
                  pl.BlockSpec((tile, D_MODEL), lambda i, d: (i + blk_off, 0)),
                  pl.BlockSpec((tile, LANES), lambda i, d: (i + blk_off, 0)),
                  pl.BlockSpec((1, D_MODEL), lambda i, d: (0, 0)),
                  pl.BlockSpec((1, D_MODEL), lambda i, d: (0, 0))],
        out_specs=pl.BlockSpec((tile, D_MODEL), lambda i, d: (i, 0)),
        scratch_shapes=[pltpu.VMEM((2, TOP_K, tile * SLAB_ROWS, LANES), jnp.uint32),
                        pltpu.VMEM((tile, D_MODEL), F32), pltpu.SemaphoreType.DMA((2,))],
    )
    return pl.pallas_call(
        functools.partial(_combine_kernel, tok_off, tile),
        grid_spec=grid_spec,
        out_shape=jax.ShapeDtypeStruct((n_tok, D_MODEL), F32),
        compiler_params=_params("arbitrary"),
        name="combine_ln2",
    )(dest_flat, y_slabs, h_all, gates, ln_g, ln_b)


def _moe_plan(idx, counts_f):
    m = idx.shape[0]
    n_asg = m * TOP_K
    experts = idx[:, 0:TOP_K]
    ranks = idx[:, TOP_K:2 * TOP_K]
    counts = counts_f[0, :N_EXPERTS].astype(jnp.int32)
    padded = (counts + MOE_BLK - 1) // MOE_BLK * MOE_BLK
    pad_end = jnp.cumsum(padded)
    pad_start = pad_end - padded
    expert_ids = jnp.arange(N_EXPERTS, dtype=jnp.int32)
    lookup = lambda hot, table: jnp.sum(jnp.where(hot, table, 0), axis=-1)
    dest = (lookup(experts[:, :, None] == expert_ids, pad_start) + ranks).reshape(n_asg).astype(jnp.int32)
    n_blocks = -(-(n_asg + N_EXPERTS * (MOE_BLK - 1)) // MOE_BLK)
    n_used = jnp.maximum(pad_end[-1] // MOE_BLK, 1).astype(jnp.int32)
    blk = jnp.minimum(jnp.arange(n_blocks, dtype=jnp.int32), n_used - 1)
    blk_row = blk * MOE_BLK
    blk_e = jnp.minimum(jnp.sum((pad_end[None, :] <= blk_row[:, None]).astype(jnp.int32), axis=-1), N_EXPERTS - 1)
    blk_hot = blk_e[:, None] == expert_ids
    blk_off = blk_row - lookup(blk_hot, pad_start)
    blk_first = (blk_off == 0).astype(jnp.int32)
    blk_valid = jnp.clip(lookup(blk_hot, counts) - blk_off, 0, MOE_BLK).astype(jnp.int32)
    used = counts > 0
    exp_list = jnp.argsort(jnp.logical_not(used), stable=True).astype(jnp.int32)
    blk_ord = lookup(blk_hot, jnp.cumsum(used.astype(jnp.int32)) - 1).astype(jnp.int32)
    n_exp = jnp.sum(used.astype(jnp.int32)).reshape(1)
    plan = (blk_e, blk_first, blk_valid, blk_ord, exp_list, n_used.reshape(1), n_exp)
    return dest, n_blocks * MOE_BLK, plan


def kernel(x_prompt, x_sample, cache_kv_w128, cache_kv_w512, cache_kv_w2048, state_conv, w_in, conv_w, w_conv_out, w_att_out, w_o, ln1_g, ln1_b, w_router_group, b_router_group, w_router_expert, b_router_expert, w_expert_gate, w_expert_up, w_expert_down, ln2_g, ln2_b):
    bsz, seq, _ = x_prompt.shape
    bd, t_new, _ = x_sample.shape
    m_p = bsz * seq
    m_s = bd * t_new
    m_all = m_p + m_s
    caches = (cache_kv_w128, cache_kv_w512, cache_kv_w2048)
    lyr = 0

    w_in16 = w_in[lyr].astype(BF16)
    wc16 = w_conv_out[lyr].astype(BF16)
    wa16 = w_att_out[lyr].astype(BF16)
    wo16 = w_o[lyr].astype(BF16)
    cw = conv_w[lyr]
    g1, b1 = ln1_g[lyr][None, :], ln1_b[lyr][None, :]
    g2, b2 = ln2_g[lyr][None, :], ln2_b[lyr][None, :]
    w_re = jnp.transpose(w_router_expert[lyr], (1, 0, 2)).reshape(D_MODEL, N_EXPERTS)
    w_r = jnp.pad(jnp.concatenate([w_router_group[lyr], w_re], axis=1),
                  ((0, 0), (0, LANES - N_EXPERT_GROUPS - N_EXPERTS)))
    w_r_hi = w_r.astype(BF16)
    w_r = jnp.concatenate([w_r_hi, (w_r - w_r_hi.astype(F32)).astype(BF16)], axis=1)
    b_r = jnp.pad(jnp.concatenate([b_router_group[lyr], b_router_expert[lyr].reshape(N_EXPERTS)]),
                  (0, LANES - N_EXPERT_GROUPS - N_EXPERTS))[None, :]

    xp = x_prompt.reshape(m_p, D_MODEL)
    xs = x_sample.reshape(m_s, D_MODEL)

    caches3 = [c[lyr].reshape(bd, -1, HEAD_DIM) for c in caches]
    z_p, qkv_p, shifted = _proj_in(xp, w_in16, min(PROJ_TM, m_p),
                                   [c.reshape(-1, HEAD_DIM) for c in caches3], t_new * KV_ROWS)
    shifted = [s.reshape(c.shape) for s, c in zip(shifted, caches3)]
    o_p = _attn_prompt(qkv_p, bsz, seq)
    h_buf, conv_p = _mix(None, o_p, z_p, xp, wc16, wa16, wo16, g1, b1, min(MIX_TM, seq), m_all, conv=(cw, seq))

    z_s, qkv_s, _ = _proj_in(xs, w_in16, m_s)
    p_s, conv_s = _conv_sample(z_s, cw, state_conv[lyr], t_new)
    q3 = qkv_s.reshape(bd, t_new, QKV_COLS)
    kv5 = qkv_s.reshape(bd, t_new, 3, N_DIL, HEADS_PER_GROUP, HEAD_DIM)
    new_kvs = [kv5[:, :, 1:3, g].reshape(bd, t_new * KV_ROWS, HEAD_DIM) for g in range(N_DIL)]
    bufs_s, o_s = _attn_sample(q3, caches3, new_kvs, shifted)
    bufs_s = [buf.reshape(1, bd, window, 2, HEADS_PER_GROUP, HEAD_DIM)
              for buf, (window, _) in zip(bufs_s, DILATED_GROUPS)]
    h_buf = _mix(p_s, o_s.reshape(m_s, D_GRP), z_s, xs, wc16, wa16, wo16, g1, b1, m_s, m_all, h_all=h_buf, row_off=m_p)

    h_all, h_slabs = h_buf
    idx, gates, counts = _router(h_all, w_r, b_r)
    dest, n_rows, plan = _moe_plan(idx, counts)
    x_slabs = _dispatch(h_slabs, dest, n_rows)
    y_slabs = _experts(x_slabs, plan, w_expert_gate[lyr], w_expert_up[lyr], w_expert_down[lyr])
    y_p = _combine(y_slabs, dest, h_all, gates, g2, b2, 0, m_p)
    y_s = _combine(y_slabs, dest, h_all, gates, g2, b2, m_p, m_s)

    qkv_p3 = qkv_p.reshape(bsz, seq, QKV_COLS)
    bufs_p = []
    for g, (w, _) in enumerate(DILATED_GROUPS):
        keep = min(w, seq)
        bufs_p.append(_kv_pack(qkv_p3, g, keep).reshape(1, bsz, keep, 2, HEADS_PER_GROUP, HEAD_DIM))
    return (y_p.reshape(bsz, seq, D_MODEL), y_s.reshape(bd, t_new, D_MODEL),
            bufs_p[0], bufs_p[1], bufs_p[2], conv_p[None],
            bufs_s[0], bufs_s[1], bufs_s[2], conv_s[None])
```

```python
import functools

import jax
import jax.numpy as jnp
from jax import lax
from jax.experimental import pallas as pl
from jax.experimental.pallas import tpu as pltpu

F32 = jnp.float32
BF16 = jnp.bfloat16

D_MODEL = 2048
D_CONV = D_MODEL // 2
CONV_W = 3
DILATED_GROUPS = ((128, 1), (512, 4), (2048, 16))
N_DIL = len(DILATED_GROUPS)
HEADS_PER_GROUP = 4
HEAD_DIM = 128
SPAN = 128
D_GRP = HEADS_PER_GROUP * HEAD_DIM
D_ATT = N_DIL * D_GRP
N_EXPERT_GROUPS = 8
EXPERTS_PER_GROUP = 8
N_EXPERTS = N_EXPERT_GROUPS * EXPERTS_PER_GROUP
TOP_K = 2
D_EXPERT = D_MODEL // 4
DEPTH = 1
LN_EPS = 1e-5
ALPHA = (2 * DEPTH) ** 0.25
NEG_INF = -1e30
ATT_SCALE = HEAD_DIM ** -0.5

OFF_B = 0
OFF_C = D_CONV
OFF_H = 2 * D_CONV
OFF_Q = 3 * D_CONV
OFF_K = OFF_Q + D_ATT
OFF_V = OFF_K + D_ATT
OFF_GC = OFF_V + D_ATT
OFF_GA = OFF_GC + D_MODEL
N_IN = OFF_GA + D_MODEL

LANES = 128
SUBLANES = 8
VMEM_LIMIT = 56 * 1024 * 1024

PROJ_TN = 512
PROJ_TM = 1024
Z_COLS = 2 * D_MODEL + 3 * D_CONV
QKV_COLS = 3 * D_ATT
NZ_TILES = Z_COLS // PROJ_TN
NQ_TILES = QKV_COLS // PROJ_TN
MIX_TM = 256
ROUTER_TILE = 640
ROW_TILE = 128
MOE_BLK = 304
W_SLOTS = 3
ATTN_UNROLL = 8
DISPATCH_TILE = 640
DISPATCH_SLOTS = 3
COMBINE_TILE = 256
KV_PACK_TM = 512
SHIFT_CHUNK_ROWS = 4128
SHIFT_AHEAD = 2
SHIFT_SLOTS = 2 * SHIFT_AHEAD
DMA_UNROLL = 8
DMA_PRIORITIES = 2


def _proj_tile_order():
    t = lambda off, width: list(range(off // PROJ_TN, (off + width) // PROJ_TN))
    z_tiles = t(OFF_GC, D_MODEL) + t(OFF_GA, D_MODEL) + t(OFF_B, D_CONV) + t(OFF_C, D_CONV) + t(OFF_H, D_CONV)
    q_tiles = t(OFF_Q, D_ATT) + t(OFF_K, D_ATT) + t(OFF_V, D_ATT)
    return z_tiles + q_tiles


def _params(*sem):
    return pltpu.CompilerParams(dimension_semantics=sem, vmem_limit_bytes=VMEM_LIMIT)


def _shift_chunks(rows, drop, max_rows):
    keep = rows - drop
    best = SUBLANES
    for c in range(SUBLANES, max_rows + 1, SUBLANES):
        if keep % c == 0:
            best = c
    return best, keep // best


def _proj_kernel(shift_plan, drop, order_ref, x_ref, w_ref, *rest):
    del order_ref
    n_shift = len(shift_plan)
    srcs = rest[:n_shift]
    z_ref, qkv_ref = rest[n_shift:n_shift + 2]
    dsts = rest[n_shift + 2:2 * n_shift + 2]
    x16 = rest[2 * n_shift + 2]
    j = pl.program_id(1)

    if n_shift:
        stage, sem_in, sem_out = rest[2 * n_shift + 3:]
        t = pl.program_id(0) * pl.num_programs(1) + j
        n_chunks = shift_plan[-1][2]

        def for_chunk(q, action):
            for n, (chunk, lo, hi) in enumerate(shift_plan):
                @pl.when(jnp.logical_and(q >= lo, q < hi))
                def _(n=n, chunk=chunk, lo=lo):
                    action(n, chunk, pl.multiple_of((q - lo) * chunk, SUBLANES), q & (SHIFT_SLOTS - 1))

        def load(n, chunk, start, slot):
            return pltpu.make_async_copy(srcs[n].at[pl.ds(start + drop, chunk), :],
                                         stage.at[slot, pl.ds(0, chunk), :], sem_in.at[slot])

        def store(n, chunk, start, slot):
            return pltpu.make_async_copy(stage.at[slot, pl.ds(0, chunk), :],
                                         dsts[n].at[pl.ds(start, chunk), :], sem_out.at[slot])

        @pl.when(t == 0)
        def _():
            for q in range(SHIFT_AHEAD):
                for_chunk(t + q, lambda *a: load(*a).start())

        @pl.when(t < n_chunks)
        def _():
            for_chunk(t, lambda *a: load(*a).wait())
            for_chunk(t, lambda *a: store(*a).start())

        @pl.when(jnp.logical_and(t >= SHIFT_AHEAD, t < n_chunks + SHIFT_AHEAD))
        def _():
            for_chunk(t - SHIFT_AHEAD, lambda *a: store(*a).wait())

        @pl.when(t + SHIFT_AHEAD < n_chunks)
        def _():
            for_chunk(t + SHIFT_AHEAD, lambda *a: load(*a).start())

    @pl.when(j == 0)
    def _():
        x16[...] = x_ref[...].astype(BF16)

    @pl.when(j < NZ_TILES)
    def _():
        z_ref[...] = jnp.dot(x16[...], w_ref[...], preferred_element_type=F32).astype(BF16)

    @pl.when(j >= NZ_TILES)
    def _():
        qkv_ref[...] = jnp.dot(x16[...], w_ref[...], preferred_element_type=F32)


def _proj_in(x, w16, tm, shift_srcs=(), drop=0):
    m = x.shape[0]
    n_shift = len(shift_srcs)
    n_steps = (m // tm) * (NZ_TILES + NQ_TILES)
    order = jnp.asarray(_proj_tile_order(), jnp.int32)
    hbm = pl.BlockSpec(memory_space=pl.ANY)
    shift_plan, first = [], 0
    for s in shift_srcs:
        chunk, count = _shift_chunks(s.shape[0], drop, SHIFT_CHUNK_ROWS)
        shift_plan.append((chunk, first, first + count))
        first += count
    assert first + SHIFT_AHEAD <= n_steps, "one shift chunk per grid step, plus the steps that drain"
    shift_scratch = [pltpu.VMEM((SHIFT_SLOTS, SHIFT_CHUNK_ROWS, LANES), F32),
                     pltpu.SemaphoreType.DMA((SHIFT_SLOTS,)), pltpu.SemaphoreType.DMA((SHIFT_SLOTS,))] if n_shift else []
    grid_spec = pltpu.PrefetchScalarGridSpec(
        num_scalar_prefetch=1,
        grid=(m // tm, NZ_TILES + NQ_TILES),
        in_specs=[
            pl.BlockSpec((tm, D_MODEL), lambda i, j, o: (i, 0)),
            pl.BlockSpec((D_MODEL, PROJ_TN), lambda i, j, o: (0, o[j])),
        ] + [hbm] * n_shift,
        out_specs=[
            pl.BlockSpec((tm, PROJ_TN), lambda i, j, o: (i, jnp.minimum(j, NZ_TILES - 1))),
            pl.BlockSpec((tm, PROJ_TN), lambda i, j, o: (i, jnp.maximum(j - NZ_TILES, 0))),
        ] + [hbm] * n_shift,
        scratch_shapes=[pltpu.VMEM((tm, D_MODEL), BF16)] + shift_scratch,
    )
    outs = pl.pallas_call(
        functools.partial(_proj_kernel, tuple(shift_plan), drop),
        grid_spec=grid_spec,
        out_shape=[jax.ShapeDtypeStruct((m, Z_COLS), BF16), jax.ShapeDtypeStruct((m, QKV_COLS), F32)]
        + [jax.ShapeDtypeStruct(s.shape, s.dtype) for s in shift_srcs],
        compiler_params=_params("arbitrary", "arbitrary"),
        name="proj_in",
    )(order, x, w16, *shift_srcs)
    return outs[0], outs[1], outs[2:]


ZB_BLK = (2 * D_MODEL) // D_CONV


def _gated_conv_tile(zb_ref, zc_ref, zh_ref, cw_ref, st_ref, uext, seq_start):
    tm = zb_ref.shape[0]

    @pl.when(seq_start)
    def _():
        uext[0:SUBLANES, :] = jnp.zeros((SUBLANES, D_CONV), F32)

    u = zc_ref[...].astype(F32) * zh_ref[...].astype(F32)
    uext[SUBLANES:SUBLANES + tm, :] = u
    conv = (cw_ref[0:1, :] * uext[SUBLANES - 2:SUBLANES - 2 + tm, :]
            + cw_ref[1:2, :] * uext[SUBLANES - 1:SUBLANES - 1 + tm, :]
            + cw_ref[2:3, :] * u)
    last = uext[tm:tm + SUBLANES, :]
    uext[0:SUBLANES, :] = last
    st_ref[0] = last[SUBLANES - 2:SUBLANES, :]
    return (zb_ref[...].astype(F32) * conv).astype(BF16)


def _conv_sample_kernel(t_new, zb_ref, zc_ref, zh_ref, cw_ref, s1_ref, s2_ref, p_ref, u_ref, uext):
    m = zb_ref.shape[0]
    u = zc_ref[...].astype(F32) * zh_ref[...].astype(F32)
    u_ref[...] = u
    uext[0:SUBLANES, :] = jnp.zeros((SUBLANES, D_CONV), F32)
    uext[SUBLANES:SUBLANES + m, :] = u
    t = lax.broadcasted_iota(jnp.int32, (m, D_CONV), 0) % t_new
    um1 = jnp.where(t < 1, s1_ref[...], uext[SUBLANES - 1:SUBLANES - 1 + m, :])
    um2 = jnp.where(t < 2, s2_ref[...], uext[SUBLANES - 2:SUBLANES - 2 + m, :])
    conv = cw_ref[0:1, :] * um2 + cw_ref[1:2, :] * um1 + cw_ref[2:3, :] * u
    p_ref[...] = (zb_ref[...].astype(F32) * conv).astype(BF16)


def _conv_sample(z, conv_w, state, t_new):
    m = z.shape[0]
    bd = m // t_new
    s1 = jnp.pad(state[:, 1:2], ((0, 0), (0, t_new - 1), (0, 0))).reshape(m, D_CONV)
    s2 = jnp.pad(state, ((0, 0), (0, t_new - 2), (0, 0))).reshape(m, D_CONV)
    zspec = lambda blk: pl.BlockSpec((m, D_CONV), lambda i: (0, blk))
    full = lambda r: pl.BlockSpec((r, D_CONV), lambda i: (0, 0))
    p, u = pl.pallas_call(
        functools.partial(_conv_sample_kernel, t_new),
        grid=(1,),
        in_specs=[zspec(ZB_BLK), zspec(ZB_BLK + 1), zspec(ZB_BLK + 2), full(CONV_W), full(m), full(m)],
        out_specs=[full(m), full(m)],
        out_shape=[jax.ShapeDtypeStruct((m, D_CONV), BF16), jax.ShapeDtypeStruct((m, D_CONV), F32)],
        scratch_shapes=[pltpu.VMEM((m + SUBLANES, D_CONV), F32)],
        compiler_params=_params("arbitrary"),
        name="conv_sample",
    )(z, z, z, conv_w, s1, s2)
    new_state = u.reshape(bd, t_new, D_CONV)[:, t_new - (CONV_W - 1):]
    return p, new_state


def _merge_groups(o_list, lse_list):
    top = functools.reduce(jnp.maximum, lse_list)
    es = [jnp.exp(l - top) for l in lse_list]
    num = functools.reduce(lambda a, b: a + b, [e * o for e, o in zip(es, o_list)])
    den = functools.reduce(lambda a, b: a + b, es)
    return num / den


def _nt_dot(a, b):
    return lax.dot_general(a, b, (((1,), (1,)), ((), ())), preferred_element_type=F32)


def _attn_prompt_group(q_ref, k_ref, v_ref, o_sc, lse_sc, g, dil, seq):
    shift = dil.bit_length() - 1
    qi = lax.broadcasted_iota(jnp.int32, (SPAN, 2 * SPAN), 0)
    kj = lax.broadcasted_iota(jnp.int32, (SPAN, 2 * SPAN), 1)
    band = jnp.logical_and(kj >= qi, kj <= qi + SPAN)
    is_prev = lax.broadcasted_iota(jnp.int32, (1, 2 * SPAN), 1) < SPAN
    ones = jnp.ones((2 * SPAN, HEAD_DIM), BF16)

    def rows(ref, start):
        if dil == 1:
            return ref[pl.ds(start, SPAN), :]
        return ref[pl.ds(start, SPAN, stride=dil), :]

    def body(trip, carry):
        blocks = range(ATTN_UNROLL)
        idx = [trip * ATTN_UNROLL + u for u in blocks]
        c = [i >> shift for i in idx]
        base = [c[u] * (SPAN * dil) + (idx[u] & (dil - 1)) for u in blocks]
        pbase = [jnp.maximum(c[u] - 1, 0) * (SPAN * dil) + (idx[u] & (dil - 1)) for u in blocks]
        q = [rows(q_ref, base[u]).astype(BF16) for u in blocks]
        k = [jnp.concatenate([rows(k_ref, pbase[u]), rows(k_ref, base[u])], axis=0).astype(BF16) for u in blocks]
        s = [_nt_dot(q[u], k[u]) for u in blocks]
        pen = [jnp.where(jnp.logical_and(is_prev, c[u] == 0), NEG_INF, 0.0) for u in blocks]
        s = [jnp.where(band, s[u] * ATT_SCALE + pen[u], NEG_INF) for u in blocks]
        m = [jnp.max(jnp.maximum(s[u][:, :SPAN], s[u][:, SPAN:]), axis=-1, keepdims=True) for u in blocks]
        p = [jnp.exp(s[u] - m[u]).astype(BF16) for u in blocks]
        v = [jnp.concatenate(
            [jnp.concatenate([rows(v_ref, pbase[u]), rows(v_ref, base[u])], axis=0).astype(BF16), ones], axis=1)
            for u in blocks]
        acc = [jnp.dot(p[u], v[u], preferred_element_type=F32) for u in blocks]
        for u in blocks:
            den = acc[u][:, HEAD_DIM:]
            o = acc[u][:, :HEAD_DIM] / den
            lse = m[u] + jnp.log(den)
            if dil == 1:
                o_sc[g, pl.ds(base[u], SPAN), :] = o
                lse_sc[g, pl.ds(base[u], SPAN), :] = lse
            else:
                o_sc[g, pl.ds(base[u], SPAN, stride=dil), :] = o
                lse_sc[g, pl.ds(base[u], SPAN, stride=dil), :] = lse
        return carry

    lax.fori_loop(0, seq // (SPAN * ATTN_UNROLL), body, 0)


def _attn_prompt_kernel(q_ref, k_ref, v_ref, o_ref, o_sc, lse_sc):
    seq = q_ref.shape[0]
    g_id = pl.program_id(2)
    for g, (_, dil) in enumerate(DILATED_GROUPS):
        @pl.when(g_id == g)
        def _(g=g, dil=dil):
            _attn_prompt_group(q_ref, k_ref, v_ref, o_sc, lse_sc, g, dil, seq)

    @pl.when(g_id == N_DIL - 1)
    def _():
        chunk = 256

        def body(i, carry):
            sl = pl.ds(pl.multiple_of(i * chunk, chunk), chunk)
            o = _merge_groups([o_sc[g, sl, :] for g in range(N_DIL)], [lse_sc[g, sl, :] for g in range(N_DIL)])
            o_ref[sl, :] = o.astype(BF16)
            return carry

        lax.fori_loop(0, seq // chunk, body, 0)


def _attn_prompt(qkv, bsz, seq):
    qkv3 = qkv.reshape(bsz, seq, QKV_COLS)
    per_sel = D_ATT // HEAD_DIM
    spec = lambda sel: pl.BlockSpec((None, seq, HEAD_DIM),
                                    lambda b, j, g: (b, 0, sel * per_sel + g * HEADS_PER_GROUP + j))
    out = pl.pallas_call(
        _attn_prompt_kernel,
        grid=(bsz, HEADS_PER_GROUP, N_DIL),
        in_specs=[spec(0), spec(1), spec(2)],
        out_specs=pl.BlockSpec((None, seq, HEAD_DIM), lambda b, j, g: (b, 0, j)),
        out_shape=jax.ShapeDtypeStruct((bsz, seq, D_GRP), BF16),
        scratch_shapes=[pltpu.VMEM((N_DIL, seq, HEAD_DIM), F32), pltpu.VMEM((N_DIL, seq, HEAD_DIM), F32)],
        compiler_params=_params("arbitrary", "arbitrary", "arbitrary"),
        name="attn_prompt",
    )(qkv3, qkv3, qkv3)
    return out.reshape(bsz * seq, D_GRP)


KV_ROWS = 2 * HEADS_PER_GROUP


def _kv_pack_kernel(k_ref, v_ref, out_ref):
    tm = k_ref.shape[0]
    for h in range(HEADS_PER_GROUP):
        cols = slice(h * HEAD_DIM, (h + 1) * HEAD_DIM)
        out_ref[pl.ds(h, tm, stride=KV_ROWS), :] = k_ref[:, cols]
        out_ref[pl.ds(HEADS_PER_GROUP + h, tm, stride=KV_ROWS), :] = v_ref[:, cols]


def _kv_pack(qkv3, g, keep):
    bsz, seq, _ = qkv3.shape
    tm = min(KV_PACK_TM, keep)
    assert keep % tm == 0 and (seq - keep) % tm == 0
    first = (seq - keep) // tm
    col = lambda sel: (sel * D_ATT + g * D_GRP) // D_GRP
    spec = lambda sel: pl.BlockSpec((None, tm, D_GRP), lambda b, s: (b, first + s, col(sel)))
    return pl.pallas_call(
        _kv_pack_kernel,
        grid=(bsz, keep // tm),
        in_specs=[spec(1), spec(2)],
        out_specs=pl.BlockSpec((None, tm * KV_ROWS, HEAD_DIM), lambda b, s: (b, s, 0)),
        out_shape=jax.ShapeDtypeStruct((bsz, keep * KV_ROWS, HEAD_DIM), F32),
        compiler_params=_params("arbitrary", "arbitrary"),
        name="kv_pack",
    )(qkv3, qkv3)


def _sample_key_bias(dil, t_new, n_rows, n_buf):
    shape = (HEADS_PER_GROUP * t_new, n_rows)
    rr = lax.broadcasted_iota(jnp.int32, shape, 0)
    cc = lax.broadcasted_iota(jnp.int32, shape, 1)
    head = rr // t_new
    back = n_buf + (rr - head * t_new) - cc // KV_ROWS
    ok = jnp.logical_and(cc % KV_ROWS == HEADS_PER_GROUP + head,
                         jnp.logical_and(jnp.logical_and(back >= 0, back <= dil * SPAN), back % dil == 0))
    return jnp.where(ok, 0.0, NEG_INF)


def _attn_sample_group(g, q_ref, cache_ref, new_ref, bias_c, bias_n, o_sc, lse_sc):
    t_new = q_ref.shape[0]
    q = jnp.concatenate([q_ref[:, g * D_GRP + h * HEAD_DIM:g * D_GRP + (h + 1) * HEAD_DIM]
                         for h in range(HEADS_PER_GROUP)], axis=0).astype(BF16)

    def keys_values(ref):
        rows = ref[...]
        tiles = rows.reshape(-1, KV_ROWS, HEAD_DIM)
        return pltpu.roll(tiles, HEADS_PER_GROUP, 1).reshape(rows.shape).astype(BF16), rows.astype(BF16)

    k_c, v_c = keys_values(cache_ref)
    k_n, v_n = keys_values(new_ref)
    s_c = _nt_dot(q, k_c) * ATT_SCALE + bias_c[...]
    s_n = _nt_dot(q, k_n) * ATT_SCALE + bias_n[...]
    m = jnp.maximum(jnp.max(s_c, axis=-1, keepdims=True), jnp.max(s_n, axis=-1, keepdims=True))
    p_c = jnp.exp(s_c - m)
    p_n = jnp.exp(s_n - m)
    den = jnp.sum(p_c, axis=-1, keepdims=True) + jnp.sum(p_n, axis=-1, keepdims=True)
    acc = (jnp.dot(p_c.astype(BF16), v_c, preferred_element_type=F32)
           + jnp.dot(p_n.astype(BF16), v_n, preferred_element_type=F32))
    o = acc / den
    lse = jnp.broadcast_to(m + jnp.log(den), o.shape)
    for h in range(HEADS_PER_GROUP):
        cols = slice(h * HEAD_DIM, (h + 1) * HEAD_DIM)
        o_sc[g, :, cols] = o[h * t_new:(h + 1) * t_new, :]
        lse_sc[g, :, cols] = lse[h * t_new:(h + 1) * t_new, :]


def _attn_sample_kernel(q_ref, *refs):
    caches, news = refs[:N_DIL], refs[N_DIL:2 * N_DIL]
    bufs = refs[3 * N_DIL:4 * N_DIL]
    o_ref = refs[4 * N_DIL]
    bias_cs = refs[4 * N_DIL + 1:5 * N_DIL + 1]
    bias_n, o_sc, lse_sc = refs[5 * N_DIL + 1:]
    t_new = q_ref.shape[0]

    @pl.when(pl.program_id(0) == 0)
    def _():
        for g, (_, dil) in enumerate(DILATED_GROUPS):
            n_rows = caches[g].shape[0]
            n_buf = n_rows // KV_ROWS
            bias_cs[g][...] = _sample_key_bias(dil, t_new, n_rows, n_buf)
            bias_n[g] = _sample_key_bias(dil, t_new, t_new * KV_ROWS, 0)

    for g in range(N_DIL):
        _attn_sample_group(g, q_ref, caches[g], news[g], bias_cs[g], bias_n.at[g], o_sc, lse_sc)
        bufs[g][...] = news[g][...]
    o_ref[...] = _merge_groups([o_sc[g] for g in range(N_DIL)], [lse_sc[g] for g in range(N_DIL)]).astype(BF16)


def _attn_sample(q3, caches, new_kvs, shifted):
    bd, t_new, _ = q3.shape
    new_rows = t_new * KV_ROWS
    row_spec = lambda r: pl.BlockSpec((None, r, HEAD_DIM), lambda b: (b, 0, 0))
    for (window, _), cache in zip(DILATED_GROUPS, caches):
        assert cache.shape[1] == window * KV_ROWS, "the cache must hold a full window"
        assert cache.shape[1] % new_rows == 0
    hbm = pl.BlockSpec(memory_space=pl.ANY)
    tail = lambda c: pl.BlockSpec((None, new_rows, HEAD_DIM), lambda b: (b, c.shape[1] // new_rows - 1, 0))
    outs = pl.pallas_call(
        _attn_sample_kernel,
        grid=(bd,),
        in_specs=[pl.BlockSpec((None, t_new, D_ATT), lambda b: (b, 0, 0))]
        + [row_spec(c.shape[1]) for c in caches] + [row_spec(new_rows)] * N_DIL + [hbm] * N_DIL,
        out_specs=[tail(c) for c in caches] + [pl.BlockSpec((None, t_new, D_GRP), lambda b: (b, 0, 0))],
        out_shape=[jax.ShapeDtypeStruct(c.shape, F32) for c in caches]
        + [jax.ShapeDtypeStruct((bd, t_new, D_GRP), BF16)],
        scratch_shapes=[pltpu.VMEM((HEADS_PER_GROUP * t_new, c.shape[1]), F32) for c in caches]
        + [pltpu.VMEM((N_DIL, HEADS_PER_GROUP * t_new, new_rows), F32),
           pltpu.VMEM((N_DIL, t_new, D_GRP), F32), pltpu.VMEM((N_DIL, t_new, D_GRP), F32)],
        input_output_aliases={1 + 2 * N_DIL + g: g for g in range(N_DIL)},
        compiler_params=_params("arbitrary"),
        name="attn_sample",
    )(q3, *caches, *new_kvs, *shifted)
    return outs[:N_DIL], outs[N_DIL]


def _layer_norm(v, g, b):
    mu = jnp.mean(v, axis=-1, keepdims=True)
    d = v - mu
    var = jnp.mean(d * d, axis=-1, keepdims=True)
    return d * lax.rsqrt(var + LN_EPS) * g + b


SLAB_ROWS = SUBLANES
HALF = D_MODEL // 2
HI_MASK = 0xFFFF0000


def _bf16_bits(v):
    return pltpu.bitcast(v.astype(BF16).astype(F32), jnp.uint32)


def _pack_slabs(src_ref, slab_ref, n_tok):
    def body(tg, carry):
        t0 = pl.multiple_of(tg * SUBLANES, SUBLANES)
        for s in range(SLAB_ROWS):
            lo = src_ref[pl.ds(t0, SUBLANES), s * LANES:(s + 1) * LANES]
            hi = src_ref[pl.ds(t0, SUBLANES), HALF + s * LANES:HALF + (s + 1) * LANES]
            word = (_bf16_bits(lo) >> 16) | _bf16_bits(hi)
            slab_ref[pl.ds(t0 * SLAB_ROWS + s, SUBLANES, stride=SLAB_ROWS), :] = word
        return carry
    lax.fori_loop(0, n_tok // SUBLANES, body, 0, unroll=2)


def _unpack_gated_sum(slab_refs, gate_ref, dst_ref, n_tok):
    def body(tg, carry):
        t0 = pl.multiple_of(tg * SUBLANES, SUBLANES)
        gates = [gate_ref[pl.ds(t0, SUBLANES), k:k + 1] for k in range(len(slab_refs))]
        for s in range(SLAB_ROWS):
            words = [ref[pl.ds(t0 * SLAB_ROWS + s, SUBLANES, stride=SLAB_ROWS), :] for ref in slab_refs]
            lo = [g * pltpu.bitcast(w << 16, F32) for g, w in zip(gates, words)]
            hi = [g * pltpu.bitcast(w & jnp.uint32(HI_MASK), F32) for g, w in zip(gates, words)]
            dst_ref[pl.ds(t0, SUBLANES), s * LANES:(s + 1) * LANES] = functools.reduce(lambda a, b: a + b, lo)
            dst_ref[pl.ds(t0, SUBLANES), HALF + s * LANES:HALF + (s + 1) * LANES] = functools.reduce(
                lambda a, b: a + b, hi)
        return carry
    lax.fori_loop(0, n_tok // SUBLANES, body, 0, unroll=4)


def _unpack_slabs_bf16(slab_ref, dst_ref, n_tok, n_valid):
    pair = 2 * SUBLANES
    sub = lax.broadcasted_iota(jnp.int32, (SUBLANES, LANES), 0)

    def body(tg, carry):
        t0 = pl.multiple_of(tg * pair, pair)
        for s in range(SLAB_ROWS):
            words = []
            for half in range(2):
                t1 = t0 + half * SUBLANES
                w = slab_ref[pl.ds(t1 * SLAB_ROWS + s, SUBLANES, stride=SLAB_ROWS), :]
                words.append(jnp.where(t1 + sub < n_valid, w, jnp.uint32(0)))
            lo = jnp.concatenate([pltpu.bitcast(w << 16, F32) for w in words], axis=0)
            hi = jnp.concatenate([pltpu.bitcast(w & jnp.uint32(HI_MASK), F32) for w in words], axis=0)
            dst_ref[pl.ds(t0, pair), s * LANES:(s + 1) * LANES] = lo.astype(BF16)
            dst_ref[pl.ds(t0, pair), HALF + s * LANES:HALF + (s + 1) * LANES] = hi.astype(BF16)
        return carry
    lax.fori_loop(0, n_tok // pair, body, 0)


def _mix_kernel(conv_steps, *refs):
    if conv_steps:
        conv_in, refs = refs[:4], refs[4:]
    else:
        p_ref, refs = refs[0], refs[1:]
    o_ref, gc_ref, ga_ref, x_ref, wc_ref, wa_ref, wo_ref, g_ref, b_ref = refs[:9]
    if conv_steps:
        h_ref, hs_ref, st_ref, uext = refs[-4:]
        p = _gated_conv_tile(*conv_in, st_ref, uext, pl.program_id(0) % conv_steps == 0)
    else:
        h_ref, hs_ref = refs[-2:]
        p = p_ref[...]
    y_conv = jnp.dot(p, wc_ref[...], preferred_element_type=F32)
    y_att = jnp.dot(o_ref[...], wa_ref[...], preferred_element_type=F32)
    merged = (jax.nn.sigmoid(gc_ref[...].astype(F32)) * y_conv
              + jax.nn.sigmoid(ga_ref[...].astype(F32)) * y_att)
    mix = jnp.dot(merged.astype(BF16), wo_ref[...], preferred_element_type=F32)
    h_ref[...] = _layer_norm(ALPHA * x_ref[...] + mix, g_ref[...], b_ref[...])
    _pack_slabs(h_ref, hs_ref, h_ref.shape[0])


def _mix(p, o, z, x, wc16, wa16, wo16, ln_g, ln_b, tm, total, h_all=None, row_off=0, conv=None):
    m = x.shape[0]
    assert row_off % tm == 0
    blk_off = row_off // tm
    row = lambda w: pl.BlockSpec((tm, w), lambda i: (i, 0))
    const = lambda a: pl.BlockSpec(a.shape, lambda i: (0, 0), pipeline_mode=pl.Buffered(1))
    out_specs = [pl.BlockSpec((tm, D_MODEL), lambda i: (i + blk_off, 0)),
                 pl.BlockSpec((tm * SLAB_ROWS, LANES), lambda i: (i + blk_off, 0))]
    out_shape = [jax.ShapeDtypeStruct((total, D_MODEL), F32),
                 jax.ShapeDtypeStruct((total * SLAB_ROWS, LANES), jnp.uint32)]
    scratch, conv_steps = [], 0
    if conv is None:
        in_specs, args = [row(D_CONV)], [p]
    else:
        conv_w, seq = conv
        assert seq % tm == 0
        conv_steps = seq // tm
        zspec = lambda blk: pl.BlockSpec((tm, D_CONV), lambda i: (i, blk))
        in_specs, args = [zspec(ZB_BLK), zspec(ZB_BLK + 1), zspec(ZB_BLK + 2), const(conv_w)], [z, z, z, conv_w]
        out_specs.append(pl.BlockSpec((1, CONV_W - 1, D_CONV), lambda i: (i // conv_steps, 0, 0)))
        out_shape.append(jax.ShapeDtypeStruct((m // seq, CONV_W - 1, D_CONV), F32))
        scratch = [pltpu.VMEM((tm + SUBLANES, D_CONV), F32)]
    in_specs += [row(D_GRP),
                 pl.BlockSpec((tm, D_MODEL), lambda i: (i, 0)), pl.BlockSpec((tm, D_MODEL), lambda i: (i, 1)),
                 row(D_MODEL), const(wc16), const(wa16), const(wo16), const(ln_g), const(ln_b)]
    args += [o, z, z, x, wc16, wa16, wo16, ln_g, ln_b]
    aliases = {}
    if h_all is not None:
        in_specs += [pl.BlockSpec(memory_space=pl.ANY)] * 2
        args += list(h_all)
        aliases = {len(args) - 2: 0, len(args) - 1: 1}
    outs = pl.pallas_call(
        functools.partial(_mix_kernel, conv_steps),
        grid=(m // tm,),
        in_specs=in_specs,
        out_specs=out_specs,
        out_shape=out_shape,
        scratch_shapes=scratch,
        input_output_aliases=aliases,
        compiler_params=_params("arbitrary"),
        name="mix_ln1",
    )(*args)
    return (tuple(outs[:2]), outs[2]) if conv is not None else tuple(outs)


def _route(h, w_ref, b_ref, carry):
    tile = h.shape[0]
    h_hi = h.astype(BF16)
    h_lo = (h - h_hi.astype(F32)).astype(BF16)
    w_both = w_ref[...]
    by_hi = jnp.dot(h_hi, w_both, preferred_element_type=F32)
    by_lo = jnp.dot(h_lo, w_both[:, :LANES], preferred_element_type=F32)
    logits = by_hi[:, :LANES] + (by_hi[:, LANES:] + by_lo) + b_ref[...]
    lane = lax.broadcasted_iota(jnp.int32, (tile, LANES), 1)
    lane_f = lane.astype(F32)
    first_max = lambda v, mx: jnp.min(jnp.where(v == mx, lane_f, float(LANES)), axis=-1, keepdims=True)

    is_grp = lane < N_EXPERT_GROUPS
    lg = jnp.where(is_grp, logits, -jnp.inf)
    mg = jnp.max(lg, axis=-1, keepdims=True)
    gsel = first_max(lg, mg).astype(jnp.int32)
    p_group = 1.0 / jnp.sum(jnp.where(is_grp, jnp.exp(logits - mg), 0.0), axis=-1, keepdims=True)

    e_lane = lane - N_EXPERT_GROUPS
    in_grp = jnp.logical_and(e_lane >= 0, (e_lane >> 3) == gsel)
    le = jnp.where(in_grp, logits, -jnp.inf)
    v1 = jnp.max(le, axis=-1, keepdims=True)
    i1 = first_max(le, v1).astype(jnp.int32)
    le2 = jnp.where(lane == i1, -jnp.inf, le)
    v2 = jnp.max(le2, axis=-1, keepdims=True)
    i2 = first_max(le2, v2).astype(jnp.int32)
    ex = jnp.exp(v2 - v1)
    g1 = p_group / (1.0 + ex)
    g2 = p_group * ex / (1.0 + ex)
    e1 = i1 - N_EXPERT_GROUPS
    e2 = i2 - N_EXPERT_GROUPS

    hot1 = lane == e1
    hot2 = lane == e2
    hot = jnp.where(jnp.logical_or(hot1, hot2), 1.0, 0.0)
    rr = lax.broadcasted_iota(jnp.int32, (tile, tile), 0)
    cc = lax.broadcasted_iota(jnp.int32, (tile, tile), 1)
    tri = jnp.where(cc < rr, 1.0, 0.0).astype(BF16)
    before = jnp.dot(tri, hot.astype(BF16), preferred_element_type=F32) + carry[0:1, :]
    r1 = jnp.sum(jnp.where(hot1, before, 0.0), axis=-1, keepdims=True).astype(jnp.int32)
    r2 = jnp.sum(jnp.where(hot2, before, 0.0), axis=-1, keepdims=True).astype(jnp.int32)
    carry[...] = carry[...] + jnp.sum(hot, axis=0, keepdims=True)

    idx = jnp.where(lane == 0, e1, jnp.where(lane == 1, e2, jnp.where(lane == 2, r1, r2)))
    return idx, jnp.where(lane == 0, g1, g2)


def _router_kernel(h_ref, w_ref, b_ref, idx_ref, gate_ref, cnt_ref, carry):
    @pl.when(pl.program_id(0) == 0)
    def _():
        carry[...] = jnp.zeros(carry.shape, F32)

    idx_ref[...], gate_ref[...] = _route(h_ref[...], w_ref, b_ref, carry)
    cnt_ref[...] = carry[...]


def _router(h_all, w_r, b_r):
    m = h_all.shape[0]
    tile = ROUTER_TILE if m % ROUTER_TILE == 0 else ROW_TILE
    row = pl.BlockSpec((tile, LANES), lambda i: (i, 0))
    return pl.pallas_call(
        _router_kernel,
        grid=(m // tile,),
        in_specs=[pl.BlockSpec((tile, D_MODEL), lambda i: (i, 0)),
                  pl.BlockSpec((D_MODEL, 2 * LANES), lambda i: (0, 0)),
                  pl.BlockSpec((1, LANES), lambda i: (0, 0))],
        out_specs=[row, row, pl.BlockSpec((SUBLANES, LANES), lambda i: (0, 0))],
        out_shape=[jax.ShapeDtypeStruct((m, LANES), jnp.int32), jax.ShapeDtypeStruct((m, LANES), F32),
                   jax.ShapeDtypeStruct((SUBLANES, LANES), F32)],
        scratch_shapes=[pltpu.VMEM((SUBLANES, LANES), F32)],
        compiler_params=_params("arbitrary"),
        name="router",
    )(h_all, w_r, b_r)


def _slab(ref, row):
    return ref.at[pl.ds(pl.multiple_of(row * SLAB_ROWS, SLAB_ROWS), SLAB_ROWS), :]


def _dispatch_kernel(tile, dest_ref, h_hbm, x_hbm, stage, sem_in, sem_rows):
    i = pl.program_id(0)
    n_steps = pl.num_programs(0)
    slot_of = lambda step: lax.rem(step, DISPATCH_SLOTS)

    def load(step):
        rows = pl.ds(pl.multiple_of(step * (tile * SLAB_ROWS), SLAB_ROWS), tile * SLAB_ROWS)
        return pltpu.make_async_copy(h_hbm.at[rows, :], stage.at[slot_of(step)], sem_in.at[slot_of(step)])

    def each_row(step, action):
        slot = slot_of(step)

        def body(r, c):
            for k in range(TOP_K):
                action(pltpu.make_async_copy(_slab(stage.at[slot], r),
                                             _slab(x_hbm, dest_ref[(step * tile + r) * TOP_K + k]),
                                             sem_rows.at[slot]), k)
            return c
        lax.fori_loop(0, tile, body, 0, unroll=DMA_UNROLL)

    @pl.when(i == 0)
    def _():
        load(i).start()

    @pl.when(i + 1 < n_steps)
    def _():
        load(i + 1).start()

    load(i).wait()
    each_row(i, lambda c, k: c.start(priority=k % DMA_PRIORITIES))

    @pl.when(i > 0)
    def _():
        each_row(i - 1, lambda c, k: c.wait())

    @pl.when(i == n_steps - 1)
    def _():
        each_row(i, lambda c, k: c.wait())


def _dispatch(h_slabs, dest_flat, n_rows):
    m = h_slabs.shape[0] // SLAB_ROWS
    tile = DISPATCH_TILE if m % DISPATCH_TILE == 0 else ROW_TILE
    grid_spec = pltpu.PrefetchScalarGridSpec(
        num_scalar_prefetch=1,
        grid=(m // tile,),
        in_specs=[pl.BlockSpec(memory_space=pl.ANY)],
        out_specs=pl.BlockSpec(memory_space=pl.ANY),
        scratch_shapes=[pltpu.VMEM((DISPATCH_SLOTS, tile * SLAB_ROWS, LANES), jnp.uint32),
                        pltpu.SemaphoreType.DMA((DISPATCH_SLOTS,)), pltpu.SemaphoreType.DMA((DISPATCH_SLOTS,))],
    )
    return pl.pallas_call(
        functools.partial(_dispatch_kernel, tile),
        grid_spec=grid_spec,
        out_shape=jax.ShapeDtypeStruct((n_rows * SLAB_ROWS, LANES), jnp.uint32),
        compiler_params=pltpu.CompilerParams(dimension_semantics=("arbitrary",), has_side_effects=True),
        name="moe_dispatch",
    )(dest_flat, h_slabs)


def _expert_kernel(blk_e, blk_first, blk_valid, blk_ord, exp_list, n_used, n_exp,
                   x_ref, wg_hbm, wu_hbm, wd_hbm, y_ref, wg32, wu32, wd32, wg16, wu16, wd16, sems, rows32, x16):
    i = pl.program_id(0)

    def weight_copies(e, slot):
        return [pltpu.make_async_copy(hbm.at[e], buf.at[slot], sems.at[slot, n])
                for n, (hbm, buf) in enumerate(((wg_hbm, wg32), (wu_hbm, wu32), (wd_hbm, wd32)))]

    def start_weights(ordinal, slot):
        for c in weight_copies(exp_list[ordinal], slot):
            c.start()

    @pl.when(i == 0)
    def _():
        for ahead in range(W_SLOTS):
            @pl.when(ahead < n_exp[0])
            def _(ahead=ahead):
                start_weights(ahead, ahead)

    @pl.when(i < n_used[0])
    def _():
        @pl.when(blk_first[i] == 1)
        def _():
            ordinal = blk_ord[i]
            slot = lax.rem(ordinal, W_SLOTS)
            for c in weight_copies(blk_e[i], slot):
                c.wait()
            wg16[...] = wg32[slot].astype(BF16)
            wu16[...] = wu32[slot].astype(BF16)
            wd16[...] = wd32[slot].astype(BF16)

            @pl.when(ordinal + W_SLOTS < n_exp[0])
            def _():
                start_weights(ordinal + W_SLOTS, slot)

        _unpack_slabs_bf16(x_ref, x16, MOE_BLK, blk_valid[i])
        x = x16[...]
        gate = jnp.dot(x, wg16[...], preferred_element_type=F32)
        up = jnp.dot(x, wu16[...], preferred_element_type=F32)
        hid = (jax.nn.silu(gate) * up).astype(BF16)
        rows32[...] = jnp.dot(hid, wd16[...], preferred_element_type=F32)
        _pack_slabs(rows32, y_ref, MOE_BLK)


def _experts(x_slabs, plan, w_gate, w_up, w_down):
    n_rows = x_slabs.shape[0] // SLAB_ROWS
    n_blocks = n_rows // MOE_BLK
    row_map = lambda i, be, bf, bv, bo, el, nu, ne: (jnp.minimum(i, nu[0] - 1), 0)
    hbm = pl.BlockSpec(memory_space=pl.ANY)
    grid_spec = pltpu.PrefetchScalarGridSpec(
        num_scalar_prefetch=7,
        grid=(n_blocks,),
        in_specs=[pl.BlockSpec((MOE_BLK * SLAB_ROWS, LANES), row_map), hbm, hbm, hbm],
        out_specs=pl.BlockSpec((MOE_BLK * SLAB_ROWS, LANES), row_map),
        scratch_shapes=[pltpu.VMEM((W_SLOTS, D_MODEL, D_EXPERT), F32), pltpu.VMEM((W_SLOTS, D_MODEL, D_EXPERT), F32),
                        pltpu.VMEM((W_SLOTS, D_EXPERT, D_MODEL), F32),
                        pltpu.VMEM((D_MODEL, D_EXPERT), BF16), pltpu.VMEM((D_MODEL, D_EXPERT), BF16),
                        pltpu.VMEM((D_EXPERT, D_MODEL), BF16),
                        pltpu.SemaphoreType.DMA((W_SLOTS, 3)),
                        pltpu.VMEM((MOE_BLK, D_MODEL), F32), pltpu.VMEM((MOE_BLK, D_MODEL), BF16)],
    )
    return pl.pallas_call(
        _expert_kernel,
        grid_spec=grid_spec,
        out_shape=jax.ShapeDtypeStruct((n_rows * SLAB_ROWS, LANES), jnp.uint32),
        compiler_params=_params("arbitrary"),
        name="moe_experts",
    )(*plan, x_slabs, w_gate, w_up, w_down)


def _combine_kernel(tok_off, tile, dest_ref, y_hbm, h_ref, gate_ref, g_ref, b_ref, out_ref, ybuf, ffn_sum, sems):
    i = pl.program_id(0)
    n_steps = pl.num_programs(0)

    def copy(step, r, k):
        slot = step & 1
        row = dest_ref[(tok_off + step * tile + r) * TOP_K + k]
        return pltpu.make_async_copy(_slab(y_hbm, row), _slab(ybuf.at[slot, k], r), sems.at[slot])

    def gather(step):
        def body(r, c):
            for k in range(TOP_K):
                copy(step, r, k).start()
            return c
        lax.fori_loop(0, tile, body, 0, unroll=DMA_UNROLL)

    @pl.when(i == 0)
    def _():
        gather(i)

    @pl.when(i + 1 < n_steps)
    def _():
        gather(i + 1)

    def wait(r, c):
        for k in range(TOP_K):
            copy(i, r, k).wait()
        return c

    lax.fori_loop(0, tile, wait, 0, unroll=DMA_UNROLL)
    slot = i & 1
    _unpack_gated_sum([ybuf.at[slot, k] for k in range(TOP_K)], gate_ref, ffn_sum, tile)
    out_ref[...] = _layer_norm(ALPHA * h_ref[...] + ffn_sum[...], g_ref[...], b_ref[...])


def _combine(y_slabs, dest_flat, h_all, gates, ln_g, ln_b, tok_off, n_tok):
    tile = COMBINE_TILE if (n_tok % COMBINE_TILE == 0 and tok_off % COMBINE_TILE == 0) else ROW_TILE
    blk_off = tok_off // tile
    grid_spec = pltpu.PrefetchScalarGridSpec(
        num_scalar_prefetch=1,
        grid=(n_tok // tile,),
        in_specs=[pl.BlockSpec(memory_space=pl.ANY),
                  pl.BlockSpec((tile, D_MODEL), lambda i, d: (i + blk_off, 0)),
                  pl.BlockSpec((tile, LANES), lambda i, d: (i + blk_off, 0)),
                  pl.BlockSpec((1, D_MODEL), lambda i, d: (0, 0)),
                  pl.BlockSpec((1, D_MODEL), lambda i, d: (0, 0))],
        out_specs=pl.BlockSpec((tile, D_MODEL), lambda i, d: (i, 0)),
        scratch_shapes=[pltpu.VMEM((2, TOP_K, tile * SLAB_ROWS, LANES), jnp.uint32),
                        pltpu.VMEM((tile, D_MODEL), F32), pltpu.SemaphoreType.DMA((2,))],
    )
    return pl.pallas_call(
        functools.partial(_combine_kernel, tok_off, tile),
        grid_spec=grid_spec,
        out_shape=jax.ShapeDtypeStruct((n_tok, D_MODEL), F32),
        compiler_params=_params("arbitrary"),
        name="combine_ln2",
    )(dest_flat, y_slabs, h_all, gates, ln_g, ln_b)


def _moe_plan(idx, counts_f):
    m = idx.shape[0]
    n_asg = m * TOP_K
    experts = idx[:, 0:TOP_K]
    ranks = idx[:, TOP_K:2 * TOP_K]
    counts = counts_f[0, :N_EXPERTS].astype(jnp.int32)
    padded = (counts + MOE_BLK - 1) // MOE_BLK * MOE_BLK
    pad_end = jnp.cumsum(padded)
    pad_start = pad_end - padded
    expert_ids = jnp.arange(N_EXPERTS, dtype=jnp.int32)
    lookup = lambda hot, table: jnp.sum(jnp.where(hot, table, 0), axis=-1)
    dest = (lookup(experts[:, :, None] == expert_ids, pad_start) + ranks).reshape(n_asg).astype(jnp.int32)
    n_blocks = -(-(n_asg + N_EXPERTS * (MOE_BLK - 1)) // MOE_BLK)
    n_used = jnp.maximum(pad_end[-1] // MOE_BLK, 1).astype(jnp.int32)
    blk = jnp.minimum(jnp.arange(n_blocks, dtype=jnp.int32), n_used - 1)
    blk_row = blk * MOE_BLK
    blk_e = jnp.minimum(jnp.sum((pad_end[None, :] <= blk_row[:, None]).astype(jnp.int32), axis=-1), N_EXPERTS - 1)
    blk_hot = blk_e[:, None] == expert_ids
    blk_off = blk_row - lookup(blk_hot, pad_start)
    blk_first = (blk_off == 0).astype(jnp.int32)
    blk_valid = jnp.clip(lookup(blk_hot, counts) - blk_off, 0, MOE_BLK).astype(jnp.int32)
    used = counts > 0
    exp_list = jnp.argsort(jnp.logical_not(used), stable=True).astype(jnp.int32)
    blk_ord = lookup(blk_hot, jnp.cumsum(used.astype(jnp.int32)) - 1).astype(jnp.int32)
    n_exp = jnp.sum(used.astype(jnp.int32)).reshape(1)
    plan = (blk_e, blk_first, blk_valid, blk_ord, exp_list, n_used.reshape(1), n_exp)
    return dest, n_blocks * MOE_BLK, plan


def kernel(x_prompt, x_sample, cache_kv_w128, cache_kv_w512, cache_kv_w2048, state_conv, w_in, conv_w, w_conv_out, w_att_out, w_o, ln1_g, ln1_b, w_router_group, b_router_group, w_router_expert, b_router_expert, w_expert_gate, w_expert_up, w_expert_down, ln2_g, ln2_b):
    bsz, seq, _ = x_prompt.shape
    bd, t_new, _ = x_sample.shape
    m_p = bsz * seq
    m_s = bd * t_new
    m_all = m_p + m_s
    caches = (cache_kv_w128, cache_kv_w512, cache_kv_w2048)
    lyr = 0

    w_in16 = w_in[lyr].astype(BF16)
    wc16 = w_conv_out[lyr].astype(BF16)
    wa16 = w_att_out[lyr].astype(BF16)
    wo16 = w_o[lyr].astype(BF16)
    cw = conv_w[lyr]
    g1, b1 = ln1_g[lyr][None, :], ln1_b[lyr][None, :]
    g2, b2 = ln2_g[lyr][None, :], ln2_b[lyr][None, :]
    w_re = jnp.transpose(w_router_expert[lyr], (1, 0, 2)).reshape(D_MODEL, N_EXPERTS)
    w_r = jnp.pad(jnp.concatenate([w_router_group[lyr], w_re], axis=1),
                  ((0, 0), (0, LANES - N_EXPERT_GROUPS - N_EXPERTS)))
    w_r_hi = w_r.astype(BF16)
    w_r = jnp.concatenate([w_r_hi, (w_r - w_r_hi.astype(F32)).astype(BF16)], axis=1)
    b_r = jnp.pad(jnp.concatenate([b_router_group[lyr], b_router_expert[lyr].reshape(N_EXPERTS)]),
                  (0, LANES - N_EXPERT_GROUPS - N_EXPERTS))[None, :]

    xp = x_prompt.reshape(m_p, D_MODEL)
    xs = x_sample.reshape(m_s, D_MODEL)

    caches3 = [c[lyr].reshape(bd, -1, HEAD_DIM) for c in caches]
    z_p, qkv_p, shifted = _proj_in(xp, w_in16, min(PROJ_TM, m_p),
                                   [c.reshape(-1, HEAD_DIM) for c in caches3], t_new * KV_ROWS)
    shifted = [s.reshape(c.shape) for s, c in zip(shifted, caches3)]
    o_p = _attn_prompt(qkv_p, bsz, seq)
    h_buf, conv_p = _mix(None, o_p, z_p, xp, wc16, wa16, wo16, g1, b1, min(MIX_TM, seq), m_all, conv=(cw, seq))

    z_s, qkv_s, _ = _proj_in(xs, w_in16, m_s)
    p_s, conv_s = _conv_sample(z_s, cw, state_conv[lyr], t_new)
    q3 = qkv_s.reshape(bd, t_new, QKV_COLS)
    kv5 = qkv_s.reshape(bd, t_new, 3, N_DIL, HEADS_PER_GROUP, HEAD_DIM)
    new_kvs = [kv5[:, :, 1:3, g].reshape(bd, t_new * KV_ROWS, HEAD_DIM) for g in range(N_DIL)]
    bufs_s, o_s = _attn_sample(q3, caches3, new_kvs, shifted)
    bufs_s = [buf.reshape(1, bd, window, 2, HEADS_PER_GROUP, HEAD_DIM)
              for buf, (window, _) in zip(bufs_s, DILATED_GROUPS)]
    h_buf = _mix(p_s, o_s.reshape(m_s, D_GRP), z_s, xs, wc16, wa16, wo16, g1, b1, m_s, m_all, h_all=h_buf, row_off=m_p)

    h_all, h_slabs = h_buf
    idx, gates, counts = _router(h_all, w_r, b_r)
    dest, n_rows, plan = _moe_plan(idx, counts)
    x_slabs = _dispatch(h_slabs, dest, n_rows)
    y_slabs = _experts(x_slabs, plan, w_expert_gate[lyr], w_expert_up[lyr], w_expert_down[lyr])
    y_p = _combine(y_slabs, dest, h_all, gates, g2, b2, 0, m_p)
    y_s = _combine(y_slabs, dest, h_all, gates, g2, b2, m_p, m_s)

    qkv_p3 = qkv_p.reshape(bsz, seq, QKV_COLS)
    bufs_p = []
    for g, (w, _) in enumerate(DILATED_GROUPS):
        keep = min(w, seq)
        bufs_p.append(_kv_pack(qkv_p3, g, keep).reshape(1, bsz, keep, 2, HEADS_PER_GROUP, HEAD_DIM))
    return (y_p.reshape(bsz, seq, D_MODEL), y_s.reshape(bd, t_new, D_MODEL),
            bufs_p[0], bufs_p[1], bufs_p[2], conv_p[None],
            bufs_s[0], bufs_s[1], bufs_s[2], conv_s[None])
```

```python
import functools

import jax
import jax.numpy as jnp
from jax import lax
from jax.experimental import pallas as pl
from jax.experimental.pallas import tpu as pltpu

F32 = jnp.float32
BF16 = jnp.bfloat16

D_MODEL = 2048
D_CONV = D_MODEL // 2
CONV_W = 3
DILATED_GROUPS = ((128, 1), (512, 4), (2048, 16))
N_DIL = len(DILATED_GROUPS)
HEADS_PER_GROUP = 4
HEAD_DIM = 128
SPAN = 128
D_GRP = HEADS_PER_GROUP * HEAD_DIM
D_ATT = N_DIL * D_GRP
N_EXPERT_GROUPS = 8
EXPERTS_PER_GROUP = 8
N_EXPERTS = N_EXPERT_GROUPS * EXPERTS_PER_GROUP
TOP_K = 2
D_EXPERT = D_MODEL // 4
DEPTH = 1
LN_EPS = 1e-5
ALPHA = (2 * DEPTH) ** 0.25
NEG_INF = -1e30
ATT_SCALE = HEAD_DIM ** -0.5

OFF_B = 0
OFF_C = D_CONV
OFF_H = 2 * D_CONV
OFF_Q = 3 * D_CONV
OFF_K = OFF_Q + D_ATT
OFF_V = OFF_K + D_ATT
OFF_GC = OFF_V + D_ATT
OFF_GA = OFF_GC + D_MODEL
N_IN = OFF_GA + D_MODEL

LANES = 128
SUBLANES = 8
VMEM_LIMIT = 56 * 1024 * 1024

PROJ_TN = 512
PROJ_TM = 1024
Z_COLS = 2 * D_MODEL + 3 * D_CONV
QKV_COLS = 3 * D_ATT
NZ_TILES = Z_COLS // PROJ_TN
NQ_TILES = QKV_COLS // PROJ_TN
MIX_TM = 256
ROUTER_TILE = 640
ROW_TILE = 128
MOE_BLK = 304
W_SLOTS = 3
ATTN_UNROLL = 8
DISPATCH_TILE = 640
DISPATCH_SLOTS = 3
COMBINE_TILE = 256
KV_PACK_TM = 512
SHIFT_CHUNK_ROWS = 4128
SHIFT_AHEAD = 3
SHIFT_SLOTS = 8
DMA_UNROLL = 8
DMA_PRIORITIES = 2


def _proj_tile_order():
    t = lambda off, width: list(range(off // PROJ_TN, (off + width) // PROJ_TN))
    z_tiles = t(OFF_GC, D_MODEL) + t(OFF_GA, D_MODEL) + t(OFF_B, D_CONV) + t(OFF_C, D_CONV) + t(OFF_H, D_CONV)
    q_tiles = t(OFF_Q, D_ATT) + t(OFF_K, D_ATT) + t(OFF_V, D_ATT)
    return z_tiles + q_tiles


def _params(*sem):
    return pltpu.CompilerParams(dimension_semantics=sem, vmem_limit_bytes=VMEM_LIMIT)


def _shift_chunks(rows, drop, max_rows):
    keep = rows - drop
    best = SUBLANES
    for c in range(SUBLANES, max_rows + 1, SUBLANES):
        if keep % c == 0:
            best = c
    return best, keep // best


def _proj_kernel(shift_plan, drop, order_ref, x_ref, w_ref, *rest):
    del order_ref
    n_shift = len(shift_plan)
    srcs = rest[:n_shift]
    z_ref, qkv_ref = rest[n_shift:n_shift + 2]
    dsts = rest[n_shift + 2:2 * n_shift + 2]
    x16 = rest[2 * n_shift + 2]
    j = pl.program_id(1)

    if n_shift:
        stage, sem_in, sem_out = rest[2 * n_shift + 3:]
        t = pl.program_id(0) * pl.num_programs(1) + j
        n_chunks = shift_plan[-1][2]

        def for_chunk(q, action):
            for n, (chunk, lo, hi) in enumerate(shift_plan):
                @pl.when(jnp.logical_and(q >= lo, q < hi))
                def _(n=n, chunk=chunk, lo=lo):
                    action(n, chunk, pl.multiple_of((q - lo) * chunk, SUBLANES), q & (SHIFT_SLOTS - 1))

        def load(n, chunk, start, slot):
            return pltpu.make_async_copy(srcs[n].at[pl.ds(start + drop, chunk), :],
                                         stage.at[slot, pl.ds(0, chunk), :], sem_in.at[slot])

        def store(n, chunk, start, slot):
            return pltpu.make_async_copy(stage.at[slot, pl.ds(0, chunk), :],
                                         dsts[n].at[pl.ds(start, chunk), :], sem_out.at[slot])

        @pl.when(t == 0)
        def _():
            for q in range(SHIFT_AHEAD):
                for_chunk(t + q, lambda *a: load(*a).start())

        @pl.when(t < n_chunks)
        def _():
            for_chunk(t, lambda *a: load(*a).wait())
            for_chunk(t, lambda *a: store(*a).start())

        @pl.when(jnp.logical_and(t >= SHIFT_AHEAD, t < n_chunks + SHIFT_AHEAD))
        def _():
            for_chunk(t - SHIFT_AHEAD, lambda *a: store(*a).wait())

        @pl.when(t + SHIFT_AHEAD < n_chunks)
        def _():
            for_chunk(t + SHIFT_AHEAD, lambda *a: load(*a).start())

    @pl.when(j == 0)
    def _():
        x16[...] = x_ref[...].astype(BF16)

    @pl.when(j < NZ_TILES)
    def _():
        z_ref[...] = jnp.dot(x16[...], w_ref[...], preferred_element_type=F32).astype(BF16)

    @pl.when(j >= NZ_TILES)
    def _():
        qkv_ref[...] = jnp.dot(x16[...], w_ref[...], preferred_element_type=F32)


def _proj_in(x, w16, tm, shift_srcs=(), drop=0):
    m = x.shape[0]
    n_shift = len(shift_srcs)
    n_steps = (m // tm) * (NZ_TILES + NQ_TILES)
    order = jnp.asarray(_proj_tile_order(), jnp.int32)
    hbm = pl.BlockSpec(memory_space=pl.ANY)
    shift_plan, first = [], 0
    for s in shift_srcs:
        chunk, count = _shift_chunks(s.shape[0], drop, SHIFT_CHUNK_ROWS)
        shift_plan.append((chunk, first, first + count))
        first += count
    assert first + SHIFT_AHEAD <= n_steps, "one shift chunk per grid step, plus the steps that drain"
    shift_scratch = [pltpu.VMEM((SHIFT_SLOTS, SHIFT_CHUNK_ROWS, LANES), F32),
                     pltpu.SemaphoreType.DMA((SHIFT_SLOTS,)), pltpu.SemaphoreType.DMA((SHIFT_SLOTS,))] if n_shift else []
    grid_spec = pltpu.PrefetchScalarGridSpec(
        num_scalar_prefetch=1,
        grid=(m // tm, NZ_TILES + NQ_TILES),
        in_specs=[
            pl.BlockSpec((tm, D_MODEL), lambda i, j, o: (i, 0)),
            pl.BlockSpec((D_MODEL, PROJ_TN), lambda i, j, o: (0, o[j])),
        ] + [hbm] * n_shift,
        out_specs=[
            pl.BlockSpec((tm, PROJ_TN), lambda i, j, o: (i, jnp.minimum(j, NZ_TILES - 1))),
            pl.BlockSpec((tm, PROJ_TN), lambda i, j, o: (i, jnp.maximum(j - NZ_TILES, 0))),
        ] + [hbm] * n_shift,
        scratch_shapes=[pltpu.VMEM((tm, D_MODEL), BF16)] + shift_scratch,
    )
    outs = pl.pallas_call(
        functools.partial(_proj_kernel, tuple(shift_plan), drop),
        grid_spec=grid_spec,
        out_shape=[jax.ShapeDtypeStruct((m, Z_COLS), BF16), jax.ShapeDtypeStruct((m, QKV_COLS), F32)]
        + [jax.ShapeDtypeStruct(s.shape, s.dtype) for s in shift_srcs],
        compiler_params=_params("arbitrary", "arbitrary"),
        name="proj_in",
    )(order, x, w16, *shift_srcs)
    return outs[0], outs[1], outs[2:]


ZB_BLK = (2 * D_MODEL) // D_CONV


def _gated_conv_tile(zb_ref, zc_ref, zh_ref, cw_ref, st_ref, uext, seq_start):
    tm = zb_ref.shape[0]

    @pl.when(seq_start)
    def _():
        uext[0:SUBLANES, :] = jnp.zeros((SUBLANES, D_CONV), F32)

    u = zc_ref[...].astype(F32) * zh_ref[...].astype(F32)
    uext[SUBLANES:SUBLANES + tm, :] = u
    conv = (cw_ref[0:1, :] * uext[SUBLANES - 2:SUBLANES - 2 + tm, :]
            + cw_ref[1:2, :] * uext[SUBLANES - 1:SUBLANES - 1 + tm, :]
            + cw_ref[2:3, :] * u)
    last = uext[tm:tm + SUBLANES, :]
    uext[0:SUBLANES, :] = last
    st_ref[0] = last[SUBLANES - 2:SUBLANES, :]
    return (zb_ref[...].astype(F32) * conv).astype(BF16)


def _conv_sample_kernel(t_new, zb_ref, zc_ref, zh_ref, cw_ref, s1_ref, s2_ref, p_ref, u_ref, uext):
    m = zb_ref.shape[0]
    u = zc_ref[...].astype(F32) * zh_ref[...].astype(F32)
    u_ref[...] = u
    uext[0:SUBLANES, :] = jnp.zeros((SUBLANES, D_CONV), F32)
    uext[SUBLANES:SUBLANES + m, :] = u
    t = lax.broadcasted_iota(jnp.int32, (m, D_CONV), 0) % t_new
    um1 = jnp.where(t < 1, s1_ref[...], uext[SUBLANES - 1:SUBLANES - 1 + m, :])
    um2 = jnp.where(t < 2, s2_ref[...], uext[SUBLANES - 2:SUBLANES - 2 + m, :])
    conv = cw_ref[0:1, :] * um2 + cw_ref[1:2, :] * um1 + cw_ref[2:3, :] * u
    p_ref[...] = (zb_ref[...].astype(F32) * conv).astype(BF16)


def _conv_sample(z, conv_w, state, t_new):
    m = z.shape[0]
    bd = m // t_new
    s1 = jnp.pad(state[:, 1:2], ((0, 0), (0, t_new - 1), (0, 0))).reshape(m, D_CONV)
    s2 = jnp.pad(state, ((0, 0), (0, t_new - 2), (0, 0))).reshape(m, D_CONV)
    zspec = lambda blk: pl.BlockSpec((m, D_CONV), lambda i: (0, blk))
    full = lambda r: pl.BlockSpec((r, D_CONV), lambda i: (0, 0))
    p, u = pl.pallas_call(
        functools.partial(_conv_sample_kernel, t_new),
        grid=(1,),
        in_specs=[zspec(ZB_BLK), zspec(ZB_BLK + 1), zspec(ZB_BLK + 2), full(CONV_W), full(m), full(m)],
        out_specs=[full(m), full(m)],
        out_shape=[jax.ShapeDtypeStruct((m, D_CONV), BF16), jax.ShapeDtypeStruct((m, D_CONV), F32)],
        scratch_shapes=[pltpu.VMEM((m + SUBLANES, D_CONV), F32)],
        compiler_params=_params("arbitrary"),
        name="conv_sample",
    )(z, z, z, conv_w, s1, s2)
    new_state = u.reshape(bd, t_new, D_CONV)[:, t_new - (CONV_W - 1):]
    return p, new_state


def _merge_groups(o_list, lse_list):
    top = functools.reduce(jnp.maximum, lse_list)
    es = [jnp.exp(l - top) for l in lse_list]
    num = functools.reduce(lambda a, b: a + b, [e * o for e, o in zip(es, o_list)])
    den = functools.reduce(lambda a, b: a + b, es)
    return num / den


def _nt_dot(a, b):
    return lax.dot_general(a, b, (((1,), (1,)), ((), ())), preferred_element_type=F32)


def _attn_prompt_group(q_ref, k_ref, v_ref, o_sc, lse_sc, g, dil, seq):
    shift = dil.bit_length() - 1
    qi = lax.broadcasted_iota(jnp.int32, (SPAN, 2 * SPAN), 0)
    kj = lax.broadcasted_iota(jnp.int32, (SPAN, 2 * SPAN), 1)
    band = jnp.logical_and(kj >= qi, kj <= qi + SPAN)
    is_prev = lax.broadcasted_iota(jnp.int32, (1, 2 * SPAN), 1) < SPAN
    ones = jnp.ones((2 * SPAN, HEAD_DIM), BF16)

    def rows(ref, start):
        if dil == 1:
            return ref[pl.ds(start, SPAN), :]
        return ref[pl.ds(start, SPAN, stride=dil), :]

    def body(trip, carry):
        blocks = range(ATTN_UNROLL)
        idx = [trip * ATTN_UNROLL + u for u in blocks]
        c = [i >> shift for i in idx]
        base = [c[u] * (SPAN * dil) + (idx[u] & (dil - 1)) for u in blocks]
        pbase = [jnp.maximum(c[u] - 1, 0) * (SPAN * dil) + (idx[u] & (dil - 1)) for u in blocks]
        q = [rows(q_ref, base[u]).astype(BF16) for u in blocks]
        k = [jnp.concatenate([rows(k_ref, pbase[u]), rows(k_ref, base[u])], axis=0).astype(BF16) for u in blocks]
        s = [_nt_dot(q[u], k[u]) for u in blocks]
        pen = [jnp.where(jnp.logical_and(is_prev, c[u] == 0), NEG_INF, 0.0) for u in blocks]
        s = [jnp.where(band, s[u] * ATT_SCALE + pen[u], NEG_INF) for u in blocks]
        m = [jnp.max(jnp.maximum(s[u][:, :SPAN], s[u][:, SPAN:]), axis=-1, keepdims=True) for u in blocks]
        p = [jnp.exp(s[u] - m[u]).astype(BF16) for u in blocks]
        v = [jnp.concatenate(
            [jnp.concatenate([rows(v_ref, pbase[u]), rows(v_ref, base[u])], axis=0).astype(BF16), ones], axis=1)
            for u in blocks]
        acc = [jnp.dot(p[u], v[u], preferred_element_type=F32) for u in blocks]
        for u in blocks:
            den = acc[u][:, HEAD_DIM:]
            o = acc[u][:, :HEAD_DIM] / den
            lse = m[u] + jnp.log(den)
            if dil == 1:
                o_sc[g, pl.ds(base[u], SPAN), :] = o
                lse_sc[g, pl.ds(base[u], SPAN), :] = lse
            else:
                o_sc[g, pl.ds(base[u], SPAN, stride=dil), :] = o
                lse_sc[g, pl.ds(base[u], SPAN, stride=dil), :] = lse
        return carry

    lax.fori_loop(0, seq // (SPAN * ATTN_UNROLL), body, 0)


def _attn_prompt_kernel(q_ref, k_ref, v_ref, o_ref, o_sc, lse_sc):
    seq = q_ref.shape[0]
    g_id = pl.program_id(2)
    for g, (_, dil) in enumerate(DILATED_GROUPS):
        @pl.when(g_id == g)
        def _(g=g, dil=dil):
            _attn_prompt_group(q_ref, k_ref, v_ref, o_sc, lse_sc, g, dil, seq)

    @pl.when(g_id == N_DIL - 1)
    def _():
        chunk = 256

        def body(i, carry):
            sl = pl.ds(pl.multiple_of(i * chunk, chunk), chunk)
            o = _merge_groups([o_sc[g, sl, :] for g in range(N_DIL)], [lse_sc[g, sl, :] for g in range(N_DIL)])
            o_ref[sl, :] = o.astype(BF16)
            return carry

        lax.fori_loop(0, seq // chunk, body, 0)


def _attn_prompt(qkv, bsz, seq):
    qkv3 = qkv.reshape(bsz, seq, QKV_COLS)
    per_sel = D_ATT // HEAD_DIM
    spec = lambda sel: pl.BlockSpec((None, seq, HEAD_DIM),
                                    lambda b, j, g: (b, 0, sel * per_sel + g * HEADS_PER_GROUP + j))
    out = pl.pallas_call(
        _attn_prompt_kernel,
        grid=(bsz, HEADS_PER_GROUP, N_DIL),
        in_specs=[spec(0), spec(1), spec(2)],
        out_specs=pl.BlockSpec((None, seq, HEAD_DIM), lambda b, j, g: (b, 0, j)),
        out_shape=jax.ShapeDtypeStruct((bsz, seq, D_GRP), BF16),
        scratch_shapes=[pltpu.VMEM((N_DIL, seq, HEAD_DIM), F32), pltpu.VMEM((N_DIL, seq, HEAD_DIM), F32)],
        compiler_params=_params("arbitrary", "arbitrary", "arbitrary"),
        name="attn_prompt",
    )(qkv3, qkv3, qkv3)
    return out.reshape(bsz * seq, D_GRP)


KV_ROWS = 2 * HEADS_PER_GROUP


def _kv_pack_kernel(k_ref, v_ref, out_ref):
    tm = k_ref.shape[0]
    for h in range(HEADS_PER_GROUP):
        cols = slice(h * HEAD_DIM, (h + 1) * HEAD_DIM)
        out_ref[pl.ds(h, tm, stride=KV_ROWS), :] = k_ref[:, cols]
        out_ref[pl.ds(HEADS_PER_GROUP + h, tm, stride=KV_ROWS), :] = v_ref[:, cols]


def _kv_pack(qkv3, g, keep):
    bsz, seq, _ = qkv3.shape
    tm = min(KV_PACK_TM, keep)
    assert keep % tm == 0 and (seq - keep) % tm == 0
    first = (seq - keep) // tm
    col = lambda sel: (sel * D_ATT + g * D_GRP) // D_GRP
    spec = lambda sel: pl.BlockSpec((None, tm, D_GRP), lambda b, s: (b, first + s, col(sel)))
    return pl.pallas_call(
        _kv_pack_kernel,
        grid=(bsz, keep // tm),
        in_specs=[spec(1), spec(2)],
        out_specs=pl.BlockSpec((None, tm * KV_ROWS, HEAD_DIM), lambda b, s: (b, s, 0)),
        out_shape=jax.ShapeDtypeStruct((bsz, keep * KV_ROWS, HEAD_DIM), F32),
        compiler_params=_params("arbitrary", "arbitrary"),
        name="kv_pack",
    )(qkv3, qkv3)


def _sample_key_bias(dil, t_new, n_rows, n_buf):
    shape = (HEADS_PER_GROUP * t_new, n_rows)
    rr = lax.broadcasted_iota(jnp.int32, shape, 0)
    cc = lax.broadcasted_iota(jnp.int32, shape, 1)
    head = rr // t_new
    back = n_buf + (rr - head * t_new) - cc // KV_ROWS
    ok = jnp.logical_and(cc % KV_ROWS == HEADS_PER_GROUP + head,
                         jnp.logical_and(jnp.logical_and(back >= 0, back <= dil * SPAN), back % dil == 0))
    return jnp.where(ok, 0.0, NEG_INF)


def _attn_sample_group(g, q_ref, cache_ref, new_ref, bias_c, bias_n, o_sc, lse_sc):
    t_new = q_ref.shape[0]
    q = jnp.concatenate([q_ref[:, g * D_GRP + h * HEAD_DIM:g * D_GRP + (h + 1) * HEAD_DIM]
                         for h in range(HEADS_PER_GROUP)], axis=0).astype(BF16)

    def keys_values(ref):
        rows = ref[...]
        tiles = rows.reshape(-1, KV_ROWS, HEAD_DIM)
        return pltpu.roll(tiles, HEADS_PER_GROUP, 1).reshape(rows.shape).astype(BF16), rows.astype(BF16)

    k_c, v_c = keys_values(cache_ref)
    k_n, v_n = keys_values(new_ref)
    s_c = _nt_dot(q, k_c) * ATT_SCALE + bias_c[...]
    s_n = _nt_dot(q, k_n) * ATT_SCALE + bias_n[...]
    m = jnp.maximum(jnp.max(s_c, axis=-1, keepdims=True), jnp.max(s_n, axis=-1, keepdims=True))
    p_c = jnp.exp(s_c - m)
    p_n = jnp.exp(s_n - m)
    den = jnp.sum(p_c, axis=-1, keepdims=True) + jnp.sum(p_n, axis=-1, keepdims=True)
    acc = (jnp.dot(p_c.astype(BF16), v_c, preferred_element_type=F32)
           + jnp.dot(p_n.astype(BF16), v_n, preferred_element_type=F32))
    o = acc / den
    lse = jnp.broadcast_to(m + jnp.log(den), o.shape)
    for h in range(HEADS_PER_GROUP):
        cols = slice(h * HEAD_DIM, (h + 1) * HEAD_DIM)
        o_sc[g, :, cols] = o[h * t_new:(h + 1) * t_new, :]
        lse_sc[g, :, cols] = lse[h * t_new:(h + 1) * t_new, :]


def _attn_sample_kernel(q_ref, *refs):
    caches, news = refs[:N_DIL], refs[N_DIL:2 * N_DIL]
    bufs = refs[3 * N_DIL:4 * N_DIL]
    o_ref = refs[4 * N_DIL]
    bias_cs = refs[4 * N_DIL + 1:5 * N_DIL + 1]
    bias_n, o_sc, lse_sc = refs[5 * N_DIL + 1:]
    t_new = q_ref.shape[0]

    @pl.when(pl.program_id(0) == 0)
    def _():
        for g, (_, dil) in enumerate(DILATED_GROUPS):
            n_rows = caches[g].shape[0]
            n_buf = n_rows // KV_ROWS
            bias_cs[g][...] = _sample_key_bias(dil, t_new, n_rows, n_buf)
            bias_n[g] = _sample_key_bias(dil, t_new, t_new * KV_ROWS, 0)

    for g in range(N_DIL):
        _attn_sample_group(g, q_ref, caches[g], news[g], bias_cs[g], bias_n.at[g], o_sc, lse_sc)
        bufs[g][...] = news[g][...]
    o_ref[...] = _merge_groups([o_sc[g] for g in range(N_DIL)], [lse_sc[g] for g in range(N_DIL)]).astype(BF16)


def _attn_sample(q3, caches, new_kvs, shifted):
    bd, t_new, _ = q3.shape
    new_rows = t_new * KV_ROWS
    row_spec = lambda r: pl.BlockSpec((None, r, HEAD_DIM), lambda b: (b, 0, 0))
    for (window, _), cache in zip(DILATED_GROUPS, caches):
        assert cache.shape[1] == window * KV_ROWS, "the cache must hold a full window"
        assert cache.shape[1] % new_rows == 0
    hbm = pl.BlockSpec(memory_space=pl.ANY)
    tail = lambda c: pl.BlockSpec((None, new_rows, HEAD_DIM), lambda b: (b, c.shape[1] // new_rows - 1, 0))
    outs = pl.pallas_call(
        _attn_sample_kernel,
        grid=(bd,),
        in_specs=[pl.BlockSpec((None, t_new, D_ATT), lambda b: (b, 0, 0))]
        + [row_spec(c.shape[1]) for c in caches] + [row_spec(new_rows)] * N_DIL + [hbm] * N_DIL,
        out_specs=[tail(c) for c in caches] + [pl.BlockSpec((None, t_new, D_GRP), lambda b: (b, 0, 0))],
        out_shape=[jax.ShapeDtypeStruct(c.shape, F32) for c in caches]
        + [jax.ShapeDtypeStruct((bd, t_new, D_GRP), BF16)],
        scratch_shapes=[pltpu.VMEM((HEADS_PER_GROUP * t_new, c.shape[1]), F32) for c in caches]
        + [pltpu.VMEM((N_DIL, HEADS_PER_GROUP * t_new, new_rows), F32),
           pltpu.VMEM((N_DIL, t_new, D_GRP), F32), pltpu.VMEM((N_DIL, t_new, D_GRP), F32)],
        input_output_aliases={1 + 2 * N_DIL + g: g for g in range(N_DIL)},
        compiler_params=_params("arbitrary"),
        name="attn_sample",
    )(q3, *caches, *new_kvs, *shifted)
    return outs[:N_DIL], outs[N_DIL]


def _layer_norm(v, g, b):
    mu = jnp.mean(v, axis=-1, keepdims=True)
    d = v - mu
    var = jnp.mean(d * d, axis=-1, keepdims=True)
    return d * lax.rsqrt(var + LN_EPS) * g + b


SLAB_ROWS = SUBLANES
HALF = D_MODEL // 2
HI_MASK = 0xFFFF0000


def _bf16_bits(v):
    return pltpu.bitcast(v.astype(BF16).astype(F32), jnp.uint32)


def _pack_slabs(src_ref, slab_ref, n_tok):
    def body(tg, carry):
        t0 = pl.multiple_of(tg * SUBLANES, SUBLANES)
        for s in range(SLAB_ROWS):
            lo = src_ref[pl.ds(t0, SUBLANES), s * LANES:(s + 1) * LANES]
            hi = src_ref[pl.ds(t0, SUBLANES), HALF + s * LANES:HALF + (s + 1) * LANES]
            word = (_bf16_bits(lo) >> 16) | _bf16_bits(hi)
            slab_ref[pl.ds(t0 * SLAB_ROWS + s, SUBLANES, stride=SLAB_ROWS), :] = word
        return carry
    lax.fori_loop(0, n_tok // SUBLANES, body, 0, unroll=2)


def _unpack_gated_sum(slab_refs, gate_ref, dst_ref, n_tok):
    def body(tg, carry):
        t0 = pl.multiple_of(tg * SUBLANES, SUBLANES)
        gates = [gate_ref[pl.ds(t0, SUBLANES), k:k + 1] for k in range(len(slab_refs))]
        for s in range(SLAB_ROWS):
            words = [ref[pl.ds(t0 * SLAB_ROWS + s, SUBLANES, stride=SLAB_ROWS), :] for ref in slab_refs]
            lo = [g * pltpu.bitcast(w << 16, F32) for g, w in zip(gates, words)]
            hi = [g * pltpu.bitcast(w & jnp.uint32(HI_MASK), F32) for g, w in zip(gates, words)]
            dst_ref[pl.ds(t0, SUBLANES), s * LANES:(s + 1) * LANES] = functools.reduce(lambda a, b: a + b, lo)
            dst_ref[pl.ds(t0, SUBLANES), HALF + s * LANES:HALF + (s + 1) * LANES] = functools.reduce(
                lambda a, b: a + b, hi)
        return carry
    lax.fori_loop(0, n_tok // SUBLANES, body, 0, unroll=4)


def _unpack_slabs_bf16(slab_ref, dst_ref, n_tok, n_valid):
    pair = 2 * SUBLANES
    sub = lax.broadcasted_iota(jnp.int32, (SUBLANES, LANES), 0)

    def body(tg, carry):
        t0 = pl.multiple_of(tg * pair, pair)
        for s in range(SLAB_ROWS):
            words = []
            for half in range(2):
                t1 = t0 + half * SUBLANES
                w = slab_ref[pl.ds(t1 * SLAB_ROWS + s, SUBLANES, stride=SLAB_ROWS), :]
                words.append(jnp.where(t1 + sub < n_valid, w, jnp.uint32(0)))
            lo = jnp.concatenate([pltpu.bitcast(w << 16, F32) for w in words], axis=0)
            hi = jnp.concatenate([pltpu.bitcast(w & jnp.uint32(HI_MASK), F32) for w in words], axis=0)
            dst_ref[pl.ds(t0, pair), s * LANES:(s + 1) * LANES] = lo.astype(BF16)
            dst_ref[pl.ds(t0, pair), HALF + s * LANES:HALF + (s + 1) * LANES] = hi.astype(BF16)
        return carry
    lax.fori_loop(0, n_tok // pair, body, 0)


def _mix_kernel(conv_steps, *refs):
    if conv_steps:
        conv_in, refs = refs[:4], refs[4:]
    else:
        p_ref, refs = refs[0], refs[1:]
    o_ref, gc_ref, ga_ref, x_ref, wc_ref, wa_ref, wo_ref, g_ref, b_ref = refs[:9]
    if conv_steps:
        h_ref, hs_ref, st_ref, uext = refs[-4:]
        p = _gated_conv_tile(*conv_in, st_ref, uext, pl.program_id(0) % conv_steps == 0)
    else:
        h_ref, hs_ref = refs[-2:]
        p = p_ref[...]
    y_conv = jnp.dot(p, wc_ref[...], preferred_element_type=F32)
    y_att = jnp.dot(o_ref[...], wa_ref[...], preferred_element_type=F32)
    merged = (jax.nn.sigmoid(gc_ref[...].astype(F32)) * y_conv
              + jax.nn.sigmoid(ga_ref[...].astype(F32)) * y_att)
    mix = jnp.dot(merged.astype(BF16), wo_ref[...], preferred_element_type=F32)
    h_ref[...] = _layer_norm(ALPHA * x_ref[...] + mix, g_ref[...], b_ref[...])
    _pack_slabs(h_ref, hs_ref, h_ref.shape[0])


def _mix(p, o, z, x, wc16, wa16, wo16, ln_g, ln_b, tm, total, h_all=None, row_off=0, conv=None):
    m = x.shape[0]
    assert row_off % tm == 0
    blk_off = row_off // tm
    row = lambda w: pl.BlockSpec((tm, w), lambda i: (i, 0))
    const = lambda a: pl.BlockSpec(a.shape, lambda i: (0, 0), pipeline_mode=pl.Buffered(1))
    out_specs = [pl.BlockSpec((tm, D_MODEL), lambda i: (i + blk_off, 0)),
                 pl.BlockSpec((tm * SLAB_ROWS, LANES), lambda i: (i + blk_off, 0))]
    out_shape = [jax.ShapeDtypeStruct((total, D_MODEL), F32),
                 jax.ShapeDtypeStruct((total * SLAB_ROWS, LANES), jnp.uint32)]
    scratch, conv_steps = [], 0
    if conv is None:
        in_specs, args = [row(D_CONV)], [p]
    else:
        conv_w, seq = conv
        assert seq % tm == 0
        conv_steps = seq // tm
        zspec = lambda blk: pl.BlockSpec((tm, D_CONV), lambda i: (i, blk))
        in_specs, args = [zspec(ZB_BLK), zspec(ZB_BLK + 1), zspec(ZB_BLK + 2), const(conv_w)], [z, z, z, conv_w]
        out_specs.append(pl.BlockSpec((1, CONV_W - 1, D_CONV), lambda i: (i // conv_steps, 0, 0)))
        out_shape.append(jax.ShapeDtypeStruct((m // seq, CONV_W - 1, D_CONV), F32))
        scratch = [pltpu.VMEM((tm + SUBLANES, D_CONV), F32)]
    in_specs += [row(D_GRP),
                 pl.BlockSpec((tm, D_MODEL), lambda i: (i, 0)), pl.BlockSpec((tm, D_MODEL), lambda i: (i, 1)),
                 row(D_MODEL), const(wc16), const(wa16), const(wo16), const(ln_g), const(ln_b)]
    args += [o, z, z, x, wc16, wa16, wo16, ln_g, ln_b]
    aliases = {}
    if h_all is not None:
        in_specs += [pl.BlockSpec(memory_space=pl.ANY)] * 2
        args += list(h_all)
        aliases = {len(args) - 2: 0, len(args) - 1: 1}
    outs = pl.pallas_call(
        functools.partial(_mix_kernel, conv_steps),
        grid=(m // tm,),
        in_specs=in_specs,
        out_specs=out_specs,
        out_shape=out_shape,
        scratch_shapes=scratch,
        input_output_aliases=aliases,
        compiler_params=_params("arbitrary"),
        name="mix_ln1",
    )(*args)
    return (tuple(outs[:2]), outs[2]) if conv is not None else tuple(outs)


def _route(h, w_ref, b_ref, carry):
    tile = h.shape[0]
    h_hi = h.astype(BF16)
    h_lo = (h - h_hi.astype(F32)).astype(BF16)
    w_both = w_ref[...]
    by_hi = jnp.dot(h_hi, w_both, preferred_element_type=F32)
    by_lo = jnp.dot(h_lo, w_both[:, :LANES], preferred_element_type=F32)
    logits = by_hi[:, :LANES] + (by_hi[:, LANES:] + by_lo) + b_ref[...]
    lane = lax.broadcasted_iota(jnp.int32, (tile, LANES), 1)
    lane_f = lane.astype(F32)
    first_max = lambda v, mx: jnp.min(jnp.where(v == mx, lane_f, float(LANES)), axis=-1, keepdims=True)

    is_grp = lane < N_EXPERT_GROUPS
    lg = jnp.where(is_grp, logits, -jnp.inf)
    mg = jnp.max(lg, axis=-1, keepdims=True)
    gsel = first_max(lg, mg).astype(jnp.int32)
    p_group = 1.0 / jnp.sum(jnp.where(is_grp, jnp.exp(logits - mg), 0.0), axis=-1, keepdims=True)

    e_lane = lane - N_EXPERT_GROUPS
    in_grp = jnp.logical_and(e_lane >= 0, (e_lane >> 3) == gsel)
    le = jnp.where(in_grp, logits, -jnp.inf)
    v1 = jnp.max(le, axis=-1, keepdims=True)
    i1 = first_max(le, v1).astype(jnp.int32)
    le2 = jnp.where(lane == i1, -jnp.inf, le)
    v2 = jnp.max(le2, axis=-1, keepdims=True)
    i2 = first_max(le2, v2).astype(jnp.int32)
    ex = jnp.exp(v2 - v1)
    g1 = p_group / (1.0 + ex)
    g2 = p_group * ex / (1.0 + ex)
    e1 = i1 - N_EXPERT_GROUPS
    e2 = i2 - N_EXPERT_GROUPS

    hot1 = lane == e1
    hot2 = lane == e2
    hot = jnp.where(jnp.logical_or(hot1, hot2), 1.0, 0.0)
    rr = lax.broadcasted_iota(jnp.int32, (tile, tile), 0)
    cc = lax.broadcasted_iota(jnp.int32, (tile, tile), 1)
    tri = jnp.where(cc < rr, 1.0, 0.0).astype(BF16)
    before = jnp.dot(tri, hot.astype(BF16), preferred_element_type=F32) + carry[0:1, :]
    r1 = jnp.sum(jnp.where(hot1, before, 0.0), axis=-1, keepdims=True).astype(jnp.int32)
    r2 = jnp.sum(jnp.where(hot2, before, 0.0), axis=-1, keepdims=True).astype(jnp.int32)
    carry[...] = carry[...] + jnp.sum(hot, axis=0, keepdims=True)

    idx = jnp.where(lane == 0, e1, jnp.where(lane == 1, e2, jnp.where(lane == 2, r1, r2)))
    return idx, jnp.where(lane == 0, g1, g2)


def _router_kernel(h_ref, w_ref, b_ref, idx_ref, gate_ref, cnt_ref, carry):
    @pl.when(pl.program_id(0) == 0)
    def _():
        carry[...] = jnp.zeros(carry.shape, F32)

    idx_ref[...], gate_ref[...] = _route(h_ref[...], w_ref, b_ref, carry)
    cnt_ref[...] = carry[...]


def _router(h_all, w_r, b_r):
    m = h_all.shape[0]
    tile = ROUTER_TILE if m % ROUTER_TILE == 0 else ROW_TILE
    row = pl.BlockSpec((tile, LANES), lambda i: (i, 0))
    return pl.pallas_call(
        _router_kernel,
        grid=(m // tile,),
        in_specs=[pl.BlockSpec((tile, D_MODEL), lambda i: (i, 0)),
                  pl.BlockSpec((D_MODEL, 2 * LANES), lambda i: (0, 0)),
                  pl.BlockSpec((1, LANES), lambda i: (0, 0))],
        out_specs=[row, row, pl.BlockSpec((SUBLANES, LANES), lambda i: (0, 0))],
        out_shape=[jax.ShapeDtypeStruct((m, LANES), jnp.int32), jax.ShapeDtypeStruct((m, LANES), F32),
                   jax.ShapeDtypeStruct((SUBLANES, LANES), F32)],
        scratch_shapes=[pltpu.VMEM((SUBLANES, LANES), F32)],
        compiler_params=_params("arbitrary"),
        name="router",
    )(h_all, w_r, b_r)


def _slab(ref, row):
    return ref.at[pl.ds(pl.multiple_of(row * SLAB_ROWS, SLAB_ROWS), SLAB_ROWS), :]


def _dispatch_kernel(tile, dest_ref, h_hbm, x_hbm, stage, sem_in, sem_rows):
    i = pl.program_id(0)
    n_steps = pl.num_programs(0)
    slot_of = lambda step: lax.rem(step, DISPATCH_SLOTS)

    def load(step):
        rows = pl.ds(pl.multiple_of(step * (tile * SLAB_ROWS), SLAB_ROWS), tile * SLAB_ROWS)
        return pltpu.make_async_copy(h_hbm.at[rows, :], stage.at[slot_of(step)], sem_in.at[slot_of(step)])

    def each_row(step, action):
        slot = slot_of(step)

        def body(r, c):
            for k in range(TOP_K):
                action(pltpu.make_async_copy(_slab(stage.at[slot], r),
                                             _slab(x_hbm, dest_ref[(step * tile + r) * TOP_K + k]),
                                             sem_rows.at[slot]), k)
            return c
        lax.fori_loop(0, tile, body, 0, unroll=DMA_UNROLL)

    @pl.when(i == 0)
    def _():
        load(i).start()

    @pl.when(i + 1 < n_steps)
    def _():
        load(i + 1).start()

    load(i).wait()
    each_row(i, lambda c, k: c.start(priority=k % DMA_PRIORITIES))

    @pl.when(i > 0)
    def _():
        each_row(i - 1, lambda c, k: c.wait())

    @pl.when(i == n_steps - 1)
    def _():
        each_row(i, lambda c, k: c.wait())


def _dispatch(h_slabs, dest_flat, n_rows):
    m = h_slabs.shape[0] // SLAB_ROWS
    tile = DISPATCH_TILE if m % DISPATCH_TILE == 0 else ROW_TILE
    grid_spec = pltpu.PrefetchScalarGridSpec(
        num_scalar_prefetch=1,
        grid=(m // tile,),
        in_specs=[pl.BlockSpec(memory_space=pl.ANY)],
        out_specs=pl.BlockSpec(memory_space=pl.ANY),
        scratch_shapes=[pltpu.VMEM((DISPATCH_SLOTS, tile * SLAB_ROWS, LANES), jnp.uint32),
                        pltpu.SemaphoreType.DMA((DISPATCH_SLOTS,)), pltpu.SemaphoreType.DMA((DISPATCH_SLOTS,))],
    )
    return pl.pallas_call(
        functools.partial(_dispatch_kernel, tile),
        grid_spec=grid_spec,
        out_shape=jax.ShapeDtypeStruct((n_rows * SLAB_ROWS, LANES), jnp.uint32),
        compiler_params=pltpu.CompilerParams(dimension_semantics=("arbitrary",), has_side_effects=True),
        name="moe_dispatch",
    )(dest_flat, h_slabs)


def _expert_kernel(blk_e, blk_first, blk_valid, blk_ord, exp_list, n_used, n_exp,
                   x_ref, wg_hbm, wu_hbm, wd_hbm, y_ref, wg32, wu32, wd32, wg16, wu16, wd16, sems, rows32, x16):
    i = pl.program_id(0)

    def weight_copies(e, slot):
        return [pltpu.make_async_copy(hbm.at[e], buf.at[slot], sems.at[slot, n])
                for n, (hbm, buf) in enumerate(((wg_hbm, wg32), (wu_hbm, wu32), (wd_hbm, wd32)))]

    def start_weights(ordinal, slot):
        for c in weight_copies(exp_list[ordinal], slot):
            c.start()

    @pl.when(i == 0)
    def _():
        for ahead in range(W_SLOTS):
            @pl.when(ahead < n_exp[0])
            def _(ahead=ahead):
                start_weights(ahead, ahead)

    @pl.when(i < n_used[0])
    def _():
        @pl.when(blk_first[i] == 1)
        def _():
            ordinal = blk_ord[i]
            slot = lax.rem(ordinal, W_SLOTS)
            for c in weight_copies(blk_e[i], slot):
                c.wait()
            wg16[...] = wg32[slot].astype(BF16)
            wu16[...] = wu32[slot].astype(BF16)
            wd16[...] = wd32[slot].astype(BF16)

            @pl.when(ordinal + W_SLOTS < n_exp[0])
            def _():
                start_weights(ordinal + W_SLOTS, slot)

        _unpack_slabs_bf16(x_ref, x16, MOE_BLK, blk_valid[i])
        x = x16[...]
        gate = jnp.dot(x, wg16[...], preferred_element_type=F32)
        up = jnp.dot(x, wu16[...], preferred_element_type=F32)
        hid = (jax.nn.silu(gate) * up).astype(BF16)
        rows32[...] = jnp.dot(hid, wd16[...], preferred_element_type=F32)
        _pack_slabs(rows32, y_ref, MOE_BLK)


def _experts(x_slabs, plan, w_gate, w_up, w_down):
    n_rows = x_slabs.shape[0] // SLAB_ROWS
    n_blocks = n_rows // MOE_BLK
    row_map = lambda i, be, bf, bv, bo, el, nu, ne: (jnp.minimum(i, nu[0] - 1), 0)
    hbm = pl.BlockSpec(memory_space=pl.ANY)
    grid_spec = pltpu.PrefetchScalarGridSpec(
        num_scalar_prefetch=7,
        grid=(n_blocks,),
        in_specs=[pl.BlockSpec((MOE_BLK * SLAB_ROWS, LANES), row_map), hbm, hbm, hbm],
        out_specs=pl.BlockSpec((MOE_BLK * SLAB_ROWS, LANES), row_map),
        scratch_shapes=[pltpu.VMEM((W_SLOTS, D_MODEL, D_EXPERT), F32), pltpu.VMEM((W_SLOTS, D_MODEL, D_EXPERT), F32),
                        pltpu.VMEM((W_SLOTS, D_EXPERT, D_MODEL), F32),
                        pltpu.VMEM((D_MODEL, D_EXPERT), BF16), pltpu.VMEM((D_MODEL, D_EXPERT), BF16),
                        pltpu.VMEM((D_EXPERT, D_MODEL), BF16),
                        pltpu.SemaphoreType.DMA((W_SLOTS, 3)),
                        pltpu.VMEM((MOE_BLK, D_MODEL), F32), pltpu.VMEM((MOE_BLK, D_MODEL), BF16)],
    )
    return pl.pallas_call(
        _expert_kernel,
        grid_spec=grid_spec,
        out_shape=jax.ShapeDtypeStruct((n_rows * SLAB_ROWS, LANES), jnp.uint32),
        compiler_params=_params("arbitrary"),
        name="moe_experts",
    )(*plan, x_slabs, w_gate, w_up, w_down)


def _combine_kernel(tok_off, tile, dest_ref, y_hbm, h_ref, gate_ref, g_ref, b_ref, out_ref, ybuf, ffn_sum, sems):
    i = pl.program_id(0)
    n_steps = pl.num_programs(0)

    def copy(step, r, k):
        slot = step & 1
        row = dest_ref[(tok_off + step * tile + r) * TOP_K + k]
        return pltpu.make_async_copy(_slab(y_hbm, row), _slab(ybuf.at[slot, k], r), sems.at[slot])

    def gather(step):
        def body(r, c):
            for k in range(TOP_K):
                copy(step, r, k).start(priority=DMA_PRIORITIES - 1)
            return c
        lax.fori_loop(0, tile, body, 0, unroll=DMA_UNROLL)

    @pl.when(i == 0)
    def _():
        gather(i)

    @pl.when(i + 1 < n_steps)
    def _():
        gather(i + 1)

    def wait(r, c):
        for k in range(TOP_K):
            copy(i, r, k).wait()
        return c

    lax.fori_loop(0, tile, wait, 0, unroll=DMA_UNROLL)
    slot = i & 1
    _unpack_gated_sum([ybuf.at[slot, k] for k in range(TOP_K)], gate_ref, ffn_sum, tile)
    out_ref[...] = _layer_norm(ALPHA * h_ref[...] + ffn_sum[...], g_ref[...], b_ref[...])


def _combine(y_slabs, dest_flat, h_all, gates, ln_g, ln_b, tok_off, n_tok):
    tile = COMBINE_TILE if (n_tok % COMBINE_TILE == 0 and tok_off % COMBINE_TILE == 0) else ROW_TILE
    blk_off = tok_off // tile
    grid_spec = pltpu.PrefetchScalarGridSpec(
        num_scalar_prefetch=1,
        grid=(n_tok // tile,),
        in_specs=[pl.BlockSpec(memory_space=pl.ANY),
                  pl.BlockSpec((tile, D_MODEL), lambda i, d: (i + blk_off, 0)),
                  pl.BlockSpec((tile, LANES), lambda i, d: (i + blk_off, 0)),
                  pl.BlockSpec((1, D_MODEL), lambda i, d: (0, 0)),
                  pl.BlockSpec((1, D_MODEL), lambda i, d: (0, 0))],
        out_specs=pl.BlockSpec((tile, D_MODEL), lambda i, d: (i, 0)),
        scratch_shapes=[pltpu.VMEM((2, TOP_K, tile * SLAB_ROWS, LANES), jnp.uint32),
                        pltpu.VMEM((tile, D_MODEL), F32), pltpu.SemaphoreType.DMA((2,))],
    )
    return pl.pallas_call(
        functools.partial(_combine_kernel, tok_off, tile),
        grid_spec=grid_spec,
        out_shape=jax.ShapeDtypeStruct((n_tok, D_MODEL), F32),
        compiler_params=_params("arbitrary"),
        name="combine_ln2",
    )(dest_flat, y_slabs, h_all, gates, ln_g, ln_b)


def _moe_plan(idx, counts_f):
    m = idx.shape[0]
    n_asg = m * TOP_K
    experts = idx[:, 0:TOP_K]
    ranks = idx[:, TOP_K:2 * TOP_K]
    counts = counts_f[0, :N_EXPERTS].astype(jnp.int32)
    padded = (counts + MOE_BLK - 1) // MOE_BLK * MOE_BLK
    pad_end = jnp.cumsum(padded)
    pad_start = pad_end - padded
    expert_ids = jnp.arange(N_EXPERTS, dtype=jnp.int32)
    lookup = lambda hot, table: jnp.sum(jnp.where(hot, table, 0), axis=-1)
    dest = (lookup(experts[:, :, None] == expert_ids, pad_start) + ranks).reshape(n_asg).astype(jnp.int32)
    n_blocks = -(-(n_asg + N_EXPERTS * (MOE_BLK - 1)) // MOE_BLK)
    n_used = jnp.maximum(pad_end[-1] // MOE_BLK, 1).astype(jnp.int32)
    blk = jnp.minimum(jnp.arange(n_blocks, dtype=jnp.int32), n_used - 1)
    blk_row = blk * MOE_BLK
    blk_e = jnp.minimum(jnp.sum((pad_end[None, :] <= blk_row[:, None]).astype(jnp.int32), axis=-1), N_EXPERTS - 1)
    blk_hot = blk_e[:, None] == expert_ids
    blk_off = blk_row - lookup(blk_hot, pad_start)
    blk_first = (blk_off == 0).astype(jnp.int32)
    blk_valid = jnp.clip(lookup(blk_hot, counts) - blk_off, 0, MOE_BLK).astype(jnp.int32)
    used = counts > 0
    exp_list = jnp.argsort(jnp.logical_not(used), stable=True).astype(jnp.int32)
    blk_ord = lookup(blk_hot, jnp.cumsum(used.astype(jnp.int32)) - 1).astype(jnp.int32)
    n_exp = jnp.sum(used.astype(jnp.int32)).reshape(1)
    plan = (blk_e, blk_first, blk_valid, blk_ord, exp_list, n_used.reshape(1), n_exp)
    return dest, n_blocks * MOE_BLK, plan


def kernel(x_prompt, x_sample, cache_kv_w128, cache_kv_w512, cache_kv_w2048, state_conv, w_in, conv_w, w_conv_out, w_att_out, w_o, ln1_g, ln1_b, w_router_group, b_router_group, w_router_expert, b_router_expert, w_expert_gate, w_expert_up, w_expert_down, ln2_g, ln2_b):
    bsz, seq, _ = x_prompt.shape
    bd, t_new, _ = x_sample.shape
    m_p = bsz * seq
    m_s = bd * t_new
    m_all = m_p + m_s
    caches = (cache_kv_w128, cache_kv_w512, cache_kv_w2048)
    lyr = 0

    w_in16 = w_in[lyr].astype(BF16)
    wc16 = w_conv_out[lyr].astype(BF16)
    wa16 = w_att_out[lyr].astype(BF16)
    wo16 = w_o[lyr].astype(BF16)
    cw = conv_w[lyr]
    g1, b1 = ln1_g[lyr][None, :], ln1_b[lyr][None, :]
    g2, b2 = ln2_g[lyr][None, :], ln2_b[lyr][None, :]
    w_re = jnp.transpose(w_router_expert[lyr], (1, 0, 2)).reshape(D_MODEL, N_EXPERTS)
    w_r = jnp.pad(jnp.concatenate([w_router_group[lyr], w_re], axis=1),
                  ((0, 0), (0, LANES - N_EXPERT_GROUPS - N_EXPERTS)))
    w_r_hi = w_r.astype(BF16)
    w_r = jnp.concatenate([w_r_hi, (w_r - w_r_hi.astype(F32)).astype(BF16)], axis=1)
    b_r = jnp.pad(jnp.concatenate([b_router_group[lyr], b_router_expert[lyr].reshape(N_EXPERTS)]),
                  (0, LANES - N_EXPERT_GROUPS - N_EXPERTS))[None, :]

    xp = x_prompt.reshape(m_p, D_MODEL)
    xs = x_sample.reshape(m_s, D_MODEL)

    caches3 = [c[lyr].reshape(bd, -1, HEAD_DIM) for c in caches]
    z_p, qkv_p, shifted = _proj_in(xp, w_in16, min(PROJ_TM, m_p),
                                   [c.reshape(-1, HEAD_DIM) for c in caches3], t_new * KV_ROWS)
    shifted = [s.reshape(c.shape) for s, c in zip(shifted, caches3)]
    o_p = _attn_prompt(qkv_p, bsz, seq)
    h_buf, conv_p = _mix(None, o_p, z_p, xp, wc16, wa16, wo16, g1, b1, min(MIX_TM, seq), m_all, conv=(cw, seq))

    z_s, qkv_s, _ = _proj_in(xs, w_in16, m_s)
    p_s, conv_s = _conv_sample(z_s, cw, state_conv[lyr], t_new)
    q3 = qkv_s.reshape(bd, t_new, QKV_COLS)
    kv5 = qkv_s.reshape(bd, t_new, 3, N_DIL, HEADS_PER_GROUP, HEAD_DIM)
    new_kvs = [kv5[:, :, 1:3, g].reshape(bd, t_new * KV_ROWS, HEAD_DIM) for g in range(N_DIL)]
    bufs_s, o_s = _attn_sample(q3, caches3, new_kvs, shifted)
    bufs_s = [buf.reshape(1, bd, window, 2, HEADS_PER_GROUP, HEAD_DIM)
              for buf, (window, _) in zip(bufs_s, DILATED_GROUPS)]
    h_buf = _mix(p_s, o_s.reshape(m_s, D_GRP), z_s, xs, wc16, wa16, wo16, g1, b1, m_s, m_all, h_all=h_buf, row_off=m_p)

    h_all, h_slabs = h_buf
    idx, gates, counts = _router(h_all, w_r, b_r)
    dest, n_rows, plan = _moe_plan(idx, counts)
    x_slabs = _dispatch(h_slabs, dest, n_rows)
    y_slabs = _experts(x_slabs, plan, w_expert_gate[lyr], w_expert_up[lyr], w_expert_down[lyr])
    y_p = _combine(y_slabs, dest, h_all, gates, g2, b2, 0, m_p)
    y_s = _combine(y_slabs, dest, h_all, gates, g2, b2, m_p, m_s)

    qkv_p3 = qkv_p.reshape(bsz, seq, QKV_COLS)
    bufs_p = []
    for g, (w, _) in enumerate(DILATED_GROUPS):
        keep = min(w, seq)
        bufs_p.append(_kv_pack(qkv_p3, g, keep).reshape(1, bsz, keep, 2, HEADS_PER_GROUP, HEAD_DIM))
    return (y_p.reshape(bsz, seq, D_MODEL), y_s.reshape(bd, t_new, D_MODEL),
            bufs_p[0], bufs_p[1], bufs_p[2], conv_p[None],
            bufs_s[0], bufs_s[1], bufs_s[2], conv_s[None])
```

```python
import functools

import jax
import jax.numpy as jnp
from jax import lax
from jax.experimental import pallas as pl
from jax.experimental.pallas import tpu as pltpu

F32 = jnp.float32
BF16 = jnp.bfloat16

D_MODEL = 2048
D_CONV = D_MODEL // 2
CONV_W = 3
DILATED_GROUPS = ((128, 1), (512, 4), (2048, 16))
N_DIL = len(DILATED_GROUPS)
HEADS_PER_GROUP = 4
HEAD_DIM = 128
SPAN = 128
D_GRP = HEADS_PER_GROUP * HEAD_DIM
D_ATT = N_DIL * D_GRP
N_EXPERT_GROUPS = 8
EXPERTS_PER_GROUP = 8
N_EXPERTS = N_EXPERT_GROUPS * EXPERTS_PER_GROUP
TOP_K = 2
D_EXPERT = D_MODEL // 4
DEPTH = 1
LN_EPS = 1e-5
ALPHA = (2 * DEPTH) ** 0.25
NEG_INF = -1e30
ATT_SCALE = HEAD_DIM ** -0.5

OFF_B = 0
OFF_C = D_CONV
OFF_H = 2 * D_CONV
OFF_Q = 3 * D_CONV
OFF_K = OFF_Q + D_ATT
OFF_V = OFF_K + D_ATT
OFF_GC = OFF_V + D_ATT
OFF_GA = OFF_GC + D_MODEL
N_IN = OFF_GA + D_MODEL

LANES = 128
SUBLANES = 8
VMEM_LIMIT = 56 * 1024 * 1024

PROJ_TN = 512
PROJ_TM = 1024
Z_COLS = 2 * D_MODEL + 3 * D_CONV
QKV_COLS = 3 * D_ATT
NZ_TILES = Z_COLS // PROJ_TN
NQ_TILES = QKV_COLS // PROJ_TN
MIX_TM = 256
ROUTER_TILE = 640
ROW_TILE = 128
MOE_BLK = 304
W_SLOTS = 3
ATTN_UNROLL = 8
DISPATCH_TILE = 640
DISPATCH_SLOTS = 3
COMBINE_TILE = 256
KV_PACK_TM = 512
SHIFT_CHUNK_ROWS = 4128
SHIFT_AHEAD = 2
SHIFT_SLOTS = 2 * SHIFT_AHEAD
DMA_UNROLL = 8
DMA_PRIORITIES = 2


def _proj_tile_order():
    t = lambda off, width: list(range(off // PROJ_TN, (off + width) // PROJ_TN))
    z_tiles = t(OFF_GC, D_MODEL) + t(OFF_GA, D_MODEL) + t(OFF_B, D_CONV) + t(OFF_C, D_CONV) + t(OFF_H, D_CONV)
    q_tiles = t(OFF_Q, D_ATT) + t(OFF_K, D_ATT) + t(OFF_V, D_ATT)
    return z_tiles + q_tiles


def _params(*sem):
    return pltpu.CompilerParams(dimension_semantics=sem, vmem_limit_bytes=VMEM_LIMIT)


def _shift_chunks(rows, drop, max_rows):
    keep = rows - drop
    best = SUBLANES
    for c in range(SUBLANES, max_rows + 1, SUBLANES):
        if keep % c == 0:
            best = c
    return best, keep // best


def _proj_kernel(shift_plan, drop, order_ref, x_ref, w_ref, *rest):
    del order_ref
    n_shift = len(shift_plan)
    srcs = rest[:n_shift]
    z_ref, qkv_ref = rest[n_shift:n_shift + 2]
    dsts = rest[n_shift + 2:2 * n_shift + 2]
    x16 = rest[2 * n_shift + 2]
    j = pl.program_id(1)

    if n_shift:
        stage, sem_in, sem_out = rest[2 * n_shift + 3:]
        t = pl.program_id(0) * pl.num_programs(1) + j
        n_chunks = shift_plan[-1][2]

        def for_chunk(q, action):
            for n, (chunk, lo, hi) in enumerate(shift_plan):
                @pl.when(jnp.logical_and(q >= lo, q < hi))
                def _(n=n, chunk=chunk, lo=lo):
                    action(n, chunk, pl.multiple_of((q - lo) * chunk, SUBLANES), q & (SHIFT_SLOTS - 1))

        def load(n, chunk, start, slot):
            return pltpu.make_async_copy(srcs[n].at[pl.ds(start + drop, chunk), :],
                                         stage.at[slot, pl.ds(0, chunk), :], sem_in.at[slot])

        def store(n, chunk, start, slot):
            return pltpu.make_async_copy(stage.at[slot, pl.ds(0, chunk), :],
                                         dsts[n].at[pl.ds(start, chunk), :], sem_out.at[slot])

        @pl.when(t == 0)
        def _():
            for q in range(SHIFT_AHEAD):
                for_chunk(t + q, lambda *a: load(*a).start())

        @pl.when(t < n_chunks)
        def _():
            for_chunk(t, lambda *a: load(*a).wait())
            for_chunk(t, lambda *a: store(*a).start())

        @pl.when(jnp.logical_and(t >= SHIFT_AHEAD, t < n_chunks + SHIFT_AHEAD))
        def _():
            for_chunk(t - SHIFT_AHEAD, lambda *a: store(*a).wait())

        @pl.when(t + SHIFT_AHEAD < n_chunks)
        def _():
            for_chunk(t + SHIFT_AHEAD, lambda *a: load(*a).start())

    @pl.when(j == 0)
    def _():
        x16[...] = x_ref[...].astype(BF16)

    @pl.when(j < NZ_TILES)
    def _():
        z_ref[...] = jnp.dot(x16[...], w_ref[...].astype(BF16), preferred_element_type=F32).astype(BF16)

    @pl.when(j >= NZ_TILES)
    def _():
        qkv_ref[...] = jnp.dot(x16[...], w_ref[...].astype(BF16), preferred_element_type=F32)


def _proj_in(x, w16, tm, shift_srcs=(), drop=0):
    m = x.shape[0]
    n_shift = len(shift_srcs)
    n_steps = (m // tm) * (NZ_TILES + NQ_TILES)
    order = jnp.asarray(_proj_tile_order(), jnp.int32)
    hbm = pl.BlockSpec(memory_space=pl.ANY)
    shift_plan, first = [], 0
    for s in shift_srcs:
        chunk, count = _shift_chunks(s.shape[0], drop, SHIFT_CHUNK_ROWS)
        shift_plan.append((chunk, first, first + count))
        first += count
    assert first + SHIFT_AHEAD <= n_steps, "one shift chunk per grid step, plus the steps that drain"
    shift_scratch = [pltpu.VMEM((SHIFT_SLOTS, SHIFT_CHUNK_ROWS, LANES), F32),
                     pltpu.SemaphoreType.DMA((SHIFT_SLOTS,)), pltpu.SemaphoreType.DMA((SHIFT_SLOTS,))] if n_shift else []
    grid_spec = pltpu.PrefetchScalarGridSpec(
        num_scalar_prefetch=1,
        grid=(m // tm, NZ_TILES + NQ_TILES),
        in_specs=[
            pl.BlockSpec((tm, D_MODEL), lambda i, j, o: (i, 0)),
            pl.BlockSpec((D_MODEL, PROJ_TN), lambda i, j, o: (0, o[j])),
        ] + [hbm] * n_shift,
        out_specs=[
            pl.BlockSpec((tm, PROJ_TN), lambda i, j, o: (i, jnp.minimum(j, NZ_TILES - 1))),
            pl.BlockSpec((tm, PROJ_TN), lambda i, j, o: (i, jnp.maximum(j - NZ_TILES, 0))),
        ] + [hbm] * n_shift,
        scratch_shapes=[pltpu.VMEM((tm, D_MODEL), BF16)] + shift_scratch,
    )
    outs = pl.pallas_call(
        functools.partial(_proj_kernel, tuple(shift_plan), drop),
        grid_spec=grid_spec,
        out_shape=[jax.ShapeDtypeStruct((m, Z_COLS), BF16), jax.ShapeDtypeStruct((m, QKV_COLS), F32)]
        + [jax.ShapeDtypeStruct(s.shape, s.dtype) for s in shift_srcs],
        compiler_params=_params("arbitrary", "arbitrary"),
        name="proj_in",
    )(order, x, w16, *shift_srcs)
    return outs[0], outs[1], outs[2:]


ZB_BLK = (2 * D_MODEL) // D_CONV


def _gated_conv_tile(zb_ref, zc_ref, zh_ref, cw_ref, st_ref, uext, seq_start):
    tm = zb_ref.shape[0]

    @pl.when(seq_start)
    def _():
        uext[0:SUBLANES, :] = jnp.zeros((SUBLANES, D_CONV), F32)

    u = zc_ref[...].astype(F32) * zh_ref[...].astype(F32)
    uext[SUBLANES:SUBLANES + tm, :] = u
    conv = (cw_ref[0:1, :] * uext[SUBLANES - 2:SUBLANES - 2 + tm, :]
            + cw_ref[1:2, :] * uext[SUBLANES - 1:SUBLANES - 1 + tm, :]
            + cw_ref[2:3, :] * u)
    last = uext[tm:tm + SUBLANES, :]
    uext[0:SUBLANES, :] = last
    st_ref[0] = last[SUBLANES - 2:SUBLANES, :]
    return (zb_ref[...].astype(F32) * conv).astype(BF16)


def _conv_sample_kernel(t_new, zb_ref, zc_ref, zh_ref, cw_ref, s1_ref, s2_ref, p_ref, u_ref, uext):
    m = zb_ref.shape[0]
    u = zc_ref[...].astype(F32) * zh_ref[...].astype(F32)
    u_ref[...] = u
    uext[0:SUBLANES, :] = jnp.zeros((SUBLANES, D_CONV), F32)
    uext[SUBLANES:SUBLANES + m, :] = u
    t = lax.broadcasted_iota(jnp.int32, (m, D_CONV), 0) % t_new
    um1 = jnp.where(t < 1, s1_ref[...], uext[SUBLANES - 1:SUBLANES - 1 + m, :])
    um2 = jnp.where(t < 2, s2_ref[...], uext[SUBLANES - 2:SUBLANES - 2 + m, :])
    conv = cw_ref[0:1, :] * um2 + cw_ref[1:2, :] * um1 + cw_ref[2:3, :] * u
    p_ref[...] = (zb_ref[...].astype(F32) * conv).astype(BF16)


def _conv_sample(z, conv_w, state, t_new):
    m = z.shape[0]
    bd = m // t_new
    s1 = jnp.pad(state[:, 1:2], ((0, 0), (0, t_new - 1), (0, 0))).reshape(m, D_CONV)
    s2 = jnp.pad(state, ((0, 0), (0, t_new - 2), (0, 0))).reshape(m, D_CONV)
    zspec = lambda blk: pl.BlockSpec((m, D_CONV), lambda i: (0, blk))
    full = lambda r: pl.BlockSpec((r, D_CONV), lambda i: (0, 0))
    p, u = pl.pallas_call(
        functools.partial(_conv_sample_kernel, t_new),
        grid=(1,),
        in_specs=[zspec(ZB_BLK), zspec(ZB_BLK + 1), zspec(ZB_BLK + 2), full(CONV_W), full(m), full(m)],
        out_specs=[full(m), full(m)],
        out_shape=[jax.ShapeDtypeStruct((m, D_CONV), BF16), jax.ShapeDtypeStruct((m, D_CONV), F32)],
        scratch_shapes=[pltpu.VMEM((m + SUBLANES, D_CONV), F32)],
        compiler_params=_params("arbitrary"),
        name="conv_sample",
    )(z, z, z, conv_w, s1, s2)
    new_state = u.reshape(bd, t_new, D_CONV)[:, t_new - (CONV_W - 1):]
    return p, new_state


def _merge_groups(o_list, lse_list):
    top = functools.reduce(jnp.maximum, lse_list)
    es = [jnp.exp(l - top) for l in lse_list]
    num = functools.reduce(lambda a, b: a + b, [e * o for e, o in zip(es, o_list)])
    den = functools.reduce(lambda a, b: a + b, es)
    return num / den


def _nt_dot(a, b):
    return lax.dot_general(a, b, (((1,), (1,)), ((), ())), preferred_element_type=F32)


def _attn_prompt_group(q_ref, k_ref, v_ref, o_sc, lse_sc, g, dil, seq):
    shift = dil.bit_length() - 1
    qi = lax.broadcasted_iota(jnp.int32, (SPAN, 2 * SPAN), 0)
    kj = lax.broadcasted_iota(jnp.int32, (SPAN, 2 * SPAN), 1)
    band = jnp.logical_and(kj >= qi, kj <= qi + SPAN)
    is_prev = lax.broadcasted_iota(jnp.int32, (1, 2 * SPAN), 1) < SPAN
    ones = jnp.ones((2 * SPAN, HEAD_DIM), BF16)

    def rows(ref, start):
        if dil == 1:
            return ref[pl.ds(start, SPAN), :]
        return ref[pl.ds(start, SPAN, stride=dil), :]

    def body(trip, carry):
        blocks = range(ATTN_UNROLL)
        idx = [trip * ATTN_UNROLL + u for u in blocks]
        c = [i >> shift for i in idx]
        base = [c[u] * (SPAN * dil) + (idx[u] & (dil - 1)) for u in blocks]
        pbase = [jnp.maximum(c[u] - 1, 0) * (SPAN * dil) + (idx[u] & (dil - 1)) for u in blocks]
        q = [rows(q_ref, base[u]).astype(BF16) for u in blocks]
        k = [jnp.concatenate([rows(k_ref, pbase[u]), rows(k_ref, base[u])], axis=0).astype(BF16) for u in blocks]
        s = [_nt_dot(q[u], k[u]) for u in blocks]
        pen = [jnp.where(jnp.logical_and(is_prev, c[u] == 0), NEG_INF, 0.0) for u in blocks]
        s = [jnp.where(band, s[u] * ATT_SCALE + pen[u], NEG_INF) for u in blocks]
        m = [jnp.max(jnp.maximum(s[u][:, :SPAN], s[u][:, SPAN:]), axis=-1, keepdims=True) for u in blocks]
        p = [jnp.exp(s[u] - m[u]).astype(BF16) for u in blocks]
        v = [jnp.concatenate(
            [jnp.concatenate([rows(v_ref, pbase[u]), rows(v_ref, base[u])], axis=0).astype(BF16), ones], axis=1)
            for u in blocks]
        acc = [jnp.dot(p[u], v[u], preferred_element_type=F32) for u in blocks]
        for u in blocks:
            den = acc[u][:, HEAD_DIM:]
            o = acc[u][:, :HEAD_DIM] / den
            lse = m[u] + jnp.log(den)
            if dil == 1:
                o_sc[g, pl.ds(base[u], SPAN), :] = o
                lse_sc[g, pl.ds(base[u], SPAN), :] = lse
            else:
                o_sc[g, pl.ds(base[u], SPAN, stride=dil), :] = o
                lse_sc[g, pl.ds(base[u], SPAN, stride=dil), :] = lse
        return carry

    lax.fori_loop(0, seq // (SPAN * ATTN_UNROLL), body, 0)


def _attn_prompt_kernel(q_ref, k_ref, v_ref, o_ref, o_sc, lse_sc):
    seq = q_ref.shape[0]
    g_id = pl.program_id(2)
    for g, (_, dil) in enumerate(DILATED_GROUPS):
        @pl.when(g_id == g)
        def _(g=g, dil=dil):
            _attn_prompt_group(q_ref, k_ref, v_ref, o_sc, lse_sc, g, dil, seq)

    @pl.when(g_id == N_DIL - 1)
    def _():
        chunk = 256

        def body(i, carry):
            sl = pl.ds(pl.multiple_of(i * chunk, chunk), chunk)
            o = _merge_groups([o_sc[g, sl, :] for g in range(N_DIL)], [lse_sc[g, sl, :] for g in range(N_DIL)])
            o_ref[sl, :] = o.astype(BF16)
            return carry

        lax.fori_loop(0, seq // chunk, body, 0)


def _attn_prompt(qkv, bsz, seq):
    qkv3 = qkv.reshape(bsz, seq, QKV_COLS)
    per_sel = D_ATT // HEAD_DIM
    spec = lambda sel: pl.BlockSpec((None, seq, HEAD_DIM),
                                    lambda b, j, g: (b, 0, sel * per_sel + g * HEADS_PER_GROUP + j))
    out = pl.pallas_call(
        _attn_prompt_kernel,
        grid=(bsz, HEADS_PER_GROUP, N_DIL),
        in_specs=[spec(0), spec(1), spec(2)],
        out_specs=pl.BlockSpec((None, seq, HEAD_DIM), lambda b, j, g: (b, 0, j)),
        out_shape=jax.ShapeDtypeStruct((bsz, seq, D_GRP), BF16),
        scratch_shapes=[pltpu.VMEM((N_DIL, seq, HEAD_DIM), F32), pltpu.VMEM((N_DIL, seq, HEAD_DIM), F32)],
        compiler_params=_params("arbitrary", "arbitrary", "arbitrary"),
        name="attn_prompt",
    )(qkv3, qkv3, qkv3)
    return out.reshape(bsz * seq, D_GRP)


KV_ROWS = 2 * HEADS_PER_GROUP


def _kv_pack_kernel(k_ref, v_ref, out_ref):
    tm = k_ref.shape[0]
    for h in range(HEADS_PER_GROUP):
        cols = slice(h * HEAD_DIM, (h + 1) * HEAD_DIM)
        out_ref[pl.ds(h, tm, stride=KV_ROWS), :] = k_ref[:, cols]
        out_ref[pl.ds(HEADS_PER_GROUP + h, tm, stride=KV_ROWS), :] = v_ref[:, cols]


def _kv_pack(qkv3, g, keep):
    bsz, seq, _ = qkv3.shape
    tm = min(KV_PACK_TM, keep)
    assert keep % tm == 0 and (seq - keep) % tm == 0
    first = (seq - keep) // tm
    col = lambda sel: (sel * D_ATT + g * D_GRP) // D_GRP
    spec = lambda sel: pl.BlockSpec((None, tm, D_GRP), lambda b, s: (b, first + s, col(sel)))
    return pl.pallas_call(
        _kv_pack_kernel,
        grid=(bsz, keep // tm),
        in_specs=[spec(1), spec(2)],
        out_specs=pl.BlockSpec((None, tm * KV_ROWS, HEAD_DIM), lambda b, s: (b, s, 0)),
        out_shape=jax.ShapeDtypeStruct((bsz, keep * KV_ROWS, HEAD_DIM), F32),
        compiler_params=_params("arbitrary", "arbitrary"),
        name="kv_pack",
    )(qkv3, qkv3)


def _sample_key_bias(dil, t_new, n_rows, n_buf):
    shape = (HEADS_PER_GROUP * t_new, n_rows)
    rr = lax.broadcasted_iota(jnp.int32, shape, 0)
    cc = lax.broadcasted_iota(jnp.int32, shape, 1)
    head = rr // t_new
    back = n_buf + (rr - head * t_new) - cc // KV_ROWS
    ok = jnp.logical_and(cc % KV_ROWS == HEADS_PER_GROUP + head,
                         jnp.logical_and(jnp.logical_and(back >= 0, back <= dil * SPAN), back % dil == 0))
    return jnp.where(ok, 0.0, NEG_INF)


def _attn_sample_group(g, q_ref, cache_ref, new_ref, bias_c, bias_n, o_sc, lse_sc):
    t_new = q_ref.shape[0]
    q = jnp.concatenate([q_ref[:, g * D_GRP + h * HEAD_DIM:g * D_GRP + (h + 1) * HEAD_DIM]
                         for h in range(HEADS_PER_GROUP)], axis=0).astype(BF16)

    def keys_values(ref):
        rows = ref[...]
        tiles = rows.reshape(-1, KV_ROWS, HEAD_DIM)
        return pltpu.roll(tiles, HEADS_PER_GROUP, 1).reshape(rows.shape).astype(BF16), rows.astype(BF16)

    k_c, v_c = keys_values(cache_ref)
    k_n, v_n = keys_values(new_ref)
    s_c = _nt_dot(q, k_c) * ATT_SCALE + bias_c[...]
    s_n = _nt_dot(q, k_n) * ATT_SCALE + bias_n[...]
    m = jnp.maximum(jnp.max(s_c, axis=-1, keepdims=True), jnp.max(s_n, axis=-1, keepdims=True))
    p_c = jnp.exp(s_c - m)
    p_n = jnp.exp(s_n - m)
    den = jnp.sum(p_c, axis=-1, keepdims=True) + jnp.sum(p_n, axis=-1, keepdims=True)
    acc = (jnp.dot(p_c.astype(BF16), v_c, preferred_element_type=F32)
           + jnp.dot(p_n.astype(BF16), v_n, preferred_element_type=F32))
    o = acc / den
    lse = jnp.broadcast_to(m + jnp.log(den), o.shape)
    for h in range(HEADS_PER_GROUP):
        cols = slice(h * HEAD_DIM, (h + 1) * HEAD_DIM)
        o_sc[g, :, cols] = o[h * t_new:(h + 1) * t_new, :]
        lse_sc[g, :, cols] = lse[h * t_new:(h + 1) * t_new, :]


def _attn_sample_kernel(q_ref, *refs):
    caches, news = refs[:N_DIL], refs[N_DIL:2 * N_DIL]
    bufs = refs[3 * N_DIL:4 * N_DIL]
    o_ref = refs[4 * N_DIL]
    bias_cs = refs[4 * N_DIL + 1:5 * N_DIL + 1]
    bias_n, o_sc, lse_sc = refs[5 * N_DIL + 1:]
    t_new = q_ref.shape[0]

    @pl.when(pl.program_id(0) == 0)
    def _():
        for g, (_, dil) in enumerate(DILATED_GROUPS):
            n_rows = caches[g].shape[0]
            n_buf = n_rows // KV_ROWS
            bias_cs[g][...] = _sample_key_bias(dil, t_new, n_rows, n_buf)
            bias_n[g] = _sample_key_bias(dil, t_new, t_new * KV_ROWS, 0)

    for g in range(N_DIL):
        _attn_sample_group(g, q_ref, caches[g], news[g], bias_cs[g], bias_n.at[g], o_sc, lse_sc)
        bufs[g][...] = news[g][...]
    o_ref[...] = _merge_groups([o_sc[g] for g in range(N_DIL)], [lse_sc[g] for g in range(N_DIL)]).astype(BF16)


def _attn_sample(q3, caches, new_kvs, shifted):
    bd, t_new, _ = q3.shape
    new_rows = t_new * KV_ROWS
    row_spec = lambda r: pl.BlockSpec((None, r, HEAD_DIM), lambda b: (b, 0, 0))
    for (window, _), cache in zip(DILATED_GROUPS, caches):
        assert cache.shape[1] == window * KV_ROWS, "the cache must hold a full window"
        assert cache.shape[1] % new_rows == 0
    hbm = pl.BlockSpec(memory_space=pl.ANY)
    tail = lambda c: pl.BlockSpec((None, new_rows, HEAD_DIM), lambda b: (b, c.shape[1] // new_rows - 1, 0))
    outs = pl.pallas_call(
        _attn_sample_kernel,
        grid=(bd,),
        in_specs=[pl.BlockSpec((None, t_new, D_ATT), lambda b: (b, 0, 0))]
        + [row_spec(c.shape[1]) for c in caches] + [row_spec(new_rows)] * N_DIL + [hbm] * N_DIL,
        out_specs=[tail(c) for c in caches] + [pl.BlockSpec((None, t_new, D_GRP), lambda b: (b, 0, 0))],
        out_shape=[jax.ShapeDtypeStruct(c.shape, F32) for c in caches]
        + [jax.ShapeDtypeStruct((bd, t_new, D_GRP), BF16)],
        scratch_shapes=[pltpu.VMEM((HEADS_PER_GROUP * t_new, c.shape[1]), F32) for c in caches]
        + [pltpu.VMEM((N_DIL, HEADS_PER_GROUP * t_new, new_rows), F32),
           pltpu.VMEM((N_DIL, t_new, D_GRP), F32), pltpu.VMEM((N_DIL, t_new, D_GRP), F32)],
        input_output_aliases={1 + 2 * N_DIL + g: g for g in range(N_DIL)},
        compiler_params=_params("arbitrary"),
        name="attn_sample",
    )(q3, *caches, *new_kvs, *shifted)
    return outs[:N_DIL], outs[N_DIL]


def _layer_norm(v, g, b):
    mu = jnp.mean(v, axis=-1, keepdims=True)
    d = v - mu
    var = jnp.mean(d * d, axis=-1, keepdims=True)
    return d * lax.rsqrt(var + LN_EPS) * g + b


SLAB_ROWS = SUBLANES
HALF = D_MODEL // 2
HI_MASK = 0xFFFF0000


def _bf16_bits(v):
    return pltpu.bitcast(v.astype(BF16).astype(F32), jnp.uint32)


def _pack_slabs(src_ref, slab_ref, n_tok):
    def body(tg, carry):
        t0 = pl.multiple_of(tg * SUBLANES, SUBLANES)
        for s in range(SLAB_ROWS):
            lo = src_ref[pl.ds(t0, SUBLANES), s * LANES:(s + 1) * LANES]
            hi = src_ref[pl.ds(t0, SUBLANES), HALF + s * LANES:HALF + (s + 1) * LANES]
            word = (_bf16_bits(lo) >> 16) | _bf16_bits(hi)
            slab_ref[pl.ds(t0 * SLAB_ROWS + s, SUBLANES, stride=SLAB_ROWS), :] = word
        return carry
    lax.fori_loop(0, n_tok // SUBLANES, body, 0, unroll=2)


def _unpack_gated_sum(slab_refs, gate_ref, dst_ref, n_tok):
    def body(tg, carry):
        t0 = pl.multiple_of(tg * SUBLANES, SUBLANES)
        gates = [gate_ref[pl.ds(t0, SUBLANES), k:k + 1] for k in range(len(slab_refs))]
        for s in range(SLAB_ROWS):
            words = [ref[pl.ds(t0 * SLAB_ROWS + s, SUBLANES, stride=SLAB_ROWS), :] for ref in slab_refs]
            lo = [g * pltpu.bitcast(w << 16, F32) for g, w in zip(gates, words)]
            hi = [g * pltpu.bitcast(w & jnp.uint32(HI_MASK), F32) for g, w in zip(gates, words)]
            dst_ref[pl.ds(t0, SUBLANES), s * LANES:(s + 1) * LANES] = functools.reduce(lambda a, b: a + b, lo)
            dst_ref[pl.ds(t0, SUBLANES), HALF + s * LANES:HALF + (s + 1) * LANES] = functools.reduce(
                lambda a, b: a + b, hi)
        return carry
    lax.fori_loop(0, n_tok // SUBLANES, body, 0, unroll=4)


def _unpack_slabs_bf16(slab_ref, dst_ref, n_tok, n_valid):
    pair = 2 * SUBLANES
    sub = lax.broadcasted_iota(jnp.int32, (SUBLANES, LANES), 0)

    def body(tg, carry):
        t0 = pl.multiple_of(tg * pair, pair)
        for s in range(SLAB_ROWS):
            words = []
            for half in range(2):
                t1 = t0 + half * SUBLANES
                w = slab_ref[pl.ds(t1 * SLAB_ROWS + s, SUBLANES, stride=SLAB_ROWS), :]
                words.append(jnp.where(t1 + sub < n_valid, w, jnp.uint32(0)))
            lo = jnp.concatenate([pltpu.bitcast(w << 16, F32) for w in words], axis=0)
            hi = jnp.concatenate([pltpu.bitcast(w & jnp.uint32(HI_MASK), F32) for w in words], axis=0)
            dst_ref[pl.ds(t0, pair), s * LANES:(s + 1) * LANES] = lo.astype(BF16)
            dst_ref[pl.ds(t0, pair), HALF + s * LANES:HALF + (s + 1) * LANES] = hi.astype(BF16)
        return carry
    lax.fori_loop(0, n_tok // pair, body, 0)


def _mix_kernel(conv_steps, *refs):
    if conv_steps:
        conv_in, refs = refs[:4], refs[4:]
    else:
        p_ref, refs = refs[0], refs[1:]
    o_ref, gc_ref, ga_ref, x_ref, wc_ref, wa_ref, wo_ref, g_ref, b_ref = refs[:9]
    if conv_steps:
        h_ref, hs_ref, st_ref, uext = refs[-4:]
        p = _gated_conv_tile(*conv_in, st_ref, uext, pl.program_id(0) % conv_steps == 0)
    else:
        h_ref, hs_ref = refs[-2:]
        p = p_ref[...]
    y_conv = jnp.dot(p, wc_ref[...], preferred_element_type=F32)
    y_att = jnp.dot(o_ref[...], wa_ref[...], preferred_element_type=F32)
    merged = (jax.nn.sigmoid(gc_ref[...].astype(F32)) * y_conv
              + jax.nn.sigmoid(ga_ref[...].astype(F32)) * y_att)
    mix = jnp.dot(merged.astype(BF16), wo_ref[...], preferred_element_type=F32)
    h_ref[...] = _layer_norm(ALPHA * x_ref[...] + mix, g_ref[...], b_ref[...])
    _pack_slabs(h_ref, hs_ref, h_ref.shape[0])


def _mix(p, o, z, x, wc16, wa16, wo16, ln_g, ln_b, tm, total, h_all=None, row_off=0, conv=None):
    m = x.shape[0]
    assert row_off % tm == 0
    blk_off = row_off // tm
    row = lambda w: pl.BlockSpec((tm, w), lambda i: (i, 0))
    const = lambda a: pl.BlockSpec(a.shape, lambda i: (0, 0), pipeline_mode=pl.Buffered(1))
    out_specs = [pl.BlockSpec((tm, D_MODEL), lambda i: (i + blk_off, 0)),
                 pl.BlockSpec((tm * SLAB_ROWS, LANES), lambda i: (i + blk_off, 0))]
    out_shape = [jax.ShapeDtypeStruct((total, D_MODEL), F32),
                 jax.ShapeDtypeStruct((total * SLAB_ROWS, LANES), jnp.uint32)]
    scratch, conv_steps = [], 0
    if conv is None:
        in_specs, args = [row(D_CONV)], [p]
    else:
        conv_w, seq = conv
        assert seq % tm == 0
        conv_steps = seq // tm
        zspec = lambda blk: pl.BlockSpec((tm, D_CONV), lambda i: (i, blk))
        in_specs, args = [zspec(ZB_BLK), zspec(ZB_BLK + 1), zspec(ZB_BLK + 2), const(conv_w)], [z, z, z, conv_w]
        out_specs.append(pl.BlockSpec((1, CONV_W - 1, D_CONV), lambda i: (i // conv_steps, 0, 0)))
        out_shape.append(jax.ShapeDtypeStruct((m // seq, CONV_W - 1, D_CONV), F32))
        scratch = [pltpu.VMEM((tm + SUBLANES, D_CONV), F32)]
    in_specs += [row(D_GRP),
                 pl.BlockSpec((tm, D_MODEL), lambda i: (i, 0)), pl.BlockSpec((tm, D_MODEL), lambda i: (i, 1)),
                 row(D_MODEL), const(wc16), const(wa16), const(wo16), const(ln_g), const(ln_b)]
    args += [o, z, z, x, wc16, wa16, wo16, ln_g, ln_b]
    aliases = {}
    if h_all is not None:
        in_specs += [pl.BlockSpec(memory_space=pl.ANY)] * 2
        args += list(h_all)
        aliases = {len(args) - 2: 0, len(args) - 1: 1}
    outs = pl.pallas_call(
        functools.partial(_mix_kernel, conv_steps),
        grid=(m // tm,),
        in_specs=in_specs,
        out_specs=out_specs,
        out_shape=out_shape,
        scratch_shapes=scratch,
        input_output_aliases=aliases,
        compiler_params=_params("arbitrary"),
        name="mix_ln1",
    )(*args)
    return (tuple(outs[:2]), outs[2]) if conv is not None else tuple(outs)


def _route(h, w_ref, b_ref, carry):
    tile = h.shape[0]
    h_hi = h.astype(BF16)
    h_lo = (h - h_hi.astype(F32)).astype(BF16)
    w_both = w_ref[...]
    by_hi = jnp.dot(h_hi, w_both, preferred_element_type=F32)
    by_lo = jnp.dot(h_lo, w_both[:, :LANES], preferred_element_type=F32)
    logits = by_hi[:, :LANES] + (by_hi[:, LANES:] + by_lo) + b_ref[...]
    lane = lax.broadcasted_iota(jnp.int32, (tile, LANES), 1)
    lane_f = lane.astype(F32)
    first_max = lambda v, mx: jnp.min(jnp.where(v == mx, lane_f, float(LANES)), axis=-1, keepdims=True)

    is_grp = lane < N_EXPERT_GROUPS
    lg = jnp.where(is_grp, logits, -jnp.inf)
    mg = jnp.max(lg, axis=-1, keepdims=True)
    gsel = first_max(lg, mg).astype(jnp.int32)
    p_group = 1.0 / jnp.sum(jnp.where(is_grp, jnp.exp(logits - mg), 0.0), axis=-1, keepdims=True)

    e_lane = lane - N_EXPERT_GROUPS
    in_grp = jnp.logical_and(e_lane >= 0, (e_lane >> 3) == gsel)
    le = jnp.where(in_grp, logits, -jnp.inf)
    v1 = jnp.max(le, axis=-1, keepdims=True)
    i1 = first_max(le, v1).astype(jnp.int32)
    le2 = jnp.where(lane == i1, -jnp.inf, le)
    v2 = jnp.max(le2, axis=-1, keepdims=True)
    i2 = first_max(le2, v2).astype(jnp.int32)
    ex = jnp.exp(v2 - v1)
    g1 = p_group / (1.0 + ex)
    g2 = p_group * ex / (1.0 + ex)
    e1 = i1 - N_EXPERT_GROUPS
    e2 = i2 - N_EXPERT_GROUPS

    hot1 = lane == e1
    hot2 = lane == e2
    hot = jnp.where(jnp.logical_or(hot1, hot2), 1.0, 0.0)
    rr = lax.broadcasted_iota(jnp.int32, (tile, tile), 0)
    cc = lax.broadcasted_iota(jnp.int32, (tile, tile), 1)
    tri = jnp.where(cc < rr, 1.0, 0.0).astype(BF16)
    before = jnp.dot(tri, hot.astype(BF16), preferred_element_type=F32) + carry[0:1, :]
    r1 = jnp.sum(jnp.where(hot1, before, 0.0), axis=-1, keepdims=True).astype(jnp.int32)
    r2 = jnp.sum(jnp.where(hot2, before, 0.0), axis=-1, keepdims=True).astype(jnp.int32)
    carry[...] = carry[...] + jnp.sum(hot, axis=0, keepdims=True)

    idx = jnp.where(lane == 0, e1, jnp.where(lane == 1, e2, jnp.where(lane == 2, r1, r2)))
    return idx, jnp.where(lane == 0, g1, g2)


def _router_kernel(h_ref, w_ref, b_ref, idx_ref, gate_ref, cnt_ref, carry):
    @pl.when(pl.program_id(0) == 0)
    def _():
        carry[...] = jnp.zeros(carry.shape, F32)

    idx_ref[...], gate_ref[...] = _route(h_ref[...], w_ref, b_ref, carry)
    cnt_ref[...] = carry[...]


def _router(h_all, w_r, b_r):
    m = h_all.shape[0]
    tile = ROUTER_TILE if m % ROUTER_TILE == 0 else ROW_TILE
    row = pl.BlockSpec((tile, LANES), lambda i: (i, 0))
    return pl.pallas_call(
        _router_kernel,
        grid=(m // tile,),
        in_specs=[pl.BlockSpec((tile, D_MODEL), lambda i: (i, 0)),
                  pl.BlockSpec((D_MODEL, 2 * LANES), lambda i: (0, 0)),
                  pl.BlockSpec((1, LANES), lambda i: (0, 0))],
        out_specs=[row, row, pl.BlockSpec((SUBLANES, LANES), lambda i: (0, 0))],
        out_shape=[jax.ShapeDtypeStruct((m, LANES), jnp.int32), jax.ShapeDtypeStruct((m, LANES), F32),
                   jax.ShapeDtypeStruct((SUBLANES, LANES), F32)],
        scratch_shapes=[pltpu.VMEM((SUBLANES, LANES), F32)],
        compiler_params=_params("arbitrary"),
        name="router",
    )(h_all, w_r, b_r)


def _slab(ref, row):
    return ref.at[pl.ds(pl.multiple_of(row * SLAB_ROWS, SLAB_ROWS), SLAB_ROWS), :]


def _dispatch_kernel(tile, dest_ref, h_hbm, x_hbm, stage, sem_in, sem_rows):
    i = pl.program_id(0)
    n_steps = pl.num_programs(0)
    slot_of = lambda step: lax.rem(step, DISPATCH_SLOTS)

    def load(step):
        rows = pl.ds(pl.multiple_of(step * (tile * SLAB_ROWS), SLAB_ROWS), tile * SLAB_ROWS)
        return pltpu.make_async_copy(h_hbm.at[rows, :], stage.at[slot_of(step)], sem_in.at[slot_of(step)])

    def each_row(step, action):
        slot = slot_of(step)

        def body(r, c):
            for k in range(TOP_K):
                action(pltpu.make_async_copy(_slab(stage.at[slot], r),
                                             _slab(x_hbm, dest_ref[(step * tile + r) * TOP_K + k]),
                                             sem_rows.at[slot]), k)
            return c
        lax.fori_loop(0, tile, body, 0, unroll=DMA_UNROLL)

    @pl.when(i == 0)
    def _():
        load(i).start()

    @pl.when(i + 1 < n_steps)
    def _():
        load(i + 1).start()

    load(i).wait()
    each_row(i, lambda c, k: c.start(priority=k % DMA_PRIORITIES))

    @pl.when(i > 0)
    def _():
        each_row(i - 1, lambda c, k: c.wait())

    @pl.when(i == n_steps - 1)
    def _():
        each_row(i, lambda c, k: c.wait())


def _dispatch(h_slabs, dest_flat, n_rows):
    m = h_slabs.shape[0] // SLAB_ROWS
    tile = DISPATCH_TILE if m % DISPATCH_TILE == 0 else ROW_TILE
    grid_spec = pltpu.PrefetchScalarGridSpec(
        num_scalar_prefetch=1,
        grid=(m // tile,),
        in_specs=[pl.BlockSpec(memory_space=pl.ANY)],
        out_specs=pl.BlockSpec(memory_space=pl.ANY),
        scratch_shapes=[pltpu.VMEM((DISPATCH_SLOTS, tile * SLAB_ROWS, LANES), jnp.uint32),
                        pltpu.SemaphoreType.DMA((DISPATCH_SLOTS,)), pltpu.SemaphoreType.DMA((DISPATCH_SLOTS,))],
    )
    return pl.pallas_call(
        functools.partial(_dispatch_kernel, tile),
        grid_spec=grid_spec,
        out_shape=jax.ShapeDtypeStruct((n_rows * SLAB_ROWS, LANES), jnp.uint32),
        compiler_params=pltpu.CompilerParams(dimension_semantics=("arbitrary",), has_side_effects=True),
        name="moe_dispatch",
    )(dest_flat, h_slabs)


def _expert_kernel(blk_e, blk_first, blk_valid, blk_ord, exp_list, n_used, n_exp,
                   x_ref, wg_hbm, wu_hbm, wd_hbm, y_ref, wg32, wu32, wd32, wg16, wu16, wd16, sems, rows32, x16):
    i = pl.program_id(0)

    def weight_copies(e, slot):
        return [pltpu.make_async_copy(hbm.at[e], buf.at[slot], sems.at[slot, n])
                for n, (hbm, buf) in enumerate(((wg_hbm, wg32), (wu_hbm, wu32), (wd_hbm, wd32)))]

    def start_weights(ordinal, slot):
        for c in weight_copies(exp_list[ordinal], slot):
            c.start()

    @pl.when(i == 0)
    def _():
        for ahead in range(W_SLOTS):
            @pl.when(ahead < n_exp[0])
            def _(ahead=ahead):
                start_weights(ahead, ahead)

    @pl.when(i < n_used[0])
    def _():
        @pl.when(blk_first[i] == 1)
        def _():
            ordinal = blk_ord[i]
            slot = lax.rem(ordinal, W_SLOTS)
            for c in weight_copies(blk_e[i], slot):
                c.wait()
            wg16[...] = wg32[slot].astype(BF16)
            wu16[...] = wu32[slot].astype(BF16)
            wd16[...] = wd32[slot].astype(BF16)

            @pl.when(ordinal + W_SLOTS < n_exp[0])
            def _():
                start_weights(ordinal + W_SLOTS, slot)

        _unpack_slabs_bf16(x_ref, x16, MOE_BLK, blk_valid[i])
        x = x16[...]
        gate = jnp.dot(x, wg16[...], preferred_element_type=F32)
        up = jnp.dot(x, wu16[...], preferred_element_type=F32)
        hid = (jax.nn.silu(gate) * up).astype(BF16)
        rows32[...] = jnp.dot(hid, wd16[...], preferred_element_type=F32)
        _pack_slabs(rows32, y_ref, MOE_BLK)


def _experts(x_slabs, plan, w_gate, w_up, w_down):
    n_rows = x_slabs.shape[0] // SLAB_ROWS
    n_blocks = n_rows // MOE_BLK
    row_map = lambda i, be, bf, bv, bo, el, nu, ne: (jnp.minimum(i, nu[0] - 1), 0)
    hbm = pl.BlockSpec(memory_space=pl.ANY)
    grid_spec = pltpu.PrefetchScalarGridSpec(
        num_scalar_prefetch=7,
        grid=(n_blocks,),
        in_specs=[pl.BlockSpec((MOE_BLK * SLAB_ROWS, LANES), row_map), hbm, hbm, hbm],
        out_specs=pl.BlockSpec((MOE_BLK * SLAB_ROWS, LANES), row_map),
        scratch_shapes=[pltpu.VMEM((W_SLOTS, D_MODEL, D_EXPERT), F32), pltpu.VMEM((W_SLOTS, D_MODEL, D_EXPERT), F32),
                        pltpu.VMEM((W_SLOTS, D_EXPERT, D_MODEL), F32),
                        pltpu.VMEM((D_MODEL, D_EXPERT), BF16), pltpu.VMEM((D_MODEL, D_EXPERT), BF16),
                        pltpu.VMEM((D_EXPERT, D_MODEL), BF16),
                        pltpu.SemaphoreType.DMA((W_SLOTS, 3)),
                        pltpu.VMEM((MOE_BLK, D_MODEL), F32), pltpu.VMEM((MOE_BLK, D_MODEL), BF16)],
    )
    return pl.pallas_call(
        _expert_kernel,
        grid_spec=grid_spec,
        out_shape=jax.ShapeDtypeStruct((n_rows * SLAB_ROWS, LANES), jnp.uint32),
        compiler_params=_params("arbitrary"),
        name="moe_experts",
    )(*plan, x_slabs, w_gate, w_up, w_down)


def _combine_kernel(tok_off, tile, dest_ref, y_hbm, h_ref, gate_ref, g_ref, b_ref, out_ref, ybuf, ffn_sum, sems):
    i = pl.program_id(0)
    n_steps = pl.num_programs(0)

    def copy(step, r, k):
        slot = step & 1
        row = dest_ref[(tok_off + step * tile + r) * TOP_K + k]
        return pltpu.make_async_copy(_slab(y_hbm, row), _slab(ybuf.at[slot, k], r), sems.at[slot])

    def gather(step):
        def body(r, c):
            for k in range(TOP_K):
                copy(step, r, k).start()
            return c
        lax.fori_loop(0, tile, body, 0, unroll=DMA_UNROLL)

    @pl.when(i == 0)
    def _():
        gather(i)

    @pl.when(i + 1 < n_steps)
    def _():
        gather(i + 1)

    def wait(r, c):
        for k in range(TOP_K):
            copy(i, r, k).wait()
        return c

    lax.fori_loop(0, tile, wait, 0, unroll=DMA_UNROLL)
    slot = i & 1
    _unpack_gated_sum([ybuf.at[slot, k] for k in range(TOP_K)], gate_ref, ffn_sum, tile)
    out_ref[...] = _layer_norm(ALPHA * h_ref[...] + ffn_sum[...], g_ref[...], b_ref[...])


def _combine(y_slabs, dest_flat, h_all, gates, ln_g, ln_b, tok_off, n_tok):
    tile = COMBINE_TILE if (n_tok % COMBINE_TILE == 0 and tok_off % COMBINE_TILE == 0) else ROW_TILE
    blk_off = tok_off // tile
    grid_spec = pltpu.PrefetchScalarGridSpec(
        num_scalar_prefetch=1,
        grid=(n_tok // tile,),
        in_specs=[pl.BlockSpec(memory_space=pl.ANY),
                  pl.BlockSpec((tile, D_MODEL), lambda i, d: (i + blk_off, 0)),
                  pl.BlockSpec((tile, LANES), lambda i, d: (i + blk_off, 0)),
                  pl.BlockSpec((1, D_MODEL), lambda i, d: (0, 0)),
                  pl.BlockSpec((1, D_MODEL), lambda i, d: (0, 0))],
        out_specs=pl.BlockSpec((tile, D_MODEL), lambda i, d: (i, 0)),
        scratch_shapes=[pltpu.VMEM((2, TOP_K, tile * SLAB_ROWS, LANES), jnp.uint32),
                        pltpu.VMEM((tile, D_MODEL), F32), pltpu.SemaphoreType.DMA((2,))],
    )
    return pl.pallas_call(
        functools.partial(_combine_kernel, tok_off, tile),
        grid_spec=grid_spec,
        out_shape=jax.ShapeDtypeStruct((n_tok, D_MODEL), F32),
        compiler_params=_params("arbitrary"),
        name="combine_ln2",
    )(dest_flat, y_slabs, h_all, gates, ln_g, ln_b)


def _moe_plan(idx, counts_f):
    m = idx.shape[0]
    n_asg = m * TOP_K
    experts = idx[:, 0:TOP_K]
    ranks = idx[:, TOP_K:2 * TOP_K]
    counts = counts_f[0, :N_EXPERTS].astype(jnp.int32)
    padded = (counts + MOE_BLK - 1) // MOE_BLK * MOE_BLK
    pad_end = jnp.cumsum(padded)
    pad_start = pad_end - padded
    expert_ids = jnp.arange(N_EXPERTS, dtype=jnp.int32)
    lookup = lambda hot, table: jnp.sum(jnp.where(hot, table, 0), axis=-1)
    dest = (lookup(experts[:, :, None] == expert_ids, pad_start) + ranks).reshape(n_asg).astype(jnp.int32)
    n_blocks = -(-(n_asg + N_EXPERTS * (MOE_BLK - 1)) // MOE_BLK)
    n_used = jnp.maximum(pad_end[-1] // MOE_BLK, 1).astype(jnp.int32)
    blk = jnp.minimum(jnp.arange(n_blocks, dtype=jnp.int32), n_used - 1)
    blk_row = blk * MOE_BLK
    blk_e = jnp.minimum(jnp.sum((pad_end[None, :] <= blk_row[:, None]).astype(jnp.int32), axis=-1), N_EXPERTS - 1)
    blk_hot = blk_e[:, None] == expert_ids
    blk_off = blk_row - lookup(blk_hot, pad_start)
    blk_first = (blk_off == 0).astype(jnp.int32)
    blk_valid = jnp.clip(lookup(blk_hot, counts) - blk_off, 0, MOE_BLK).astype(jnp.int32)
    used = counts > 0
    exp_list = jnp.argsort(jnp.logical_not(used), stable=True).astype(jnp.int32)
    blk_ord = lookup(blk_hot, jnp.cumsum(used.astype(jnp.int32)) - 1).astype(jnp.int32)
    n_exp = jnp.sum(used.astype(jnp.int32)).reshape(1)
    plan = (blk_e, blk_first, blk_valid, blk_ord, exp_list, n_used.reshape(1), n_exp)
    return dest, n_blocks * MOE_BLK, plan


def kernel(x_prompt, x_sample, cache_kv_w128, cache_kv_w512, cache_kv_w2048, state_conv, w_in, conv_w, w_conv_out, w_att_out, w_o, ln1_g, ln1_b, w_router_group, b_router_group, w_router_expert, b_router_expert, w_expert_gate, w_expert_up, w_expert_down, ln2_g, ln2_b):
    bsz, seq, _ = x_prompt.shape
    bd, t_new, _ = x_sample.shape
    m_p = bsz * seq
    m_s = bd * t_new
    m_all = m_p + m_s
    caches = (cache_kv_w128, cache_kv_w512, cache_kv_w2048)
    lyr = 0

    w_in16 = w_in[lyr]
    wc16 = w_conv_out[lyr].astype(BF16)
    wa16 = w_att_out[lyr].astype(BF16)
    wo16 = w_o[lyr].astype(BF16)
    cw = conv_w[lyr]
    g1, b1 = ln1_g[lyr][None, :], ln1_b[lyr][None, :]
    g2, b2 = ln2_g[lyr][None, :], ln2_b[lyr][None, :]
    w_re = jnp.transpose(w_router_expert[lyr], (1, 0, 2)).reshape(D_MODEL, N_EXPERTS)
    w_r = jnp.pad(jnp.concatenate([w_router_group[lyr], w_re], axis=1),
                  ((0, 0), (0, LANES - N_EXPERT_GROUPS - N_EXPERTS)))
    w_r_hi = w_r.astype(BF16)
    w_r = jnp.concatenate([w_r_hi, (w_r - w_r_hi.astype(F32)).astype(BF16)], axis=1)
    b_r = jnp.pad(jnp.concatenate([b_router_group[lyr], b_router_expert[lyr].reshape(N_EXPERTS)]),
                  (0, LANES - N_EXPERT_GROUPS - N_EXPERTS))[None, :]

    xp = x_prompt.reshape(m_p, D_MODEL)
    xs = x_sample.reshape(m_s, D_MODEL)

    caches3 = [c[lyr].reshape(bd, -1, HEAD_DIM) for c in caches]
    z_p, qkv_p, shifted = _proj_in(xp, w_in16, min(PROJ_TM, m_p),
                                   [c.reshape(-1, HEAD_DIM) for c in caches3], t_new * KV_ROWS)
    shifted = [s.reshape(c.shape) for s, c in zip(shifted, caches3)]
    o_p = _attn_prompt(qkv_p, bsz, seq)
    h_buf, conv_p = _mix(None, o_p, z_p, xp, wc16, wa16, wo16, g1, b1, min(MIX_TM, seq), m_all, conv=(cw, seq))

    z_s, qkv_s, _ = _proj_in(xs, w_in16, m_s)
    p_s, conv_s = _conv_sample(z_s, cw, state_conv[lyr], t_new)
    q3 = qkv_s.reshape(bd, t_new, QKV_COLS)
    kv5 = qkv_s.reshape(bd, t_new, 3, N_DIL, HEADS_PER_GROUP, HEAD_DIM)
    new_kvs = [kv5[:, :, 1:3, g].reshape(bd, t_new * KV_ROWS, HEAD_DIM) for g in range(N_DIL)]
    bufs_s, o_s = _attn_sample(q3, caches3, new_kvs, shifted)
    bufs_s = [buf.reshape(1, bd, window, 2, HEADS_PER_GROUP, HEAD_DIM)
              for buf, (window, _) in zip(bufs_s, DILATED_GROUPS)]
    h_buf = _mix(p_s, o_s.reshape(m_s, D_GRP), z_s, xs, wc16, wa16, wo16, g1, b1, m_s, m_all, h_all=h_buf, row_off=m_p)

    h_all, h_slabs = h_buf
    idx, gates, counts = _router(h_all, w_r, b_r)
    dest, n_rows, plan = _moe_plan(idx, counts)
    x_slabs = _dispatch(h_slabs, dest, n_rows)
    y_slabs = _experts(x_slabs, plan, w_expert_gate[lyr], w_expert_up[lyr], w_expert_down[lyr])
    y_p = _combine(y_slabs, dest, h_all, gates, g2, b2, 0, m_p)
    y_s = _combine(y_slabs, dest, h_all, gates, g2, b2, m_p, m_s)

    qkv_p3 = qkv_p.reshape(bsz, seq, QKV_COLS)
    bufs_p = []
    for g, (w, _) in enumerate(DILATED_GROUPS):
        keep = min(w, seq)
        bufs_p.append(_kv_pack(qkv_p3, g, keep).reshape(1, bsz, keep, 2, HEADS_PER_GROUP, HEAD_DIM))
    return (y_p.reshape(bsz, seq, D_MODEL), y_s.reshape(bd, t_new, D_MODEL),
            bufs_p[0], bufs_p[1], bufs_p[2], conv_p[None],
            bufs_s[0], bufs_s[1], bufs_s[2], conv_s[None])
```
